```python
import math
import jax, jax.numpy as jnp
from jax import lax
import numpy as np

D_MODEL = 1024
BATCH = 32
SEQ = 256
DEPTH = 2
DEC_BATCH = 2
DEC_SEQ = 4096
PAST_LEN = 256

GRID_W = 64
HEAD_DIM = 64
NA_HEADS = 8
NA_WIDTH = NA_HEADS * HEAD_DIM
FNET_GROUPS = 4
FNET_GROUP_DIM = 128
FNET_WIDTH = FNET_GROUPS * FNET_GROUP_DIM
WIN_ROWS = 8
WIN_COLS = 16
D_FF = 2816
N_SUB = 3
N_MOD = 3 * N_SUB
IN_WIDTH = 3 * NA_WIDTH + FNET_WIDTH + 2 * D_MODEL
Q_BLOCK = 128
ATTN_SCALE = HEAD_DIM ** -0.5
ALPHA = (2 * DEPTH) ** 0.25
BETA = (8 * DEPTH) ** -0.25
LN_EPS = 1e-5
NEG_INF = -1e30

kernel_name = 'hybrid_fnet_natten_macaron_deepnorm_step'


def _layer_norm(x, g, b):
    xf = x.astype(jnp.float32)
    mu = jnp.mean(xf, axis=-1, keepdims=True)
    var = jnp.mean(jnp.square(xf - mu), axis=-1, keepdims=True)
    y = (xf - mu) * lax.rsqrt(var + LN_EPS)
    return (y * g.astype(jnp.float32) + b.astype(jnp.float32)).astype(x.dtype)


def _mods(cvec, w_ada, b_ada):
    m = jax.nn.silu(cvec) @ w_ada + b_ada
    return m.reshape(cvec.shape[0], 1, N_MOD, D_MODEL)


def _modulate(x, mods, i):
    return x * (1.0 + mods[:, :, 3 * i + 1]) + mods[:, :, 3 * i]


def _post_norm(x, y, mods, i, res_w, ln_g, ln_b):
    return _layer_norm(ALPHA * x + res_w * mods[:, :, 3 * i + 2] * y, ln_g[i], ln_b[i])


def _swiglu(u, w_up, w_down):
    a, g = jnp.split(u @ w_up, 2, axis=-1)
    return (jax.nn.silu(g) * a) @ w_down


def _ffn_sublayer(x, mods, i, w_up, w_down, ln_g, ln_b):
    y = _swiglu(_modulate(x, mods, i), w_up, w_down)
    return _post_norm(x, y, mods, i, 0.5, ln_g, ln_b)


def _heads(t):
    b_, l_ = t.shape[:2]
    return t.reshape(b_, l_, NA_HEADS, HEAD_DIM).transpose(0, 2, 1, 3)


def _merge_heads(o):
    b_, _, l_, _ = o.shape
    return o.transpose(0, 2, 1, 3).reshape(b_, l_, NA_WIDTH)


def _project_in(u, w_in):
    z = u @ w_in
    cuts = [NA_WIDTH, 2 * NA_WIDTH, 3 * NA_WIDTH, 3 * NA_WIDTH + FNET_WIDTH,
            3 * NA_WIDTH + FNET_WIDTH + D_MODEL]
    q, k, v, f, g_a, g_b = jnp.split(z, cuts, axis=-1)
    return _heads(q) * ATTN_SCALE, _heads(k), _heads(v), f, g_a, g_b


def _fourier_mix(f):
    b_, l_ = f.shape[:2]
    fg = f.reshape(b_, l_, FNET_GROUPS, FNET_GROUP_DIM).transpose(0, 2, 1, 3).astype(jnp.float32)
    y = jnp.fft.fft2(fg, norm='ortho').real
    return y.transpose(0, 2, 1, 3).reshape(b_, l_, FNET_WIDTH).astype(f.dtype)


def _merge_branches(f, o_na, g_a, g_b, w_fourier, w_na_out, w_out):
    branch_a = _fourier_mix(f) @ w_fourier
    branch_b = _merge_heads(o_na) @ w_na_out
    return (jax.nn.sigmoid(g_a) * branch_a + jax.nn.sigmoid(g_b) * branch_b) @ w_out


def _context_attention(q, k, v):
    b_, h_, lc, hd = q.shape
    nblk = lc // Q_BLOCK
    qb = q.reshape(b_, h_, nblk, Q_BLOCK, hd).transpose(2, 0, 1, 3, 4)

    def one_block(qi):
        s = jnp.einsum('bhqd,bhkd->bhqk', qi, k).astype(jnp.float32)
        p = jax.nn.softmax(s, axis=-1).astype(v.dtype)
        return jnp.einsum('bhqk,bhkd->bhqd', p, v)

    o = lax.map(one_block, qb)
    return o.transpose(1, 2, 0, 3, 4).reshape(b_, h_, lc, hd)


def _neighbourhood_attention(q, k, v, k_ctx, v_ctx, rpb):
    b_, h_, l_, hd = q.shape
    rows = l_ // GRID_W
    kr = min(WIN_ROWS, rows)
    qg = q.reshape(b_, h_, rows, GRID_W, hd)
    kg = k.reshape(b_, h_, rows, GRID_W, hd)
    vg = v.reshape(b_, h_, rows, GRID_W, hd)
    r_idx = jnp.arange(rows)
    r0 = jnp.clip(r_idx - kr // 2, 0, rows - kr)
    key_rows = r0[:, None] + jnp.arange(kr)[None, :]
    k_band = jnp.take(kg, key_rows, axis=2)
    v_band = jnp.take(vg, key_rows, axis=2)
    c_idx = jnp.arange(GRID_W)
    c0 = jnp.clip(c_idx - WIN_COLS // 2, 0, GRID_W - WIN_COLS)
    col_ok = (c_idx[None, :] >= c0[:, None]) & (c_idx[None, :] < c0[:, None] + WIN_COLS)
    dr = key_rows - r_idx[:, None] + (WIN_ROWS - 1)
    dc = jnp.clip(c_idx[None, :] - c_idx[:, None] + (WIN_COLS - 1), 0, 2 * WIN_COLS - 2)
    bias = rpb[:, dr[:, None, :, None], dc[None, :, None, :]].astype(jnp.float32)
    s_win = jnp.einsum('bhrqd,bhrkcd->bhrqkc', qg, k_band).astype(jnp.float32) + bias[None]
    s_win = jnp.where(col_ok[:, None, :], s_win, NEG_INF)
    s_ctx = jnp.einsum('bhrqd,bhjd->bhrqj', qg, k_ctx).astype(jnp.float32)
    n_win = kr * GRID_W
    s = jnp.concatenate([s_win.reshape(b_, h_, rows, GRID_W, n_win), s_ctx], axis=-1)
    p = jax.nn.softmax(s, axis=-1).astype(v.dtype)
    p_win = p[..., :n_win].reshape(b_, h_, rows, GRID_W, kr, GRID_W)
    p_ctx = p[..., n_win:]
    o = (jnp.einsum('bhrqkc,bhrkcd->bhrqd', p_win, v_band)
         + jnp.einsum('bhrqj,bhjd->bhrqd', p_ctx, v_ctx))
    return o.reshape(b_, h_, l_, hd)


def setup_inputs(seed: int = 0) -> dict:
    key = jax.random.key(seed)
    ks = jax.random.split(key, 20)
    nrm = lambda k_, shape, s: jax.random.normal(k_, shape, jnp.float32) * s
    return {
        'x_prompt': nrm(ks[0], (BATCH, SEQ, D_MODEL), 1.0),
        'x_sample': nrm(ks[1], (DEC_BATCH, DEC_SEQ, D_MODEL), 1.0),
        'cache_k': nrm(ks[2], (DEC_BATCH, DEPTH, NA_HEADS, PAST_LEN, HEAD_DIM), 1.0),
        'cache_v': nrm(ks[3], (DEC_BATCH, DEPTH, NA_HEADS, PAST_LEN, HEAD_DIM), 1.0),
        'c': nrm(ks[4], (DEC_BATCH, D_MODEL), 1.0),
        'c_ctx': nrm(ks[5], (D_MODEL,), 1.0),
        'w_ada': nrm(ks[6], (DEPTH, D_MODEL, N_MOD * D_MODEL), 0.5 * D_MODEL ** -0.5),
        'b_ada': nrm(ks[7], (DEPTH, N_MOD * D_MODEL), 0.01),
        'ln_g': 1.0 + nrm(ks[8], (DEPTH, N_SUB, D_MODEL), 0.01),
        'ln_b': nrm(ks[9], (DEPTH, N_SUB, D_MODEL), 0.01),
        'w_ff1_up': nrm(ks[10], (DEPTH, D_MODEL, 2 * D_FF), D_MODEL ** -0.5),
        'w_ff1_down': nrm(ks[11], (DEPTH, D_FF, D_MODEL), BETA * D_FF ** -0.5),
        'w_in': nrm(ks[12], (DEPTH, D_MODEL, IN_WIDTH), D_MODEL ** -0.5),
        'rpb': nrm(ks[13], (DEPTH, NA_HEADS, 2 * WIN_ROWS - 1, 2 * WIN_COLS - 1), 0.02),
        'w_fourier': nrm(ks[14], (DEPTH, FNET_WIDTH, D_MODEL), BETA * FNET_WIDTH ** -0.5),
        'w_na_out': nrm(ks[15], (DEPTH, NA_WIDTH, D_MODEL), BETA * NA_WIDTH ** -0.5),
        'w_out': nrm(ks[16], (DEPTH, D_MODEL, D_MODEL), BETA * D_MODEL ** -0.5),
        'w_ff2_up': nrm(ks[17], (DEPTH, D_MODEL, 2 * D_FF), D_MODEL ** -0.5),
        'w_ff2_down': nrm(ks[18], (DEPTH, D_FF, D_MODEL), BETA * D_FF ** -0.5),
    }


def reference(x_prompt, x_sample, cache_k, cache_v, c, c_ctx, w_ada, b_ada, ln_g, ln_b,
              w_ff1_up, w_ff1_down, w_in, rpb, w_fourier, w_na_out, w_out, w_ff2_up, w_ff2_down):
    y_p = x_prompt
    y_s = x_sample
    k_list, v_list = [], []
    for l in range(DEPTH):
        m_ctx = _mods(c_ctx[None, :], w_ada[l], b_ada[l])
        m_lat = _mods(c, w_ada[l], b_ada[l])

        y_p = _ffn_sublayer(y_p, m_ctx, 0, w_ff1_up[l], w_ff1_down[l], ln_g[l], ln_b[l])
        q, k, v, f, g_a, g_b = _project_in(_modulate(y_p, m_ctx, 1), w_in[l])
        o = _context_attention(q, k, v)
        mix = _merge_branches(f, o, g_a, g_b, w_fourier[l], w_na_out[l], w_out[l])
        y_p = _post_norm(y_p, mix, m_ctx, 1, 1.0, ln_g[l], ln_b[l])
        y_p = _ffn_sublayer(y_p, m_ctx, 2, w_ff2_up[l], w_ff2_down[l], ln_g[l], ln_b[l])
        k_list.append(k)
        v_list.append(v)

        y_s = _ffn_sublayer(y_s, m_lat, 0, w_ff1_up[l], w_ff1_down[l], ln_g[l], ln_b[l])
        q, k, v, f, g_a, g_b = _project_in(_modulate(y_s, m_lat, 1), w_in[l])
        o = _neighbourhood_attention(q, k, v, cache_k[:, l], cache_v[:, l], rpb[l])
        mix = _merge_branches(f, o, g_a, g_b, w_fourier[l], w_na_out[l], w_out[l])
        y_s = _post_norm(y_s, mix, m_lat, 1, 1.0, ln_g[l], ln_b[l])
        y_s = _ffn_sublayer(y_s, m_lat, 2, w_ff2_up[l], w_ff2_down[l], ln_g[l], ln_b[l])

    new_k = jnp.stack(k_list, axis=1)
    new_v = jnp.stack(v_list, axis=1)
    return (y_p, y_s, new_k, new_v)
```

```python
import functools
import math

import numpy as np
import jax
import jax.numpy as jnp
from jax import lax
from jax.experimental import pallas as pl
from jax.experimental.pallas import tpu as pltpu

F32 = jnp.float32
BF16 = jnp.bfloat16

HEAD_DIM = 64
NA_HEADS = 8
NA_WIDTH = NA_HEADS * HEAD_DIM
FNET_GROUPS = 4
FNET_GROUP_DIM = 128
FNET_WIDTH = FNET_GROUPS * FNET_GROUP_DIM
GRID_W = 64
WIN_ROWS = 8
WIN_COLS = 16
N_SUB = 3
N_MOD = 3 * N_SUB
ATTN_SCALE = HEAD_DIM ** -0.5
LN_EPS = 1e-5
NEG_INF = -1e30

LANES = 128
MXU_DIM = 256
VMEM_LIMIT_BYTES = 56 * 1024 * 1024

ROW_TILE = 512
FF_CHUNK = MXU_DIM
Q_ROWS = 8
K_ROWS = 2 * Q_ROWS
MOD_ROWS = 8
HI = lax.Precision.HIGHEST


def _params(sem):
    return pltpu.CompilerParams(dimension_semantics=sem, vmem_limit_bytes=VMEM_LIMIT_BYTES)


def _const_spec(shape):
    nd = len(shape)
    return pl.BlockSpec(shape, lambda *_: (0,) * nd, pipeline_mode=pl.Buffered(1))


def _layer_norm(r, g, b):
    mu = jnp.mean(r, axis=-1, keepdims=True)
    d = r - mu
    var = jnp.mean(d * d, axis=-1, keepdims=True)
    return d * lax.rsqrt(var + LN_EPS) * g + b


def _bdot(a, b):
    return jnp.dot(a, b, preferred_element_type=F32)


def _dot_nt(a, b):
    return lax.dot_general(a, b, (((1,), (1,)), ((), ())), preferred_element_type=F32)


def _dft_mats(n):
    k = np.arange(n)
    ang = 2.0 * np.pi * ((k[:, None] * k[None, :]) % n) / n
    return np.cos(ang), np.sin(ang)


def _mods_kernel(c_ref, w_ref, b_ref, o_ref):
    c = c_ref[...]
    s = (c * jax.nn.sigmoid(c)).astype(BF16)
    o_ref[0] = _bdot(s, w_ref[0].astype(BF16)) + b_ref[0]


def _mods(cvec, w_ada, b_ada):
    depth, d, n = w_ada.shape
    tn = d
    return pl.pallas_call(
        _mods_kernel,
        grid=(depth, n // tn),
        in_specs=[
            pl.BlockSpec((MOD_ROWS, d), lambda l, j: (0, 0)),
            pl.BlockSpec((1, d, tn), lambda l, j: (l, 0, j)),
            pl.BlockSpec((1, 1, tn), lambda l, j: (l, 0, j)),
        ],
        out_specs=pl.BlockSpec((1, MOD_ROWS, tn), lambda l, j: (l, 0, j)),
        out_shape=jax.ShapeDtypeStruct((depth, MOD_ROWS, n), F32),
        compiler_params=_params(("arbitrary", "arbitrary")),
        name="mods",
    )(cvec, w_ada, b_ada.reshape(depth, 1, n))


def _ffn_kernel(x_ref, m_ref, wup_ref, wdn_ref, g_ref, b_ref, o_ref, acc_ref, *, sub, alpha):
    x = x_ref[...]
    shift = m_ref[0, 3 * sub:3 * sub + 1, :]
    scale = m_ref[0, 3 * sub + 1:3 * sub + 2, :]
    gate = m_ref[0, 3 * sub + 2:3 * sub + 3, :]
    u = (x * (1.0 + scale) + shift).astype(BF16)
    n_chunks = wup_ref.shape[0]
    half = wup_ref.shape[2] // 2

    def body(c, carry):
        h = _bdot(u, wup_ref[c])
        a = h[:, :half]
        g = h[:, half:]
        act = ((g * jax.nn.sigmoid(g)) * a).astype(BF16)
        y = _bdot(act, wdn_ref[c])

        @pl.when(c == 0)
        def _():
            acc_ref[...] = y

        @pl.when(c > 0)
        def _():
            acc_ref[...] += y

        return carry

    lax.fori_loop(0, n_chunks, body, 0)
    r = alpha * x + (0.5 * gate) * acc_ref[...]
    o_ref[...] = _layer_norm(r, g_ref[sub:sub + 1, :], b_ref[sub:sub + 1, :])


def _ffn(x2, mods_l, rows_per_group, group0, wup, wdn, ln_g, ln_b, sub, alpha):
    rows, d = x2.shape
    tm = min(ROW_TILE, rows_per_group)
    tiles_per_group = rows_per_group // tm
    return pl.pallas_call(
        functools.partial(_ffn_kernel, sub=sub, alpha=alpha),
        grid=(rows // tm,),
        in_specs=[
            pl.BlockSpec((tm, d), lambda i: (i, 0)),
            pl.BlockSpec((1, N_MOD, d), lambda i: (group0 + i // tiles_per_group, 0, 0)),
            _const_spec(wup.shape),
            _const_spec(wdn.shape),
            _const_spec(ln_g.shape),
            _const_spec(ln_b.shape),
        ],
        out_specs=pl.BlockSpec((tm, d), lambda i: (i, 0)),
        out_shape=jax.ShapeDtypeStruct((rows, d), F32),
        scratch_shapes=[pltpu.VMEM((tm, d), F32)],
        compiler_params=_params(("arbitrary",)),
        name=f"ffn{sub}",
    )(x2, mods_l, wup, wdn, ln_g, ln_b)


def _head_pair_attention(q, k_list, v_list, bias_list, n_pairs_lanes=LANES):
    lane = lax.broadcasted_iota(jnp.int32, (1, n_pairs_lanes), 1)
    outs = []
    for h in range(2):
        in_head = (lane // HEAD_DIM) == h
        qm = jnp.where(in_head, q, jnp.zeros_like(q))
        s = []
        for k, bias in zip(k_list, bias_list):
            sk = _dot_nt(qm, k)
            if bias is not None:
                sk = sk + bias[h]
            s.append(sk)
        m = s[0].max(axis=-1, keepdims=True)
        for sk in s[1:]:
            m = jnp.maximum(m, sk.max(axis=-1, keepdims=True))
        den = None
        o = None
        for sk, v in zip(s, v_list):
            p = jnp.exp(sk - m)
            ps = p.sum(axis=-1, keepdims=True)
            den = ps if den is None else den + ps
            pv = _bdot(p.astype(BF16), v)
            o = pv if o is None else o + pv
        outs.append(o / den)
    return jnp.where(lane < HEAD_DIM, outs[0], outs[1])


def _merge_out(fm, o, ga, gb, x, shift_gate, wf_ref, wna_ref, wout_ref, ln_g, ln_b, alpha):
    branch_a = _bdot(fm.astype(BF16), wf_ref[...])
    branch_b = _bdot(o, wna_ref[...])
    mg = jax.nn.sigmoid(ga) * branch_a + jax.nn.sigmoid(gb) * branch_b
    mix = _bdot(mg.astype(BF16), wout_ref[...])
    r = alpha * x + shift_gate * mix
    return _layer_norm(r, ln_g, ln_b)


def _ctx_mix_kernel(x_ref, m_ref, win_ref, wf_ref, wna_ref, wout_ref, g_ref, b_ref,
                    cdft_ref, pdft_ref, o_ref, k_ref, v_ref,
                    z_ref, oatt_ref, fm_ref, *, alpha):
    nb, seq, d = x_ref.shape
    rows = nb * seq
    x = x_ref[...].reshape(rows, d)
    shift = m_ref[0, 3:4, :]
    scale = m_ref[0, 4:5, :]
    gate = m_ref[0, 5:6, :]
    u = (x * (1.0 + scale) + shift).astype(BF16)
    n_chunks = win_ref.shape[0]
    cw = win_ref.shape[2]
    for c in range(n_chunks):
        z_ref[:, c * cw:(c + 1) * cw] = _bdot(u, win_ref[c])

    q_off, k_off, v_off, f_off = 0, NA_WIDTH, 2 * NA_WIDTH, 3 * NA_WIDTH
    ga_off = f_off + FNET_WIDTH
    gb_off = ga_off + d
    for b in range(nb):
        r0 = b * seq
        for h in range(NA_HEADS):
            k_ref[b, h] = z_ref[r0:r0 + seq, k_off + h * HEAD_DIM:k_off + (h + 1) * HEAD_DIM]
            v_ref[b, h] = z_ref[r0:r0 + seq, v_off + h * HEAD_DIM:v_off + (h + 1) * HEAD_DIM]
        for hp in range(NA_WIDTH // LANES):
            c0 = hp * LANES
            q = (z_ref[r0:r0 + seq, q_off + c0:q_off + c0 + LANES] * ATTN_SCALE).astype(BF16)
            k = z_ref[r0:r0 + seq, k_off + c0:k_off + c0 + LANES].astype(BF16)
            v = z_ref[r0:r0 + seq, v_off + c0:v_off + c0 + LANES].astype(BF16)
            o = _head_pair_attention(q, [k], [v], [None])
            oatt_ref[r0:r0 + seq, c0:c0 + LANES] = o.astype(BF16)
        for g in range(FNET_GROUPS):
            c0 = f_off + g * FNET_GROUP_DIM
            xg = z_ref[r0:r0 + seq, c0:c0 + FNET_GROUP_DIM]
            zc = jnp.dot(xg, cdft_ref[...], precision=HI, preferred_element_type=F32)
            stacked = jnp.concatenate([zc[:, :FNET_GROUP_DIM], zc[:, FNET_GROUP_DIM:]], axis=0)
            fm_ref[r0:r0 + seq, g * FNET_GROUP_DIM:(g + 1) * FNET_GROUP_DIM] = jnp.dot(
                pdft_ref[...], stacked, precision=HI, preferred_element_type=F32)

    out = _merge_out(fm_ref[...], oatt_ref[...], z_ref[:, ga_off:ga_off + d],
                     z_ref[:, gb_off:gb_off + d], x, gate, wf_ref, wna_ref, wout_ref,
                     g_ref[1:2, :], b_ref[1:2, :], alpha)
    o_ref[...] = out.reshape(nb, seq, d)


def _ctx_mix(x3, mods_l, win, wf, wna, wout, ln_g, ln_b, alpha):
    batch, seq, d = x3.shape
    nb = 2 if batch % 2 == 0 else 1
    in_width = win.shape[0] * win.shape[2]
    cc, cs = _dft_mats(FNET_GROUP_DIM)
    pc, ps = _dft_mats(seq)
    norm = 1.0 / math.sqrt(seq * FNET_GROUP_DIM)
    cdft = jnp.asarray(np.concatenate([cc, -cs], axis=1) * norm, F32)
    pdft = jnp.asarray(np.concatenate([pc, ps], axis=1), F32)
    kv_shape = jax.ShapeDtypeStruct((batch, NA_HEADS, seq, HEAD_DIM), F32)
    kv_spec = pl.BlockSpec((nb, NA_HEADS, seq, HEAD_DIM), lambda i: (i, 0, 0, 0))
    return pl.pallas_call(
        functools.partial(_ctx_mix_kernel, alpha=alpha),
        grid=(batch // nb,),
        in_specs=[
            pl.BlockSpec((nb, seq, d), lambda i: (i, 0, 0)),
            pl.BlockSpec((1, N_MOD, d), lambda i: (0, 0, 0)),
            _const_spec(win.shape),
            _const_spec(wf.shape),
            _const_spec(wna.shape),
            _const_spec(wout.shape),
            _const_spec(ln_g.shape),
            _const_spec(ln_b.shape),
            _const_spec(cdft.shape),
            _const_spec(pdft.shape),
        ],
        out_specs=[pl.BlockSpec((nb, seq, d), lambda i: (i, 0, 0)), kv_spec, kv_spec],
        out_shape=[jax.ShapeDtypeStruct((batch, seq, d), F32), kv_shape, kv_shape],
        scratch_shapes=[
            pltpu.VMEM((nb * seq, in_width), F32),
            pltpu.VMEM((nb * seq, NA_WIDTH), BF16),
            pltpu.VMEM((nb * seq, FNET_WIDTH), F32),
        ],
        compiler_params=_params(("arbitrary",)),
        name="ctx_mix",
    )(x3, mods_l, win, wf, wna, wout, ln_g, ln_b, cdft, pdft)


def _in_proj_kernel(x_ref, m_ref, win_ref, q_ref, k_ref, v_ref, f_ref, gab_ref):
    x = x_ref[...]
    shift = m_ref[0, 3:4, :]
    scale = m_ref[0, 4:5, :]
    u = (x * (1.0 + scale) + shift).astype(BF16)
    cw = win_ref.shape[2]
    q_ref[...] = (_bdot(u, win_ref[0]) * ATTN_SCALE).astype(BF16)
    k_ref[...] = _bdot(u, win_ref[1]).astype(BF16)
    v_ref[...] = _bdot(u, win_ref[2]).astype(BF16)
    f_ref[...] = _bdot(u, win_ref[3])
    for c in range(4, win_ref.shape[0]):
        gab_ref[:, (c - 4) * cw:(c - 3) * cw] = _bdot(u, win_ref[c])


def _in_proj(x2, mods_l, rows_per_group, group0, win):
    rows, d = x2.shape
    tm = min(ROW_TILE, rows_per_group)
    tiles_per_group = rows_per_group // tm
    cw = win.shape[2]
    row_spec = lambda w: pl.BlockSpec((tm, w), lambda i: (i, 0))
    return pl.pallas_call(
        _in_proj_kernel,
        grid=(rows // tm,),
        in_specs=[
            row_spec(d),
            pl.BlockSpec((1, N_MOD, d), lambda i: (group0 + i // tiles_per_group, 0, 0)),
            _const_spec(win.shape),
        ],
        out_specs=[row_spec(cw), row_spec(cw), row_spec(cw), row_spec(cw), row_spec(2 * d)],
        out_shape=[
            jax.ShapeDtypeStruct((rows, cw), BF16),
            jax.ShapeDtypeStruct((rows, cw), BF16),
            jax.ShapeDtypeStruct((rows, cw), BF16),
            jax.ShapeDtypeStruct((rows, cw), F32),
            jax.ShapeDtypeStruct((rows, 2 * d), F32),
        ],
        compiler_params=_params(("arbitrary",)),
        name="in_proj",
    )(x2, mods_l, win)


def _window_start_row(i, rows):
    return jnp.clip(Q_ROWS * i - Q_ROWS // 2, 0, rows - K_ROWS)


def _nbr_attn_kernel(q_ref, k_ref, v_ref, ck_ref, cv_ref, bias_ref, o_ref, *, rows):
    i = pl.program_id(1)
    b = pl.program_id(2)
    start = pl.multiple_of(_window_start_row(i, rows) * GRID_W, GRID_W * (Q_ROWS // 2))
    n_keys = K_ROWS * GRID_W
    kw = k_ref[b, pl.ds(start, n_keys), :]
    vw = v_ref[b, pl.ds(start, n_keys), :]
    o = _head_pair_attention(
        q_ref[0], [kw, ck_ref[b]], [vw, cv_ref[b]],
        [(bias_ref[0, 0], bias_ref[1, 0]), None])
    o_ref[0] = o.astype(BF16)


def _nbr_bias_table(rpb_l, rows):
    n_blocks = rows // Q_ROWS
    kr = min(WIN_ROWS, rows)
    variants = jnp.array([0, min(1, n_blocks - 1), n_blocks - 1])
    r = Q_ROWS * variants[:, None] + jnp.arange(Q_ROWS)[None, :]
    ks = jnp.clip(Q_ROWS * variants - Q_ROWS // 2, 0, rows - K_ROWS)
    key_row = ks[:, None] + jnp.arange(K_ROWS)[None, :]
    r0 = jnp.clip(r - kr // 2, 0, rows - kr)
    row_ok = (key_row[:, None, :] >= r0[:, :, None]) & (key_row[:, None, :] < r0[:, :, None] + kr)
    dr = jnp.clip(key_row[:, None, :] - r[:, :, None] + (WIN_ROWS - 1), 0, 2 * WIN_ROWS - 2)
    c_idx = jnp.arange(GRID_W)
    c0 = jnp.clip(c_idx - WIN_COLS // 2, 0, GRID_W - WIN_COLS)
    col_ok = (c_idx[None, :] >= c0[:, None]) & (c_idx[None, :] < c0[:, None] + WIN_COLS)
    dc = jnp.clip(c_idx[None, :] - c_idx[:, None] + (WIN_COLS - 1), 0, 2 * WIN_COLS - 2)
    bias = rpb_l[:, dr[:, :, None, :, None], dc[None, None, :, None, :]]
    ok = row_ok[:, :, None, :, None] & col_ok[None, None, :, None, :]
    bias = jnp.where(ok[None], bias.astype(F32), NEG_INF)
    h = rpb_l.shape[0]
    return bias.reshape(h, 3, Q_ROWS * GRID_W, K_ROWS * GRID_W)


def _nbr_attn(q3, k3, v3, ck3, cv3, bias):
    nb, seq, width = q3.shape
    rows = seq // GRID_W
    n_blocks = rows // Q_ROWS
    n_pairs = width // LANES
    past = ck3.shape[1]
    tq = Q_ROWS * GRID_W
    tk = K_ROWS * GRID_W

    def variant(i):
        return jnp.where(i == 0, 0, jnp.where(i == n_blocks - 1, 2, 1))

    return pl.pallas_call(
        functools.partial(_nbr_attn_kernel, rows=rows),
        grid=(n_pairs, n_blocks, nb),
        in_specs=[
            pl.BlockSpec((1, tq, LANES), lambda p, i, b: (b, i, p)),
            pl.BlockSpec((nb, seq, LANES), lambda p, i, b: (0, 0, p)),
            pl.BlockSpec((nb, seq, LANES), lambda p, i, b: (0, 0, p)),
            pl.BlockSpec((nb, past, LANES), lambda p, i, b: (0, 0, p)),
            pl.BlockSpec((nb, past, LANES), lambda p, i, b: (0, 0, p)),
            pl.BlockSpec((2, 1, tq, tk), lambda p, i, b: (p, variant(i), 0, 0)),
        ],
        out_specs=pl.BlockSpec((1, tq, LANES), lambda p, i, b: (b, i, p)),
        out_shape=jax.ShapeDtypeStruct((nb, seq, width), BF16),
        compiler_params=_params(("arbitrary", "arbitrary", "arbitrary")),
        name="nbr_attn",
    )(q3, k3, v3, ck3, cv3, bias)


def _fourier_kernel(x_ref, cdft_ref, d1_ref, d2_ref, twc_ref, tws_ref, o_ref,
                    zr_ref, zi_ref, tr_ref, ti_ref, *, n1, n2):
    gd = x_ref.shape[2]
    zc = jnp.dot(x_ref[0], cdft_ref[...], precision=HI, preferred_element_type=F32)
    zr_ref[...] = zc[:, :gd]
    zi_ref[...] = zc[:, gd:]

    def stage1(c, carry):
        zs = jnp.concatenate([zr_ref[pl.ds(c, n1, stride=n2), :],
                              zi_ref[pl.ds(c, n1, stride=n2), :]], axis=0)
        t = jnp.dot(d1_ref[...], zs, precision=HI, preferred_element_type=F32)
        tr = t[:n1]
        ti = t[n1:]
        cs = twc_ref[c]
        sn = tws_ref[c]
        row = pl.multiple_of(c * n1, n1)
        tr_ref[pl.ds(row, n1), :] = tr * cs + ti * sn
        ti_ref[pl.ds(row, n1), :] = ti * cs - tr * sn
        return carry

    lax.fori_loop(0, n2, stage1, 0)

    def stage2(k1, carry):
        ts = jnp.concatenate([tr_ref[pl.ds(k1, n2, stride=n1), :],
                              ti_ref[pl.ds(k1, n2, stride=n1), :]], axis=0)
        y = jnp.dot(d2_ref[...], ts, precision=HI, preferred_element_type=F32)
        o_ref[0, pl.ds(k1, n2, stride=n1), :] = y
        return carry

    lax.fori_loop(0, n1, stage2, 0)


def _fourier(f3):
    nb, seq, width = f3.shape
    n2 = GRID_W
    n1 = seq // n2
    gd = FNET_GROUP_DIM
    cc, cs = _dft_mats(gd)
    c1, s1 = _dft_mats(n1)
    c2, s2 = _dft_mats(n2)
    norm = 1.0 / math.sqrt(seq * gd)
    cdft = jnp.asarray(np.concatenate([cc, -cs], axis=1) * norm, F32)
    d1 = jnp.asarray(np.block([[c1, s1], [-s1, c1]]), F32)
    d2 = jnp.asarray(np.concatenate([c2, s2], axis=1), F32)
    ang = 2.0 * np.pi * ((np.arange(n2)[:, None] * np.arange(n1)[None, :]) % seq) / seq
    twc = jnp.asarray(np.broadcast_to(np.cos(ang)[:, :, None], (n2, n1, gd)), F32)
    tws = jnp.asarray(np.broadcast_to(np.sin(ang)[:, :, None], (n2, n1, gd)), F32)
    return pl.pallas_call(
        functools.partial(_fourier_kernel, n1=n1, n2=n2),
        grid=(nb, width // gd),
        in_specs=[
            pl.BlockSpec((1, seq, gd), lambda b, g: (b, 0, g)),
            _const_spec(cdft.shape),
            _const_spec(d1.shape),
            _const_spec(d2.shape),
            _const_spec(twc.shape),
            _const_spec(tws.shape),
        ],
        out_specs=pl.BlockSpec((1, seq, gd), lambda b, g: (b, 0, g)),
        out_shape=jax.ShapeDtypeStruct((nb, seq, width), F32),
        scratch_shapes=[pltpu.VMEM((seq, gd), F32)] * 4,
        compiler_params=_params(("arbitrary", "arbitrary")),
        name="fourier",
    )(f3, cdft, d1, d2, twc, tws)


def _merge_kernel(fm_ref, o_ref_in, gab_ref, x_ref, m_ref, wf_ref, wna_ref, wout_ref,
                  g_ref, b_ref, out_ref, *, alpha):
    d = x_ref.shape[1]
    gate = m_ref[0, 5:6, :]
    out_ref[...] = _merge_out(fm_ref[...], o_ref_in[...], gab_ref[:, :d], gab_ref[:, d:],
                              x_ref[...], gate, wf_ref, wna_ref, wout_ref,
                              g_ref[1:2, :], b_ref[1:2, :], alpha)


def _merge(fm2, o2, gab2, x2, mods_l, rows_per_group, group0, wf, wna, wout, ln_g, ln_b, alpha):
    rows, d = x2.shape
    tm = min(ROW_TILE, rows_per_group)
    tiles_per_group = rows_per_group // tm
    row_spec = lambda w: pl.BlockSpec((tm, w), lambda i: (i, 0))
    return pl.pallas_call(
        functools.partial(_merge_kernel, alpha=alpha),
        grid=(rows // tm,),
        in_specs=[
            row_spec(fm2.shape[1]), row_spec(o2.shape[1]), row_spec(gab2.shape[1]), row_spec(d),
            pl.BlockSpec((1, N_MOD, d), lambda i: (group0 + i // tiles_per_group, 0, 0)),
            _const_spec(wf.shape), _const_spec(wna.shape), _const_spec(wout.shape),
            _const_spec(ln_g.shape), _const_spec(ln_b.shape),
        ],
        out_specs=row_spec(d),
        out_shape=jax.ShapeDtypeStruct((rows, d), F32),
        compiler_params=_params(("arbitrary",)),
        name="merge",
    )(fm2, o2, gab2, x2, mods_l, wf, wna, wout, ln_g, ln_b)


def _prep_ff(w_up, w_down):
    d, two_ff = w_up.shape
    ff = two_ff // 2
    nc = ff // FF_CHUNK
    a = w_up[:, :ff].reshape(d, nc, FF_CHUNK)
    g = w_up[:, ff:].reshape(d, nc, FF_CHUNK)
    up = jnp.concatenate([a, g], axis=2).transpose(1, 0, 2).astype(BF16)
    dn = w_down.reshape(nc, FF_CHUNK, d).astype(BF16)
    return up, dn


def _prep_in(w_in):
    d, n = w_in.shape
    cw = NA_WIDTH
    return w_in.reshape(d, n // cw, cw).transpose(1, 0, 2).astype(BF16)


def kernel(x_prompt, x_sample, cache_k, cache_v, c, c_ctx, w_ada, b_ada, ln_g, ln_b, w_ff1_up,
           w_ff1_down, w_in, rpb, w_fourier, w_na_out, w_out, w_ff2_up, w_ff2_down):
    batch, seq, d = x_prompt.shape
    dec_batch, dec_seq, _ = x_sample.shape
    depth = w_ada.shape[0]
    alpha = (2 * depth) ** 0.25
    rows_lat = dec_seq // GRID_W

    cvec = jnp.zeros((MOD_ROWS, d), F32).at[0].set(c_ctx).at[1:1 + dec_batch].set(c)
    mods = _mods(cvec, w_ada, b_ada).reshape(depth, MOD_ROWS, N_MOD, d)

    y_p = x_prompt.reshape(batch * seq, d)
    y_s = x_sample.reshape(dec_batch * dec_seq, d)
    k_list, v_list = [], []
    for l in range(depth):
        m_l = mods[l]
        up1, dn1 = _prep_ff(w_ff1_up[l], w_ff1_down[l])
        up2, dn2 = _prep_ff(w_ff2_up[l], w_ff2_down[l])
        win = _prep_in(w_in[l])
        wf = w_fourier[l].astype(BF16)
        wna = w_na_out[l].astype(BF16)
        wout = w_out[l].astype(BF16)
        g_l, b_l = ln_g[l], ln_b[l]

        y_p = _ffn(y_p, m_l, batch * seq, 0, up1, dn1, g_l, b_l, 0, alpha)
        y_p3, k_l, v_l = _ctx_mix(y_p.reshape(batch, seq, d), m_l, win, wf, wna, wout, g_l, b_l, alpha)
        y_p = _ffn(y_p3.reshape(batch * seq, d), m_l, batch * seq, 0, up2, dn2, g_l, b_l, 2, alpha)
        k_list.append(k_l)
        v_list.append(v_l)

        y_s = _ffn(y_s, m_l, dec_seq, 1, up1, dn1, g_l, b_l, 0, alpha)
        q2, k2, v2, f2, gab2 = _in_proj(y_s, m_l, dec_seq, 1, win)
        to3 = lambda t: t.reshape(dec_batch, dec_seq, t.shape[-1])
        ck3 = cache_k[:, l].transpose(0, 2, 1, 3).reshape(dec_batch, -1, NA_WIDTH).astype(BF16)
        cv3 = cache_v[:, l].transpose(0, 2, 1, 3).reshape(dec_batch, -1, NA_WIDTH).astype(BF16)
        bias = _nbr_bias_table(rpb[l], rows_lat)
        o3 = _nbr_attn(to3(q2), to3(k2), to3(v2), ck3, cv3, bias)
        fm3 = _fourier(to3(f2))
        y_s = _merge(fm3.reshape(-1, FNET_WIDTH), o3.reshape(-1, NA_WIDTH), gab2, y_s, m_l,
                     dec_seq, 1, wf, wna, wout, g_l, b_l, alpha)
        y_s = _ffn(y_s, m_l, dec_seq, 1, up2, dn2, g_l, b_l, 2, alpha)

    new_k = jnp.stack(k_list, axis=1)
    new_v = jnp.stack(v_list, axis=1)
    return (y_p.reshape(batch, seq, d), y_s.reshape(dec_batch, dec_seq, d), new_k, new_v)
```

```python
import functools
import math

import numpy as np
import jax
import jax.numpy as jnp
from jax import lax
from jax.experimental import pallas as pl
from jax.experimental.pallas import tpu as pltpu

F32 = jnp.float32
BF16 = jnp.bfloat16

HEAD_DIM = 64
NA_HEADS = 8
NA_WIDTH = NA_HEADS * HEAD_DIM
FNET_GROUPS = 4
FNET_GROUP_DIM = 128
FNET_WIDTH = FNET_GROUPS * FNET_GROUP_DIM
GRID_W = 64
WIN_ROWS = 8
WIN_COLS = 16
N_SUB = 3
N_MOD = 3 * N_SUB
ATTN_SCALE = HEAD_DIM ** -0.5
LN_EPS = 1e-5
NEG_INF = -1e30

LANES = 128
MXU_DIM = 256
VMEM_LIMIT_BYTES = 56 * 1024 * 1024

ROW_TILE = 512
FF_CHUNK = MXU_DIM
Q_ROWS = 8
K_ROWS = 2 * Q_ROWS
MOD_ROWS = 8
HI = lax.Precision.HIGHEST


def _params(sem):
    return pltpu.CompilerParams(dimension_semantics=sem, vmem_limit_bytes=VMEM_LIMIT_BYTES)


def _const_spec(shape):
    nd = len(shape)
    return pl.BlockSpec(shape, lambda *_: (0,) * nd, pipeline_mode=pl.Buffered(1))


def _layer_norm(r, g, b):
    mu = jnp.mean(r, axis=-1, keepdims=True)
    d = r - mu
    var = jnp.mean(d * d, axis=-1, keepdims=True)
    return d * lax.rsqrt(var + LN_EPS) * g + b


def _bdot(a, b):
    return jnp.dot(a, b, preferred_element_type=F32)


def _dot_nt(a, b):
    return lax.dot_general(a, b, (((1,), (1,)), ((), ())), preferred_element_type=F32)


def _dft_mats(n):
    k = np.arange(n)
    ang = 2.0 * np.pi * ((k[:, None] * k[None, :]) % n) / n
    return np.cos(ang), np.sin(ang)


def _mods_kernel(c_ref, w_ref, b_ref, o_ref):
    c = c_ref[...]
    s = (c * jax.nn.sigmoid(c)).astype(BF16)
    o_ref[0] = _bdot(s, w_ref[0].astype(BF16)) + b_ref[0]


def _mods(cvec, w_ada, b_ada):
    depth, d, n = w_ada.shape
    tn = d
    return pl.pallas_call(
        _mods_kernel,
        grid=(depth, n // tn),
        in_specs=[
            pl.BlockSpec((MOD_ROWS, d), lambda l, j: (0, 0)),
            pl.BlockSpec((1, d, tn), lambda l, j: (l, 0, j)),
            pl.BlockSpec((1, 1, tn), lambda l, j: (l, 0, j)),
        ],
        out_specs=pl.BlockSpec((1, MOD_ROWS, tn), lambda l, j: (l, 0, j)),
        out_shape=jax.ShapeDtypeStruct((depth, MOD_ROWS, n), F32),
        compiler_params=_params(("arbitrary", "arbitrary")),
        name="mods",
    )(cvec, w_ada, b_ada.reshape(depth, 1, n))


def _ffn_kernel(x_ref, m_ref, wup_ref, wdn_ref, g_ref, b_ref, o_ref, acc_ref, *, sub, alpha):
    x = x_ref[...]
    shift = m_ref[0, 3 * sub:3 * sub + 1, :]
    scale = m_ref[0, 3 * sub + 1:3 * sub + 2, :]
    gate = m_ref[0, 3 * sub + 2:3 * sub + 3, :]
    u = (x * (1.0 + scale) + shift).astype(BF16)
    n_chunks = wup_ref.shape[0]
    half = wup_ref.shape[2] // 2

    def body(c, carry):
        h = _bdot(u, wup_ref[c])
        a = h[:, :half]
        g = h[:, half:]
        act = ((g * jax.nn.sigmoid(g)) * a).astype(BF16)
        y = _bdot(act, wdn_ref[c])

        @pl.when(c == 0)
        def _():
            acc_ref[...] = y

        @pl.when(c > 0)
        def _():
            acc_ref[...] += y

        return carry

    lax.fori_loop(0, n_chunks, body, 0)
    r = alpha * x + (0.5 * gate) * acc_ref[...]
    o_ref[...] = _layer_norm(r, g_ref[sub:sub + 1, :], b_ref[sub:sub + 1, :])


def _ffn(x2, mods_l, rows_per_group, group0, wup, wdn, ln_g, ln_b, sub, alpha):
    rows, d = x2.shape
    tm = min(ROW_TILE, rows_per_group)
    tiles_per_group = rows_per_group // tm
    return pl.pallas_call(
        functools.partial(_ffn_kernel, sub=sub, alpha=alpha),
        grid=(rows // tm,),
        in_specs=[
            pl.BlockSpec((tm, d), lambda i: (i, 0)),
            pl.BlockSpec((1, N_MOD, d), lambda i: (group0 + i // tiles_per_group, 0, 0)),
            _const_spec(wup.shape),
            _const_spec(wdn.shape),
            _const_spec(ln_g.shape),
            _const_spec(ln_b.shape),
        ],
        out_specs=pl.BlockSpec((tm, d), lambda i: (i, 0)),
        out_shape=jax.ShapeDtypeStruct((rows, d), F32),
        scratch_shapes=[pltpu.VMEM((tm, d), F32)],
        compiler_params=_params(("arbitrary",)),
        name=f"ffn{sub}",
    )(x2, mods_l, wup, wdn, ln_g, ln_b)


def _head_pair_attention(q, k_list, v_list, bias_list, n_pairs_lanes=LANES):
    lane = lax.broadcasted_iota(jnp.int32, (1, n_pairs_lanes), 1)
    outs = []
    for h in range(2):
        in_head = (lane // HEAD_DIM) == h
        qm = jnp.where(in_head, q, jnp.zeros_like(q))
        s = []
        for k, bias in zip(k_list, bias_list):
            sk = _dot_nt(qm, k)
            if bias is not None:
                sk = sk + bias[h]
            s.append(sk)
        m = s[0].max(axis=-1, keepdims=True)
        for sk in s[1:]:
            m = jnp.maximum(m, sk.max(axis=-1, keepdims=True))
        den = None
        o = None
        for sk, v in zip(s, v_list):
            p = jnp.exp(sk - m)
            ps = p.sum(axis=-1, keepdims=True)
            den = ps if den is None else den + ps
            pv = _bdot(p.astype(BF16), v)
            o = pv if o is None else o + pv
        outs.append(o / den)
    return jnp.where(lane < HEAD_DIM, outs[0], outs[1])


def _merge_out(fm, o, ga, gb, x, shift_gate, wf_ref, wna_ref, wout_ref, ln_g, ln_b, alpha):
    branch_a = _bdot(fm.astype(BF16), wf_ref[...])
    branch_b = _bdot(o, wna_ref[...])
    mg = jax.nn.sigmoid(ga) * branch_a + jax.nn.sigmoid(gb) * branch_b
    mix = _bdot(mg.astype(BF16), wout_ref[...])
    r = alpha * x + shift_gate * mix
    return _layer_norm(r, ln_g, ln_b)


def _ctx_mix_kernel(x_ref, m_ref, win_ref, wf_ref, wna_ref, wout_ref, g_ref, b_ref,
                    cdft_ref, pdft_ref, o_ref, k_ref, v_ref,
                    z_ref, oatt_ref, fm_ref, *, alpha):
    nb, seq, d = x_ref.shape
    rows = nb * seq
    x = x_ref[...].reshape(rows, d)
    shift = m_ref[0, 3:4, :]
    scale = m_ref[0, 4:5, :]
    gate = m_ref[0, 5:6, :]
    u = (x * (1.0 + scale) + shift).astype(BF16)
    n_chunks = win_ref.shape[0]
    cw = win_ref.shape[2]
    for c in range(n_chunks):
        z_ref[:, c * cw:(c + 1) * cw] = _bdot(u, win_ref[c])

    q_off, k_off, v_off, f_off = 0, NA_WIDTH, 2 * NA_WIDTH, 3 * NA_WIDTH
    ga_off = f_off + FNET_WIDTH
    gb_off = ga_off + d
    for b in range(nb):
        r0 = b * seq
        for h in range(NA_HEADS):
            k_ref[b, h] = z_ref[r0:r0 + seq, k_off + h * HEAD_DIM:k_off + (h + 1) * HEAD_DIM]
            v_ref[b, h] = z_ref[r0:r0 + seq, v_off + h * HEAD_DIM:v_off + (h + 1) * HEAD_DIM]
        for hp in range(NA_WIDTH // LANES):
            c0 = hp * LANES
            q = (z_ref[r0:r0 + seq, q_off + c0:q_off + c0 + LANES] * ATTN_SCALE).astype(BF16)
            k = z_ref[r0:r0 + seq, k_off + c0:k_off + c0 + LANES].astype(BF16)
            v = z_ref[r0:r0 + seq, v_off + c0:v_off + c0 + LANES].astype(BF16)
            o = _head_pair_attention(q, [k], [v], [None])
            oatt_ref[r0:r0 + seq, c0:c0 + LANES] = o.astype(BF16)
        for g in range(FNET_GROUPS):
            c0 = f_off + g * FNET_GROUP_DIM
            xg = z_ref[r0:r0 + seq, c0:c0 + FNET_GROUP_DIM]
            zc = jnp.dot(xg, cdft_ref[...], precision=HI, preferred_element_type=F32)
            stacked = jnp.concatenate([zc[:, :FNET_GROUP_DIM], zc[:, FNET_GROUP_DIM:]], axis=0)
            fm_ref[r0:r0 + seq, g * FNET_GROUP_DIM:(g + 1) * FNET_GROUP_DIM] = jnp.dot(
                pdft_ref[...], stacked, precision=HI, preferred_element_type=F32)

    out = _merge_out(fm_ref[...], oatt_ref[...], z_ref[:, ga_off:ga_off + d],
                     z_ref[:, gb_off:gb_off + d], x, gate, wf_ref, wna_ref, wout_ref,
                     g_ref[1:2, :], b_ref[1:2, :], alpha)
    o_ref[...] = out.reshape(nb, seq, d)


def _ctx_mix(x3, mods_l, win, wf, wna, wout, ln_g, ln_b, alpha):
    batch, seq, d = x3.shape
    nb = 2 if batch % 2 == 0 else 1
    in_width = win.shape[0] * win.shape[2]
    cc, cs = _dft_mats(FNET_GROUP_DIM)
    pc, ps = _dft_mats(seq)
    norm = 1.0 / math.sqrt(seq * FNET_GROUP_DIM)
    cdft = jnp.asarray(np.concatenate([cc, -cs], axis=1) * norm, F32)
    pdft = jnp.asarray(np.concatenate([pc, ps], axis=1), F32)
    kv_shape = jax.ShapeDtypeStruct((batch, NA_HEADS, seq, HEAD_DIM), F32)
    kv_spec = pl.BlockSpec((nb, NA_HEADS, seq, HEAD_DIM), lambda i: (i, 0, 0, 0))
    return pl.pallas_call(
        functools.partial(_ctx_mix_kernel, alpha=alpha),
        grid=(batch // nb,),
        in_specs=[
            pl.BlockSpec((nb, seq, d), lambda i: (i, 0, 0)),
            pl.BlockSpec((1, N_MOD, d), lambda i: (0, 0, 0)),
            _const_spec(win.shape),
            _const_spec(wf.shape),
            _const_spec(wna.shape),
            _const_spec(wout.shape),
            _const_spec(ln_g.shape),
            _const_spec(ln_b.shape),
            _const_spec(cdft.shape),
            _const_spec(pdft.shape),
        ],
        out_specs=[pl.BlockSpec((nb, seq, d), lambda i: (i, 0, 0)), kv_spec, kv_spec],
        out_shape=[jax.ShapeDtypeStruct((batch, seq, d), F32), kv_shape, kv_shape],
        scratch_shapes=[
            pltpu.VMEM((nb * seq, in_width), F32),
            pltpu.VMEM((nb * seq, NA_WIDTH), BF16),
            pltpu.VMEM((nb * seq, FNET_WIDTH), F32),
        ],
        compiler_params=_params(("arbitrary",)),
        name="ctx_mix",
    )(x3, mods_l, win, wf, wna, wout, ln_g, ln_b, cdft, pdft)


def _in_proj_kernel(x_ref, m_ref, win_ref, q_ref, k_ref, v_ref, f_ref, gab_ref):
    x = x_ref[...]
    shift = m_ref[0, 3:4, :]
    scale = m_ref[0, 4:5, :]
    u = (x * (1.0 + scale) + shift).astype(BF16)
    cw = win_ref.shape[2]
    q_ref[...] = (_bdot(u, win_ref[0]) * ATTN_SCALE).astype(BF16)
    k_ref[...] = _bdot(u, win_ref[1]).astype(BF16)
    v_ref[...] = _bdot(u, win_ref[2]).astype(BF16)
    f_ref[...] = _bdot(u, win_ref[3])
    for c in range(4, win_ref.shape[0]):
        gab_ref[:, (c - 4) * cw:(c - 3) * cw] = _bdot(u, win_ref[c])


def _in_proj(x2, mods_l, rows_per_group, group0, win):
    rows, d = x2.shape
    tm = min(ROW_TILE, rows_per_group)
    tiles_per_group = rows_per_group // tm
    cw = win.shape[2]
    row_spec = lambda w: pl.BlockSpec((tm, w), lambda i: (i, 0))
    return pl.pallas_call(
        _in_proj_kernel,
        grid=(rows // tm,),
        in_specs=[
            row_spec(d),
            pl.BlockSpec((1, N_MOD, d), lambda i: (group0 + i // tiles_per_group, 0, 0)),
            _const_spec(win.shape),
        ],
        out_specs=[row_spec(cw), row_spec(cw), row_spec(cw), row_spec(cw), row_spec(2 * d)],
        out_shape=[
            jax.ShapeDtypeStruct((rows, cw), BF16),
            jax.ShapeDtypeStruct((rows, cw), BF16),
            jax.ShapeDtypeStruct((rows, cw), BF16),
            jax.ShapeDtypeStruct((rows, cw), F32),
            jax.ShapeDtypeStruct((rows, 2 * d), F32),
        ],
        compiler_params=_params(("arbitrary",)),
        name="in_proj",
    )(x2, mods_l, win)


def _window_start_row(i, rows):
    return jnp.clip(Q_ROWS * i - Q_ROWS // 2, 0, rows - K_ROWS)


def _nbr_attn_kernel(q_ref, k_ref, v_ref, ck_ref, cv_ref, bias_ref, o_ref, *, rows):
    i = pl.program_id(1)
    b = pl.program_id(2)
    start = pl.multiple_of(_window_start_row(i, rows) * GRID_W, GRID_W * (Q_ROWS // 2))
    n_keys = K_ROWS * GRID_W
    kw = k_ref[b, pl.ds(start, n_keys), :]
    vw = v_ref[b, pl.ds(start, n_keys), :]
    o = _head_pair_attention(
        q_ref[0], [kw, ck_ref[b]], [vw, cv_ref[b]],
        [(bias_ref[0, 0], bias_ref[1, 0]), None])
    o_ref[0] = o.astype(BF16)


def _bias_rows_kernel(rpb_ref, onehot_ref, neg_ref, o_ref):
    o_ref[...] = jnp.dot(rpb_ref[...], onehot_ref[...], precision=HI,
                         preferred_element_type=F32) + neg_ref[...]


def _nbr_bias_table(rpb_l, rows):
    h, n_dr, n_dc = rpb_l.shape
    n_blocks = rows // Q_ROWS
    kr = min(WIN_ROWS, rows)
    variants = np.array([0, min(1, n_blocks - 1), n_blocks - 1])
    r = Q_ROWS * variants[:, None] + np.arange(Q_ROWS)[None, :]
    ks = np.clip(Q_ROWS * variants - Q_ROWS // 2, 0, rows - K_ROWS)
    key_row = ks[:, None] + np.arange(K_ROWS)[None, :]
    r0 = np.clip(r - kr // 2, 0, rows - kr)
    row_ok = (key_row[:, None, :] >= r0[:, :, None]) & (key_row[:, None, :] < r0[:, :, None] + kr)
    dr = np.clip(key_row[:, None, :] - r[:, :, None] + (WIN_ROWS - 1), 0, n_dr - 1)
    c_idx = np.arange(GRID_W)
    c0 = np.clip(c_idx - WIN_COLS // 2, 0, GRID_W - WIN_COLS)
    col_ok = (c_idx[None, :] >= c0[:, None]) & (c_idx[None, :] < c0[:, None] + WIN_COLS)
    dc = np.clip(c_idx[None, :] - c_idx[:, None] + (WIN_COLS - 1), 0, n_dc - 1)

    k_pad = -(-n_dc // LANES) * LANES
    m_pad = -(-(h * n_dr) // 8) * 8
    onehot = np.zeros((k_pad, GRID_W * GRID_W), np.float32)
    flat = np.arange(GRID_W * GRID_W).reshape(GRID_W, GRID_W)
    onehot[dc[col_ok], flat[col_ok]] = 1.0
    neg = np.where(col_ok, 0.0, NEG_INF).astype(np.float32).reshape(1, -1)
    rpb_pad = jnp.zeros((m_pad, k_pad), F32).at[:h * n_dr, :n_dc].set(
        rpb_l.reshape(h * n_dr, n_dc).astype(F32))
    col_tab = pl.pallas_call(
        _bias_rows_kernel,
        out_shape=jax.ShapeDtypeStruct((m_pad, GRID_W * GRID_W), F32),
        name="bias_rows",
    )(rpb_pad, jnp.asarray(onehot), jnp.asarray(neg))
    col_tab = col_tab[:h * n_dr].reshape(h, n_dr, GRID_W, GRID_W)
    blocks = jnp.take(col_tab, jnp.asarray(dr.reshape(-1)), axis=1)
    blocks = blocks.reshape(h, 3, Q_ROWS, K_ROWS, GRID_W, GRID_W)
    blocks = jnp.where(jnp.asarray(row_ok)[None, :, :, :, None, None], blocks, NEG_INF)
    return blocks.transpose(0, 1, 2, 4, 3, 5).reshape(h, 3, Q_ROWS * GRID_W, K_ROWS * GRID_W)


def _nbr_attn(q3, k3, v3, ck3, cv3, bias):
    nb, seq, width = q3.shape
    rows = seq // GRID_W
    n_blocks = rows // Q_ROWS
    n_pairs = width // LANES
    past = ck3.shape[1]
    tq = Q_ROWS * GRID_W
    tk = K_ROWS * GRID_W

    def variant(i):
        return jnp.where(i == 0, 0, jnp.where(i == n_blocks - 1, 2, 1))

    return pl.pallas_call(
        functools.partial(_nbr_attn_kernel, rows=rows),
        grid=(n_pairs, n_blocks, nb),
        in_specs=[
            pl.BlockSpec((1, tq, LANES), lambda p, i, b: (b, i, p)),
            pl.BlockSpec((nb, seq, LANES), lambda p, i, b: (0, 0, p)),
            pl.BlockSpec((nb, seq, LANES), lambda p, i, b: (0, 0, p)),
            pl.BlockSpec((nb, past, LANES), lambda p, i, b: (0, 0, p)),
            pl.BlockSpec((nb, past, LANES), lambda p, i, b: (0, 0, p)),
            pl.BlockSpec((2, 1, tq, tk), lambda p, i, b: (p, variant(i), 0, 0)),
        ],
        out_specs=pl.BlockSpec((1, tq, LANES), lambda p, i, b: (b, i, p)),
        out_shape=jax.ShapeDtypeStruct((nb, seq, width), BF16),
        compiler_params=_params(("arbitrary", "arbitrary", "arbitrary")),
        name="nbr_attn",
    )(q3, k3, v3, ck3, cv3, bias)


def _fourier_kernel(x_ref, cdft_ref, d1_ref, d2_ref, twc_ref, tws_ref, o_ref,
                    zr_ref, zi_ref, tr_ref, ti_ref, *, n1, n2):
    gd = x_ref.shape[2]
    zc = jnp.dot(x_ref[0], cdft_ref[...], precision=HI, preferred_element_type=F32)
    zr_ref[...] = zc[:, :gd]
    zi_ref[...] = zc[:, gd:]

    def stage1(c, carry):
        zs = jnp.concatenate([zr_ref[pl.ds(c, n1, stride=n2), :],
                              zi_ref[pl.ds(c, n1, stride=n2), :]], axis=0)
        t = jnp.dot(d1_ref[...], zs, precision=HI, preferred_element_type=F32)
        tr = t[:n1]
        ti = t[n1:]
        cs = twc_ref[c]
        sn = tws_ref[c]
        row = pl.multiple_of(c * n1, n1)
        tr_ref[pl.ds(row, n1), :] = tr * cs + ti * sn
        ti_ref[pl.ds(row, n1), :] = ti * cs - tr * sn
        return carry

    lax.fori_loop(0, n2, stage1, 0)

    def stage2(k1, carry):
        ts = jnp.concatenate([tr_ref[pl.ds(k1, n2, stride=n1), :],
                              ti_ref[pl.ds(k1, n2, stride=n1), :]], axis=0)
        y = jnp.dot(d2_ref[...], ts, precision=HI, preferred_element_type=F32)
        o_ref[0, pl.ds(k1, n2, stride=n1), :] = y
        return carry

    lax.fori_loop(0, n1, stage2, 0)


def _fourier(f3):
    nb, seq, width = f3.shape
    n2 = GRID_W
    n1 = seq // n2
    gd = FNET_GROUP_DIM
    cc, cs = _dft_mats(gd)
    c1, s1 = _dft_mats(n1)
    c2, s2 = _dft_mats(n2)
    norm = 1.0 / math.sqrt(seq * gd)
    cdft = jnp.asarray(np.concatenate([cc, -cs], axis=1) * norm, F32)
    d1 = jnp.asarray(np.block([[c1, s1], [-s1, c1]]), F32)
    d2 = jnp.asarray(np.concatenate([c2, s2], axis=1), F32)
    ang = 2.0 * np.pi * ((np.arange(n2)[:, None] * np.arange(n1)[None, :]) % seq) / seq
    twc = jnp.asarray(np.broadcast_to(np.cos(ang)[:, :, None], (n2, n1, gd)), F32)
    tws = jnp.asarray(np.broadcast_to(np.sin(ang)[:, :, None], (n2, n1, gd)), F32)
    return pl.pallas_call(
        functools.partial(_fourier_kernel, n1=n1, n2=n2),
        grid=(nb, width // gd),
        in_specs=[
            pl.BlockSpec((1, seq, gd), lambda b, g: (b, 0, g)),
            _const_spec(cdft.shape),
            _const_spec(d1.shape),
            _const_spec(d2.shape),
            _const_spec(twc.shape),
            _const_spec(tws.shape),
        ],
        out_specs=pl.BlockSpec((1, seq, gd), lambda b, g: (b, 0, g)),
        out_shape=jax.ShapeDtypeStruct((nb, seq, width), F32),
        scratch_shapes=[pltpu.VMEM((seq, gd), F32)] * 4,
        compiler_params=_params(("arbitrary", "arbitrary")),
        name="fourier",
    )(f3, cdft, d1, d2, twc, tws)


def _merge_kernel(fm_ref, o_ref_in, gab_ref, x_ref, m_ref, wf_ref, wna_ref, wout_ref,
                  g_ref, b_ref, out_ref, *, alpha):
    d = x_ref.shape[1]
    gate = m_ref[0, 5:6, :]
    out_ref[...] = _merge_out(fm_ref[...], o_ref_in[...], gab_ref[:, :d], gab_ref[:, d:],
                              x_ref[...], gate, wf_ref, wna_ref, wout_ref,
                              g_ref[1:2, :], b_ref[1:2, :], alpha)


def _merge(fm2, o2, gab2, x2, mods_l, rows_per_group, group0, wf, wna, wout, ln_g, ln_b, alpha):
    rows, d = x2.shape
    tm = min(ROW_TILE, rows_per_group)
    tiles_per_group = rows_per_group // tm
    row_spec = lambda w: pl.BlockSpec((tm, w), lambda i: (i, 0))
    return pl.pallas_call(
        functools.partial(_merge_kernel, alpha=alpha),
        grid=(rows // tm,),
        in_specs=[
            row_spec(fm2.shape[1]), row_spec(o2.shape[1]), row_spec(gab2.shape[1]), row_spec(d),
            pl.BlockSpec((1, N_MOD, d), lambda i: (group0 + i // tiles_per_group, 0, 0)),
            _const_spec(wf.shape), _const_spec(wna.shape), _const_spec(wout.shape),
            _const_spec(ln_g.shape), _const_spec(ln_b.shape),
        ],
        out_specs=row_spec(d),
        out_shape=jax.ShapeDtypeStruct((rows, d), F32),
        compiler_params=_params(("arbitrary",)),
        name="merge",
    )(fm2, o2, gab2, x2, mods_l, wf, wna, wout, ln_g, ln_b)


def _prep_ff(w_up, w_down):
    d, two_ff = w_up.shape
    ff = two_ff // 2
    nc = ff // FF_CHUNK
    a = w_up[:, :ff].reshape(d, nc, FF_CHUNK)
    g = w_up[:, ff:].reshape(d, nc, FF_CHUNK)
    up = jnp.concatenate([a, g], axis=2).transpose(1, 0, 2).astype(BF16)
    dn = w_down.reshape(nc, FF_CHUNK, d).astype(BF16)
    return up, dn


def _prep_in(w_in):
    d, n = w_in.shape
    cw = NA_WIDTH
    return w_in.reshape(d, n // cw, cw).transpose(1, 0, 2).astype(BF16)


def kernel(x_prompt, x_sample, cache_k, cache_v, c, c_ctx, w_ada, b_ada, ln_g, ln_b, w_ff1_up,
           w_ff1_down, w_in, rpb, w_fourier, w_na_out, w_out, w_ff2_up, w_ff2_down):
    batch, seq, d = x_prompt.shape
    dec_batch, dec_seq, _ = x_sample.shape
    depth = w_ada.shape[0]
    alpha = (2 * depth) ** 0.25
    rows_lat = dec_seq // GRID_W

    cvec = jnp.zeros((MOD_ROWS, d), F32).at[0].set(c_ctx).at[1:1 + dec_batch].set(c)
    mods = _mods(cvec, w_ada, b_ada).reshape(depth, MOD_ROWS, N_MOD, d)

    y_p = x_prompt.reshape(batch * seq, d)
    y_s = x_sample.reshape(dec_batch * dec_seq, d)
    k_list, v_list = [], []
    for l in range(depth):
        m_l = mods[l]
        up1, dn1 = _prep_ff(w_ff1_up[l], w_ff1_down[l])
        up2, dn2 = _prep_ff(w_ff2_up[l], w_ff2_down[l])
        win = _prep_in(w_in[l])
        wf = w_fourier[l].astype(BF16)
        wna = w_na_out[l].astype(BF16)
        wout = w_out[l].astype(BF16)
        g_l, b_l = ln_g[l], ln_b[l]

        y_p = _ffn(y_p, m_l, batch * seq, 0, up1, dn1, g_l, b_l, 0, alpha)
        y_p3, k_l, v_l = _ctx_mix(y_p.reshape(batch, seq, d), m_l, win, wf, wna, wout, g_l, b_l, alpha)
        y_p = _ffn(y_p3.reshape(batch * seq, d), m_l, batch * seq, 0, up2, dn2, g_l, b_l, 2, alpha)
        k_list.append(k_l)
        v_list.append(v_l)

        y_s = _ffn(y_s, m_l, dec_seq, 1, up1, dn1, g_l, b_l, 0, alpha)
        q2, k2, v2, f2, gab2 = _in_proj(y_s, m_l, dec_seq, 1, win)
        to3 = lambda t: t.reshape(dec_batch, dec_seq, t.shape[-1])
        ck3 = cache_k[:, l].transpose(0, 2, 1, 3).reshape(dec_batch, -1, NA_WIDTH).astype(BF16)
        cv3 = cache_v[:, l].transpose(0, 2, 1, 3).reshape(dec_batch, -1, NA_WIDTH).astype(BF16)
        bias = _nbr_bias_table(rpb[l], rows_lat)
        o3 = _nbr_attn(to3(q2), to3(k2), to3(v2), ck3, cv3, bias)
        fm3 = _fourier(to3(f2))
        y_s = _merge(fm3.reshape(-1, FNET_WIDTH), o3.reshape(-1, NA_WIDTH), gab2, y_s, m_l,
                     dec_seq, 1, wf, wna, wout, g_l, b_l, alpha)
        y_s = _ffn(y_s, m_l, dec_seq, 1, up2, dn2, g_l, b_l, 2, alpha)

    new_k = jnp.stack(k_list, axis=1)
    new_v = jnp.stack(v_list, axis=1)
    return (y_p.reshape(batch, seq, d), y_s.reshape(dec_batch, dec_seq, d), new_k, new_v)
```

```python
import functools
import math

import numpy as np
import jax
import jax.numpy as jnp
from jax import lax
from jax.experimental import pallas as pl
from jax.experimental.pallas import tpu as pltpu

F32 = jnp.float32
BF16 = jnp.bfloat16

HEAD_DIM = 64
NA_HEADS = 8
NA_WIDTH = NA_HEADS * HEAD_DIM
FNET_GROUPS = 4
FNET_GROUP_DIM = 128
FNET_WIDTH = FNET_GROUPS * FNET_GROUP_DIM
GRID_W = 64
WIN_ROWS = 8
WIN_COLS = 16
N_SUB = 3
N_MOD = 3 * N_SUB
ATTN_SCALE = HEAD_DIM ** -0.5
LN_EPS = 1e-5
NEG_INF = -1e30

LANES = 128
MXU_DIM = 256
VMEM_LIMIT_BYTES = 56 * 1024 * 1024

ROW_TILE = 512
FF_CHUNK = MXU_DIM
Q_ROWS = 8
K_ROWS = 2 * Q_ROWS
MOD_ROWS = 8
FOURIER_COLS = 2 * FNET_GROUP_DIM
LOOP_UNROLL = 4
HI = lax.Precision.HIGHEST


def _params(sem):
    return pltpu.CompilerParams(dimension_semantics=sem, vmem_limit_bytes=VMEM_LIMIT_BYTES)


def _const_spec(shape):
    nd = len(shape)
    return pl.BlockSpec(shape, lambda *_: (0,) * nd, pipeline_mode=pl.Buffered(1))


def _layer_norm(r, g, b):
    mu = jnp.mean(r, axis=-1, keepdims=True)
    d = r - mu
    var = jnp.mean(d * d, axis=-1, keepdims=True)
    return d * lax.rsqrt(var + LN_EPS) * g + b


def _bdot(a, b):
    return jnp.dot(a, b, preferred_element_type=F32)


def _dot_nt(a, b):
    return lax.dot_general(a, b, (((1,), (1,)), ((), ())), preferred_element_type=F32)


def _dft_mats(n):
    k = np.arange(n)
    ang = 2.0 * np.pi * ((k[:, None] * k[None, :]) % n) / n
    return np.cos(ang), np.sin(ang)


def _mods_kernel(c_ref, w_ref, b_ref, o_ref):
    c = c_ref[...]
    s = (c * jax.nn.sigmoid(c)).astype(BF16)
    o_ref[0] = _bdot(s, w_ref[0].astype(BF16)) + b_ref[0]


def _mods(cvec, w_ada, b_ada):
    depth, d, n = w_ada.shape
    tn = d
    return pl.pallas_call(
        _mods_kernel,
        grid=(depth, n // tn),
        in_specs=[
            pl.BlockSpec((MOD_ROWS, d), lambda l, j: (0, 0)),
            pl.BlockSpec((1, d, tn), lambda l, j: (l, 0, j)),
            pl.BlockSpec((1, 1, tn), lambda l, j: (l, 0, j)),
        ],
        out_specs=pl.BlockSpec((1, MOD_ROWS, tn), lambda l, j: (l, 0, j)),
        out_shape=jax.ShapeDtypeStruct((depth, MOD_ROWS, n), F32),
        compiler_params=_params(("arbitrary", "arbitrary")),
        name="mods",
    )(cvec, w_ada, b_ada.reshape(depth, 1, n))


def _ffn_kernel(x_ref, m_ref, wup_ref, wdn_ref, g_ref, b_ref, o_ref, act_ref, *, sub, alpha):
    x = x_ref[...]
    shift = m_ref[0, 3 * sub:3 * sub + 1, :]
    scale = m_ref[0, 3 * sub + 1:3 * sub + 2, :]
    gate = m_ref[0, 3 * sub + 2:3 * sub + 3, :]
    u = (x * (1.0 + scale) + shift).astype(BF16)
    n_chunks = wup_ref.shape[0]
    half = wup_ref.shape[2] // 2
    for c in range(n_chunks):
        h = _bdot(u, wup_ref[c])
        a = h[:, :half]
        g = h[:, half:]
        act_ref[:, c * half:(c + 1) * half] = ((g * jax.nn.sigmoid(g)) * a).astype(BF16)
    y = _bdot(act_ref[...], wdn_ref[...])
    r = alpha * x + (0.5 * gate) * y
    o_ref[...] = _layer_norm(r, g_ref[sub:sub + 1, :], b_ref[sub:sub + 1, :])


def _ffn(x2, mods_l, rows_per_group, group0, wup, wdn, ln_g, ln_b, sub, alpha):
    rows, d = x2.shape
    tm = min(ROW_TILE, rows_per_group)
    tiles_per_group = rows_per_group // tm
    return pl.pallas_call(
        functools.partial(_ffn_kernel, sub=sub, alpha=alpha),
        grid=(rows // tm,),
        in_specs=[
            pl.BlockSpec((tm, d), lambda i: (i, 0)),
            pl.BlockSpec((1, N_MOD, d), lambda i: (group0 + i // tiles_per_group, 0, 0)),
            _const_spec(wup.shape),
            _const_spec(wdn.shape),
            _const_spec(ln_g.shape),
            _const_spec(ln_b.shape),
        ],
        out_specs=pl.BlockSpec((tm, d), lambda i: (i, 0)),
        out_shape=jax.ShapeDtypeStruct((rows, d), F32),
        scratch_shapes=[pltpu.VMEM((tm, wdn.shape[0]), BF16)],
        compiler_params=_params(("arbitrary",)),
        name=f"ffn{sub}",
    )(x2, mods_l, wup, wdn, ln_g, ln_b)


def _head_pair_attention(q, k_list, v_list, bias_list, n_pairs_lanes=LANES):
    lane = lax.broadcasted_iota(jnp.int32, (1, n_pairs_lanes), 1)
    outs = []
    for h in range(2):
        in_head = (lane // HEAD_DIM) == h
        qm = jnp.where(in_head, q, jnp.zeros_like(q))
        s = []
        for k, bias in zip(k_list, bias_list):
            sk = _dot_nt(qm, k)
            if bias is not None:
                sk = sk + bias[h]
            s.append(sk)
        m = s[0].max(axis=-1, keepdims=True)
        for sk in s[1:]:
            m = jnp.maximum(m, sk.max(axis=-1, keepdims=True))
        den = None
        o = None
        for sk, v in zip(s, v_list):
            p = jnp.exp(sk - m)
            ps = p.sum(axis=-1, keepdims=True)
            den = ps if den is None else den + ps
            pv = _bdot(p.astype(BF16), v)
            o = pv if o is None else o + pv
        outs.append(o / den)
    return jnp.where(lane < HEAD_DIM, outs[0], outs[1])


def _merge_out(fm, o, ga, gb, x, shift_gate, wf_ref, wna_ref, wout_ref, ln_g, ln_b, alpha):
    branch_a = _bdot(fm.astype(BF16), wf_ref[...])
    branch_b = _bdot(o, wna_ref[...])
    mg = jax.nn.sigmoid(ga) * branch_a + jax.nn.sigmoid(gb) * branch_b
    mix = _bdot(mg.astype(BF16), wout_ref[...])
    r = alpha * x + shift_gate * mix
    return _layer_norm(r, ln_g, ln_b)


def _ctx_mix_kernel(x_ref, m_ref, win_ref, wf_ref, wna_ref, wout_ref, g_ref, b_ref,
                    cdft_ref, pdft_ref, o_ref, k_ref, v_ref,
                    z_ref, oatt_ref, fm_ref, *, alpha):
    nb, seq, d = x_ref.shape
    rows = nb * seq
    x = x_ref[...].reshape(rows, d)
    shift = m_ref[0, 3:4, :]
    scale = m_ref[0, 4:5, :]
    gate = m_ref[0, 5:6, :]
    u = (x * (1.0 + scale) + shift).astype(BF16)
    n_chunks = win_ref.shape[0]
    cw = win_ref.shape[2]
    for c in range(n_chunks):
        z_ref[:, c * cw:(c + 1) * cw] = _bdot(u, win_ref[c])

    q_off, k_off, v_off, f_off = 0, NA_WIDTH, 2 * NA_WIDTH, 3 * NA_WIDTH
    ga_off = f_off + FNET_WIDTH
    gb_off = ga_off + d
    for b in range(nb):
        r0 = b * seq
        for h in range(NA_HEADS):
            k_ref[b, h] = z_ref[r0:r0 + seq, k_off + h * HEAD_DIM:k_off + (h + 1) * HEAD_DIM]
            v_ref[b, h] = z_ref[r0:r0 + seq, v_off + h * HEAD_DIM:v_off + (h + 1) * HEAD_DIM]
        for hp in range(NA_WIDTH // LANES):
            c0 = hp * LANES
            q = (z_ref[r0:r0 + seq, q_off + c0:q_off + c0 + LANES] * ATTN_SCALE).astype(BF16)
            k = z_ref[r0:r0 + seq, k_off + c0:k_off + c0 + LANES].astype(BF16)
            v = z_ref[r0:r0 + seq, v_off + c0:v_off + c0 + LANES].astype(BF16)
            o = _head_pair_attention(q, [k], [v], [None])
            oatt_ref[r0:r0 + seq, c0:c0 + LANES] = o.astype(BF16)
        for g in range(FNET_GROUPS):
            c0 = f_off + g * FNET_GROUP_DIM
            xg = z_ref[r0:r0 + seq, c0:c0 + FNET_GROUP_DIM]
            zc = jnp.dot(xg, cdft_ref[...], precision=HI, preferred_element_type=F32)
            stacked = jnp.concatenate([zc[:, :FNET_GROUP_DIM], zc[:, FNET_GROUP_DIM:]], axis=0)
            fm_ref[r0:r0 + seq, g * FNET_GROUP_DIM:(g + 1) * FNET_GROUP_DIM] = jnp.dot(
                pdft_ref[...], stacked, precision=HI, preferred_element_type=F32)

    out = _merge_out(fm_ref[...], oatt_ref[...], z_ref[:, ga_off:ga_off + d],
                     z_ref[:, gb_off:gb_off + d], x, gate, wf_ref, wna_ref, wout_ref,
                     g_ref[1:2, :], b_ref[1:2, :], alpha)
    o_ref[...] = out.reshape(nb, seq, d)


def _ctx_mix(x3, mods_l, win, wf, wna, wout, ln_g, ln_b, alpha):
    batch, seq, d = x3.shape
    nb = 2 if batch % 2 == 0 else 1
    in_width = win.shape[0] * win.shape[2]
    cc, cs = _dft_mats(FNET_GROUP_DIM)
    pc, ps = _dft_mats(seq)
    norm = 1.0 / math.sqrt(seq * FNET_GROUP_DIM)
    cdft = jnp.asarray(np.concatenate([cc, -cs], axis=1) * norm, F32)
    pdft = jnp.asarray(np.concatenate([pc, ps], axis=1), F32)
    kv_shape = jax.ShapeDtypeStruct((batch, NA_HEADS, seq, HEAD_DIM), F32)
    kv_spec = pl.BlockSpec((nb, NA_HEADS, seq, HEAD_DIM), lambda i: (i, 0, 0, 0))
    return pl.pallas_call(
        functools.partial(_ctx_mix_kernel, alpha=alpha),
        grid=(batch // nb,),
        in_specs=[
            pl.BlockSpec((nb, seq, d), lambda i: (i, 0, 0)),
            pl.BlockSpec((1, N_MOD, d), lambda i: (0, 0, 0)),
            _const_spec(win.shape),
            _const_spec(wf.shape),
            _const_spec(wna.shape),
            _const_spec(wout.shape),
            _const_spec(ln_g.shape),
            _const_spec(ln_b.shape),
            _const_spec(cdft.shape),
            _const_spec(pdft.shape),
        ],
        out_specs=[pl.BlockSpec((nb, seq, d), lambda i: (i, 0, 0)), kv_spec, kv_spec],
        out_shape=[jax.ShapeDtypeStruct((batch, seq, d), F32), kv_shape, kv_shape],
        scratch_shapes=[
            pltpu.VMEM((nb * seq, in_width), F32),
            pltpu.VMEM((nb * seq, NA_WIDTH), BF16),
            pltpu.VMEM((nb * seq, FNET_WIDTH), F32),
        ],
        compiler_params=_params(("arbitrary",)),
        name="ctx_mix",
    )(x3, mods_l, win, wf, wna, wout, ln_g, ln_b, cdft, pdft)


def _in_proj_kernel(x_ref, m_ref, win_ref, q_ref, k_ref, v_ref, f_ref, gab_ref):
    x = x_ref[...]
    shift = m_ref[0, 3:4, :]
    scale = m_ref[0, 4:5, :]
    u = (x * (1.0 + scale) + shift).astype(BF16)
    cw = win_ref.shape[2]
    q_ref[...] = (_bdot(u, win_ref[0]) * ATTN_SCALE).astype(BF16)
    k_ref[...] = _bdot(u, win_ref[1]).astype(BF16)
    v_ref[...] = _bdot(u, win_ref[2]).astype(BF16)
    f_ref[...] = _bdot(u, win_ref[3])
    for c in range(4, win_ref.shape[0]):
        gab_ref[:, (c - 4) * cw:(c - 3) * cw] = _bdot(u, win_ref[c])


def _in_proj(x2, mods_l, rows_per_group, group0, win):
    rows, d = x2.shape
    tm = min(ROW_TILE, rows_per_group)
    tiles_per_group = rows_per_group // tm
    cw = win.shape[2]
    row_spec = lambda w: pl.BlockSpec((tm, w), lambda i: (i, 0))
    return pl.pallas_call(
        _in_proj_kernel,
        grid=(rows // tm,),
        in_specs=[
            row_spec(d),
            pl.BlockSpec((1, N_MOD, d), lambda i: (group0 + i // tiles_per_group, 0, 0)),
            _const_spec(win.shape),
        ],
        out_specs=[row_spec(cw), row_spec(cw), row_spec(cw), row_spec(cw), row_spec(2 * d)],
        out_shape=[
            jax.ShapeDtypeStruct((rows, cw), BF16),
            jax.ShapeDtypeStruct((rows, cw), BF16),
            jax.ShapeDtypeStruct((rows, cw), BF16),
            jax.ShapeDtypeStruct((rows, cw), F32),
            jax.ShapeDtypeStruct((rows, 2 * d), F32),
        ],
        compiler_params=_params(("arbitrary",)),
        name="in_proj",
    )(x2, mods_l, win)


def _window_start_row(i, rows):
    return jnp.clip(Q_ROWS * i - Q_ROWS // 2, 0, rows - K_ROWS)


def _nbr_attn_kernel(q_ref, k_ref, v_ref, ck_ref, cv_ref, bias_ref, o_ref, *, rows):
    i = pl.program_id(1)
    b = pl.program_id(2)
    start = pl.multiple_of(_window_start_row(i, rows) * GRID_W, GRID_W * (Q_ROWS // 2))
    n_keys = K_ROWS * GRID_W
    kw = k_ref[b, pl.ds(start, n_keys), :]
    vw = v_ref[b, pl.ds(start, n_keys), :]
    o = _head_pair_attention(
        q_ref[0], [kw, ck_ref[b]], [vw, cv_ref[b]],
        [(bias_ref[0, 0], bias_ref[1, 0]), None])
    o_ref[0] = o.astype(BF16)


def _bias_rows_kernel(rpb_ref, onehot_ref, neg_ref, o_ref):
    o_ref[...] = jnp.dot(rpb_ref[...], onehot_ref[...], precision=HI,
                         preferred_element_type=F32) + neg_ref[...]


def _nbr_bias_table(rpb_l, rows):
    h, n_dr, n_dc = rpb_l.shape
    n_blocks = rows // Q_ROWS
    kr = min(WIN_ROWS, rows)
    variants = np.array([0, min(1, n_blocks - 1), n_blocks - 1])
    r = Q_ROWS * variants[:, None] + np.arange(Q_ROWS)[None, :]
    ks = np.clip(Q_ROWS * variants - Q_ROWS // 2, 0, rows - K_ROWS)
    key_row = ks[:, None] + np.arange(K_ROWS)[None, :]
    r0 = np.clip(r - kr // 2, 0, rows - kr)
    row_ok = (key_row[:, None, :] >= r0[:, :, None]) & (key_row[:, None, :] < r0[:, :, None] + kr)
    dr = np.clip(key_row[:, None, :] - r[:, :, None] + (WIN_ROWS - 1), 0, n_dr - 1)
    c_idx = np.arange(GRID_W)
    c0 = np.clip(c_idx - WIN_COLS // 2, 0, GRID_W - WIN_COLS)
    col_ok = (c_idx[None, :] >= c0[:, None]) & (c_idx[None, :] < c0[:, None] + WIN_COLS)
    dc = np.clip(c_idx[None, :] - c_idx[:, None] + (WIN_COLS - 1), 0, n_dc - 1)

    k_pad = -(-n_dc // LANES) * LANES
    m_pad = -(-(h * n_dr) // 8) * 8
    onehot = np.zeros((k_pad, GRID_W * GRID_W), np.float32)
    flat = np.arange(GRID_W * GRID_W).reshape(GRID_W, GRID_W)
    onehot[dc[col_ok], flat[col_ok]] = 1.0
    neg = np.where(col_ok, 0.0, NEG_INF).astype(np.float32).reshape(1, -1)
    rpb_pad = jnp.zeros((m_pad, k_pad), F32).at[:h * n_dr, :n_dc].set(
        rpb_l.reshape(h * n_dr, n_dc).astype(F32))
    col_tab = pl.pallas_call(
        _bias_rows_kernel,
        out_shape=jax.ShapeDtypeStruct((m_pad, GRID_W * GRID_W), F32),
        name="bias_rows",
    )(rpb_pad, jnp.asarray(onehot), jnp.asarray(neg))
    col_tab = col_tab[:h * n_dr].reshape(h, n_dr, GRID_W, GRID_W).transpose(0, 2, 1, 3)
    pad = K_ROWS + Q_ROWS
    col_tab = jnp.pad(col_tab, ((0, 0), (0, 0), (pad, pad), (0, 0)))
    slabs = []
    for v in range(3):
        for qr in range(Q_ROWS):
            d0 = int(ks[v] - r[v, qr]) + (WIN_ROWS - 1) + pad
            slab = col_tab[:, :, d0:d0 + K_ROWS, :]
            slab = jnp.where(jnp.asarray(row_ok[v, qr])[None, None, :, None], slab, NEG_INF)
            slabs.append(slab.reshape(h, GRID_W, K_ROWS * GRID_W))
    return jnp.stack(slabs, axis=1).reshape(h, 3, Q_ROWS * GRID_W, K_ROWS * GRID_W)


def _nbr_attn(q3, k3, v3, ck3, cv3, bias):
    nb, seq, width = q3.shape
    rows = seq // GRID_W
    n_blocks = rows // Q_ROWS
    n_pairs = width // LANES
    past = ck3.shape[1]
    tq = Q_ROWS * GRID_W
    tk = K_ROWS * GRID_W

    def variant(i):
        return jnp.where(i == 0, 0, jnp.where(i == n_blocks - 1, 2, 1))

    return pl.pallas_call(
        functools.partial(_nbr_attn_kernel, rows=rows),
        grid=(n_pairs, n_blocks, nb),
        in_specs=[
            pl.BlockSpec((1, tq, LANES), lambda p, i, b: (b, i, p)),
            pl.BlockSpec((nb, seq, LANES), lambda p, i, b: (0, 0, p)),
            pl.BlockSpec((nb, seq, LANES), lambda p, i, b: (0, 0, p)),
            pl.BlockSpec((nb, past, LANES), lambda p, i, b: (0, 0, p)),
            pl.BlockSpec((nb, past, LANES), lambda p, i, b: (0, 0, p)),
            pl.BlockSpec((2, 1, tq, tk), lambda p, i, b: (p, variant(i), 0, 0)),
        ],
        out_specs=pl.BlockSpec((1, tq, LANES), lambda p, i, b: (b, i, p)),
        out_shape=jax.ShapeDtypeStruct((nb, seq, width), BF16),
        compiler_params=_params(("arbitrary", "arbitrary", "arbitrary")),
        name="nbr_attn",
    )(q3, k3, v3, ck3, cv3, bias)


def _fourier_kernel(x_ref, cdft_ref, d1_ref, d2_ref, twc_ref, tws_ref, o_ref,
                    zr_ref, zi_ref, tr_ref, ti_ref, *, n1, n2):
    gd = cdft_ref.shape[0]
    n_groups = x_ref.shape[2] // gd
    groups = range(n_groups)
    for g in groups:
        zc = jnp.dot(x_ref[0, :, g * gd:(g + 1) * gd], cdft_ref[...], precision=HI,
                     preferred_element_type=F32)
        zr_ref[g] = zc[:, :gd]
        zi_ref[g] = zc[:, gd:]

    def stacked(re_ref, im_ref, rows):
        return jnp.concatenate(
            [jnp.concatenate([re_ref[g, rows, :], im_ref[g, rows, :]], axis=0) for g in groups],
            axis=1)

    def stage1(c, carry):
        t = jnp.dot(d1_ref[...], stacked(zr_ref, zi_ref, pl.ds(c, n1, stride=n2)),
                    precision=HI, preferred_element_type=F32)
        cs = twc_ref[c]
        sn = tws_ref[c]
        row = pl.multiple_of(c * n1, n1)
        for g in groups:
            tr = t[:n1, g * gd:(g + 1) * gd]
            ti = t[n1:, g * gd:(g + 1) * gd]
            tr_ref[g, pl.ds(row, n1), :] = tr * cs + ti * sn
            ti_ref[g, pl.ds(row, n1), :] = ti * cs - tr * sn
        return carry

    lax.fori_loop(0, n2, stage1, 0, unroll=LOOP_UNROLL)

    def stage2(k1, carry):
        rows = pl.ds(k1, n2, stride=n1)
        y = jnp.dot(d2_ref[...], stacked(tr_ref, ti_ref, rows), precision=HI,
                    preferred_element_type=F32)
        for g in groups:
            zr_ref[g, rows, :] = y[:, g * gd:(g + 1) * gd]
        return carry

    lax.fori_loop(0, n1, stage2, 0, unroll=LOOP_UNROLL)
    for g in groups:
        o_ref[0, :, g * gd:(g + 1) * gd] = zr_ref[g]


def _fourier(f3):
    nb, seq, width = f3.shape
    n2 = GRID_W
    n1 = seq // n2
    gd = FNET_GROUP_DIM
    cc, cs = _dft_mats(gd)
    c1, s1 = _dft_mats(n1)
    c2, s2 = _dft_mats(n2)
    norm = 1.0 / math.sqrt(seq * gd)
    cdft = jnp.asarray(np.concatenate([cc, -cs], axis=1) * norm, F32)
    d1 = jnp.asarray(np.block([[c1, s1], [-s1, c1]]), F32)
    d2 = jnp.asarray(np.concatenate([c2, s2], axis=1), F32)
    ang = 2.0 * np.pi * ((np.arange(n2)[:, None] * np.arange(n1)[None, :]) % seq) / seq
    twc = jnp.asarray(np.broadcast_to(np.cos(ang)[:, :, None], (n2, n1, gd)), F32)
    tws = jnp.asarray(np.broadcast_to(np.sin(ang)[:, :, None], (n2, n1, gd)), F32)
    return pl.pallas_call(
        functools.partial(_fourier_kernel, n1=n1, n2=n2),
        grid=(nb, width // FOURIER_COLS),
        in_specs=[
            pl.BlockSpec((1, seq, FOURIER_COLS), lambda b, g: (b, 0, g)),
            _const_spec(cdft.shape),
            _const_spec(d1.shape),
            _const_spec(d2.shape),
            _const_spec(twc.shape),
            _const_spec(tws.shape),
        ],
        out_specs=pl.BlockSpec((1, seq, FOURIER_COLS), lambda b, g: (b, 0, g)),
        out_shape=jax.ShapeDtypeStruct((nb, seq, width), F32),
        scratch_shapes=[pltpu.VMEM((FOURIER_COLS // gd, seq, gd), F32)] * 4,
        compiler_params=_params(("arbitrary", "arbitrary")),
        name="fourier",
    )(f3, cdft, d1, d2, twc, tws)


def _merge_kernel(fm_ref, o_ref_in, gab_ref, x_ref, m_ref, wf_ref, wna_ref, wout_ref,
                  g_ref, b_ref, out_ref, *, alpha):
    d = x_ref.shape[1]
    gate = m_ref[0, 5:6, :]
    out_ref[...] = _merge_out(fm_ref[...], o_ref_in[...], gab_ref[:, :d], gab_ref[:, d:],
                              x_ref[...], gate, wf_ref, wna_ref, wout_ref,
                              g_ref[1:2, :], b_ref[1:2, :], alpha)


def _merge(fm2, o2, gab2, x2, mods_l, rows_per_group, group0, wf, wna, wout, ln_g, ln_b, alpha):
    rows, d = x2.shape
    tm = min(ROW_TILE, rows_per_group)
    tiles_per_group = rows_per_group // tm
    row_spec = lambda w: pl.BlockSpec((tm, w), lambda i: (i, 0))
    return pl.pallas_call(
        functools.partial(_merge_kernel, alpha=alpha),
        grid=(rows // tm,),
        in_specs=[
            row_spec(fm2.shape[1]), row_spec(o2.shape[1]), row_spec(gab2.shape[1]), row_spec(d),
            pl.BlockSpec((1, N_MOD, d), lambda i: (group0 + i // tiles_per_group, 0, 0)),
            _const_spec(wf.shape), _const_spec(wna.shape), _const_spec(wout.shape),
            _const_spec(ln_g.shape), _const_spec(ln_b.shape),
        ],
        out_specs=row_spec(d),
        out_shape=jax.ShapeDtypeStruct((rows, d), F32),
        compiler_params=_params(("arbitrary",)),
        name="merge",
    )(fm2, o2, gab2, x2, mods_l, wf, wna, wout, ln_g, ln_b)


def _prep_ff(w_up, w_down):
    d, two_ff = w_up.shape
    ff = two_ff // 2
    nc = ff // FF_CHUNK
    a = w_up[:, :ff].reshape(d, nc, FF_CHUNK)
    g = w_up[:, ff:].reshape(d, nc, FF_CHUNK)
    up = jnp.concatenate([a, g], axis=2).transpose(1, 0, 2).astype(BF16)
    dn = w_down.astype(BF16)
    return up, dn


def _prep_in(w_in):
    d, n = w_in.shape
    cw = NA_WIDTH
    return w_in.reshape(d, n // cw, cw).transpose(1, 0, 2).astype(BF16)


def kernel(x_prompt, x_sample, cache_k, cache_v, c, c_ctx, w_ada, b_ada, ln_g, ln_b, w_ff1_up,
           w_ff1_down, w_in, rpb, w_fourier, w_na_out, w_out, w_ff2_up, w_ff2_down):
    batch, seq, d = x_prompt.shape
    dec_batch, dec_seq, _ = x_sample.shape
    depth = w_ada.shape[0]
    alpha = (2 * depth) ** 0.25
    rows_lat = dec_seq // GRID_W

    cvec = jnp.zeros((MOD_ROWS, d), F32).at[0].set(c_ctx).at[1:1 + dec_batch].set(c)
    mods = _mods(cvec, w_ada, b_ada).reshape(depth, MOD_ROWS, N_MOD, d)

    y_p = x_prompt.reshape(batch * seq, d)
    y_s = x_sample.reshape(dec_batch * dec_seq, d)
    k_list, v_list = [], []
    for l in range(depth):
        m_l = mods[l]
        up1, dn1 = _prep_ff(w_ff1_up[l], w_ff1_down[l])
        up2, dn2 = _prep_ff(w_ff2_up[l], w_ff2_down[l])
        win = _prep_in(w_in[l])
        wf = w_fourier[l].astype(BF16)
        wna = w_na_out[l].astype(BF16)
        wout = w_out[l].astype(BF16)
        g_l, b_l = ln_g[l], ln_b[l]

        y_p = _ffn(y_p, m_l, batch * seq, 0, up1, dn1, g_l, b_l, 0, alpha)
        y_p3, k_l, v_l = _ctx_mix(y_p.reshape(batch, seq, d), m_l, win, wf, wna, wout, g_l, b_l, alpha)
        y_p = _ffn(y_p3.reshape(batch * seq, d), m_l, batch * seq, 0, up2, dn2, g_l, b_l, 2, alpha)
        k_list.append(k_l)
        v_list.append(v_l)

        y_s = _ffn(y_s, m_l, dec_seq, 1, up1, dn1, g_l, b_l, 0, alpha)
        q2, k2, v2, f2, gab2 = _in_proj(y_s, m_l, dec_seq, 1, win)
        to3 = lambda t: t.reshape(dec_batch, dec_seq, t.shape[-1])
        ck3 = cache_k[:, l].transpose(0, 2, 1, 3).reshape(dec_batch, -1, NA_WIDTH).astype(BF16)
        cv3 = cache_v[:, l].transpose(0, 2, 1, 3).reshape(dec_batch, -1, NA_WIDTH).astype(BF16)
        bias = _nbr_bias_table(rpb[l], rows_lat)
        o3 = _nbr_attn(to3(q2), to3(k2), to3(v2), ck3, cv3, bias)
        fm3 = _fourier(to3(f2))
        y_s = _merge(fm3.reshape(-1, FNET_WIDTH), o3.reshape(-1, NA_WIDTH), gab2, y_s, m_l,
                     dec_seq, 1, wf, wna, wout, g_l, b_l, alpha)
        y_s = _ffn(y_s, m_l, dec_seq, 1, up2, dn2, g_l, b_l, 2, alpha)

    new_k = jnp.stack(k_list, axis=1)
    new_v = jnp.stack(v_list, axis=1)
    return (y_p.reshape(batch, seq, d), y_s.reshape(dec_batch, dec_seq, d), new_k, new_v)
```

```python
import functools
import math

import numpy as np
import jax
import jax.numpy as jnp
from jax import lax
from jax.experimental import pallas as pl
from jax.experimental.pallas import tpu as pltpu

F32 = jnp.float32
BF16 = jnp.bfloat16

HEAD_DIM = 64
NA_HEADS = 8
NA_WIDTH = NA_HEADS * HEAD_DIM
FNET_GROUPS = 4
FNET_GROUP_DIM = 128
FNET_WIDTH = FNET_GROUPS * FNET_GROUP_DIM
GRID_W = 64
WIN_ROWS = 8
WIN_COLS = 16
N_SUB = 3
N_MOD = 3 * N_SUB
ATTN_SCALE = HEAD_DIM ** -0.5
LN_EPS = 1e-5
NEG_INF = -1e30

LANES = 128
MXU_DIM = 256
VMEM_LIMIT_BYTES = 56 * 1024 * 1024

ROW_TILE = 512
FF_CHUNK = MXU_DIM
Q_ROWS = 4
K_ROWS = Q_ROWS + WIN_ROWS
IN_CHUNK = 2 * MXU_DIM
MOD_ROWS = 8
FOURIER_COLS = 2 * FNET_GROUP_DIM
LOOP_UNROLL = 4
HI = lax.Precision.HIGHEST


def _params(sem):
    return pltpu.CompilerParams(dimension_semantics=sem, vmem_limit_bytes=VMEM_LIMIT_BYTES)


def _const_spec(shape):
    nd = len(shape)
    return pl.BlockSpec(shape, lambda *_: (0,) * nd, pipeline_mode=pl.Buffered(1))


def _layer_norm(r, g, b):
    mu = jnp.mean(r, axis=-1, keepdims=True)
    d = r - mu
    var = jnp.mean(d * d, axis=-1, keepdims=True)
    return d * lax.rsqrt(var + LN_EPS) * g + b


def _bdot(a, b):
    return jnp.dot(a, b, preferred_element_type=F32)


def _dot_nt(a, b):
    return lax.dot_general(a, b, (((1,), (1,)), ((), ())), preferred_element_type=F32)


def _dft_mats(n):
    k = np.arange(n)
    ang = 2.0 * np.pi * ((k[:, None] * k[None, :]) % n) / n
    return np.cos(ang), np.sin(ang)


def _mods_kernel(c_ref, w_ref, b_ref, o_ref):
    c = c_ref[...]
    s = (c * jax.nn.sigmoid(c)).astype(BF16)
    o_ref[0] = _bdot(s, w_ref[0].astype(BF16)) + b_ref[0]


def _mods(cvec, w_ada, b_ada):
    depth, d, n = w_ada.shape
    tn = d
    return pl.pallas_call(
        _mods_kernel,
        grid=(depth, n // tn),
        in_specs=[
            pl.BlockSpec((MOD_ROWS, d), lambda l, j: (0, 0)),
            pl.BlockSpec((1, d, tn), lambda l, j: (l, 0, j)),
            pl.BlockSpec((1, 1, tn), lambda l, j: (l, 0, j)),
        ],
        out_specs=pl.BlockSpec((1, MOD_ROWS, tn), lambda l, j: (l, 0, j)),
        out_shape=jax.ShapeDtypeStruct((depth, MOD_ROWS, n), F32),
        compiler_params=_params(("arbitrary", "arbitrary")),
        name="mods",
    )(cvec, w_ada, b_ada.reshape(depth, 1, n))


def _ffn_kernel(x_ref, m_ref, wup_ref, wdn_ref, g_ref, b_ref, o_ref, act_ref, *, sub, alpha):
    x = x_ref[...]
    shift = m_ref[0, 3 * sub:3 * sub + 1, :]
    scale = m_ref[0, 3 * sub + 1:3 * sub + 2, :]
    gate = m_ref[0, 3 * sub + 2:3 * sub + 3, :]
    u = (x * (1.0 + scale) + shift).astype(BF16)
    ff = wdn_ref.shape[0]
    for c0 in range(0, ff, FF_CHUNK):
        a = _bdot(u, wup_ref[:, c0:c0 + FF_CHUNK])
        g = _bdot(u, wup_ref[:, ff + c0:ff + c0 + FF_CHUNK])
        act_ref[:, c0:c0 + FF_CHUNK] = ((g * jax.nn.sigmoid(g)) * a).astype(BF16)
    y = _bdot(act_ref[...], wdn_ref[...])
    r = alpha * x + (0.5 * gate) * y
    o_ref[...] = _layer_norm(r, g_ref[sub:sub + 1, :], b_ref[sub:sub + 1, :])


def _ffn(x2, mods_l, rows_per_group, group0, wup, wdn, ln_g, ln_b, sub, alpha):
    rows, d = x2.shape
    tm = min(ROW_TILE, rows_per_group)
    tiles_per_group = rows_per_group // tm
    return pl.pallas_call(
        functools.partial(_ffn_kernel, sub=sub, alpha=alpha),
        grid=(rows // tm,),
        in_specs=[
            pl.BlockSpec((tm, d), lambda i: (i, 0)),
            pl.BlockSpec((1, N_MOD, d), lambda i: (group0 + i // tiles_per_group, 0, 0)),
            _const_spec(wup.shape),
            _const_spec(wdn.shape),
            _const_spec(ln_g.shape),
            _const_spec(ln_b.shape),
        ],
        out_specs=pl.BlockSpec((tm, d), lambda i: (i, 0)),
        out_shape=jax.ShapeDtypeStruct((rows, d), F32),
        scratch_shapes=[pltpu.VMEM((tm, wdn.shape[0]), BF16)],
        compiler_params=_params(("arbitrary",)),
        name=f"ffn{sub}",
    )(x2, mods_l, wup, wdn, ln_g, ln_b)


def _head_pair_attention(q, k_list, v_list, bias_list, n_pairs_lanes=LANES):
    lane = lax.broadcasted_iota(jnp.int32, (1, n_pairs_lanes), 1)
    outs = []
    for h in range(2):
        in_head = (lane // HEAD_DIM) == h
        qm = jnp.where(in_head, q, jnp.zeros_like(q))
        s = []
        for k, bias in zip(k_list, bias_list):
            sk = _dot_nt(qm, k)
            if bias is not None:
                sk = sk + bias[h]
            s.append(sk)
        m = s[0].max(axis=-1, keepdims=True)
        for sk in s[1:]:
            m = jnp.maximum(m, sk.max(axis=-1, keepdims=True))
        den = None
        o = None
        for sk, v in zip(s, v_list):
            p = jnp.exp(sk - m)
            ps = p.sum(axis=-1, keepdims=True)
            den = ps if den is None else den + ps
            pv = _bdot(p.astype(BF16), v)
            o = pv if o is None else o + pv
        outs.append(o / den)
    return jnp.where(lane < HEAD_DIM, outs[0], outs[1])


def _merge_out(fm, o, ga, gb, x, shift_gate, wf_ref, wna_ref, wout_ref, ln_g, ln_b, alpha):
    branch_a = _bdot(fm.astype(BF16), wf_ref[...])
    branch_b = _bdot(o, wna_ref[...])
    mg = jax.nn.sigmoid(ga) * branch_a + jax.nn.sigmoid(gb) * branch_b
    mix = _bdot(mg.astype(BF16), wout_ref[...])
    r = alpha * x + shift_gate * mix
    return _layer_norm(r, ln_g, ln_b)


def _ctx_mix_kernel(x_ref, m_ref, win_ref, wf_ref, wna_ref, wout_ref, g_ref, b_ref,
                    cdft_ref, pdft_ref, *rest, alpha, n_alias):
    o_ref, k_ref, v_ref, z_ref, oatt_ref, fm_ref = rest[n_alias:]
    nb, seq, d = x_ref.shape
    rows = nb * seq
    x = x_ref[...].reshape(rows, d)
    shift = m_ref[0, 3:4, :]
    scale = m_ref[0, 4:5, :]
    gate = m_ref[0, 5:6, :]
    u = (x * (1.0 + scale) + shift).astype(BF16)
    for c0 in range(0, win_ref.shape[1], IN_CHUNK):
        z_ref[:, c0:c0 + IN_CHUNK] = _bdot(u, win_ref[:, c0:c0 + IN_CHUNK])

    cdft = cdft_ref[...].astype(BF16)
    pdft = pdft_ref[...].astype(BF16)
    q_off, k_off, v_off, f_off = 0, NA_WIDTH, 2 * NA_WIDTH, 3 * NA_WIDTH
    ga_off = f_off + FNET_WIDTH
    gb_off = ga_off + d
    for b in range(nb):
        r0 = b * seq
        for h in range(NA_HEADS):
            k_ref[b, 0, h] = z_ref[r0:r0 + seq, k_off + h * HEAD_DIM:k_off + (h + 1) * HEAD_DIM]
            v_ref[b, 0, h] = z_ref[r0:r0 + seq, v_off + h * HEAD_DIM:v_off + (h + 1) * HEAD_DIM]
        for hp in range(NA_WIDTH // LANES):
            c0 = hp * LANES
            q = (z_ref[r0:r0 + seq, q_off + c0:q_off + c0 + LANES] * ATTN_SCALE).astype(BF16)
            k = z_ref[r0:r0 + seq, k_off + c0:k_off + c0 + LANES].astype(BF16)
            v = z_ref[r0:r0 + seq, v_off + c0:v_off + c0 + LANES].astype(BF16)
            o = _head_pair_attention(q, [k], [v], [None])
            oatt_ref[r0:r0 + seq, c0:c0 + LANES] = o.astype(BF16)
        for g in range(FNET_GROUPS):
            c0 = f_off + g * FNET_GROUP_DIM
            xg = z_ref[r0:r0 + seq, c0:c0 + FNET_GROUP_DIM]
            zc = _bdot(xg.astype(BF16), cdft).astype(BF16)
            stacked = jnp.concatenate([zc[:, :FNET_GROUP_DIM], zc[:, FNET_GROUP_DIM:]], axis=0)
            fm_ref[r0:r0 + seq, g * FNET_GROUP_DIM:(g + 1) * FNET_GROUP_DIM] = _bdot(
                pdft, stacked)

    out = _merge_out(fm_ref[...], oatt_ref[...], z_ref[:, ga_off:ga_off + d],
                     z_ref[:, gb_off:gb_off + d], x, gate, wf_ref, wna_ref, wout_ref,
                     g_ref[1:2, :], b_ref[1:2, :], alpha)
    o_ref[...] = out.reshape(nb, seq, d)


def _ctx_mix(x3, mods_l, win, wf, wna, wout, ln_g, ln_b, alpha, layer, depth, kv_prev):
    batch, seq, d = x3.shape
    nb = 2 if batch % 2 == 0 else 1
    in_width = win.shape[1]
    cc, cs = _dft_mats(FNET_GROUP_DIM)
    pc, ps = _dft_mats(seq)
    norm = 1.0 / math.sqrt(seq * FNET_GROUP_DIM)
    cdft = jnp.asarray(np.concatenate([cc, -cs], axis=1) * norm, F32)
    pdft = jnp.asarray(np.concatenate([pc, ps], axis=1), F32)
    kv_shape = jax.ShapeDtypeStruct((batch, depth, NA_HEADS, seq, HEAD_DIM), F32)
    kv_spec = pl.BlockSpec((nb, 1, NA_HEADS, seq, HEAD_DIM), lambda i: (i, layer, 0, 0, 0))
    operands = [x3, mods_l, win, wf, wna, wout, ln_g, ln_b, cdft, pdft]
    in_specs = [
        pl.BlockSpec((nb, seq, d), lambda i: (i, 0, 0)),
        pl.BlockSpec((1, N_MOD, d), lambda i: (0, 0, 0)),
        _const_spec(win.shape),
        _const_spec(wf.shape),
        _const_spec(wna.shape),
        _const_spec(wout.shape),
        _const_spec(ln_g.shape),
        _const_spec(ln_b.shape),
        _const_spec(cdft.shape),
        _const_spec(pdft.shape),
    ]
    aliases = {}
    if kv_prev is not None:
        aliases = {len(operands): 1, len(operands) + 1: 2}
        operands += list(kv_prev)
        in_specs += [pl.BlockSpec(memory_space=pl.ANY)] * 2
    return pl.pallas_call(
        functools.partial(_ctx_mix_kernel, alpha=alpha, n_alias=len(aliases)),
        grid=(batch // nb,),
        in_specs=in_specs,
        out_specs=[pl.BlockSpec((nb, seq, d), lambda i: (i, 0, 0)), kv_spec, kv_spec],
        out_shape=[jax.ShapeDtypeStruct((batch, seq, d), F32), kv_shape, kv_shape],
        input_output_aliases=aliases,
        scratch_shapes=[
            pltpu.VMEM((nb * seq, in_width), F32),
            pltpu.VMEM((nb * seq, NA_WIDTH), BF16),
            pltpu.VMEM((nb * seq, FNET_WIDTH), F32),
        ],
        compiler_params=_params(("arbitrary",)),
        name="ctx_mix",
    )(*operands)


def _in_proj_kernel(x_ref, m_ref, win_ref, q_ref, k_ref, v_ref, f_ref):
    x = x_ref[...]
    shift = m_ref[0, 3:4, :]
    scale = m_ref[0, 4:5, :]
    u = (x * (1.0 + scale) + shift).astype(BF16)
    cw = NA_WIDTH
    q_ref[...] = (_bdot(u, win_ref[:, 0:cw]) * ATTN_SCALE).astype(BF16)
    k_ref[...] = _bdot(u, win_ref[:, cw:2 * cw]).astype(BF16)
    v_ref[...] = _bdot(u, win_ref[:, 2 * cw:3 * cw]).astype(BF16)
    f_ref[...] = _bdot(u, win_ref[:, 3 * cw:3 * cw + FNET_WIDTH])


def _in_proj(x2, mods_l, rows_per_group, group0, win):
    rows, d = x2.shape
    tm = min(ROW_TILE, rows_per_group)
    tiles_per_group = rows_per_group // tm
    cw = NA_WIDTH
    qkvf = 3 * NA_WIDTH + FNET_WIDTH
    row_spec = lambda w: pl.BlockSpec((tm, w), lambda i: (i, 0))
    return pl.pallas_call(
        _in_proj_kernel,
        grid=(rows // tm,),
        in_specs=[
            row_spec(d),
            pl.BlockSpec((1, N_MOD, d), lambda i: (group0 + i // tiles_per_group, 0, 0)),
            pl.BlockSpec((d, qkvf), lambda i: (0, 0), pipeline_mode=pl.Buffered(1)),
        ],
        out_specs=[row_spec(cw), row_spec(cw), row_spec(cw), row_spec(FNET_WIDTH)],
        out_shape=[
            jax.ShapeDtypeStruct((rows, cw), BF16),
            jax.ShapeDtypeStruct((rows, cw), BF16),
            jax.ShapeDtypeStruct((rows, cw), BF16),
            jax.ShapeDtypeStruct((rows, FNET_WIDTH), F32),
        ],
        compiler_params=_params(("arbitrary",)),
        name="in_proj",
    )(x2, mods_l, win)


def _window_start_row(i, rows):
    return jnp.clip(Q_ROWS * i - WIN_ROWS // 2, 0, rows - K_ROWS)


def _nbr_attn_kernel(q_ref, k_ref, v_ref, ck_ref, cv_ref, bias_ref, o_ref, *, rows):
    i = pl.program_id(1)
    start = pl.multiple_of(_window_start_row(i, rows) * GRID_W, GRID_W * math.gcd(Q_ROWS, WIN_ROWS // 2))
    n_keys = K_ROWS * GRID_W
    for b in range(q_ref.shape[0]):
        kw = k_ref[b, pl.ds(start, n_keys), :]
        vw = v_ref[b, pl.ds(start, n_keys), :]
        o = _head_pair_attention(
            q_ref[b], [kw, ck_ref[b]], [vw, cv_ref[b]],
            [(bias_ref[0, 0], bias_ref[1, 0]), None])
        o_ref[b] = o.astype(BF16)


def _bias_rows_kernel(rpb_ref, onehot_ref, neg_ref, o_ref):
    o_ref[...] = jnp.dot(rpb_ref[...], onehot_ref[...], precision=HI,
                         preferred_element_type=F32) + neg_ref[...]


def _nbr_bias_table(rpb_l, rows):
    h, n_dr, n_dc = rpb_l.shape
    n_blocks = rows // Q_ROWS
    kr = min(WIN_ROWS, rows)
    variants = np.array([0, min(1, n_blocks - 1), n_blocks - 1])
    r = Q_ROWS * variants[:, None] + np.arange(Q_ROWS)[None, :]
    ks = np.clip(Q_ROWS * variants - WIN_ROWS // 2, 0, rows - K_ROWS)
    key_row = ks[:, None] + np.arange(K_ROWS)[None, :]
    r0 = np.clip(r - kr // 2, 0, rows - kr)
    row_ok = (key_row[:, None, :] >= r0[:, :, None]) & (key_row[:, None, :] < r0[:, :, None] + kr)
    dr = np.clip(key_row[:, None, :] - r[:, :, None] + (WIN_ROWS - 1), 0, n_dr - 1)
    c_idx = np.arange(GRID_W)
    c0 = np.clip(c_idx - WIN_COLS // 2, 0, GRID_W - WIN_COLS)
    col_ok = (c_idx[None, :] >= c0[:, None]) & (c_idx[None, :] < c0[:, None] + WIN_COLS)
    dc = np.clip(c_idx[None, :] - c_idx[:, None] + (WIN_COLS - 1), 0, n_dc - 1)

    k_pad = -(-n_dc // LANES) * LANES
    m_pad = -(-(h * n_dr) // 8) * 8
    onehot = np.zeros((k_pad, GRID_W * GRID_W), np.float32)
    flat = np.arange(GRID_W * GRID_W).reshape(GRID_W, GRID_W)
    onehot[dc[col_ok], flat[col_ok]] = 1.0
    neg = np.where(col_ok, 0.0, NEG_INF).astype(np.float32).reshape(1, -1)
    rpb_pad = jnp.zeros((m_pad, k_pad), F32).at[:h * n_dr, :n_dc].set(
        rpb_l.reshape(h * n_dr, n_dc).astype(F32))
    col_tab = pl.pallas_call(
        _bias_rows_kernel,
        out_shape=jax.ShapeDtypeStruct((m_pad, GRID_W * GRID_W), F32),
        name="bias_rows",
    )(rpb_pad, jnp.asarray(onehot), jnp.asarray(neg))
    col_tab = col_tab[:h * n_dr].reshape(h, n_dr, GRID_W, GRID_W).transpose(0, 2, 1, 3)
    pad = K_ROWS + Q_ROWS
    col_tab = jnp.pad(col_tab, ((0, 0), (0, 0), (pad, pad), (0, 0)))
    slabs = []
    for v in range(3):
        for qr in range(Q_ROWS):
            d0 = int(ks[v] - r[v, qr]) + (WIN_ROWS - 1) + pad
            slab = col_tab[:, :, d0:d0 + K_ROWS, :]
            slab = jnp.where(jnp.asarray(row_ok[v, qr])[None, None, :, None], slab, NEG_INF)
            slabs.append(slab.reshape(h, GRID_W, K_ROWS * GRID_W))
    return jnp.stack(slabs, axis=1).reshape(h, 3, Q_ROWS * GRID_W, K_ROWS * GRID_W)


def _nbr_attn(q3, k3, v3, ck3, cv3, bias):
    nb, seq, width = q3.shape
    rows = seq // GRID_W
    n_blocks = rows // Q_ROWS
    n_pairs = width // LANES
    past = ck3.shape[1]
    tq = Q_ROWS * GRID_W
    tk = K_ROWS * GRID_W

    def variant(i):
        return jnp.where(i == 0, 0, jnp.where(i == n_blocks - 1, 2, 1))

    return pl.pallas_call(
        functools.partial(_nbr_attn_kernel, rows=rows),
        grid=(n_pairs, n_blocks),
        in_specs=[
            pl.BlockSpec((nb, tq, LANES), lambda p, i: (0, i, p)),
            pl.BlockSpec((nb, seq, LANES), lambda p, i: (0, 0, p)),
            pl.BlockSpec((nb, seq, LANES), lambda p, i: (0, 0, p)),
            pl.BlockSpec((nb, past, LANES), lambda p, i: (0, 0, p)),
            pl.BlockSpec((nb, past, LANES), lambda p, i: (0, 0, p)),
            pl.BlockSpec((2, 1, tq, tk), lambda p, i: (p, variant(i), 0, 0)),
        ],
        out_specs=pl.BlockSpec((nb, tq, LANES), lambda p, i: (0, i, p)),
        out_shape=jax.ShapeDtypeStruct((nb, seq, width), BF16),
        compiler_params=_params(("arbitrary", "arbitrary")),
        name="nbr_attn",
    )(q3, k3, v3, ck3, cv3, bias)


def _fourier_kernel(x_ref, cdft_ref, d1_ref, d2_ref, twc_ref, tws_ref, o_ref,
                    zr_ref, zi_ref, tr_ref, ti_ref, *, n1, n2):
    gd = cdft_ref.shape[0]
    n_groups = x_ref.shape[2] // gd
    groups = range(n_groups)
    cdft = cdft_ref[...].astype(BF16)
    d1 = d1_ref[...].astype(BF16)
    d2 = d2_ref[...].astype(BF16)
    for g in groups:
        zc = _bdot(x_ref[0, :, g * gd:(g + 1) * gd].astype(BF16), cdft)
        zr_ref[g] = zc[:, :gd]
        zi_ref[g] = zc[:, gd:]

    def stacked(re_ref, im_ref, rows):
        return jnp.concatenate(
            [jnp.concatenate([re_ref[g, rows, :], im_ref[g, rows, :]], axis=0) for g in groups],
            axis=1)

    def stage1(c, carry):
        t = _bdot(d1, stacked(zr_ref, zi_ref, pl.ds(c, n1, stride=n2)).astype(BF16))
        cs = twc_ref[c]
        sn = tws_ref[c]
        row = pl.multiple_of(c * n1, n1)
        for g in groups:
            tr = t[:n1, g * gd:(g + 1) * gd]
            ti = t[n1:, g * gd:(g + 1) * gd]
            tr_ref[g, pl.ds(row, n1), :] = tr * cs + ti * sn
            ti_ref[g, pl.ds(row, n1), :] = ti * cs - tr * sn
        return carry

    lax.fori_loop(0, n2, stage1, 0, unroll=LOOP_UNROLL)

    def stage2(k1, carry):
        rows = pl.ds(k1, n2, stride=n1)
        y = _bdot(d2, stacked(tr_ref, ti_ref, rows).astype(BF16))
        for g in groups:
            zr_ref[g, rows, :] = y[:, g * gd:(g + 1) * gd]
        return carry

    lax.fori_loop(0, n1, stage2, 0, unroll=LOOP_UNROLL)
    for g in groups:
        o_ref[0, :, g * gd:(g + 1) * gd] = zr_ref[g]


def _fourier(f3):
    nb, seq, width = f3.shape
    n2 = GRID_W
    n1 = seq // n2
    gd = FNET_GROUP_DIM
    cc, cs = _dft_mats(gd)
    c1, s1 = _dft_mats(n1)
    c2, s2 = _dft_mats(n2)
    norm = 1.0 / math.sqrt(seq * gd)
    cdft = jnp.asarray(np.concatenate([cc, -cs], axis=1) * norm, F32)
    d1 = jnp.asarray(np.block([[c1, s1], [-s1, c1]]), F32)
    d2 = jnp.asarray(np.concatenate([c2, s2], axis=1), F32)
    ang = 2.0 * np.pi * ((np.arange(n2)[:, None] * np.arange(n1)[None, :]) % seq) / seq
    twc = jnp.asarray(np.broadcast_to(np.cos(ang)[:, :, None], (n2, n1, gd)), F32)
    tws = jnp.asarray(np.broadcast_to(np.sin(ang)[:, :, None], (n2, n1, gd)), F32)
    return pl.pallas_call(
        functools.partial(_fourier_kernel, n1=n1, n2=n2),
        grid=(nb, width // FOURIER_COLS),
        in_specs=[
            pl.BlockSpec((1, seq, FOURIER_COLS), lambda b, g: (b, 0, g)),
            _const_spec(cdft.shape),
            _const_spec(d1.shape),
            _const_spec(d2.shape),
            _const_spec(twc.shape),
            _const_spec(tws.shape),
        ],
        out_specs=pl.BlockSpec((1, seq, FOURIER_COLS), lambda b, g: (b, 0, g)),
        out_shape=jax.ShapeDtypeStruct((nb, seq, width), F32),
        scratch_shapes=[pltpu.VMEM((FOURIER_COLS // gd, seq, gd), F32)] * 4,
        compiler_params=_params(("arbitrary", "arbitrary")),
        name="fourier",
    )(f3, cdft, d1, d2, twc, tws)


def _merge_kernel(fm_ref, o_ref_in, x_ref, m_ref, wgate_ref, wf_ref, wna_ref, wout_ref,
                  g_ref, b_ref, out_ref, *, alpha):
    d = x_ref.shape[1]
    x = x_ref[...]
    shift = m_ref[0, 3:4, :]
    scale = m_ref[0, 4:5, :]
    gate = m_ref[0, 5:6, :]
    u = (x * (1.0 + scale) + shift).astype(BF16)
    ga = _bdot(u, wgate_ref[:, :d])
    gb = _bdot(u, wgate_ref[:, d:])
    out_ref[...] = _merge_out(fm_ref[...], o_ref_in[...], ga, gb, x, gate, wf_ref, wna_ref,
                              wout_ref, g_ref[1:2, :], b_ref[1:2, :], alpha)


def _merge(fm2, o2, x2, mods_l, rows_per_group, group0, win, wf, wna, wout, ln_g, ln_b, alpha):
    rows, d = x2.shape
    tm = min(ROW_TILE, rows_per_group)
    tiles_per_group = rows_per_group // tm
    row_spec = lambda w: pl.BlockSpec((tm, w), lambda i: (i, 0))
    assert win.shape[1] == 2 * (2 * d), "gate columns must be the second half of w_in"
    return pl.pallas_call(
        functools.partial(_merge_kernel, alpha=alpha),
        grid=(rows // tm,),
        in_specs=[
            row_spec(fm2.shape[1]), row_spec(o2.shape[1]), row_spec(d),
            pl.BlockSpec((1, N_MOD, d), lambda i: (group0 + i // tiles_per_group, 0, 0)),
            pl.BlockSpec((d, 2 * d), lambda i: (0, 1), pipeline_mode=pl.Buffered(1)),
            _const_spec(wf.shape), _const_spec(wna.shape), _const_spec(wout.shape),
            _const_spec(ln_g.shape), _const_spec(ln_b.shape),
        ],
        out_specs=row_spec(d),
        out_shape=jax.ShapeDtypeStruct((rows, d), F32),
        compiler_params=_params(("arbitrary",)),
        name="merge",
    )(fm2, o2, x2, mods_l, win, wf, wna, wout, ln_g, ln_b)


def kernel(x_prompt, x_sample, cache_k, cache_v, c, c_ctx, w_ada, b_ada, ln_g, ln_b, w_ff1_up,
           w_ff1_down, w_in, rpb, w_fourier, w_na_out, w_out, w_ff2_up, w_ff2_down):
    batch, seq, d = x_prompt.shape
    dec_batch, dec_seq, _ = x_sample.shape
    depth = w_ada.shape[0]
    alpha = (2 * depth) ** 0.25
    rows_lat = dec_seq // GRID_W

    cvec = jnp.zeros((MOD_ROWS, d), F32).at[0].set(c_ctx).at[1:1 + dec_batch].set(c)
    mods = _mods(cvec, w_ada, b_ada).reshape(depth, MOD_ROWS, N_MOD, d)

    y_p = x_prompt.reshape(batch * seq, d)
    y_s = x_sample.reshape(dec_batch * dec_seq, d)
    kv = None
    for l in range(depth):
        m_l = mods[l]
        up1, dn1 = w_ff1_up[l].astype(BF16), w_ff1_down[l].astype(BF16)
        up2, dn2 = w_ff2_up[l].astype(BF16), w_ff2_down[l].astype(BF16)
        win = w_in[l].astype(BF16)
        wf = w_fourier[l].astype(BF16)
        wna = w_na_out[l].astype(BF16)
        wout = w_out[l].astype(BF16)
        g_l, b_l = ln_g[l], ln_b[l]

        y_p = _ffn(y_p, m_l, batch * seq, 0, up1, dn1, g_l, b_l, 0, alpha)
        y_p3, new_k, new_v = _ctx_mix(y_p.reshape(batch, seq, d), m_l, win, wf, wna, wout, g_l, b_l,
                                      alpha, l, depth, kv)
        kv = (new_k, new_v)
        y_p = _ffn(y_p3.reshape(batch * seq, d), m_l, batch * seq, 0, up2, dn2, g_l, b_l, 2, alpha)

        y_s = _ffn(y_s, m_l, dec_seq, 1, up1, dn1, g_l, b_l, 0, alpha)
        q2, k2, v2, f2 = _in_proj(y_s, m_l, dec_seq, 1, win)
        to3 = lambda t: t.reshape(dec_batch, dec_seq, t.shape[-1])
        ck3 = cache_k[:, l].transpose(0, 2, 1, 3).reshape(dec_batch, -1, NA_WIDTH).astype(BF16)
        cv3 = cache_v[:, l].transpose(0, 2, 1, 3).reshape(dec_batch, -1, NA_WIDTH).astype(BF16)
        bias = _nbr_bias_table(rpb[l], rows_lat)
        o3 = _nbr_attn(to3(q2), to3(k2), to3(v2), ck3, cv3, bias)
        fm3 = _fourier(to3(f2))
        y_s = _merge(fm3.reshape(-1, FNET_WIDTH), o3.reshape(-1, NA_WIDTH), y_s, m_l,
                     dec_seq, 1, win, wf, wna, wout, g_l, b_l, alpha)
        y_s = _ffn(y_s, m_l, dec_seq, 1, up2, dn2, g_l, b_l, 2, alpha)

    return (y_p.reshape(batch, seq, d), y_s.reshape(dec_batch, dec_seq, d), new_k, new_v)
```

```python
import functools
import math

import numpy as np
import jax
import jax.numpy as jnp
from jax import lax
from jax.experimental import pallas as pl
from jax.experimental.pallas import tpu as pltpu

F32 = jnp.float32
BF16 = jnp.bfloat16

HEAD_DIM = 64
NA_HEADS = 8
NA_WIDTH = NA_HEADS * HEAD_DIM
FNET_GROUPS = 4
FNET_GROUP_DIM = 128
FNET_WIDTH = FNET_GROUPS * FNET_GROUP_DIM
GRID_W = 64
WIN_ROWS = 8
WIN_COLS = 16
N_SUB = 3
N_MOD = 3 * N_SUB
ATTN_SCALE = HEAD_DIM ** -0.5
LN_EPS = 1e-5
NEG_INF = -1e30

LANES = 128
MXU_DIM = 256
VMEM_LIMIT_BYTES = 56 * 1024 * 1024

ROW_TILE = 512
FF_CHUNK = MXU_DIM
Q_ROWS = 4
K_ROWS = Q_ROWS + WIN_ROWS
IN_CHUNK = 2 * MXU_DIM
MOD_ROWS = 8
FOURIER_COLS = 2 * FNET_GROUP_DIM
LOOP_UNROLL = 4
HI = lax.Precision.HIGHEST


def _params(sem):
    return pltpu.CompilerParams(dimension_semantics=sem, vmem_limit_bytes=VMEM_LIMIT_BYTES)


def _const_spec(shape):
    nd = len(shape)
    return pl.BlockSpec(shape, lambda *_: (0,) * nd, pipeline_mode=pl.Buffered(1))


def _layer_norm(r, g, b):
    mu = jnp.mean(r, axis=-1, keepdims=True)
    d = r - mu
    var = jnp.mean(d * d, axis=-1, keepdims=True)
    return d * lax.rsqrt(var + LN_EPS) * g + b


def _bdot(a, b):
    return jnp.dot(a, b, preferred_element_type=F32)


def _dot_nt(a, b):
    return lax.dot_general(a, b, (((1,), (1,)), ((), ())), preferred_element_type=F32)


def _dft_mats(n):
    k = np.arange(n)
    ang = 2.0 * np.pi * ((k[:, None] * k[None, :]) % n) / n
    return np.cos(ang), np.sin(ang)


def _mods_kernel(c_ref, w_ref, b_ref, o_ref):
    c = c_ref[...]
    s = (c * jax.nn.sigmoid(c)).astype(BF16)
    o_ref[0] = _bdot(s, w_ref[0].astype(BF16)) + b_ref[0]


def _mods(cvec, w_ada, b_ada):
    depth, d, n = w_ada.shape
    tn = d
    return pl.pallas_call(
        _mods_kernel,
        grid=(depth, n // tn),
        in_specs=[
            pl.BlockSpec((MOD_ROWS, d), lambda l, j: (0, 0)),
            pl.BlockSpec((1, d, tn), lambda l, j: (l, 0, j)),
            pl.BlockSpec((1, 1, tn), lambda l, j: (l, 0, j)),
        ],
        out_specs=pl.BlockSpec((1, MOD_ROWS, tn), lambda l, j: (l, 0, j)),
        out_shape=jax.ShapeDtypeStruct((depth, MOD_ROWS, n), F32),
        compiler_params=_params(("arbitrary", "arbitrary")),
        name="mods",
    )(cvec, w_ada, b_ada.reshape(depth, 1, n))


def _ffn_kernel(x_ref, m_ref, wup_ref, wdn_ref, g_ref, b_ref, o_ref, act_ref, *, sub, alpha):
    x = x_ref[...]
    shift = m_ref[0, 3 * sub:3 * sub + 1, :]
    scale = m_ref[0, 3 * sub + 1:3 * sub + 2, :]
    gate = m_ref[0, 3 * sub + 2:3 * sub + 3, :]
    u = (x * (1.0 + scale) + shift).astype(BF16)
    ff = wdn_ref.shape[0]
    for c0 in range(0, ff, FF_CHUNK):
        a = _bdot(u, wup_ref[:, c0:c0 + FF_CHUNK])
        g = _bdot(u, wup_ref[:, ff + c0:ff + c0 + FF_CHUNK])
        act_ref[:, c0:c0 + FF_CHUNK] = ((g * jax.nn.sigmoid(g)) * a).astype(BF16)
    y = _bdot(act_ref[...], wdn_ref[...])
    r = alpha * x + (0.5 * gate) * y
    o_ref[...] = _layer_norm(r, g_ref[sub:sub + 1, :], b_ref[sub:sub + 1, :])


def _ffn(x2, mods_l, rows_per_group, group0, wup, wdn, ln_g, ln_b, sub, alpha):
    rows, d = x2.shape
    tm = min(ROW_TILE, rows_per_group)
    tiles_per_group = rows_per_group // tm
    return pl.pallas_call(
        functools.partial(_ffn_kernel, sub=sub, alpha=alpha),
        grid=(rows // tm,),
        in_specs=[
            pl.BlockSpec((tm, d), lambda i: (i, 0)),
            pl.BlockSpec((1, N_MOD, d), lambda i: (group0 + i // tiles_per_group, 0, 0)),
            _const_spec(wup.shape),
            _const_spec(wdn.shape),
            _const_spec(ln_g.shape),
            _const_spec(ln_b.shape),
        ],
        out_specs=pl.BlockSpec((tm, d), lambda i: (i, 0)),
        out_shape=jax.ShapeDtypeStruct((rows, d), F32),
        scratch_shapes=[pltpu.VMEM((tm, wdn.shape[0]), BF16)],
        compiler_params=_params(("arbitrary",)),
        name=f"ffn{sub}",
    )(x2, mods_l, wup, wdn, ln_g, ln_b)


def _head_pair_attention(q, k_list, v_list, bias_list, n_pairs_lanes=LANES):
    lane = lax.broadcasted_iota(jnp.int32, (1, n_pairs_lanes), 1)
    outs = []
    for h in range(2):
        in_head = (lane // HEAD_DIM) == h
        qm = jnp.where(in_head, q, jnp.zeros_like(q))
        s = []
        for k, bias in zip(k_list, bias_list):
            sk = _dot_nt(qm, k)
            if bias is not None:
                sk = sk + bias[h]
            s.append(sk)
        m = s[0].max(axis=-1, keepdims=True)
        for sk in s[1:]:
            m = jnp.maximum(m, sk.max(axis=-1, keepdims=True))
        den = None
        o = None
        for sk, v in zip(s, v_list):
            p = jnp.exp(sk - m)
            ps = p.sum(axis=-1, keepdims=True)
            den = ps if den is None else den + ps
            pv = _bdot(p.astype(BF16), v)
            o = pv if o is None else o + pv
        outs.append(o / den)
    return jnp.where(lane < HEAD_DIM, outs[0], outs[1])


def _merge_out(fm, o, ga, gb, x, shift_gate, wf_ref, wna_ref, wout_ref, ln_g, ln_b, alpha):
    branch_a = _bdot(fm.astype(BF16), wf_ref[...])
    branch_b = _bdot(o, wna_ref[...])
    mg = jax.nn.sigmoid(ga) * branch_a + jax.nn.sigmoid(gb) * branch_b
    mix = _bdot(mg.astype(BF16), wout_ref[...])
    r = alpha * x + shift_gate * mix
    return _layer_norm(r, ln_g, ln_b)


def _ctx_mix_kernel(x_ref, m_ref, win_ref, wf_ref, wna_ref, wout_ref, g_ref, b_ref,
                    cdft_ref, pdft_ref, *rest, alpha, n_alias):
    o_ref, k_ref, v_ref, z_ref, oatt_ref, fm_ref = rest[n_alias:]
    nb, seq, d = x_ref.shape
    rows = nb * seq
    x = x_ref[...].reshape(rows, d)
    shift = m_ref[0, 3:4, :]
    scale = m_ref[0, 4:5, :]
    gate = m_ref[0, 5:6, :]
    u = (x * (1.0 + scale) + shift).astype(BF16)
    for c0 in range(0, win_ref.shape[1], IN_CHUNK):
        z_ref[:, c0:c0 + IN_CHUNK] = _bdot(u, win_ref[:, c0:c0 + IN_CHUNK])

    cdft = cdft_ref[...].astype(BF16)
    pdft = pdft_ref[...].astype(BF16)
    q_off, k_off, v_off, f_off = 0, NA_WIDTH, 2 * NA_WIDTH, 3 * NA_WIDTH
    ga_off = f_off + FNET_WIDTH
    gb_off = ga_off + d
    for b in range(nb):
        r0 = b * seq
        for h in range(NA_HEADS):
            k_ref[b, 0, h] = z_ref[r0:r0 + seq, k_off + h * HEAD_DIM:k_off + (h + 1) * HEAD_DIM]
            v_ref[b, 0, h] = z_ref[r0:r0 + seq, v_off + h * HEAD_DIM:v_off + (h + 1) * HEAD_DIM]
        for hp in range(NA_WIDTH // LANES):
            c0 = hp * LANES
            q = (z_ref[r0:r0 + seq, q_off + c0:q_off + c0 + LANES] * ATTN_SCALE).astype(BF16)
            k = z_ref[r0:r0 + seq, k_off + c0:k_off + c0 + LANES].astype(BF16)
            v = z_ref[r0:r0 + seq, v_off + c0:v_off + c0 + LANES].astype(BF16)
            o = _head_pair_attention(q, [k], [v], [None])
            oatt_ref[r0:r0 + seq, c0:c0 + LANES] = o.astype(BF16)
        for g in range(FNET_GROUPS):
            c0 = f_off + g * FNET_GROUP_DIM
            xg = z_ref[r0:r0 + seq, c0:c0 + FNET_GROUP_DIM]
            zc = _bdot(xg.astype(BF16), cdft).astype(BF16)
            stacked = jnp.concatenate([zc[:, :FNET_GROUP_DIM], zc[:, FNET_GROUP_DIM:]], axis=0)
            fm_ref[r0:r0 + seq, g * FNET_GROUP_DIM:(g + 1) * FNET_GROUP_DIM] = _bdot(
                pdft, stacked)

    out = _merge_out(fm_ref[...], oatt_ref[...], z_ref[:, ga_off:ga_off + d],
                     z_ref[:, gb_off:gb_off + d], x, gate, wf_ref, wna_ref, wout_ref,
                     g_ref[1:2, :], b_ref[1:2, :], alpha)
    o_ref[...] = out.reshape(nb, seq, d)


def _ctx_mix(x3, mods_l, win, wf, wna, wout, ln_g, ln_b, alpha, layer, depth, kv_prev):
    batch, seq, d = x3.shape
    nb = 2 if batch % 2 == 0 else 1
    in_width = win.shape[1]
    cc, cs = _dft_mats(FNET_GROUP_DIM)
    pc, ps = _dft_mats(seq)
    norm = 1.0 / math.sqrt(seq * FNET_GROUP_DIM)
    cdft = jnp.asarray(np.concatenate([cc, -cs], axis=1) * norm, F32)
    pdft = jnp.asarray(np.concatenate([pc, ps], axis=1), F32)
    kv_shape = jax.ShapeDtypeStruct((batch, depth, NA_HEADS, seq, HEAD_DIM), F32)
    kv_spec = pl.BlockSpec((nb, 1, NA_HEADS, seq, HEAD_DIM), lambda i: (i, layer, 0, 0, 0))
    operands = [x3, mods_l, win, wf, wna, wout, ln_g, ln_b, cdft, pdft]
    in_specs = [
        pl.BlockSpec((nb, seq, d), lambda i: (i, 0, 0)),
        pl.BlockSpec((1, N_MOD, d), lambda i: (0, 0, 0)),
        _const_spec(win.shape),
        _const_spec(wf.shape),
        _const_spec(wna.shape),
        _const_spec(wout.shape),
        _const_spec(ln_g.shape),
        _const_spec(ln_b.shape),
        _const_spec(cdft.shape),
        _const_spec(pdft.shape),
    ]
    aliases = {}
    if kv_prev is not None:
        aliases = {len(operands): 1, len(operands) + 1: 2}
        operands += list(kv_prev)
        in_specs += [pl.BlockSpec(memory_space=pl.ANY)] * 2
    return pl.pallas_call(
        functools.partial(_ctx_mix_kernel, alpha=alpha, n_alias=len(aliases)),
        grid=(batch // nb,),
        in_specs=in_specs,
        out_specs=[pl.BlockSpec((nb, seq, d), lambda i: (i, 0, 0)), kv_spec, kv_spec],
        out_shape=[jax.ShapeDtypeStruct((batch, seq, d), F32), kv_shape, kv_shape],
        input_output_aliases=aliases,
        scratch_shapes=[
            pltpu.VMEM((nb * seq, in_width), F32),
            pltpu.VMEM((nb * seq, NA_WIDTH), BF16),
            pltpu.VMEM((nb * seq, FNET_WIDTH), F32),
        ],
        compiler_params=_params(("arbitrary",)),
        name="ctx_mix",
    )(*operands)


def _in_proj_kernel(x_ref, m_ref, win_ref, q_ref, k_ref, v_ref, f_ref):
    x = x_ref[...]
    shift = m_ref[0, 3:4, :]
    scale = m_ref[0, 4:5, :]
    u = (x * (1.0 + scale) + shift).astype(BF16)
    cw = NA_WIDTH
    q_ref[...] = (_bdot(u, win_ref[:, 0:cw]) * ATTN_SCALE).astype(BF16)
    k_ref[...] = _bdot(u, win_ref[:, cw:2 * cw]).astype(BF16)
    v_ref[...] = _bdot(u, win_ref[:, 2 * cw:3 * cw]).astype(BF16)
    f_ref[...] = _bdot(u, win_ref[:, 3 * cw:3 * cw + FNET_WIDTH])


def _in_proj(x2, mods_l, rows_per_group, group0, win):
    rows, d = x2.shape
    tm = min(ROW_TILE, rows_per_group)
    tiles_per_group = rows_per_group // tm
    cw = NA_WIDTH
    qkvf = 3 * NA_WIDTH + FNET_WIDTH
    row_spec = lambda w: pl.BlockSpec((tm, w), lambda i: (i, 0))
    return pl.pallas_call(
        _in_proj_kernel,
        grid=(rows // tm,),
        in_specs=[
            row_spec(d),
            pl.BlockSpec((1, N_MOD, d), lambda i: (group0 + i // tiles_per_group, 0, 0)),
            pl.BlockSpec((d, qkvf), lambda i: (0, 0), pipeline_mode=pl.Buffered(1)),
        ],
        out_specs=[row_spec(cw), row_spec(cw), row_spec(cw), row_spec(FNET_WIDTH)],
        out_shape=[
            jax.ShapeDtypeStruct((rows, cw), BF16),
            jax.ShapeDtypeStruct((rows, cw), BF16),
            jax.ShapeDtypeStruct((rows, cw), BF16),
            jax.ShapeDtypeStruct((rows, FNET_WIDTH), F32),
        ],
        compiler_params=_params(("arbitrary",)),
        name="in_proj",
    )(x2, mods_l, win)


def _window_start_row(i, rows):
    return jnp.clip(Q_ROWS * i - WIN_ROWS // 2, 0, rows - K_ROWS)


def _nbr_attn_kernel(q_ref, k_ref, v_ref, ck_ref, cv_ref, bias_ref, o_ref, *, rows):
    i = pl.program_id(1)
    start = pl.multiple_of(_window_start_row(i, rows) * GRID_W, GRID_W * math.gcd(Q_ROWS, WIN_ROWS // 2))
    n_keys = K_ROWS * GRID_W
    for b in range(q_ref.shape[0]):
        kw = k_ref[b, pl.ds(start, n_keys), :]
        vw = v_ref[b, pl.ds(start, n_keys), :]
        o = _head_pair_attention(
            q_ref[b], [kw, ck_ref[b]], [vw, cv_ref[b]],
            [(bias_ref[0, 0], bias_ref[1, 0]), None])
        o_ref[b] = o.astype(BF16)


def _bias_rows_kernel(rpb_ref, onehot_ref, neg_ref, o_ref):
    o_ref[...] = jnp.dot(rpb_ref[...], onehot_ref[...], precision=HI,
                         preferred_element_type=F32) + neg_ref[...]


def _nbr_bias_table(rpb_l, rows):
    h, n_dr, n_dc = rpb_l.shape
    n_blocks = rows // Q_ROWS
    kr = min(WIN_ROWS, rows)
    variants = np.array([0, min(1, n_blocks - 1), n_blocks - 1])
    r = Q_ROWS * variants[:, None] + np.arange(Q_ROWS)[None, :]
    ks = np.clip(Q_ROWS * variants - WIN_ROWS // 2, 0, rows - K_ROWS)
    key_row = ks[:, None] + np.arange(K_ROWS)[None, :]
    r0 = np.clip(r - kr // 2, 0, rows - kr)
    row_ok = (key_row[:, None, :] >= r0[:, :, None]) & (key_row[:, None, :] < r0[:, :, None] + kr)
    dr = np.clip(key_row[:, None, :] - r[:, :, None] + (WIN_ROWS - 1), 0, n_dr - 1)
    c_idx = np.arange(GRID_W)
    c0 = np.clip(c_idx - WIN_COLS // 2, 0, GRID_W - WIN_COLS)
    col_ok = (c_idx[None, :] >= c0[:, None]) & (c_idx[None, :] < c0[:, None] + WIN_COLS)
    dc = np.clip(c_idx[None, :] - c_idx[:, None] + (WIN_COLS - 1), 0, n_dc - 1)

    k_pad = -(-n_dc // LANES) * LANES
    m_pad = -(-(h * n_dr) // 8) * 8
    onehot = np.zeros((k_pad, GRID_W * GRID_W), np.float32)
    flat = np.arange(GRID_W * GRID_W).reshape(GRID_W, GRID_W)
    onehot[dc[col_ok], flat[col_ok]] = 1.0
    neg = np.where(col_ok, 0.0, NEG_INF).astype(np.float32).reshape(1, -1)
    rpb_pad = jnp.zeros((m_pad, k_pad), F32).at[:h * n_dr, :n_dc].set(
        rpb_l.reshape(h * n_dr, n_dc).astype(F32))
    col_tab = pl.pallas_call(
        _bias_rows_kernel,
        out_shape=jax.ShapeDtypeStruct((m_pad, GRID_W * GRID_W), F32),
        name="bias_rows",
    )(rpb_pad, jnp.asarray(onehot), jnp.asarray(neg))
    col_tab = col_tab[:h * n_dr].reshape(h, n_dr, GRID_W, GRID_W).transpose(0, 2, 1, 3)
    pad = K_ROWS + Q_ROWS
    col_tab = jnp.pad(col_tab, ((0, 0), (0, 0), (pad, pad), (0, 0)))
    col_tab = col_tab.reshape(h, GRID_W, (n_dr + 2 * pad) * GRID_W)
    slabs = []
    for v in range(3):
        for qr in range(Q_ROWS):
            d0 = int(ks[v] - r[v, qr]) + (WIN_ROWS - 1) + pad
            slab = col_tab[:, :, d0 * GRID_W:(d0 + K_ROWS) * GRID_W]
            lane_ok = jnp.asarray(np.repeat(row_ok[v, qr], GRID_W))
            slabs.append(jnp.where(lane_ok[None, None, :], slab, NEG_INF))
    return jnp.stack(slabs, axis=1).reshape(h, 3, Q_ROWS * GRID_W, K_ROWS * GRID_W)


def _nbr_attn(q3, k3, v3, ck3, cv3, bias):
    nb, seq, width = q3.shape
    rows = seq // GRID_W
    n_blocks = rows // Q_ROWS
    n_pairs = width // LANES
    past = ck3.shape[1]
    tq = Q_ROWS * GRID_W
    tk = K_ROWS * GRID_W

    def variant(i):
        return jnp.where(i == 0, 0, jnp.where(i == n_blocks - 1, 2, 1))

    return pl.pallas_call(
        functools.partial(_nbr_attn_kernel, rows=rows),
        grid=(n_pairs, n_blocks),
        in_specs=[
            pl.BlockSpec((nb, tq, LANES), lambda p, i: (0, i, p)),
            pl.BlockSpec((nb, seq, LANES), lambda p, i: (0, 0, p)),
            pl.BlockSpec((nb, seq, LANES), lambda p, i: (0, 0, p)),
            pl.BlockSpec((nb, past, LANES), lambda p, i: (0, 0, p)),
            pl.BlockSpec((nb, past, LANES), lambda p, i: (0, 0, p)),
            pl.BlockSpec((2, 1, tq, tk), lambda p, i: (p, variant(i), 0, 0)),
        ],
        out_specs=pl.BlockSpec((nb, tq, LANES), lambda p, i: (0, i, p)),
        out_shape=jax.ShapeDtypeStruct((nb, seq, width), BF16),
        compiler_params=_params(("arbitrary", "arbitrary")),
        name="nbr_attn",
    )(q3, k3, v3, ck3, cv3, bias)


def _fourier_kernel(x_ref, cdft_ref, d1_ref, d2_ref, twc_ref, tws_ref, o_ref,
                    zr_ref, zi_ref, tr_ref, ti_ref, *, n1, n2):
    gd = cdft_ref.shape[0]
    n_groups = x_ref.shape[2] // gd
    groups = range(n_groups)
    cdft = cdft_ref[...].astype(BF16)
    d1 = d1_ref[...].astype(BF16)
    d2 = d2_ref[...].astype(BF16)
    for g in groups:
        zc = _bdot(x_ref[0, :, g * gd:(g + 1) * gd].astype(BF16), cdft)
        zr_ref[g] = zc[:, :gd]
        zi_ref[g] = zc[:, gd:]

    def stacked(re_ref, im_ref, rows):
        return jnp.concatenate(
            [jnp.concatenate([re_ref[g, rows, :], im_ref[g, rows, :]], axis=0) for g in groups],
            axis=1)

    def stage1(c, carry):
        t = _bdot(d1, stacked(zr_ref, zi_ref, pl.ds(c, n1, stride=n2)).astype(BF16))
        cs = twc_ref[c]
        sn = tws_ref[c]
        row = pl.multiple_of(c * n1, n1)
        for g in groups:
            tr = t[:n1, g * gd:(g + 1) * gd]
            ti = t[n1:, g * gd:(g + 1) * gd]
            tr_ref[g, pl.ds(row, n1), :] = tr * cs + ti * sn
            ti_ref[g, pl.ds(row, n1), :] = ti * cs - tr * sn
        return carry

    lax.fori_loop(0, n2, stage1, 0, unroll=LOOP_UNROLL)

    def stage2(k1, carry):
        rows = pl.ds(k1, n2, stride=n1)
        y = _bdot(d2, stacked(tr_ref, ti_ref, rows).astype(BF16))
        for g in groups:
            zr_ref[g, rows, :] = y[:, g * gd:(g + 1) * gd]
        return carry

    lax.fori_loop(0, n1, stage2, 0, unroll=LOOP_UNROLL)
    for g in groups:
        o_ref[0, :, g * gd:(g + 1) * gd] = zr_ref[g]


def _fourier(f3):
    nb, seq, width = f3.shape
    n2 = GRID_W
    n1 = seq // n2
    gd = FNET_GROUP_DIM
    cc, cs = _dft_mats(gd)
    c1, s1 = _dft_mats(n1)
    c2, s2 = _dft_mats(n2)
    norm = 1.0 / math.sqrt(seq * gd)
    cdft = jnp.asarray(np.concatenate([cc, -cs], axis=1) * norm, F32)
    d1 = jnp.asarray(np.block([[c1, s1], [-s1, c1]]), F32)
    d2 = jnp.asarray(np.concatenate([c2, s2], axis=1), F32)
    ang = 2.0 * np.pi * ((np.arange(n2)[:, None] * np.arange(n1)[None, :]) % seq) / seq
    twc = jnp.asarray(np.broadcast_to(np.cos(ang)[:, :, None], (n2, n1, gd)), F32)
    tws = jnp.asarray(np.broadcast_to(np.sin(ang)[:, :, None], (n2, n1, gd)), F32)
    return pl.pallas_call(
        functools.partial(_fourier_kernel, n1=n1, n2=n2),
        grid=(nb, width // FOURIER_COLS),
        in_specs=[
            pl.BlockSpec((1, seq, FOURIER_COLS), lambda b, g: (b, 0, g)),
            _const_spec(cdft.shape),
            _const_spec(d1.shape),
            _const_spec(d2.shape),
            _const_spec(twc.shape),
            _const_spec(tws.shape),
        ],
        out_specs=pl.BlockSpec((1, seq, FOURIER_COLS), lambda b, g: (b, 0, g)),
        out_shape=jax.ShapeDtypeStruct((nb, seq, width), F32),
        scratch_shapes=[pltpu.VMEM((FOURIER_COLS // gd, seq, gd), F32)] * 4,
        compiler_params=_params(("arbitrary", "arbitrary")),
        name="fourier",
    )(f3, cdft, d1, d2, twc, tws)


def _merge_kernel(fm_ref, o_ref_in, x_ref, m_ref, wgate_ref, wf_ref, wna_ref, wout_ref,
                  g_ref, b_ref, out_ref, *, alpha):
    d = x_ref.shape[1]
    x = x_ref[...]
    shift = m_ref[0, 3:4, :]
    scale = m_ref[0, 4:5, :]
    gate = m_ref[0, 5:6, :]
    u = (x * (1.0 + scale) + shift).astype(BF16)
    ga = _bdot(u, wgate_ref[:, :d])
    gb = _bdot(u, wgate_ref[:, d:])
    out_ref[...] = _merge_out(fm_ref[...], o_ref_in[...], ga, gb, x, gate, wf_ref, wna_ref,
                              wout_ref, g_ref[1:2, :], b_ref[1:2, :], alpha)


def _merge(fm2, o2, x2, mods_l, rows_per_group, group0, win, wf, wna, wout, ln_g, ln_b, alpha):
    rows, d = x2.shape
    tm = min(ROW_TILE, rows_per_group)
    tiles_per_group = rows_per_group // tm
    row_spec = lambda w: pl.BlockSpec((tm, w), lambda i: (i, 0))
    assert win.shape[1] == 2 * (2 * d), "gate columns must be the second half of w_in"
    return pl.pallas_call(
        functools.partial(_merge_kernel, alpha=alpha),
        grid=(rows // tm,),
        in_specs=[
            row_spec(fm2.shape[1]), row_spec(o2.shape[1]), row_spec(d),
            pl.BlockSpec((1, N_MOD, d), lambda i: (group0 + i // tiles_per_group, 0, 0)),
            pl.BlockSpec((d, 2 * d), lambda i: (0, 1), pipeline_mode=pl.Buffered(1)),
            _const_spec(wf.shape), _const_spec(wna.shape), _const_spec(wout.shape),
            _const_spec(ln_g.shape), _const_spec(ln_b.shape),
        ],
        out_specs=row_spec(d),
        out_shape=jax.ShapeDtypeStruct((rows, d), F32),
        compiler_params=_params(("arbitrary",)),
        name="merge",
    )(fm2, o2, x2, mods_l, win, wf, wna, wout, ln_g, ln_b)


def kernel(x_prompt, x_sample, cache_k, cache_v, c, c_ctx, w_ada, b_ada, ln_g, ln_b, w_ff1_up,
           w_ff1_down, w_in, rpb, w_fourier, w_na_out, w_out, w_ff2_up, w_ff2_down):
    batch, seq, d = x_prompt.shape
    dec_batch, dec_seq, _ = x_sample.shape
    depth = w_ada.shape[0]
    alpha = (2 * depth) ** 0.25
    rows_lat = dec_seq // GRID_W

    cvec = jnp.zeros((MOD_ROWS, d), F32).at[0].set(c_ctx).at[1:1 + dec_batch].set(c)
    mods = _mods(cvec, w_ada, b_ada).reshape(depth, MOD_ROWS, N_MOD, d)

    y_p = x_prompt.reshape(batch * seq, d)
    y_s = x_sample.reshape(dec_batch * dec_seq, d)
    kv = None
    for l in range(depth):
        m_l = mods[l]
        up1, dn1 = w_ff1_up[l].astype(BF16), w_ff1_down[l].astype(BF16)
        up2, dn2 = w_ff2_up[l].astype(BF16), w_ff2_down[l].astype(BF16)
        win = w_in[l].astype(BF16)
        wf = w_fourier[l].astype(BF16)
        wna = w_na_out[l].astype(BF16)
        wout = w_out[l].astype(BF16)
        g_l, b_l = ln_g[l], ln_b[l]

        y_p = _ffn(y_p, m_l, batch * seq, 0, up1, dn1, g_l, b_l, 0, alpha)
        y_p3, new_k, new_v = _ctx_mix(y_p.reshape(batch, seq, d), m_l, win, wf, wna, wout, g_l, b_l,
                                      alpha, l, depth, kv)
        kv = (new_k, new_v)
        y_p = _ffn(y_p3.reshape(batch * seq, d), m_l, batch * seq, 0, up2, dn2, g_l, b_l, 2, alpha)

        y_s = _ffn(y_s, m_l, dec_seq, 1, up1, dn1, g_l, b_l, 0, alpha)
        q2, k2, v2, f2 = _in_proj(y_s, m_l, dec_seq, 1, win)
        to3 = lambda t: t.reshape(dec_batch, dec_seq, t.shape[-1])
        ck3 = cache_k[:, l].transpose(0, 2, 1, 3).reshape(dec_batch, -1, NA_WIDTH).astype(BF16)
        cv3 = cache_v[:, l].transpose(0, 2, 1, 3).reshape(dec_batch, -1, NA_WIDTH).astype(BF16)
        bias = _nbr_bias_table(rpb[l], rows_lat)
        o3 = _nbr_attn(to3(q2), to3(k2), to3(v2), ck3, cv3, bias)
        fm3 = _fourier(to3(f2))
        y_s = _merge(fm3.reshape(-1, FNET_WIDTH), o3.reshape(-1, NA_WIDTH), y_s, m_l,
                     dec_seq, 1, win, wf, wna, wout, g_l, b_l, alpha)
        y_s = _ffn(y_s, m_l, dec_seq, 1, up2, dn2, g_l, b_l, 2, alpha)

    return (y_p.reshape(batch, seq, d), y_s.reshape(dec_batch, dec_seq, d), new_k, new_v)
```

```python
import functools
import math

import numpy as np
import jax
import jax.numpy as jnp
from jax import lax
from jax.experimental import pallas as pl
from jax.experimental.pallas import tpu as pltpu

F32 = jnp.float32
BF16 = jnp.bfloat16

HEAD_DIM = 64
NA_HEADS = 8
NA_WIDTH = NA_HEADS * HEAD_DIM
FNET_GROUPS = 4
FNET_GROUP_DIM = 128
FNET_WIDTH = FNET_GROUPS * FNET_GROUP_DIM
GRID_W = 64
WIN_ROWS = 8
WIN_COLS = 16
N_SUB = 3
N_MOD = 3 * N_SUB
ATTN_SCALE = HEAD_DIM ** -0.5
LN_EPS = 1e-5
NEG_INF = -1e30

LANES = 128
MXU_DIM = 256
VMEM_LIMIT_BYTES = 56 * 1024 * 1024

ROW_TILE = 512
FFN_ROW_TILE = 1024
FFN_SUB_ROWS = 512
FF_CHUNK = MXU_DIM
Q_ROWS = 4
K_ROWS = Q_ROWS + WIN_ROWS
IN_CHUNK = 2 * MXU_DIM
MOD_ROWS = 8
FOURIER_COLS = 2 * FNET_GROUP_DIM
LOOP_UNROLL = 4
HI = lax.Precision.HIGHEST


def _params(sem):
    return pltpu.CompilerParams(dimension_semantics=sem, vmem_limit_bytes=VMEM_LIMIT_BYTES)


def _const_spec(shape):
    nd = len(shape)
    return pl.BlockSpec(shape, lambda *_: (0,) * nd, pipeline_mode=pl.Buffered(1))


def _layer_norm(r, g, b):
    mu = jnp.mean(r, axis=-1, keepdims=True)
    d = r - mu
    var = jnp.mean(d * d, axis=-1, keepdims=True)
    return d * lax.rsqrt(var + LN_EPS) * g + b


def _bdot(a, b):
    return jnp.dot(a, b, preferred_element_type=F32)


def _dot_nt(a, b):
    return lax.dot_general(a, b, (((1,), (1,)), ((), ())), preferred_element_type=F32)


def _dft_mats(n):
    k = np.arange(n)
    ang = 2.0 * np.pi * ((k[:, None] * k[None, :]) % n) / n
    return np.cos(ang), np.sin(ang)


def _mods_kernel(c_ref, w_ref, b_ref, o_ref):
    c = c_ref[...]
    s = (c * jax.nn.sigmoid(c)).astype(BF16)
    o_ref[0] = _bdot(s, w_ref[0].astype(BF16)) + b_ref[0]


def _mods(cvec, w_ada, b_ada):
    depth, d, n = w_ada.shape
    tn = d
    return pl.pallas_call(
        _mods_kernel,
        grid=(depth, n // tn),
        in_specs=[
            pl.BlockSpec((MOD_ROWS, d), lambda l, j: (0, 0)),
            pl.BlockSpec((1, d, tn), lambda l, j: (l, 0, j)),
            pl.BlockSpec((1, 1, tn), lambda l, j: (l, 0, j)),
        ],
        out_specs=pl.BlockSpec((1, MOD_ROWS, tn), lambda l, j: (l, 0, j)),
        out_shape=jax.ShapeDtypeStruct((depth, MOD_ROWS, n), F32),
        compiler_params=_params(("arbitrary", "arbitrary")),
        name="mods",
    )(cvec, w_ada, b_ada.reshape(depth, 1, n))


def _ffn_kernel(x_ref, m_ref, wup_ref, wdn_ref, g_ref, b_ref, o_ref, act_ref, *, sub, alpha):
    shift = m_ref[0, 3 * sub:3 * sub + 1, :]
    scale = m_ref[0, 3 * sub + 1:3 * sub + 2, :]
    gate = m_ref[0, 3 * sub + 2:3 * sub + 3, :]
    ff = wdn_ref.shape[0]
    tm = x_ref.shape[0]
    sr = min(tm, FFN_SUB_ROWS)
    slabs = [slice(r0, r0 + sr) for r0 in range(0, tm, sr)]
    us = [(x_ref[rs, :] * (1.0 + scale) + shift).astype(BF16) for rs in slabs]
    for c0 in range(0, ff, FF_CHUNK):
        for rs, u in zip(slabs, us):
            a = _bdot(u, wup_ref[:, c0:c0 + FF_CHUNK])
            g = _bdot(u, wup_ref[:, ff + c0:ff + c0 + FF_CHUNK])
            act_ref[rs, c0:c0 + FF_CHUNK] = ((g * jax.nn.sigmoid(g)) * a).astype(BF16)
    for rs in slabs:
        y = _bdot(act_ref[rs, :], wdn_ref[...])
        r = alpha * x_ref[rs, :] + (0.5 * gate) * y
        o_ref[rs, :] = _layer_norm(r, g_ref[sub:sub + 1, :], b_ref[sub:sub + 1, :])


def _ffn(x2, mods_l, rows_per_group, group0, wup, wdn, ln_g, ln_b, sub, alpha):
    rows, d = x2.shape
    tm = min(FFN_ROW_TILE, rows_per_group)
    tiles_per_group = rows_per_group // tm
    return pl.pallas_call(
        functools.partial(_ffn_kernel, sub=sub, alpha=alpha),
        grid=(rows // tm,),
        in_specs=[
            pl.BlockSpec((tm, d), lambda i: (i, 0)),
            pl.BlockSpec((1, N_MOD, d), lambda i: (group0 + i // tiles_per_group, 0, 0)),
            _const_spec(wup.shape),
            _const_spec(wdn.shape),
            _const_spec(ln_g.shape),
            _const_spec(ln_b.shape),
        ],
        out_specs=pl.BlockSpec((tm, d), lambda i: (i, 0)),
        out_shape=jax.ShapeDtypeStruct((rows, d), F32),
        scratch_shapes=[pltpu.VMEM((tm, wdn.shape[0]), BF16)],
        compiler_params=_params(("arbitrary",)),
        name=f"ffn{sub}",
    )(x2, mods_l, wup, wdn, ln_g, ln_b)


def _head_pair_attention(q, k_list, v_list, bias_list, n_pairs_lanes=LANES):
    lane = lax.broadcasted_iota(jnp.int32, (1, n_pairs_lanes), 1)
    outs = []
    for h in range(2):
        in_head = (lane // HEAD_DIM) == h
        qm = jnp.where(in_head, q, jnp.zeros_like(q))
        s = []
        for k, bias in zip(k_list, bias_list):
            sk = _dot_nt(qm, k)
            if bias is not None:
                sk = sk + bias[h]
            s.append(sk)
        m = s[0].max(axis=-1, keepdims=True)
        for sk in s[1:]:
            m = jnp.maximum(m, sk.max(axis=-1, keepdims=True))
        den = None
        o = None
        for sk, v in zip(s, v_list):
            p = jnp.exp(sk - m)
            ps = p.sum(axis=-1, keepdims=True)
            den = ps if den is None else den + ps
            pv = _bdot(p.astype(BF16), v)
            o = pv if o is None else o + pv
        outs.append(o / den)
    return jnp.where(lane < HEAD_DIM, outs[0], outs[1])


def _merge_out(fm, o, ga, gb, x, shift_gate, wf_ref, wna_ref, wout_ref, ln_g, ln_b, alpha):
    branch_a = _bdot(fm.astype(BF16), wf_ref[...])
    branch_b = _bdot(o, wna_ref[...])
    mg = jax.nn.sigmoid(ga) * branch_a + jax.nn.sigmoid(gb) * branch_b
    mix = _bdot(mg.astype(BF16), wout_ref[...])
    r = alpha * x + shift_gate * mix
    return _layer_norm(r, ln_g, ln_b)


def _ctx_mix_kernel(x_ref, m_ref, win_ref, wf_ref, wna_ref, wout_ref, g_ref, b_ref,
                    cdft_ref, pdft_ref, *rest, alpha, n_alias, kv_slot):
    o_ref, k_ref, v_ref, z_ref, oatt_ref, fm_ref = rest[n_alias:]
    nb, seq, d = x_ref.shape
    rows = nb * seq
    x = x_ref[...].reshape(rows, d)
    shift = m_ref[0, 3:4, :]
    scale = m_ref[0, 4:5, :]
    gate = m_ref[0, 5:6, :]
    u = (x * (1.0 + scale) + shift).astype(BF16)
    for c0 in range(0, win_ref.shape[1], IN_CHUNK):
        z_ref[:, c0:c0 + IN_CHUNK] = _bdot(u, win_ref[:, c0:c0 + IN_CHUNK])

    cdft = cdft_ref[...].astype(BF16)
    pdft = pdft_ref[...].astype(BF16)
    q_off, k_off, v_off, f_off = 0, NA_WIDTH, 2 * NA_WIDTH, 3 * NA_WIDTH
    ga_off = f_off + FNET_WIDTH
    gb_off = ga_off + d
    for b in range(nb):
        r0 = b * seq
        for h in range(NA_HEADS):
            k_ref[b, kv_slot, h] = z_ref[r0:r0 + seq, k_off + h * HEAD_DIM:k_off + (h + 1) * HEAD_DIM]
            v_ref[b, kv_slot, h] = z_ref[r0:r0 + seq, v_off + h * HEAD_DIM:v_off + (h + 1) * HEAD_DIM]
        for slot in range(k_ref.shape[1]):
            if slot != kv_slot:
                k_ref[b, slot] = jnp.zeros(k_ref.shape[2:], F32)
                v_ref[b, slot] = jnp.zeros(v_ref.shape[2:], F32)
        for hp in range(NA_WIDTH // LANES):
            c0 = hp * LANES
            q = (z_ref[r0:r0 + seq, q_off + c0:q_off + c0 + LANES] * ATTN_SCALE).astype(BF16)
            k = z_ref[r0:r0 + seq, k_off + c0:k_off + c0 + LANES].astype(BF16)
            v = z_ref[r0:r0 + seq, v_off + c0:v_off + c0 + LANES].astype(BF16)
            o = _head_pair_attention(q, [k], [v], [None])
            oatt_ref[r0:r0 + seq, c0:c0 + LANES] = o.astype(BF16)
        for g in range(FNET_GROUPS):
            c0 = f_off + g * FNET_GROUP_DIM
            xg = z_ref[r0:r0 + seq, c0:c0 + FNET_GROUP_DIM]
            zc = _bdot(xg.astype(BF16), cdft).astype(BF16)
            stacked = jnp.concatenate([zc[:, :FNET_GROUP_DIM], zc[:, FNET_GROUP_DIM:]], axis=0)
            fm_ref[r0:r0 + seq, g * FNET_GROUP_DIM:(g + 1) * FNET_GROUP_DIM] = _bdot(
                pdft, stacked)

    out = _merge_out(fm_ref[...], oatt_ref[...], z_ref[:, ga_off:ga_off + d],
                     z_ref[:, gb_off:gb_off + d], x, gate, wf_ref, wna_ref, wout_ref,
                     g_ref[1:2, :], b_ref[1:2, :], alpha)
    o_ref[...] = out.reshape(nb, seq, d)


def _ctx_mix(x3, mods_l, win, wf, wna, wout, ln_g, ln_b, alpha, layer, depth, kv_prev):
    batch, seq, d = x3.shape
    nb = 2 if batch % 2 == 0 else 1
    in_width = win.shape[1]
    cc, cs = _dft_mats(FNET_GROUP_DIM)
    pc, ps = _dft_mats(seq)
    norm = 1.0 / math.sqrt(seq * FNET_GROUP_DIM)
    cdft = jnp.asarray(np.concatenate([cc, -cs], axis=1) * norm, F32)
    pdft = jnp.asarray(np.concatenate([pc, ps], axis=1), F32)
    kv_shape = jax.ShapeDtypeStruct((batch, depth, NA_HEADS, seq, HEAD_DIM), F32)
    if kv_prev is None:
        kv_spec = pl.BlockSpec((nb, depth, NA_HEADS, seq, HEAD_DIM), lambda i: (i, 0, 0, 0, 0))
        kv_slot = layer
    else:
        kv_spec = pl.BlockSpec((nb, 1, NA_HEADS, seq, HEAD_DIM), lambda i: (i, layer, 0, 0, 0))
        kv_slot = 0
    operands = [x3, mods_l, win, wf, wna, wout, ln_g, ln_b, cdft, pdft]
    in_specs = [
        pl.BlockSpec((nb, seq, d), lambda i: (i, 0, 0)),
        pl.BlockSpec((1, N_MOD, d), lambda i: (0, 0, 0)),
        _const_spec(win.shape),
        _const_spec(wf.shape),
        _const_spec(wna.shape),
        _const_spec(wout.shape),
        _const_spec(ln_g.shape),
        _const_spec(ln_b.shape),
        _const_spec(cdft.shape),
        _const_spec(pdft.shape),
    ]
    aliases = {}
    if kv_prev is not None:
        aliases = {len(operands): 1, len(operands) + 1: 2}
        operands += list(kv_prev)
        in_specs += [pl.BlockSpec(memory_space=pl.ANY)] * 2
    return pl.pallas_call(
        functools.partial(_ctx_mix_kernel, alpha=alpha, n_alias=len(aliases), kv_slot=kv_slot),
        grid=(batch // nb,),
        in_specs=in_specs,
        out_specs=[pl.BlockSpec((nb, seq, d), lambda i: (i, 0, 0)), kv_spec, kv_spec],
        out_shape=[jax.ShapeDtypeStruct((batch, seq, d), F32), kv_shape, kv_shape],
        input_output_aliases=aliases,
        scratch_shapes=[
            pltpu.VMEM((nb * seq, in_width), F32),
            pltpu.VMEM((nb * seq, NA_WIDTH), BF16),
            pltpu.VMEM((nb * seq, FNET_WIDTH), F32),
        ],
        compiler_params=_params(("arbitrary",)),
        name="ctx_mix",
    )(*operands)


def _in_proj_kernel(x_ref, m_ref, win_ref, q_ref, k_ref, v_ref, f_ref):
    x = x_ref[...]
    shift = m_ref[0, 3:4, :]
    scale = m_ref[0, 4:5, :]
    u = (x * (1.0 + scale) + shift).astype(BF16)
    cw = NA_WIDTH
    q_ref[...] = (_bdot(u, win_ref[:, 0:cw]) * ATTN_SCALE).astype(BF16)
    k_ref[...] = _bdot(u, win_ref[:, cw:2 * cw]).astype(BF16)
    v_ref[...] = _bdot(u, win_ref[:, 2 * cw:3 * cw]).astype(BF16)
    f_ref[...] = _bdot(u, win_ref[:, 3 * cw:3 * cw + FNET_WIDTH])


def _in_proj(x2, mods_l, rows_per_group, group0, win):
    rows, d = x2.shape
    tm = min(ROW_TILE, rows_per_group)
    tiles_per_group = rows_per_group // tm
    cw = NA_WIDTH
    qkvf = 3 * NA_WIDTH + FNET_WIDTH
    row_spec = lambda w: pl.BlockSpec((tm, w), lambda i: (i, 0))
    return pl.pallas_call(
        _in_proj_kernel,
        grid=(rows // tm,),
        in_specs=[
            row_spec(d),
            pl.BlockSpec((1, N_MOD, d), lambda i: (group0 + i // tiles_per_group, 0, 0)),
            pl.BlockSpec((d, qkvf), lambda i: (0, 0), pipeline_mode=pl.Buffered(1)),
        ],
        out_specs=[row_spec(cw), row_spec(cw), row_spec(cw), row_spec(FNET_WIDTH)],
        out_shape=[
            jax.ShapeDtypeStruct((rows, cw), BF16),
            jax.ShapeDtypeStruct((rows, cw), BF16),
            jax.ShapeDtypeStruct((rows, cw), BF16),
            jax.ShapeDtypeStruct((rows, FNET_WIDTH), F32),
        ],
        compiler_params=_params(("arbitrary",)),
        name="in_proj",
    )(x2, mods_l, win)


def _window_start_row(i, rows):
    return jnp.clip(Q_ROWS * i - WIN_ROWS // 2, 0, rows - K_ROWS)


def _nbr_attn_kernel(q_ref, k_ref, v_ref, ck_ref, cv_ref, bias_ref, o_ref, *, rows):
    i = pl.program_id(1)
    start = pl.multiple_of(_window_start_row(i, rows) * GRID_W, GRID_W * math.gcd(Q_ROWS, WIN_ROWS // 2))
    n_keys = K_ROWS * GRID_W
    for b in range(q_ref.shape[0]):
        kw = k_ref[b, pl.ds(start, n_keys), :]
        vw = v_ref[b, pl.ds(start, n_keys), :]
        o = _head_pair_attention(
            q_ref[b], [kw, ck_ref[b]], [vw, cv_ref[b]],
            [(bias_ref[0, 0], bias_ref[1, 0]), None])
        o_ref[b] = o.astype(BF16)


def _bias_rows_kernel(rpb_ref, onehot_ref, neg_ref, o_ref):
    o_ref[...] = jnp.dot(rpb_ref[...], onehot_ref[...], precision=HI,
                         preferred_element_type=F32) + neg_ref[...]


def _nbr_bias_table(rpb_l, rows):
    h, n_dr, n_dc = rpb_l.shape
    n_blocks = rows // Q_ROWS
    kr = min(WIN_ROWS, rows)
    variants = np.array([0, min(1, n_blocks - 1), n_blocks - 1])
    r = Q_ROWS * variants[:, None] + np.arange(Q_ROWS)[None, :]
    ks = np.clip(Q_ROWS * variants - WIN_ROWS // 2, 0, rows - K_ROWS)
    key_row = ks[:, None] + np.arange(K_ROWS)[None, :]
    r0 = np.clip(r - kr // 2, 0, rows - kr)
    row_ok = (key_row[:, None, :] >= r0[:, :, None]) & (key_row[:, None, :] < r0[:, :, None] + kr)
    dr = np.clip(key_row[:, None, :] - r[:, :, None] + (WIN_ROWS - 1), 0, n_dr - 1)
    c_idx = np.arange(GRID_W)
    c0 = np.clip(c_idx - WIN_COLS // 2, 0, GRID_W - WIN_COLS)
    col_ok = (c_idx[None, :] >= c0[:, None]) & (c_idx[None, :] < c0[:, None] + WIN_COLS)
    dc = np.clip(c_idx[None, :] - c_idx[:, None] + (WIN_COLS - 1), 0, n_dc - 1)

    k_pad = -(-n_dc // LANES) * LANES
    m_pad = -(-(h * n_dr) // 8) * 8
    onehot = np.zeros((k_pad, GRID_W * GRID_W), np.float32)
    flat = np.arange(GRID_W * GRID_W).reshape(GRID_W, GRID_W)
    onehot[dc[col_ok], flat[col_ok]] = 1.0
    neg = np.where(col_ok, 0.0, NEG_INF).astype(np.float32).reshape(1, -1)
    rpb_pad = jnp.zeros((m_pad, k_pad), F32).at[:h * n_dr, :n_dc].set(
        rpb_l.reshape(h * n_dr, n_dc).astype(F32))
    col_tab = pl.pallas_call(
        _bias_rows_kernel,
        out_shape=jax.ShapeDtypeStruct((m_pad, GRID_W * GRID_W), F32),
        name="bias_rows",
    )(rpb_pad, jnp.asarray(onehot), jnp.asarray(neg))
    col_tab = col_tab[:h * n_dr].reshape(h, n_dr, GRID_W, GRID_W).transpose(0, 2, 1, 3)
    pad = K_ROWS + Q_ROWS
    col_tab = jnp.pad(col_tab, ((0, 0), (0, 0), (pad, pad), (0, 0)))
    col_tab = col_tab.reshape(h, GRID_W, (n_dr + 2 * pad) * GRID_W)
    slabs = []
    for v in range(3):
        for qr in range(Q_ROWS):
            d0 = int(ks[v] - r[v, qr]) + (WIN_ROWS - 1) + pad
            slab = col_tab[:, :, d0 * GRID_W:(d0 + K_ROWS) * GRID_W]
            lane_ok = jnp.asarray(np.repeat(row_ok[v, qr], GRID_W))
            slabs.append(jnp.where(lane_ok[None, None, :], slab, NEG_INF))
    return jnp.stack(slabs, axis=1).reshape(h, 3, Q_ROWS * GRID_W, K_ROWS * GRID_W)


def _nbr_attn(q3, k3, v3, ck3, cv3, bias):
    nb, seq, width = q3.shape
    rows = seq // GRID_W
    n_blocks = rows // Q_ROWS
    n_pairs = width // LANES
    past = ck3.shape[1]
    tq = Q_ROWS * GRID_W
    tk = K_ROWS * GRID_W

    def variant(i):
        return jnp.where(i == 0, 0, jnp.where(i == n_blocks - 1, 2, 1))

    return pl.pallas_call(
        functools.partial(_nbr_attn_kernel, rows=rows),
        grid=(n_pairs, n_blocks),
        in_specs=[
            pl.BlockSpec((nb, tq, LANES), lambda p, i: (0, i, p)),
            pl.BlockSpec((nb, seq, LANES), lambda p, i: (0, 0, p)),
            pl.BlockSpec((nb, seq, LANES), lambda p, i: (0, 0, p)),
            pl.BlockSpec((nb, past, LANES), lambda p, i: (0, 0, p)),
            pl.BlockSpec((nb, past, LANES), lambda p, i: (0, 0, p)),
            pl.BlockSpec((2, 1, tq, tk), lambda p, i: (p, variant(i), 0, 0)),
        ],
        out_specs=pl.BlockSpec((nb, tq, LANES), lambda p, i: (0, i, p)),
        out_shape=jax.ShapeDtypeStruct((nb, seq, width), BF16),
        compiler_params=_params(("arbitrary", "arbitrary")),
        name="nbr_attn",
    )(q3, k3, v3, ck3, cv3, bias)


def _fourier_kernel(x_ref, cdft_ref, d1_ref, d2_ref, twc_ref, tws_ref, o_ref,
                    zr_ref, zi_ref, tr_ref, ti_ref, *, n1, n2):
    gd = cdft_ref.shape[0]
    n_groups = x_ref.shape[2] // gd
    groups = range(n_groups)
    cdft = cdft_ref[...].astype(BF16)
    d1 = d1_ref[...].astype(BF16)
    d2 = d2_ref[...].astype(BF16)
    for g in groups:
        zc = _bdot(x_ref[0, :, g * gd:(g + 1) * gd].astype(BF16), cdft)
        zr_ref[g] = zc[:, :gd]
        zi_ref[g] = zc[:, gd:]

    def stacked(re_ref, im_ref, rows):
        return jnp.concatenate(
            [jnp.concatenate([re_ref[g, rows, :], im_ref[g, rows, :]], axis=0) for g in groups],
            axis=1)

    def stage1(c, carry):
        t = _bdot(d1, stacked(zr_ref, zi_ref, pl.ds(c, n1, stride=n2)).astype(BF16))
        cs = twc_ref[c]
        sn = tws_ref[c]
        row = pl.multiple_of(c * n1, n1)
        for g in groups:
            tr = t[:n1, g * gd:(g + 1) * gd]
            ti = t[n1:, g * gd:(g + 1) * gd]
            tr_ref[g, pl.ds(row, n1), :] = tr * cs + ti * sn
            ti_ref[g, pl.ds(row, n1), :] = ti * cs - tr * sn
        return carry

    lax.fori_loop(0, n2, stage1, 0, unroll=LOOP_UNROLL)

    def stage2(k1, carry):
        rows = pl.ds(k1, n2, stride=n1)
        y = _bdot(d2, stacked(tr_ref, ti_ref, rows).astype(BF16))
        for g in groups:
            zr_ref[g, rows, :] = y[:, g * gd:(g + 1) * gd]
        return carry

    lax.fori_loop(0, n1, stage2, 0, unroll=LOOP_UNROLL)
    for g in groups:
        o_ref[0, :, g * gd:(g + 1) * gd] = zr_ref[g]


def _fourier(f3):
    nb, seq, width = f3.shape
    n2 = GRID_W
    n1 = seq // n2
    gd = FNET_GROUP_DIM
    cc, cs = _dft_mats(gd)
    c1, s1 = _dft_mats(n1)
    c2, s2 = _dft_mats(n2)
    norm = 1.0 / math.sqrt(seq * gd)
    cdft = jnp.asarray(np.concatenate([cc, -cs], axis=1) * norm, F32)
    d1 = jnp.asarray(np.block([[c1, s1], [-s1, c1]]), F32)
    d2 = jnp.asarray(np.concatenate([c2, s2], axis=1), F32)
    ang = 2.0 * np.pi * ((np.arange(n2)[:, None] * np.arange(n1)[None, :]) % seq) / seq
    twc = jnp.asarray(np.broadcast_to(np.cos(ang)[:, :, None], (n2, n1, gd)), F32)
    tws = jnp.asarray(np.broadcast_to(np.sin(ang)[:, :, None], (n2, n1, gd)), F32)
    return pl.pallas_call(
        functools.partial(_fourier_kernel, n1=n1, n2=n2),
        grid=(nb, width // FOURIER_COLS),
        in_specs=[
            pl.BlockSpec((1, seq, FOURIER_COLS), lambda b, g: (b, 0, g)),
            _const_spec(cdft.shape),
            _const_spec(d1.shape),
            _const_spec(d2.shape),
            _const_spec(twc.shape),
            _const_spec(tws.shape),
        ],
        out_specs=pl.BlockSpec((1, seq, FOURIER_COLS), lambda b, g: (b, 0, g)),
        out_shape=jax.ShapeDtypeStruct((nb, seq, width), F32),
        scratch_shapes=[pltpu.VMEM((FOURIER_COLS // gd, seq, gd), F32)] * 4,
        compiler_params=_params(("arbitrary", "arbitrary")),
        name="fourier",
    )(f3, cdft, d1, d2, twc, tws)


def _merge_kernel(fm_ref, o_ref_in, x_ref, m_ref, wgate_ref, wf_ref, wna_ref, wout_ref,
                  g_ref, b_ref, out_ref, *, alpha):
    d = x_ref.shape[1]
    x = x_ref[...]
    shift = m_ref[0, 3:4, :]
    scale = m_ref[0, 4:5, :]
    gate = m_ref[0, 5:6, :]
    u = (x * (1.0 + scale) + shift).astype(BF16)
    ga = _bdot(u, wgate_ref[:, :d])
    gb = _bdot(u, wgate_ref[:, d:])
    out_ref[...] = _merge_out(fm_ref[...], o_ref_in[...], ga, gb, x, gate, wf_ref, wna_ref,
                              wout_ref, g_ref[1:2, :], b_ref[1:2, :], alpha)


def _merge(fm2, o2, x2, mods_l, rows_per_group, group0, win, wf, wna, wout, ln_g, ln_b, alpha):
    rows, d = x2.shape
    tm = min(ROW_TILE, rows_per_group)
    tiles_per_group = rows_per_group // tm
    row_spec = lambda w: pl.BlockSpec((tm, w), lambda i: (i, 0))
    assert win.shape[1] == 2 * (2 * d), "gate columns must be the second half of w_in"
    return pl.pallas_call(
        functools.partial(_merge_kernel, alpha=alpha),
        grid=(rows // tm,),
        in_specs=[
            row_spec(fm2.shape[1]), row_spec(o2.shape[1]), row_spec(d),
            pl.BlockSpec((1, N_MOD, d), lambda i: (group0 + i // tiles_per_group, 0, 0)),
            pl.BlockSpec((d, 2 * d), lambda i: (0, 1), pipeline_mode=pl.Buffered(1)),
            _const_spec(wf.shape), _const_spec(wna.shape), _const_spec(wout.shape),
            _const_spec(ln_g.shape), _const_spec(ln_b.shape),
        ],
        out_specs=row_spec(d),
        out_shape=jax.ShapeDtypeStruct((rows, d), F32),
        compiler_params=_params(("arbitrary",)),
        name="merge",
    )(fm2, o2, x2, mods_l, win, wf, wna, wout, ln_g, ln_b)


def kernel(x_prompt, x_sample, cache_k, cache_v, c, c_ctx, w_ada, b_ada, ln_g, ln_b, w_ff1_up,
           w_ff1_down, w_in, rpb, w_fourier, w_na_out, w_out, w_ff2_up, w_ff2_down):
    batch, seq, d = x_prompt.shape
    dec_batch, dec_seq, _ = x_sample.shape
    depth = w_ada.shape[0]
    alpha = (2 * depth) ** 0.25
    rows_lat = dec_seq // GRID_W

    cvec = jnp.zeros((MOD_ROWS, d), F32).at[0].set(c_ctx).at[1:1 + dec_batch].set(c)
    mods = _mods(cvec, w_ada, b_ada).reshape(depth, MOD_ROWS, N_MOD, d)

    y_p = x_prompt.reshape(batch * seq, d)
    y_s = x_sample.reshape(dec_batch * dec_seq, d)
    kv = None
    for l in range(depth):
        m_l = mods[l]
        up1, dn1 = w_ff1_up[l].astype(BF16), w_ff1_down[l].astype(BF16)
        up2, dn2 = w_ff2_up[l].astype(BF16), w_ff2_down[l].astype(BF16)
        win = w_in[l].astype(BF16)
        wf = w_fourier[l].astype(BF16)
        wna = w_na_out[l].astype(BF16)
        wout = w_out[l].astype(BF16)
        g_l, b_l = ln_g[l], ln_b[l]

        y_p = _ffn(y_p, m_l, batch * seq, 0, up1, dn1, g_l, b_l, 0, alpha)
        y_p3, new_k, new_v = _ctx_mix(y_p.reshape(batch, seq, d), m_l, win, wf, wna, wout, g_l, b_l,
                                      alpha, l, depth, kv)
        kv = (new_k, new_v)
        y_p = _ffn(y_p3.reshape(batch * seq, d), m_l, batch * seq, 0, up2, dn2, g_l, b_l, 2, alpha)

        y_s = _ffn(y_s, m_l, dec_seq, 1, up1, dn1, g_l, b_l, 0, alpha)
        q2, k2, v2, f2 = _in_proj(y_s, m_l, dec_seq, 1, win)
        to3 = lambda t: t.reshape(dec_batch, dec_seq, t.shape[-1])
        ck3 = cache_k[:, l].transpose(0, 2, 1, 3).reshape(dec_batch, -1, NA_WIDTH).astype(BF16)
        cv3 = cache_v[:, l].transpose(0, 2, 1, 3).reshape(dec_batch, -1, NA_WIDTH).astype(BF16)
        bias = _nbr_bias_table(rpb[l], rows_lat)
        o3 = _nbr_attn(to3(q2), to3(k2), to3(v2), ck3, cv3, bias)
        fm3 = _fourier(to3(f2))
        y_s = _merge(fm3.reshape(-1, FNET_WIDTH), o3.reshape(-1, NA_WIDTH), y_s, m_l,
                     dec_seq, 1, win, wf, wna, wout, g_l, b_l, alpha)
        y_s = _ffn(y_s, m_l, dec_seq, 1, up2, dn2, g_l, b_l, 2, alpha)

    return (y_p.reshape(batch, seq, d), y_s.reshape(dec_batch, dec_seq, d), new_k, new_v)
```

```python
import functools
import math

import numpy as np
import jax
import jax.numpy as jnp
from jax import lax
from jax.experimental import pallas as pl
from jax.experimental.pallas import tpu as pltpu

F32 = jnp.float32
BF16 = jnp.bfloat16

HEAD_DIM = 64
NA_HEADS = 8
NA_WIDTH = NA_HEADS * HEAD_DIM
FNET_GROUPS = 4
FNET_GROUP_DIM = 128
FNET_WIDTH = FNET_GROUPS * FNET_GROUP_DIM
GRID_W = 64
WIN_ROWS = 8
WIN_COLS = 16
N_SUB = 3
N_MOD = 3 * N_SUB
ATTN_SCALE = HEAD_DIM ** -0.5
LN_EPS = 1e-5
NEG_INF = -1e30

LANES = 128
MXU_DIM = 256
VMEM_LIMIT_BYTES = 56 * 1024 * 1024

ROW_TILE = 512
FFN_ROW_TILE = 1024
FFN_SUB_ROWS = 512
FF_CHUNK = MXU_DIM
Q_ROWS = 4
K_ROWS = Q_ROWS + WIN_ROWS
KEY_BLOCK = GRID_W * math.gcd(Q_ROWS, WIN_ROWS // 2)
IN_CHUNK = 2 * MXU_DIM
MOD_ROWS = 8
FOURIER_COLS = 2 * FNET_GROUP_DIM
LOOP_UNROLL = 4
HI = lax.Precision.HIGHEST


def _params(sem):
    return pltpu.CompilerParams(dimension_semantics=sem, vmem_limit_bytes=VMEM_LIMIT_BYTES)


def _const_spec(shape):
    nd = len(shape)
    return pl.BlockSpec(shape, lambda *_: (0,) * nd, pipeline_mode=pl.Buffered(1))


def _layer_spec(w, layer, cols=None, col_block=0):
    _, r, n = w.shape
    cols = n if cols is None else cols
    return pl.BlockSpec((None, r, cols), lambda *_: (layer, 0, col_block),
                        pipeline_mode=pl.Buffered(1))


def _layer_norm(r, g, b):
    mu = jnp.mean(r, axis=-1, keepdims=True)
    d = r - mu
    var = jnp.mean(d * d, axis=-1, keepdims=True)
    return d * lax.rsqrt(var + LN_EPS) * g + b


def _bdot(a, b):
    return jnp.dot(a, b, preferred_element_type=F32)


def _dot_nt(a, b):
    return lax.dot_general(a, b, (((1,), (1,)), ((), ())), preferred_element_type=F32)


def _dft_mats(n):
    k = np.arange(n)
    ang = 2.0 * np.pi * ((k[:, None] * k[None, :]) % n) / n
    return np.cos(ang), np.sin(ang)


def _mods_kernel(c_ref, w_ref, b_ref, o_ref):
    c = c_ref[...]
    s = (c * jax.nn.sigmoid(c)).astype(BF16)
    o_ref[0] = _bdot(s, w_ref[0].astype(BF16)) + b_ref[0]


def _mods(cvec, w_ada, b_ada):
    depth, d, n = w_ada.shape
    tn = d
    return pl.pallas_call(
        _mods_kernel,
        grid=(depth, n // tn),
        in_specs=[
            pl.BlockSpec((MOD_ROWS, d), lambda l, j: (0, 0)),
            pl.BlockSpec((1, d, tn), lambda l, j: (l, 0, j)),
            pl.BlockSpec((1, 1, tn), lambda l, j: (l, 0, j)),
        ],
        out_specs=pl.BlockSpec((1, MOD_ROWS, tn), lambda l, j: (l, 0, j)),
        out_shape=jax.ShapeDtypeStruct((depth, MOD_ROWS, n), F32),
        compiler_params=_params(("arbitrary", "arbitrary")),
        name="mods",
    )(cvec, w_ada, b_ada.reshape(depth, 1, n))


def _ffn_kernel(x_ref, m_ref, wup_ref, wdn_ref, g_ref, b_ref, o_ref, act_ref, *, sub, alpha):
    shift = m_ref[0, 3 * sub:3 * sub + 1, :]
    scale = m_ref[0, 3 * sub + 1:3 * sub + 2, :]
    gate = m_ref[0, 3 * sub + 2:3 * sub + 3, :]
    ff = wdn_ref.shape[0]
    tm = x_ref.shape[0]
    sr = min(tm, FFN_SUB_ROWS)
    slabs = [slice(r0, r0 + sr) for r0 in range(0, tm, sr)]
    us = [(x_ref[rs, :] * (1.0 + scale) + shift).astype(BF16) for rs in slabs]
    for c0 in range(0, ff, FF_CHUNK):
        for rs, u in zip(slabs, us):
            a = _bdot(u, wup_ref[:, c0:c0 + FF_CHUNK])
            g = _bdot(u, wup_ref[:, ff + c0:ff + c0 + FF_CHUNK])
            act_ref[rs, c0:c0 + FF_CHUNK] = ((g * jax.nn.sigmoid(g)) * a).astype(BF16)
    for rs in slabs:
        y = _bdot(act_ref[rs, :], wdn_ref[...])
        r = alpha * x_ref[rs, :] + (0.5 * gate) * y
        o_ref[rs, :] = _layer_norm(r, g_ref[sub:sub + 1, :], b_ref[sub:sub + 1, :])


def _ffn(x2, mods_l, rows_per_group, group0, wup, wdn, layer, ln_g, ln_b, sub, alpha):
    rows, d = x2.shape
    tm = min(FFN_ROW_TILE, rows_per_group)
    tiles_per_group = rows_per_group // tm
    return pl.pallas_call(
        functools.partial(_ffn_kernel, sub=sub, alpha=alpha),
        grid=(rows // tm,),
        in_specs=[
            pl.BlockSpec((tm, d), lambda i: (i, 0)),
            pl.BlockSpec((1, N_MOD, d), lambda i: (group0 + i // tiles_per_group, 0, 0)),
            _layer_spec(wup, layer),
            _layer_spec(wdn, layer),
            _const_spec(ln_g.shape),
            _const_spec(ln_b.shape),
        ],
        out_specs=pl.BlockSpec((tm, d), lambda i: (i, 0)),
        out_shape=jax.ShapeDtypeStruct((rows, d), F32),
        scratch_shapes=[pltpu.VMEM((tm, wdn.shape[1]), BF16)],
        compiler_params=_params(("arbitrary",)),
        name=f"ffn{sub}",
    )(x2, mods_l, wup, wdn, ln_g, ln_b)


def _row_reduce(arrays, combine, reduce):
    acc = None
    for a in arrays:
        for c0 in range(0, a.shape[1], LANES):
            chunk = a[:, c0:c0 + LANES]
            acc = chunk if acc is None else combine(acc, chunk)
    return reduce(acc, axis=-1, keepdims=True)


def _head_pair_attention(q, k_list, v_list, bias_list):
    lane = lax.broadcasted_iota(jnp.int32, (1, LANES), 1)
    outs = []
    for h in range(2):
        in_head = (lane < HEAD_DIM) if h == 0 else (lane >= HEAD_DIM)
        qm = jnp.where(in_head, q, jnp.zeros_like(q))
        s = []
        for k, bias in zip(k_list, bias_list):
            sk = _dot_nt(qm, k)
            if bias is not None:
                sk = sk + bias[h]
            s.append(sk)
        m = _row_reduce(s, jnp.maximum, jnp.max)
        p = [jnp.exp(sk - m) for sk in s]
        den = _row_reduce(p, jnp.add, jnp.sum)
        o = None
        for pk, v in zip(p, v_list):
            pv = _bdot(pk.astype(BF16), v)
            o = pv if o is None else o + pv
        outs.append(o / den)
    return jnp.where(lane < HEAD_DIM, outs[0], outs[1])


def _merge_out(fm, o, ga, gb, x, shift_gate, wf_ref, wna_ref, wout_ref, ln_g, ln_b, alpha):
    branch_a = _bdot(fm.astype(BF16), wf_ref[...])
    branch_b = _bdot(o, wna_ref[...])
    mg = jax.nn.sigmoid(ga) * branch_a + jax.nn.sigmoid(gb) * branch_b
    mix = _bdot(mg.astype(BF16), wout_ref[...])
    r = alpha * x + shift_gate * mix
    return _layer_norm(r, ln_g, ln_b)


def _ctx_mix_kernel(x_ref, m_ref, win_ref, wf_ref, wna_ref, wout_ref, g_ref, b_ref,
                    cdft_ref, pdft_ref, *rest, alpha, n_alias, kv_slot):
    o_ref, k_ref, v_ref, z_ref, oatt_ref, fm_ref = rest[n_alias:]
    nb, seq, d = x_ref.shape
    rows = nb * seq
    x = x_ref[...].reshape(rows, d)
    shift = m_ref[0, 3:4, :]
    scale = m_ref[0, 4:5, :]
    gate = m_ref[0, 5:6, :]
    u = (x * (1.0 + scale) + shift).astype(BF16)
    for c0 in range(0, win_ref.shape[1], IN_CHUNK):
        z_ref[:, c0:c0 + IN_CHUNK] = _bdot(u, win_ref[:, c0:c0 + IN_CHUNK])

    cdft = cdft_ref[...].astype(BF16)
    pdft = pdft_ref[...].astype(BF16)
    q_off, k_off, v_off, f_off = 0, NA_WIDTH, 2 * NA_WIDTH, 3 * NA_WIDTH
    ga_off = f_off + FNET_WIDTH
    gb_off = ga_off + d
    for b in range(nb):
        r0 = b * seq
        for h in range(NA_HEADS):
            k_ref[b, kv_slot, h] = z_ref[r0:r0 + seq, k_off + h * HEAD_DIM:k_off + (h + 1) * HEAD_DIM]
            v_ref[b, kv_slot, h] = z_ref[r0:r0 + seq, v_off + h * HEAD_DIM:v_off + (h + 1) * HEAD_DIM]
        for slot in range(k_ref.shape[1]):
            if slot != kv_slot:
                k_ref[b, slot] = jnp.zeros(k_ref.shape[2:], F32)
                v_ref[b, slot] = jnp.zeros(v_ref.shape[2:], F32)
        for hp in range(NA_WIDTH // LANES):
            c0 = hp * LANES
            q = (z_ref[r0:r0 + seq, q_off + c0:q_off + c0 + LANES] * ATTN_SCALE).astype(BF16)
            k = z_ref[r0:r0 + seq, k_off + c0:k_off + c0 + LANES].astype(BF16)
            v = z_ref[r0:r0 + seq, v_off + c0:v_off + c0 + LANES].astype(BF16)
            o = _head_pair_attention(q, [k], [v], [None])
            oatt_ref[r0:r0 + seq, c0:c0 + LANES] = o.astype(BF16)
        for g in range(FNET_GROUPS):
            c0 = f_off + g * FNET_GROUP_DIM
            xg = z_ref[r0:r0 + seq, c0:c0 + FNET_GROUP_DIM]
            zc = _bdot(xg.astype(BF16), cdft).astype(BF16)
            stacked = jnp.concatenate([zc[:, :FNET_GROUP_DIM], zc[:, FNET_GROUP_DIM:]], axis=0)
            fm_ref[r0:r0 + seq, g * FNET_GROUP_DIM:(g + 1) * FNET_GROUP_DIM] = _bdot(
                pdft, stacked)

    out = _merge_out(fm_ref[...], oatt_ref[...], z_ref[:, ga_off:ga_off + d],
                     z_ref[:, gb_off:gb_off + d], x, gate, wf_ref, wna_ref, wout_ref,
                     g_ref[1:2, :], b_ref[1:2, :], alpha)
    o_ref[...] = out.reshape(nb, seq, d)


def _ctx_mix(x3, mods_l, win, wf, wna, wout, ln_g, ln_b, alpha, layer, depth, kv_prev):
    batch, seq, d = x3.shape
    nb = 2 if batch % 2 == 0 else 1
    in_width = win.shape[2]
    cc, cs = _dft_mats(FNET_GROUP_DIM)
    pc, ps = _dft_mats(seq)
    norm = 1.0 / math.sqrt(seq * FNET_GROUP_DIM)
    cdft = jnp.asarray(np.concatenate([cc, -cs], axis=1) * norm, F32)
    pdft = jnp.asarray(np.concatenate([pc, ps], axis=1), F32)
    kv_shape = jax.ShapeDtypeStruct((batch, depth, NA_HEADS, seq, HEAD_DIM), F32)
    if kv_prev is None:
        kv_spec = pl.BlockSpec((nb, depth, NA_HEADS, seq, HEAD_DIM), lambda i: (i, 0, 0, 0, 0))
        kv_slot = layer
    else:
        kv_spec = pl.BlockSpec((nb, 1, NA_HEADS, seq, HEAD_DIM), lambda i: (i, layer, 0, 0, 0))
        kv_slot = 0
    operands = [x3, mods_l, win, wf, wna, wout, ln_g, ln_b, cdft, pdft]
    in_specs = [
        pl.BlockSpec((nb, seq, d), lambda i: (i, 0, 0)),
        pl.BlockSpec((1, N_MOD, d), lambda i: (0, 0, 0)),
        _layer_spec(win, layer),
        _layer_spec(wf, layer),
        _layer_spec(wna, layer),
        _layer_spec(wout, layer),
        _const_spec(ln_g.shape),
        _const_spec(ln_b.shape),
        _const_spec(cdft.shape),
        _const_spec(pdft.shape),
    ]
    aliases = {}
    if kv_prev is not None:
        aliases = {len(operands): 1, len(operands) + 1: 2}
        operands += list(kv_prev)
        in_specs += [pl.BlockSpec(memory_space=pl.ANY)] * 2
    return pl.pallas_call(
        functools.partial(_ctx_mix_kernel, alpha=alpha, n_alias=len(aliases), kv_slot=kv_slot),
        grid=(batch // nb,),
        in_specs=in_specs,
        out_specs=[pl.BlockSpec((nb, seq, d), lambda i: (i, 0, 0)), kv_spec, kv_spec],
        out_shape=[jax.ShapeDtypeStruct((batch, seq, d), F32), kv_shape, kv_shape],
        input_output_aliases=aliases,
        scratch_shapes=[
            pltpu.VMEM((nb * seq, in_width), F32),
            pltpu.VMEM((nb * seq, NA_WIDTH), BF16),
            pltpu.VMEM((nb * seq, FNET_WIDTH), F32),
        ],
        compiler_params=_params(("arbitrary",)),
        name="ctx_mix",
    )(*operands)


def _in_proj_kernel(x_ref, m_ref, win_ref, q_ref, k_ref, v_ref, f_ref):
    x = x_ref[...]
    shift = m_ref[0, 3:4, :]
    scale = m_ref[0, 4:5, :]
    u = (x * (1.0 + scale) + shift).astype(BF16)
    cw = NA_WIDTH
    q_ref[...] = (_bdot(u, win_ref[:, 0:cw]) * ATTN_SCALE).astype(BF16)
    k_ref[...] = _bdot(u, win_ref[:, cw:2 * cw]).astype(BF16)
    v_ref[...] = _bdot(u, win_ref[:, 2 * cw:3 * cw]).astype(BF16)
    f_ref[...] = _bdot(u, win_ref[:, 3 * cw:3 * cw + FNET_WIDTH])


def _in_proj(x2, mods_l, rows_per_group, group0, win, layer):
    rows, d = x2.shape
    tm = min(ROW_TILE, rows_per_group)
    tiles_per_group = rows_per_group // tm
    cw = NA_WIDTH
    qkvf = 3 * NA_WIDTH + FNET_WIDTH
    row_spec = lambda w: pl.BlockSpec((tm, w), lambda i: (i, 0))
    return pl.pallas_call(
        _in_proj_kernel,
        grid=(rows // tm,),
        in_specs=[
            row_spec(d),
            pl.BlockSpec((1, N_MOD, d), lambda i: (group0 + i // tiles_per_group, 0, 0)),
            _layer_spec(win, layer, cols=qkvf, col_block=0),
        ],
        out_specs=[row_spec(cw), row_spec(cw), row_spec(cw), row_spec(FNET_WIDTH)],
        out_shape=[
            jax.ShapeDtypeStruct((rows, cw), BF16),
            jax.ShapeDtypeStruct((rows, cw), BF16),
            jax.ShapeDtypeStruct((rows, cw), BF16),
            jax.ShapeDtypeStruct((rows, FNET_WIDTH), F32),
        ],
        compiler_params=_params(("arbitrary",)),
        name="in_proj",
    )(x2, mods_l, win)


def _window_start_row(i, rows):
    return jnp.clip(Q_ROWS * i - WIN_ROWS // 2, 0, rows - K_ROWS)


def _nbr_attn_kernel(q_ref, k_ref, v_ref, ck_ref, cv_ref, bias_ref, o_ref, *, rows):
    i = pl.program_id(1)
    start = pl.multiple_of(_window_start_row(i, rows) * GRID_W, KEY_BLOCK)
    n_keys = K_ROWS * GRID_W
    for b in range(q_ref.shape[0]):
        kw = k_ref[b, pl.ds(start, n_keys), :]
        vw = v_ref[b, pl.ds(start, n_keys), :]
        o = _head_pair_attention(
            q_ref[b], [kw, ck_ref[b]], [vw, cv_ref[b]],
            [(bias_ref[0, 0], bias_ref[1, 0]), None])
        o_ref[b] = o.astype(BF16)


def _bias_rows_kernel(rpb_ref, onehot_ref, neg_ref, o_ref):
    o_ref[...] = jnp.dot(rpb_ref[...], onehot_ref[...], precision=HI,
                         preferred_element_type=F32) + neg_ref[...]


def _nbr_bias_table(rpb_l, rows):
    h, n_dr, n_dc = rpb_l.shape
    n_blocks = rows // Q_ROWS
    kr = min(WIN_ROWS, rows)
    variants = np.array([0, min(1, n_blocks - 1), n_blocks - 1])
    r = Q_ROWS * variants[:, None] + np.arange(Q_ROWS)[None, :]
    ks = np.clip(Q_ROWS * variants - WIN_ROWS // 2, 0, rows - K_ROWS)
    key_row = ks[:, None] + np.arange(K_ROWS)[None, :]
    r0 = np.clip(r - kr // 2, 0, rows - kr)
    row_ok = (key_row[:, None, :] >= r0[:, :, None]) & (key_row[:, None, :] < r0[:, :, None] + kr)
    dr = np.clip(key_row[:, None, :] - r[:, :, None] + (WIN_ROWS - 1), 0, n_dr - 1)
    c_idx = np.arange(GRID_W)
    c0 = np.clip(c_idx - WIN_COLS // 2, 0, GRID_W - WIN_COLS)
    col_ok = (c_idx[None, :] >= c0[:, None]) & (c_idx[None, :] < c0[:, None] + WIN_COLS)
    dc = np.clip(c_idx[None, :] - c_idx[:, None] + (WIN_COLS - 1), 0, n_dc - 1)

    k_pad = -(-n_dc // LANES) * LANES
    m_pad = -(-(h * n_dr) // 8) * 8
    onehot = np.zeros((k_pad, GRID_W * GRID_W), np.float32)
    flat = np.arange(GRID_W * GRID_W).reshape(GRID_W, GRID_W)
    onehot[dc[col_ok], flat[col_ok]] = 1.0
    neg = np.where(col_ok, 0.0, NEG_INF).astype(np.float32).reshape(1, -1)
    rpb_pad = jnp.zeros((m_pad, k_pad), F32).at[:h * n_dr, :n_dc].set(
        rpb_l.reshape(h * n_dr, n_dc).astype(F32))
    col_tab = pl.pallas_call(
        _bias_rows_kernel,
        out_shape=jax.ShapeDtypeStruct((m_pad, GRID_W * GRID_W), F32),
        name="bias_rows",
    )(rpb_pad, jnp.asarray(onehot), jnp.asarray(neg))
    col_tab = col_tab[:h * n_dr].reshape(h, n_dr, GRID_W, GRID_W).transpose(0, 2, 1, 3)
    pad = K_ROWS + Q_ROWS
    col_tab = jnp.pad(col_tab, ((0, 0), (0, 0), (pad, pad), (0, 0)))
    col_tab = col_tab.reshape(h, GRID_W, (n_dr + 2 * pad) * GRID_W)
    slabs = []
    for v in range(3):
        for qr in range(Q_ROWS):
            d0 = int(ks[v] - r[v, qr]) + (WIN_ROWS - 1) + pad
            slab = col_tab[:, :, d0 * GRID_W:(d0 + K_ROWS) * GRID_W]
            lane_ok = jnp.asarray(np.repeat(row_ok[v, qr], GRID_W))
            slabs.append(jnp.where(lane_ok[None, None, :], slab, NEG_INF))
    return jnp.stack(slabs, axis=1).reshape(h, 3, Q_ROWS * GRID_W, K_ROWS * GRID_W)


def _nbr_attn(q3, k3, v3, ck3, cv3, bias):
    nb, seq, width = q3.shape
    rows = seq // GRID_W
    n_blocks = rows // Q_ROWS
    n_pairs = width // LANES
    past = ck3.shape[1]
    tq = Q_ROWS * GRID_W
    tk = K_ROWS * GRID_W

    def variant(i):
        return jnp.where(i == 0, 0, jnp.where(i == n_blocks - 1, 2, 1))

    return pl.pallas_call(
        functools.partial(_nbr_attn_kernel, rows=rows),
        grid=(n_pairs, n_blocks),
        in_specs=[
            pl.BlockSpec((nb, tq, LANES), lambda p, i: (0, i, p)),
            pl.BlockSpec((nb, seq, LANES), lambda p, i: (0, 0, p)),
            pl.BlockSpec((nb, seq, LANES), lambda p, i: (0, 0, p)),
            pl.BlockSpec((nb, past, LANES), lambda p, i: (0, 0, p)),
            pl.BlockSpec((nb, past, LANES), lambda p, i: (0, 0, p)),
            pl.BlockSpec((2, 1, tq, tk), lambda p, i: (p, variant(i), 0, 0)),
        ],
        out_specs=pl.BlockSpec((nb, tq, LANES), lambda p, i: (0, i, p)),
        out_shape=jax.ShapeDtypeStruct((nb, seq, width), BF16),
        compiler_params=_params(("arbitrary", "arbitrary")),
        name="nbr_attn",
    )(q3, k3, v3, ck3, cv3, bias)


def _fourier_kernel(x_ref, cdft_ref, d1_ref, d2_ref, twc_ref, tws_ref, o_ref,
                    zr_ref, zi_ref, tr_ref, ti_ref, *, n1, n2):
    gd = cdft_ref.shape[0]
    n_groups = x_ref.shape[2] // gd
    groups = range(n_groups)
    cdft = cdft_ref[...].astype(BF16)
    d1 = d1_ref[...].astype(BF16)
    d2 = d2_ref[...].astype(BF16)
    for g in groups:
        zc = _bdot(x_ref[0, :, g * gd:(g + 1) * gd].astype(BF16), cdft)
        zr_ref[g] = zc[:, :gd]
        zi_ref[g] = zc[:, gd:]

    def stacked(re_ref, im_ref, rows):
        return jnp.concatenate(
            [jnp.concatenate([re_ref[g, rows, :], im_ref[g, rows, :]], axis=0) for g in groups],
            axis=1)

    def stage1(c, carry):
        t = _bdot(d1, stacked(zr_ref, zi_ref, pl.ds(c, n1, stride=n2)).astype(BF16))
        cs = twc_ref[c]
        sn = tws_ref[c]
        row = pl.multiple_of(c * n1, n1)
        for g in groups:
            tr = t[:n1, g * gd:(g + 1) * gd]
            ti = t[n1:, g * gd:(g + 1) * gd]
            tr_ref[g, pl.ds(row, n1), :] = tr * cs + ti * sn
            ti_ref[g, pl.ds(row, n1), :] = ti * cs - tr * sn
        return carry

    lax.fori_loop(0, n2, stage1, 0, unroll=LOOP_UNROLL)

    def stage2(k1, carry):
        rows = pl.ds(k1, n2, stride=n1)
        y = _bdot(d2, stacked(tr_ref, ti_ref, rows).astype(BF16))
        for g in groups:
            zr_ref[g, rows, :] = y[:, g * gd:(g + 1) * gd]
        return carry

    lax.fori_loop(0, n1, stage2, 0, unroll=LOOP_UNROLL)
    for g in groups:
        o_ref[0, :, g * gd:(g + 1) * gd] = zr_ref[g]


def _fourier(f3):
    nb, seq, width = f3.shape
    n2 = GRID_W
    n1 = seq // n2
    gd = FNET_GROUP_DIM
    cc, cs = _dft_mats(gd)
    c1, s1 = _dft_mats(n1)
    c2, s2 = _dft_mats(n2)
    norm = 1.0 / math.sqrt(seq * gd)
    cdft = jnp.asarray(np.concatenate([cc, -cs], axis=1) * norm, F32)
    d1 = jnp.asarray(np.block([[c1, s1], [-s1, c1]]), F32)
    d2 = jnp.asarray(np.concatenate([c2, s2], axis=1), F32)
    ang = 2.0 * np.pi * ((np.arange(n2)[:, None] * np.arange(n1)[None, :]) % seq) / seq
    twc = jnp.asarray(np.broadcast_to(np.cos(ang)[:, :, None], (n2, n1, gd)), F32)
    tws = jnp.asarray(np.broadcast_to(np.sin(ang)[:, :, None], (n2, n1, gd)), F32)
    return pl.pallas_call(
        functools.partial(_fourier_kernel, n1=n1, n2=n2),
        grid=(nb, width // FOURIER_COLS),
        in_specs=[
            pl.BlockSpec((1, seq, FOURIER_COLS), lambda b, g: (b, 0, g)),
            _const_spec(cdft.shape),
            _const_spec(d1.shape),
            _const_spec(d2.shape),
            _const_spec(twc.shape),
            _const_spec(tws.shape),
        ],
        out_specs=pl.BlockSpec((1, seq, FOURIER_COLS), lambda b, g: (b, 0, g)),
        out_shape=jax.ShapeDtypeStruct((nb, seq, width), F32),
        scratch_shapes=[pltpu.VMEM((FOURIER_COLS // gd, seq, gd), F32)] * 4,
        compiler_params=_params(("arbitrary", "arbitrary")),
        name="fourier",
    )(f3, cdft, d1, d2, twc, tws)


def _merge_kernel(fm_ref, o_ref_in, x_ref, m_ref, wgate_ref, wf_ref, wna_ref, wout_ref,
                  g_ref, b_ref, out_ref, *, alpha):
    d = x_ref.shape[1]
    x = x_ref[...]
    shift = m_ref[0, 3:4, :]
    scale = m_ref[0, 4:5, :]
    gate = m_ref[0, 5:6, :]
    u = (x * (1.0 + scale) + shift).astype(BF16)
    ga = _bdot(u, wgate_ref[:, :d])
    gb = _bdot(u, wgate_ref[:, d:])
    out_ref[...] = _merge_out(fm_ref[...], o_ref_in[...], ga, gb, x, gate, wf_ref, wna_ref,
                              wout_ref, g_ref[1:2, :], b_ref[1:2, :], alpha)


def _merge(fm2, o2, x2, mods_l, rows_per_group, group0, win, wf, wna, wout, layer, ln_g, ln_b,
           alpha):
    rows, d = x2.shape
    tm = min(ROW_TILE, rows_per_group)
    tiles_per_group = rows_per_group // tm
    row_spec = lambda w: pl.BlockSpec((tm, w), lambda i: (i, 0))
    assert win.shape[2] == 2 * (2 * d), "gate columns must be the second half of w_in"
    return pl.pallas_call(
        functools.partial(_merge_kernel, alpha=alpha),
        grid=(rows // tm,),
        in_specs=[
            row_spec(fm2.shape[1]), row_spec(o2.shape[1]), row_spec(d),
            pl.BlockSpec((1, N_MOD, d), lambda i: (group0 + i // tiles_per_group, 0, 0)),
            _layer_spec(win, layer, cols=2 * d, col_block=1),
            _layer_spec(wf, layer), _layer_spec(wna, layer), _layer_spec(wout, layer),
            _const_spec(ln_g.shape), _const_spec(ln_b.shape),
        ],
        out_specs=row_spec(d),
        out_shape=jax.ShapeDtypeStruct((rows, d), F32),
        compiler_params=_params(("arbitrary",)),
        name="merge",
    )(fm2, o2, x2, mods_l, win, wf, wna, wout, ln_g, ln_b)


def kernel(x_prompt, x_sample, cache_k, cache_v, c, c_ctx, w_ada, b_ada, ln_g, ln_b, w_ff1_up,
           w_ff1_down, w_in, rpb, w_fourier, w_na_out, w_out, w_ff2_up, w_ff2_down):
    batch, seq, d = x_prompt.shape
    dec_batch, dec_seq, _ = x_sample.shape
    depth = w_ada.shape[0]
    alpha = (2 * depth) ** 0.25
    rows_lat = dec_seq // GRID_W

    cvec = jnp.zeros((MOD_ROWS, d), F32).at[0].set(c_ctx).at[1:1 + dec_batch].set(c)
    mods = _mods(cvec, w_ada, b_ada).reshape(depth, MOD_ROWS, N_MOD, d)

    y_p = x_prompt.reshape(batch * seq, d)
    y_s = x_sample.reshape(dec_batch * dec_seq, d)
    up1, dn1 = w_ff1_up.astype(BF16), w_ff1_down.astype(BF16)
    up2, dn2 = w_ff2_up.astype(BF16), w_ff2_down.astype(BF16)
    win = w_in.astype(BF16)
    wf = w_fourier.astype(BF16)
    wna = w_na_out.astype(BF16)
    wout = w_out.astype(BF16)
    ck = cache_k.transpose(1, 0, 3, 2, 4).reshape(depth, dec_batch, -1, NA_WIDTH).astype(BF16)
    cv = cache_v.transpose(1, 0, 3, 2, 4).reshape(depth, dec_batch, -1, NA_WIDTH).astype(BF16)

    kv = None
    for l in range(depth):
        m_l = mods[l]
        g_l, b_l = ln_g[l], ln_b[l]

        y_p = _ffn(y_p, m_l, batch * seq, 0, up1, dn1, l, g_l, b_l, 0, alpha)
        y_p3, new_k, new_v = _ctx_mix(y_p.reshape(batch, seq, d), m_l, win, wf, wna, wout, g_l, b_l,
                                      alpha, l, depth, kv)
        kv = (new_k, new_v)
        y_p = _ffn(y_p3.reshape(batch * seq, d), m_l, batch * seq, 0, up2, dn2, l, g_l, b_l, 2, alpha)

        y_s = _ffn(y_s, m_l, dec_seq, 1, up1, dn1, l, g_l, b_l, 0, alpha)
        q2, k2, v2, f2 = _in_proj(y_s, m_l, dec_seq, 1, win, l)
        to3 = lambda t: t.reshape(dec_batch, dec_seq, t.shape[-1])
        bias = _nbr_bias_table(rpb[l], rows_lat)
        o3 = _nbr_attn(to3(q2), to3(k2), to3(v2), ck[l], cv[l], bias)
        fm3 = _fourier(to3(f2))
        y_s = _merge(fm3.reshape(-1, FNET_WIDTH), o3.reshape(-1, NA_WIDTH), y_s, m_l,
                     dec_seq, 1, win, wf, wna, wout, l, g_l, b_l, alpha)
        y_s = _ffn(y_s, m_l, dec_seq, 1, up2, dn2, l, g_l, b_l, 2, alpha)

    return (y_p.reshape(batch, seq, d), y_s.reshape(dec_batch, dec_seq, d), new_k, new_v)
```

```python
import functools
import math

import numpy as np
import jax
import jax.numpy as jnp
from jax import lax
from jax.experimental import pallas as pl
from jax.experimental.pallas import tpu as pltpu

F32 = jnp.float32
BF16 = jnp.bfloat16

HEAD_DIM = 64
NA_HEADS = 8
NA_WIDTH = NA_HEADS * HEAD_DIM
FNET_GROUPS = 4
FNET_GROUP_DIM = 128
FNET_WIDTH = FNET_GROUPS * FNET_GROUP_DIM
GRID_W = 64
WIN_ROWS = 8
WIN_COLS = 16
N_SUB = 3
N_MOD = 3 * N_SUB
ATTN_SCALE = HEAD_DIM ** -0.5
LN_EPS = 1e-5
NEG_INF = -1e30

LANES = 128
MXU_DIM = 256
VMEM_LIMIT_BYTES = 56 * 1024 * 1024

ROW_TILE = 512
FFN_ROW_TILE = 1024
FFN_SUB_ROWS = 512
MERGE_ROW_TILE = 1024
FF_CHUNK = MXU_DIM
Q_ROWS = 4
K_ROWS = Q_ROWS + WIN_ROWS
KEY_BLOCK = GRID_W * math.gcd(Q_ROWS, WIN_ROWS // 2)
IN_CHUNK = 2 * MXU_DIM
MOD_ROWS = 8
FOURIER_COLS = 2 * FNET_GROUP_DIM
LOOP_UNROLL = 4
HI = lax.Precision.HIGHEST


def _params(sem):
    return pltpu.CompilerParams(dimension_semantics=sem, vmem_limit_bytes=VMEM_LIMIT_BYTES)


def _const_spec(shape):
    nd = len(shape)
    return pl.BlockSpec(shape, lambda *_: (0,) * nd, pipeline_mode=pl.Buffered(1))


def _layer_spec(w, layer, cols=None, col_block=0):
    _, r, n = w.shape
    cols = n if cols is None else cols
    return pl.BlockSpec((None, r, cols), lambda *_: (layer, 0, col_block),
                        pipeline_mode=pl.Buffered(1))


def _layer_norm(r, g, b):
    mu = jnp.mean(r, axis=-1, keepdims=True)
    d = r - mu
    var = jnp.mean(d * d, axis=-1, keepdims=True)
    return d * lax.rsqrt(var + LN_EPS) * g + b


def _bdot(a, b):
    return jnp.dot(a, b, preferred_element_type=F32)


def _dot_nt(a, b):
    return lax.dot_general(a, b, (((1,), (1,)), ((), ())), preferred_element_type=F32)


def _dft_mats(n):
    k = np.arange(n)
    ang = 2.0 * np.pi * ((k[:, None] * k[None, :]) % n) / n
    return np.cos(ang), np.sin(ang)


def _mods_kernel(c_ref, w_ref, b_ref, o_ref):
    c = c_ref[...]
    s = (c * jax.nn.sigmoid(c)).astype(BF16)
    o_ref[0] = _bdot(s, w_ref[0].astype(BF16)) + b_ref[0]


def _mods(cvec, w_ada, b_ada):
    depth, d, n = w_ada.shape
    tn = d
    return pl.pallas_call(
        _mods_kernel,
        grid=(depth, n // tn),
        in_specs=[
            pl.BlockSpec((MOD_ROWS, d), lambda l, j: (0, 0)),
            pl.BlockSpec((1, d, tn), lambda l, j: (l, 0, j)),
            pl.BlockSpec((1, 1, tn), lambda l, j: (l, 0, j)),
        ],
        out_specs=pl.BlockSpec((1, MOD_ROWS, tn), lambda l, j: (l, 0, j)),
        out_shape=jax.ShapeDtypeStruct((depth, MOD_ROWS, n), F32),
        compiler_params=_params(("arbitrary", "arbitrary")),
        name="mods",
    )(cvec, w_ada, b_ada.reshape(depth, 1, n))


def _ffn_kernel(x_ref, m_ref, wup_ref, wdn_ref, g_ref, b_ref, o_ref, act_ref, *, sub, alpha):
    shift = m_ref[0, 3 * sub:3 * sub + 1, :]
    scale = m_ref[0, 3 * sub + 1:3 * sub + 2, :]
    gate = m_ref[0, 3 * sub + 2:3 * sub + 3, :]
    ff = wdn_ref.shape[0]
    tm = x_ref.shape[0]
    sr = min(tm, FFN_SUB_ROWS)
    slabs = [slice(r0, r0 + sr) for r0 in range(0, tm, sr)]
    us = [(x_ref[rs, :] * (1.0 + scale) + shift).astype(BF16) for rs in slabs]
    for c0 in range(0, ff, FF_CHUNK):
        for rs, u in zip(slabs, us):
            a = _bdot(u, wup_ref[:, c0:c0 + FF_CHUNK])
            g = _bdot(u, wup_ref[:, ff + c0:ff + c0 + FF_CHUNK])
            act_ref[rs, c0:c0 + FF_CHUNK] = ((g * jax.nn.sigmoid(g)) * a).astype(BF16)
    for rs in slabs:
        y = _bdot(act_ref[rs, :], wdn_ref[...])
        r = alpha * x_ref[rs, :] + (0.5 * gate) * y
        o_ref[rs, :] = _layer_norm(r, g_ref[sub:sub + 1, :], b_ref[sub:sub + 1, :])


def _ffn(x2, mods_l, rows_per_group, group0, wup, wdn, layer, ln_g, ln_b, sub, alpha):
    rows, d = x2.shape
    tm = min(FFN_ROW_TILE, rows_per_group)
    tiles_per_group = rows_per_group // tm
    return pl.pallas_call(
        functools.partial(_ffn_kernel, sub=sub, alpha=alpha),
        grid=(rows // tm,),
        in_specs=[
            pl.BlockSpec((tm, d), lambda i: (i, 0)),
            pl.BlockSpec((1, N_MOD, d), lambda i: (group0 + i // tiles_per_group, 0, 0)),
            _layer_spec(wup, layer),
            _layer_spec(wdn, layer),
            _const_spec(ln_g.shape),
            _const_spec(ln_b.shape),
        ],
        out_specs=pl.BlockSpec((tm, d), lambda i: (i, 0)),
        out_shape=jax.ShapeDtypeStruct((rows, d), F32),
        scratch_shapes=[pltpu.VMEM((tm, wdn.shape[1]), BF16)],
        compiler_params=_params(("arbitrary",)),
        name=f"ffn{sub}",
    )(x2, mods_l, wup, wdn, ln_g, ln_b)


def _row_reduce(arrays, combine, reduce):
    acc = None
    for a in arrays:
        for c0 in range(0, a.shape[1], LANES):
            chunk = a[:, c0:c0 + LANES]
            acc = chunk if acc is None else combine(acc, chunk)
    return reduce(acc, axis=-1, keepdims=True)


def _head_pair_attention(q, k_list, v_list, bias_list):
    lane = lax.broadcasted_iota(jnp.int32, (1, LANES), 1)
    outs = []
    for h in range(2):
        in_head = (lane < HEAD_DIM) if h == 0 else (lane >= HEAD_DIM)
        qm = jnp.where(in_head, q, jnp.zeros_like(q))
        s = []
        for k, bias in zip(k_list, bias_list):
            sk = _dot_nt(qm, k)
            if bias is not None:
                sk = sk + bias[h]
            s.append(sk)
        m = _row_reduce(s, jnp.maximum, jnp.max)
        p = [jnp.exp(sk - m) for sk in s]
        den = _row_reduce(p, jnp.add, jnp.sum)
        o = None
        for pk, v in zip(p, v_list):
            pv = _bdot(pk.astype(BF16), v)
            o = pv if o is None else o + pv
        outs.append(o / den)
    return jnp.where(lane < HEAD_DIM, outs[0], outs[1])


def _merge_out(fm, o, ga, gb, x, shift_gate, wf_ref, wna_ref, wout_ref, ln_g, ln_b, alpha):
    branch_a = _bdot(fm.astype(BF16), wf_ref[...])
    branch_b = _bdot(o, wna_ref[...])
    mg = jax.nn.sigmoid(ga) * branch_a + jax.nn.sigmoid(gb) * branch_b
    mix = _bdot(mg.astype(BF16), wout_ref[...])
    r = alpha * x + shift_gate * mix
    return _layer_norm(r, ln_g, ln_b)


def _ctx_mix_kernel(x_ref, m_ref, win_ref, wf_ref, wna_ref, wout_ref, g_ref, b_ref,
                    cdft_ref, pdft_ref, *rest, alpha, n_alias, kv_slot):
    o_ref, k_ref, v_ref, z_ref, oatt_ref, fm_ref = rest[n_alias:]
    nb, seq, d = x_ref.shape
    rows = nb * seq
    x = x_ref[...].reshape(rows, d)
    shift = m_ref[0, 3:4, :]
    scale = m_ref[0, 4:5, :]
    gate = m_ref[0, 5:6, :]
    u = (x * (1.0 + scale) + shift).astype(BF16)
    for c0 in range(0, win_ref.shape[1], IN_CHUNK):
        z_ref[:, c0:c0 + IN_CHUNK] = _bdot(u, win_ref[:, c0:c0 + IN_CHUNK])

    cdft = cdft_ref[...].astype(BF16)
    pdft = pdft_ref[...].astype(BF16)
    q_off, k_off, v_off, f_off = 0, NA_WIDTH, 2 * NA_WIDTH, 3 * NA_WIDTH
    ga_off = f_off + FNET_WIDTH
    gb_off = ga_off + d
    for b in range(nb):
        r0 = b * seq
        for h in range(NA_HEADS):
            k_ref[b, kv_slot, h] = z_ref[r0:r0 + seq, k_off + h * HEAD_DIM:k_off + (h + 1) * HEAD_DIM]
            v_ref[b, kv_slot, h] = z_ref[r0:r0 + seq, v_off + h * HEAD_DIM:v_off + (h + 1) * HEAD_DIM]
        for slot in range(k_ref.shape[1]):
            if slot != kv_slot:
                k_ref[b, slot] = jnp.zeros(k_ref.shape[2:], F32)
                v_ref[b, slot] = jnp.zeros(v_ref.shape[2:], F32)
        for hp in range(NA_WIDTH // LANES):
            c0 = hp * LANES
            q = (z_ref[r0:r0 + seq, q_off + c0:q_off + c0 + LANES] * ATTN_SCALE).astype(BF16)
            k = z_ref[r0:r0 + seq, k_off + c0:k_off + c0 + LANES].astype(BF16)
            v = z_ref[r0:r0 + seq, v_off + c0:v_off + c0 + LANES].astype(BF16)
            o = _head_pair_attention(q, [k], [v], [None])
            oatt_ref[r0:r0 + seq, c0:c0 + LANES] = o.astype(BF16)
        for g in range(FNET_GROUPS):
            c0 = f_off + g * FNET_GROUP_DIM
            xg = z_ref[r0:r0 + seq, c0:c0 + FNET_GROUP_DIM]
            zc = _bdot(xg.astype(BF16), cdft).astype(BF16)
            stacked = jnp.concatenate([zc[:, :FNET_GROUP_DIM], zc[:, FNET_GROUP_DIM:]], axis=0)
            fm_ref[r0:r0 + seq, g * FNET_GROUP_DIM:(g + 1) * FNET_GROUP_DIM] = _bdot(
                pdft, stacked)

    out = _merge_out(fm_ref[...], oatt_ref[...], z_ref[:, ga_off:ga_off + d],
                     z_ref[:, gb_off:gb_off + d], x, gate, wf_ref, wna_ref, wout_ref,
                     g_ref[1:2, :], b_ref[1:2, :], alpha)
    o_ref[...] = out.reshape(nb, seq, d)


def _ctx_mix(x3, mods_l, win, wf, wna, wout, ln_g, ln_b, alpha, layer, depth, kv_prev):
    batch, seq, d = x3.shape
    nb = 2 if batch % 2 == 0 else 1
    in_width = win.shape[2]
    cc, cs = _dft_mats(FNET_GROUP_DIM)
    pc, ps = _dft_mats(seq)
    norm = 1.0 / math.sqrt(seq * FNET_GROUP_DIM)
    cdft = jnp.asarray(np.concatenate([cc, -cs], axis=1) * norm, F32)
    pdft = jnp.asarray(np.concatenate([pc, ps], axis=1), F32)
    kv_shape = jax.ShapeDtypeStruct((batch, depth, NA_HEADS, seq, HEAD_DIM), F32)
    if kv_prev is None:
        kv_spec = pl.BlockSpec((nb, depth, NA_HEADS, seq, HEAD_DIM), lambda i: (i, 0, 0, 0, 0))
        kv_slot = layer
    else:
        kv_spec = pl.BlockSpec((nb, 1, NA_HEADS, seq, HEAD_DIM), lambda i: (i, layer, 0, 0, 0))
        kv_slot = 0
    operands = [x3, mods_l, win, wf, wna, wout, ln_g, ln_b, cdft, pdft]
    in_specs = [
        pl.BlockSpec((nb, seq, d), lambda i: (i, 0, 0)),
        pl.BlockSpec((1, N_MOD, d), lambda i: (0, 0, 0)),
        _layer_spec(win, layer),
        _layer_spec(wf, layer),
        _layer_spec(wna, layer),
        _layer_spec(wout, layer),
        _const_spec(ln_g.shape),
        _const_spec(ln_b.shape),
        _const_spec(cdft.shape),
        _const_spec(pdft.shape),
    ]
    aliases = {}
    if kv_prev is not None:
        aliases = {len(operands): 1, len(operands) + 1: 2}
        operands += list(kv_prev)
        in_specs += [pl.BlockSpec(memory_space=pl.ANY)] * 2
    return pl.pallas_call(
        functools.partial(_ctx_mix_kernel, alpha=alpha, n_alias=len(aliases), kv_slot=kv_slot),
        grid=(batch // nb,),
        in_specs=in_specs,
        out_specs=[pl.BlockSpec((nb, seq, d), lambda i: (i, 0, 0)), kv_spec, kv_spec],
        out_shape=[jax.ShapeDtypeStruct((batch, seq, d), F32), kv_shape, kv_shape],
        input_output_aliases=aliases,
        scratch_shapes=[
            pltpu.VMEM((nb * seq, in_width), F32),
            pltpu.VMEM((nb * seq, NA_WIDTH), BF16),
            pltpu.VMEM((nb * seq, FNET_WIDTH), F32),
        ],
        compiler_params=_params(("arbitrary",)),
        name="ctx_mix",
    )(*operands)


def _in_proj_kernel(x_ref, m_ref, win_ref, q_ref, k_ref, v_ref, f_ref):
    x = x_ref[...]
    shift = m_ref[0, 3:4, :]
    scale = m_ref[0, 4:5, :]
    u = (x * (1.0 + scale) + shift).astype(BF16)
    cw = NA_WIDTH
    q_ref[...] = (_bdot(u, win_ref[:, 0:cw]) * ATTN_SCALE).astype(BF16)
    k_ref[...] = _bdot(u, win_ref[:, cw:2 * cw]).astype(BF16)
    v_ref[...] = _bdot(u, win_ref[:, 2 * cw:3 * cw]).astype(BF16)
    f_ref[...] = _bdot(u, win_ref[:, 3 * cw:3 * cw + FNET_WIDTH])


def _in_proj(x2, mods_l, rows_per_group, group0, win, layer):
    rows, d = x2.shape
    tm = min(ROW_TILE, rows_per_group)
    tiles_per_group = rows_per_group // tm
    cw = NA_WIDTH
    qkvf = 3 * NA_WIDTH + FNET_WIDTH
    row_spec = lambda w: pl.BlockSpec((tm, w), lambda i: (i, 0))
    return pl.pallas_call(
        _in_proj_kernel,
        grid=(rows // tm,),
        in_specs=[
            row_spec(d),
            pl.BlockSpec((1, N_MOD, d), lambda i: (group0 + i // tiles_per_group, 0, 0)),
            _layer_spec(win, layer, cols=qkvf, col_block=0),
        ],
        out_specs=[row_spec(cw), row_spec(cw), row_spec(cw), row_spec(FNET_WIDTH)],
        out_shape=[
            jax.ShapeDtypeStruct((rows, cw), BF16),
            jax.ShapeDtypeStruct((rows, cw), BF16),
            jax.ShapeDtypeStruct((rows, cw), BF16),
            jax.ShapeDtypeStruct((rows, FNET_WIDTH), F32),
        ],
        compiler_params=_params(("arbitrary",)),
        name="in_proj",
    )(x2, mods_l, win)


def _window_start_row(i, rows):
    return jnp.clip(Q_ROWS * i - WIN_ROWS // 2, 0, rows - K_ROWS)


def _nbr_attn_kernel(q_ref, k_ref, v_ref, ck_ref, cv_ref, bias_ref, o_ref, *, rows):
    i = pl.program_id(1)
    start = pl.multiple_of(_window_start_row(i, rows) * GRID_W, KEY_BLOCK)
    n_keys = K_ROWS * GRID_W
    for b in range(q_ref.shape[0]):
        kw = k_ref[b, pl.ds(start, n_keys), :]
        vw = v_ref[b, pl.ds(start, n_keys), :]
        o = _head_pair_attention(
            q_ref[b], [kw, ck_ref[b]], [vw, cv_ref[b]],
            [(bias_ref[0, 0], bias_ref[1, 0]), None])
        o_ref[b] = o.astype(BF16)


def _bias_rows_kernel(rpb_ref, onehot_ref, neg_ref, o_ref):
    o_ref[...] = jnp.dot(rpb_ref[...], onehot_ref[...], precision=HI,
                         preferred_element_type=F32) + neg_ref[...]


def _nbr_bias_table(rpb_l, rows):
    h, n_dr, n_dc = rpb_l.shape
    n_blocks = rows // Q_ROWS
    kr = min(WIN_ROWS, rows)
    variants = np.array([0, min(1, n_blocks - 1), n_blocks - 1])
    r = Q_ROWS * variants[:, None] + np.arange(Q_ROWS)[None, :]
    ks = np.clip(Q_ROWS * variants - WIN_ROWS // 2, 0, rows - K_ROWS)
    key_row = ks[:, None] + np.arange(K_ROWS)[None, :]
    r0 = np.clip(r - kr // 2, 0, rows - kr)
    row_ok = (key_row[:, None, :] >= r0[:, :, None]) & (key_row[:, None, :] < r0[:, :, None] + kr)
    dr = np.clip(key_row[:, None, :] - r[:, :, None] + (WIN_ROWS - 1), 0, n_dr - 1)
    c_idx = np.arange(GRID_W)
    c0 = np.clip(c_idx - WIN_COLS // 2, 0, GRID_W - WIN_COLS)
    col_ok = (c_idx[None, :] >= c0[:, None]) & (c_idx[None, :] < c0[:, None] + WIN_COLS)
    dc = np.clip(c_idx[None, :] - c_idx[:, None] + (WIN_COLS - 1), 0, n_dc - 1)

    k_pad = -(-n_dc // LANES) * LANES
    m_pad = -(-(h * n_dr) // 8) * 8
    onehot = np.zeros((k_pad, GRID_W * GRID_W), np.float32)
    flat = np.arange(GRID_W * GRID_W).reshape(GRID_W, GRID_W)
    onehot[dc[col_ok], flat[col_ok]] = 1.0
    neg = np.where(col_ok, 0.0, NEG_INF).astype(np.float32).reshape(1, -1)
    rpb_pad = jnp.zeros((m_pad, k_pad), F32).at[:h * n_dr, :n_dc].set(
        rpb_l.reshape(h * n_dr, n_dc).astype(F32))
    col_tab = pl.pallas_call(
        _bias_rows_kernel,
        out_shape=jax.ShapeDtypeStruct((m_pad, GRID_W * GRID_W), F32),
        name="bias_rows",
    )(rpb_pad, jnp.asarray(onehot), jnp.asarray(neg))
    col_tab = col_tab[:h * n_dr].reshape(h, n_dr, GRID_W, GRID_W).transpose(0, 2, 1, 3)
    pad = K_ROWS + Q_ROWS
    col_tab = jnp.pad(col_tab, ((0, 0), (0, 0), (pad, pad), (0, 0)))
    col_tab = col_tab.reshape(h, GRID_W, (n_dr + 2 * pad) * GRID_W)
    slabs = []
    for v in range(3):
        for qr in range(Q_ROWS):
            d0 = int(ks[v] - r[v, qr]) + (WIN_ROWS - 1) + pad
            slab = col_tab[:, :, d0 * GRID_W:(d0 + K_ROWS) * GRID_W]
            lane_ok = jnp.asarray(np.repeat(row_ok[v, qr], GRID_W))
            slabs.append(jnp.where(lane_ok[None, None, :], slab, NEG_INF))
    return jnp.stack(slabs, axis=1).reshape(h, 3, Q_ROWS * GRID_W, K_ROWS * GRID_W)


def _nbr_attn(q3, k3, v3, ck3, cv3, bias, layer):
    nb, seq, width = q3.shape
    rows = seq // GRID_W
    n_blocks = rows // Q_ROWS
    n_pairs = width // LANES
    past = ck3.shape[1]
    tq = Q_ROWS * GRID_W
    tk = K_ROWS * GRID_W

    def variant(i):
        return jnp.where(i == 0, 0, jnp.where(i == n_blocks - 1, 2, 1))

    return pl.pallas_call(
        functools.partial(_nbr_attn_kernel, rows=rows),
        grid=(n_pairs, n_blocks),
        in_specs=[
            pl.BlockSpec((nb, tq, LANES), lambda p, i: (0, i, p)),
            pl.BlockSpec((nb, seq, LANES), lambda p, i: (0, 0, p)),
            pl.BlockSpec((nb, seq, LANES), lambda p, i: (0, 0, p)),
            pl.BlockSpec((nb, past, LANES), lambda p, i: (0, 0, p)),
            pl.BlockSpec((nb, past, LANES), lambda p, i: (0, 0, p)),
            pl.BlockSpec((2, 1, tq, tk), lambda p, i: (layer * n_pairs + p, variant(i), 0, 0)),
        ],
        out_specs=pl.BlockSpec((nb, tq, LANES), lambda p, i: (0, i, p)),
        out_shape=jax.ShapeDtypeStruct((nb, seq, width), BF16),
        compiler_params=_params(("arbitrary", "arbitrary")),
        name="nbr_attn",
    )(q3, k3, v3, ck3, cv3, bias)


def _fourier_kernel(x_ref, cdft_ref, d1_ref, d2_ref, twc_ref, tws_ref, o_ref,
                    zr_ref, zi_ref, tr_ref, ti_ref, *, n1, n2):
    gd = cdft_ref.shape[0]
    n_groups = x_ref.shape[2] // gd
    groups = range(n_groups)
    cdft = cdft_ref[...].astype(BF16)
    d1 = d1_ref[...].astype(BF16)
    d2 = d2_ref[...].astype(BF16)
    for g in groups:
        zc = _bdot(x_ref[0, :, g * gd:(g + 1) * gd].astype(BF16), cdft)
        zr_ref[g] = zc[:, :gd]
        zi_ref[g] = zc[:, gd:]

    def stacked(re_ref, im_ref, rows):
        return jnp.concatenate(
            [jnp.concatenate([re_ref[g, rows, :], im_ref[g, rows, :]], axis=0) for g in groups],
            axis=1)

    def stage1(c, carry):
        t = _bdot(d1, stacked(zr_ref, zi_ref, pl.ds(c, n1, stride=n2)).astype(BF16))
        cs = twc_ref[c]
        sn = tws_ref[c]
        row = pl.multiple_of(c * n1, n1)
        for g in groups:
            tr = t[:n1, g * gd:(g + 1) * gd]
            ti = t[n1:, g * gd:(g + 1) * gd]
            tr_ref[g, pl.ds(row, n1), :] = tr * cs + ti * sn
            ti_ref[g, pl.ds(row, n1), :] = ti * cs - tr * sn
        return carry

    lax.fori_loop(0, n2, stage1, 0, unroll=LOOP_UNROLL)

    def stage2(k1, carry):
        rows = pl.ds(k1, n2, stride=n1)
        y = _bdot(d2, stacked(tr_ref, ti_ref, rows).astype(BF16))
        for g in groups:
            zr_ref[g, rows, :] = y[:, g * gd:(g + 1) * gd]
        return carry

    lax.fori_loop(0, n1, stage2, 0, unroll=LOOP_UNROLL)
    for g in groups:
        o_ref[0, :, g * gd:(g + 1) * gd] = zr_ref[g]


def _fourier(f3):
    nb, seq, width = f3.shape
    n2 = GRID_W
    n1 = seq // n2
    gd = FNET_GROUP_DIM
    cc, cs = _dft_mats(gd)
    c1, s1 = _dft_mats(n1)
    c2, s2 = _dft_mats(n2)
    norm = 1.0 / math.sqrt(seq * gd)
    cdft = jnp.asarray(np.concatenate([cc, -cs], axis=1) * norm, F32)
    d1 = jnp.asarray(np.block([[c1, s1], [-s1, c1]]), F32)
    d2 = jnp.asarray(np.concatenate([c2, s2], axis=1), F32)
    ang = 2.0 * np.pi * ((np.arange(n2)[:, None] * np.arange(n1)[None, :]) % seq) / seq
    twc = jnp.asarray(np.broadcast_to(np.cos(ang)[:, :, None], (n2, n1, gd)), F32)
    tws = jnp.asarray(np.broadcast_to(np.sin(ang)[:, :, None], (n2, n1, gd)), F32)
    return pl.pallas_call(
        functools.partial(_fourier_kernel, n1=n1, n2=n2),
        grid=(nb, width // FOURIER_COLS),
        in_specs=[
            pl.BlockSpec((1, seq, FOURIER_COLS), lambda b, g: (b, 0, g)),
            _const_spec(cdft.shape),
            _const_spec(d1.shape),
            _const_spec(d2.shape),
            _const_spec(twc.shape),
            _const_spec(tws.shape),
        ],
        out_specs=pl.BlockSpec((1, seq, FOURIER_COLS), lambda b, g: (b, 0, g)),
        out_shape=jax.ShapeDtypeStruct((nb, seq, width), F32),
        scratch_shapes=[pltpu.VMEM((FOURIER_COLS // gd, seq, gd), F32)] * 4,
        compiler_params=_params(("arbitrary", "arbitrary")),
        name="fourier",
    )(f3, cdft, d1, d2, twc, tws)


def _merge_kernel(fm_ref, o_ref_in, x_ref, m_ref, wgate_ref, wf_ref, wna_ref, wout_ref,
                  g_ref, b_ref, out_ref, *, alpha):
    tm, d = x_ref.shape
    shift = m_ref[0, 3:4, :]
    scale = m_ref[0, 4:5, :]
    gate = m_ref[0, 5:6, :]
    sr = min(tm, ROW_TILE)
    for r0 in range(0, tm, sr):
        rs = slice(r0, r0 + sr)
        x = x_ref[rs, :]
        u = (x * (1.0 + scale) + shift).astype(BF16)
        ga = _bdot(u, wgate_ref[:, :d])
        gb = _bdot(u, wgate_ref[:, d:])
        out_ref[rs, :] = _merge_out(fm_ref[rs, :], o_ref_in[rs, :], ga, gb, x, gate, wf_ref,
                                    wna_ref, wout_ref, g_ref[1:2, :], b_ref[1:2, :], alpha)


def _merge(fm2, o2, x2, mods_l, rows_per_group, group0, win, wf, wna, wout, layer, ln_g, ln_b,
           alpha):
    rows, d = x2.shape
    tm = min(MERGE_ROW_TILE, rows_per_group)
    tiles_per_group = rows_per_group // tm
    row_spec = lambda w: pl.BlockSpec((tm, w), lambda i: (i, 0))
    assert win.shape[2] == 2 * (2 * d), "gate columns must be the second half of w_in"
    return pl.pallas_call(
        functools.partial(_merge_kernel, alpha=alpha),
        grid=(rows // tm,),
        in_specs=[
            row_spec(fm2.shape[1]), row_spec(o2.shape[1]), row_spec(d),
            pl.BlockSpec((1, N_MOD, d), lambda i: (group0 + i // tiles_per_group, 0, 0)),
            _layer_spec(win, layer, cols=2 * d, col_block=1),
            _layer_spec(wf, layer), _layer_spec(wna, layer), _layer_spec(wout, layer),
            _const_spec(ln_g.shape), _const_spec(ln_b.shape),
        ],
        out_specs=row_spec(d),
        out_shape=jax.ShapeDtypeStruct((rows, d), F32),
        compiler_params=_params(("arbitrary",)),
        name="merge",
    )(fm2, o2, x2, mods_l, win, wf, wna, wout, ln_g, ln_b)


def kernel(x_prompt, x_sample, cache_k, cache_v, c, c_ctx, w_ada, b_ada, ln_g, ln_b, w_ff1_up,
           w_ff1_down, w_in, rpb, w_fourier, w_na_out, w_out, w_ff2_up, w_ff2_down):
    batch, seq, d = x_prompt.shape
    dec_batch, dec_seq, _ = x_sample.shape
    depth = w_ada.shape[0]
    alpha = (2 * depth) ** 0.25
    rows_lat = dec_seq // GRID_W

    cvec = jnp.zeros((MOD_ROWS, d), F32).at[0].set(c_ctx).at[1:1 + dec_batch].set(c)
    mods = _mods(cvec, w_ada, b_ada).reshape(depth, MOD_ROWS, N_MOD, d)

    y_p = x_prompt.reshape(batch * seq, d)
    y_s = x_sample.reshape(dec_batch * dec_seq, d)
    up1, dn1 = w_ff1_up.astype(BF16), w_ff1_down.astype(BF16)
    up2, dn2 = w_ff2_up.astype(BF16), w_ff2_down.astype(BF16)
    win = w_in.astype(BF16)
    wf = w_fourier.astype(BF16)
    wna = w_na_out.astype(BF16)
    wout = w_out.astype(BF16)
    ck = cache_k.transpose(1, 0, 3, 2, 4).reshape(depth, dec_batch, -1, NA_WIDTH).astype(BF16)
    cv = cache_v.transpose(1, 0, 3, 2, 4).reshape(depth, dec_batch, -1, NA_WIDTH).astype(BF16)

    bias = _nbr_bias_table(rpb.reshape((-1,) + rpb.shape[2:]), rows_lat)

    kv = None
    for l in range(depth):
        m_l = mods[l]
        g_l, b_l = ln_g[l], ln_b[l]

        y_p = _ffn(y_p, m_l, batch * seq, 0, up1, dn1, l, g_l, b_l, 0, alpha)
        y_p3, new_k, new_v = _ctx_mix(y_p.reshape(batch, seq, d), m_l, win, wf, wna, wout, g_l, b_l,
                                      alpha, l, depth, kv)
        kv = (new_k, new_v)
        y_p = _ffn(y_p3.reshape(batch * seq, d), m_l, batch * seq, 0, up2, dn2, l, g_l, b_l, 2, alpha)

        y_s = _ffn(y_s, m_l, dec_seq, 1, up1, dn1, l, g_l, b_l, 0, alpha)
        q2, k2, v2, f2 = _in_proj(y_s, m_l, dec_seq, 1, win, l)
        to3 = lambda t: t.reshape(dec_batch, dec_seq, t.shape[-1])
        o3 = _nbr_attn(to3(q2), to3(k2), to3(v2), ck[l], cv[l], bias, l)
        fm3 = _fourier(to3(f2))
        y_s = _merge(fm3.reshape(-1, FNET_WIDTH), o3.reshape(-1, NA_WIDTH), y_s, m_l,
                     dec_seq, 1, win, wf, wna, wout, l, g_l, b_l, alpha)
        y_s = _ffn(y_s, m_l, dec_seq, 1, up2, dn2, l, g_l, b_l, 2, alpha)

    return (y_p.reshape(batch, seq, d), y_s.reshape(dec_batch, dec_seq, d), new_k, new_v)
```

```python
import functools
import math

import numpy as np
import jax
import jax.numpy as jnp
from jax import lax
from jax.experimental import pallas as pl
from jax.experimental.pallas import tpu as pltpu

F32 = jnp.float32
BF16 = jnp.bfloat16

HEAD_DIM = 64
NA_HEADS = 8
NA_WIDTH = NA_HEADS * HEAD_DIM
FNET_GROUPS = 4
FNET_GROUP_DIM = 128
FNET_WIDTH = FNET_GROUPS * FNET_GROUP_DIM
GRID_W = 64
WIN_ROWS = 8
WIN_COLS = 16
N_SUB = 3
N_MOD = 3 * N_SUB
ATTN_SCALE = HEAD_DIM ** -0.5
LN_EPS = 1e-5
NEG_INF = -1e30

LANES = 128
MXU_DIM = 256
VMEM_LIMIT_BYTES = 56 * 1024 * 1024

ROW_TILE = 512
FFN_ROW_TILE = 1024
FFN_SUB_ROWS = 512
MERGE_ROW_TILE = 1024
FF_CHUNK = MXU_DIM
Q_ROWS = 4
K_ROWS = Q_ROWS + WIN_ROWS
KEY_BLOCK = GRID_W * math.gcd(Q_ROWS, WIN_ROWS // 2)
Q_BLOCKS_PER_STEP = 4
IN_CHUNK = 2 * MXU_DIM
MOD_ROWS = 8
MODS_COLS = 3072
FOURIER_COLS = 2 * FNET_GROUP_DIM
LOOP_UNROLL = 8
HI = lax.Precision.HIGHEST


def _params(sem):
    return pltpu.CompilerParams(dimension_semantics=sem, vmem_limit_bytes=VMEM_LIMIT_BYTES)


def _const_spec(shape):
    nd = len(shape)
    return pl.BlockSpec(shape, lambda *_: (0,) * nd, pipeline_mode=pl.Buffered(1))


def _layer_spec(w, layer, cols=None, col_block=0):
    _, r, n = w.shape
    cols = n if cols is None else cols
    return pl.BlockSpec((None, r, cols), lambda *_: (layer, 0, col_block),
                        pipeline_mode=pl.Buffered(1))


def _layer_norm(r, g, b):
    mu = jnp.mean(r, axis=-1, keepdims=True)
    d = r - mu
    var = jnp.mean(d * d, axis=-1, keepdims=True)
    return d * lax.rsqrt(var + LN_EPS) * g + b


def _bdot(a, b):
    return jnp.dot(a, b, preferred_element_type=F32)


def _dot_nt(a, b):
    return lax.dot_general(a, b, (((1,), (1,)), ((), ())), preferred_element_type=F32)


def _dft_mats(n):
    k = np.arange(n)
    ang = 2.0 * np.pi * ((k[:, None] * k[None, :]) % n) / n
    return np.cos(ang), np.sin(ang)


def _mods_kernel(c_ref, w_ref, b_ref, o_ref):
    c = c_ref[...]
    s = (c * jax.nn.sigmoid(c)).astype(BF16)
    o_ref[0] = _bdot(s, w_ref[0].astype(BF16)) + b_ref[0]


def _mods(cvec, w_ada, b_ada):
    depth, d, n = w_ada.shape
    tn = MODS_COLS if n % MODS_COLS == 0 else d
    return pl.pallas_call(
        _mods_kernel,
        grid=(depth, n // tn),
        in_specs=[
            pl.BlockSpec((MOD_ROWS, d), lambda l, j: (0, 0)),
            pl.BlockSpec((1, d, tn), lambda l, j: (l, 0, j)),
            pl.BlockSpec((1, 1, tn), lambda l, j: (l, 0, j)),
        ],
        out_specs=pl.BlockSpec((1, MOD_ROWS, tn), lambda l, j: (l, 0, j)),
        out_shape=jax.ShapeDtypeStruct((depth, MOD_ROWS, n), F32),
        compiler_params=_params(("arbitrary", "arbitrary")),
        name="mods",
    )(cvec, w_ada, b_ada.reshape(depth, 1, n))


def _ffn_kernel(x_ref, m_ref, wup_ref, wdn_ref, g_ref, b_ref, o_ref, act_ref, *, sub, alpha):
    shift = m_ref[0, 3 * sub:3 * sub + 1, :]
    scale = m_ref[0, 3 * sub + 1:3 * sub + 2, :]
    gate = m_ref[0, 3 * sub + 2:3 * sub + 3, :]
    ff = wdn_ref.shape[0]
    tm = x_ref.shape[0]
    sr = min(tm, FFN_SUB_ROWS)
    slabs = [slice(r0, r0 + sr) for r0 in range(0, tm, sr)]
    us = [(x_ref[rs, :] * (1.0 + scale) + shift).astype(BF16) for rs in slabs]
    for c0 in range(0, ff, FF_CHUNK):
        for rs, u in zip(slabs, us):
            a = _bdot(u, wup_ref[:, c0:c0 + FF_CHUNK])
            g = _bdot(u, wup_ref[:, ff + c0:ff + c0 + FF_CHUNK])
            act_ref[rs, c0:c0 + FF_CHUNK] = ((g * jax.nn.sigmoid(g)) * a).astype(BF16)
    for rs in slabs:
        y = _bdot(act_ref[rs, :], wdn_ref[...])
        r = alpha * x_ref[rs, :] + (0.5 * gate) * y
        o_ref[rs, :] = _layer_norm(r, g_ref[sub:sub + 1, :], b_ref[sub:sub + 1, :])


def _ffn(x2, mods_l, rows_per_group, group0, wup, wdn, layer, ln_g, ln_b, sub, alpha):
    rows, d = x2.shape
    tm = min(FFN_ROW_TILE, rows_per_group)
    tiles_per_group = rows_per_group // tm
    return pl.pallas_call(
        functools.partial(_ffn_kernel, sub=sub, alpha=alpha),
        grid=(rows // tm,),
        in_specs=[
            pl.BlockSpec((tm, d), lambda i: (i, 0)),
            pl.BlockSpec((1, N_MOD, d), lambda i: (group0 + i // tiles_per_group, 0, 0)),
            _layer_spec(wup, layer),
            _layer_spec(wdn, layer),
            _const_spec(ln_g.shape),
            _const_spec(ln_b.shape),
        ],
        out_specs=pl.BlockSpec((tm, d), lambda i: (i, 0)),
        out_shape=jax.ShapeDtypeStruct((rows, d), F32),
        scratch_shapes=[pltpu.VMEM((tm, wdn.shape[1]), BF16)],
        compiler_params=_params(("arbitrary",)),
        name=f"ffn{sub}",
    )(x2, mods_l, wup, wdn, ln_g, ln_b)


def _row_reduce(arrays, combine, reduce):
    acc = None
    for a in arrays:
        for c0 in range(0, a.shape[1], LANES):
            chunk = a[:, c0:c0 + LANES]
            acc = chunk if acc is None else combine(acc, chunk)
    return reduce(acc, axis=-1, keepdims=True)


def _stacked_pair_attention(problems):
    lane = lax.broadcasted_iota(jnp.int32, (1, LANES), 1)
    scores = []
    for q, k_list, _, bias_list in problems:
        zero = jnp.zeros_like(q)
        q2 = jnp.concatenate([jnp.where(lane < HEAD_DIM, q, zero),
                              jnp.where(lane >= HEAD_DIM, q, zero)], axis=0)
        s = []
        for k, bias in zip(k_list, bias_list):
            sk = _dot_nt(q2, k)
            if bias is not None:
                sk = sk + jnp.concatenate(bias, axis=0)
            s.append(sk)
        scores.append(s)
    maxes = [_row_reduce(s, jnp.maximum, jnp.max) for s in scores]
    probs = [[jnp.exp(sk - m) for sk in s] for s, m in zip(scores, maxes)]
    dens = [_row_reduce(p, jnp.add, jnp.sum) for p in probs]
    outs = []
    for (q, _, v_list, _), p, den in zip(problems, probs, dens):
        o = None
        for pk, v in zip(p, v_list):
            pv = _bdot(pk.astype(BF16), v)
            o = pv if o is None else o + pv
        o = o / den
        m_rows = q.shape[0]
        outs.append(jnp.where(lane < HEAD_DIM, o[:m_rows], o[m_rows:]))
    return outs


def _merge_out(fm, o, ga, gb, x, shift_gate, wf_ref, wna_ref, wout_ref, ln_g, ln_b, alpha):
    branch_a = _bdot(fm.astype(BF16), wf_ref[...])
    branch_b = _bdot(o, wna_ref[...])
    mg = jax.nn.sigmoid(ga) * branch_a + jax.nn.sigmoid(gb) * branch_b
    mix = _bdot(mg.astype(BF16), wout_ref[...])
    r = alpha * x + shift_gate * mix
    return _layer_norm(r, ln_g, ln_b)


def _ctx_mix_kernel(x_ref, m_ref, win_ref, wf_ref, wna_ref, wout_ref, g_ref, b_ref,
                    cdft_ref, pdft_ref, *rest, alpha, n_alias, kv_slot):
    o_ref, k_ref, v_ref, z_ref, oatt_ref, fm_ref = rest[n_alias:]
    nb, seq, d = x_ref.shape
    rows = nb * seq
    x = x_ref[...].reshape(rows, d)
    shift = m_ref[0, 3:4, :]
    scale = m_ref[0, 4:5, :]
    gate = m_ref[0, 5:6, :]
    u = (x * (1.0 + scale) + shift).astype(BF16)
    for c0 in range(0, win_ref.shape[1], IN_CHUNK):
        z_ref[:, c0:c0 + IN_CHUNK] = _bdot(u, win_ref[:, c0:c0 + IN_CHUNK])

    cdft = cdft_ref[...].astype(BF16)
    pdft = pdft_ref[...].astype(BF16)
    q_off, k_off, v_off, f_off = 0, NA_WIDTH, 2 * NA_WIDTH, 3 * NA_WIDTH
    ga_off = f_off + FNET_WIDTH
    gb_off = ga_off + d
    for b in range(nb):
        r0 = b * seq
        for h in range(NA_HEADS):
            k_ref[b, kv_slot, h] = z_ref[r0:r0 + seq, k_off + h * HEAD_DIM:k_off + (h + 1) * HEAD_DIM]
            v_ref[b, kv_slot, h] = z_ref[r0:r0 + seq, v_off + h * HEAD_DIM:v_off + (h + 1) * HEAD_DIM]
        for slot in range(k_ref.shape[1]):
            if slot != kv_slot:
                k_ref[b, slot] = jnp.zeros(k_ref.shape[2:], F32)
                v_ref[b, slot] = jnp.zeros(v_ref.shape[2:], F32)
        zc = [_bdot(z_ref[r0:r0 + seq, f_off + g * FNET_GROUP_DIM:f_off + (g + 1) * FNET_GROUP_DIM]
                    .astype(BF16), cdft).astype(BF16) for g in range(FNET_GROUPS)]
        stacked = jnp.concatenate(
            [jnp.concatenate([z[:, :FNET_GROUP_DIM] for z in zc], axis=1),
             jnp.concatenate([z[:, FNET_GROUP_DIM:] for z in zc], axis=1)], axis=0)
        fm_ref[r0:r0 + seq, :] = _bdot(pdft, stacked)

    problems = []
    for b in range(nb):
        r0 = b * seq
        for hp in range(NA_WIDTH // LANES):
            c0 = hp * LANES
            q = (z_ref[r0:r0 + seq, q_off + c0:q_off + c0 + LANES] * ATTN_SCALE).astype(BF16)
            k = z_ref[r0:r0 + seq, k_off + c0:k_off + c0 + LANES].astype(BF16)
            v = z_ref[r0:r0 + seq, v_off + c0:v_off + c0 + LANES].astype(BF16)
            problems.append((b * seq, c0, (q, [k], [v], [None])))
    outs = _stacked_pair_attention([p for _, _, p in problems])
    for (r0, c0, _), o in zip(problems, outs):
        oatt_ref[r0:r0 + seq, c0:c0 + LANES] = o.astype(BF16)

    out = _merge_out(fm_ref[...], oatt_ref[...], z_ref[:, ga_off:ga_off + d],
                     z_ref[:, gb_off:gb_off + d], x, gate, wf_ref, wna_ref, wout_ref,
                     g_ref[1:2, :], b_ref[1:2, :], alpha)
    o_ref[...] = out.reshape(nb, seq, d)


def _ctx_mix(x3, mods_l, win, wf, wna, wout, ln_g, ln_b, alpha, layer, depth, kv_prev):
    batch, seq, d = x3.shape
    nb = 2 if batch % 2 == 0 else 1
    in_width = win.shape[2]
    cc, cs = _dft_mats(FNET_GROUP_DIM)
    pc, ps = _dft_mats(seq)
    norm = 1.0 / math.sqrt(seq * FNET_GROUP_DIM)
    cdft = jnp.asarray(np.concatenate([cc, -cs], axis=1) * norm, F32)
    pdft = jnp.asarray(np.concatenate([pc, ps], axis=1), F32)
    kv_shape = jax.ShapeDtypeStruct((batch, depth, NA_HEADS, seq, HEAD_DIM), F32)
    if kv_prev is None:
        kv_spec = pl.BlockSpec((nb, depth, NA_HEADS, seq, HEAD_DIM), lambda i: (i, 0, 0, 0, 0))
        kv_slot = layer
    else:
        kv_spec = pl.BlockSpec((nb, 1, NA_HEADS, seq, HEAD_DIM), lambda i: (i, layer, 0, 0, 0))
        kv_slot = 0
    operands = [x3, mods_l, win, wf, wna, wout, ln_g, ln_b, cdft, pdft]
    in_specs = [
        pl.BlockSpec((nb, seq, d), lambda i: (i, 0, 0)),
        pl.BlockSpec((1, N_MOD, d), lambda i: (0, 0, 0)),
        _layer_spec(win, layer),
        _layer_spec(wf, layer),
        _layer_spec(wna, layer),
        _layer_spec(wout, layer),
        _const_spec(ln_g.shape),
        _const_spec(ln_b.shape),
        _const_spec(cdft.shape),
        _const_spec(pdft.shape),
    ]
    aliases = {}
    if kv_prev is not None:
        aliases = {len(operands): 1, len(operands) + 1: 2}
        operands += list(kv_prev)
        in_specs += [pl.BlockSpec(memory_space=pl.ANY)] * 2
    return pl.pallas_call(
        functools.partial(_ctx_mix_kernel, alpha=alpha, n_alias=len(aliases), kv_slot=kv_slot),
        grid=(batch // nb,),
        in_specs=in_specs,
        out_specs=[pl.BlockSpec((nb, seq, d), lambda i: (i, 0, 0)), kv_spec, kv_spec],
        out_shape=[jax.ShapeDtypeStruct((batch, seq, d), F32), kv_shape, kv_shape],
        input_output_aliases=aliases,
        scratch_shapes=[
            pltpu.VMEM((nb * seq, in_width), F32),
            pltpu.VMEM((nb * seq, NA_WIDTH), BF16),
            pltpu.VMEM((nb * seq, FNET_WIDTH), F32),
        ],
        compiler_params=_params(("arbitrary",)),
        name="ctx_mix",
    )(*operands)


def _in_proj_kernel(x_ref, m_ref, win_ref, q_ref, k_ref, v_ref, f_ref):
    x = x_ref[...]
    shift = m_ref[0, 3:4, :]
    scale = m_ref[0, 4:5, :]
    u = (x * (1.0 + scale) + shift).astype(BF16)
    cw = NA_WIDTH
    q_ref[...] = (_bdot(u, win_ref[:, 0:cw]) * ATTN_SCALE).astype(BF16)
    k_ref[...] = _bdot(u, win_ref[:, cw:2 * cw]).astype(BF16)
    v_ref[...] = _bdot(u, win_ref[:, 2 * cw:3 * cw]).astype(BF16)
    f_ref[...] = _bdot(u, win_ref[:, 3 * cw:3 * cw + FNET_WIDTH])


def _in_proj(x2, mods_l, rows_per_group, group0, win, layer):
    rows, d = x2.shape
    tm = min(ROW_TILE, rows_per_group)
    tiles_per_group = rows_per_group // tm
    cw = NA_WIDTH
    qkvf = 3 * NA_WIDTH + FNET_WIDTH
    row_spec = lambda w: pl.BlockSpec((tm, w), lambda i: (i, 0))
    return pl.pallas_call(
        _in_proj_kernel,
        grid=(rows // tm,),
        in_specs=[
            row_spec(d),
            pl.BlockSpec((1, N_MOD, d), lambda i: (group0 + i // tiles_per_group, 0, 0)),
            _layer_spec(win, layer, cols=qkvf, col_block=0),
        ],
        out_specs=[row_spec(cw), row_spec(cw), row_spec(cw), row_spec(FNET_WIDTH)],
        out_shape=[
            jax.ShapeDtypeStruct((rows, cw), BF16),
            jax.ShapeDtypeStruct((rows, cw), BF16),
            jax.ShapeDtypeStruct((rows, cw), BF16),
            jax.ShapeDtypeStruct((rows, FNET_WIDTH), F32),
        ],
        compiler_params=_params(("arbitrary",)),
        name="in_proj",
    )(x2, mods_l, win)


def _window_start_row(i, rows):
    return jnp.clip(Q_ROWS * i - WIN_ROWS // 2, 0, rows - K_ROWS)


def _nbr_attn_kernel(q_ref, k_ref, v_ref, ck_ref, cv_ref, *rest, rows):
    bias_refs, o_ref = rest[:-1], rest[-1]
    step = pl.program_id(1)
    n_keys = K_ROWS * GRID_W
    tq = Q_ROWS * GRID_W
    problems, places = [], []
    for j, bias_ref in enumerate(bias_refs):
        i = step * len(bias_refs) + j
        start = pl.multiple_of(_window_start_row(i, rows) * GRID_W, KEY_BLOCK)
        bias = (bias_ref[0, 0], bias_ref[1, 0])
        for b in range(q_ref.shape[0]):
            problems.append(
                (q_ref[b, j * tq:(j + 1) * tq, :],
                 [k_ref[b, pl.ds(start, n_keys), :], ck_ref[b]],
                 [v_ref[b, pl.ds(start, n_keys), :], cv_ref[b]],
                 [bias, None]))
            places.append((b, j))
    for (b, j), o in zip(places, _stacked_pair_attention(problems)):
        o_ref[b, j * tq:(j + 1) * tq, :] = o.astype(BF16)


def _bias_rows_kernel(rpb_ref, onehot_ref, neg_ref, o_ref):
    o_ref[...] = jnp.dot(rpb_ref[...], onehot_ref[...], precision=HI,
                         preferred_element_type=F32) + neg_ref[...]


def _nbr_bias_table(rpb_l, rows):
    h, n_dr, n_dc = rpb_l.shape
    n_blocks = rows // Q_ROWS
    kr = min(WIN_ROWS, rows)
    variants = np.array([0, min(1, n_blocks - 1), n_blocks - 1])
    r = Q_ROWS * variants[:, None] + np.arange(Q_ROWS)[None, :]
    ks = np.clip(Q_ROWS * variants - WIN_ROWS // 2, 0, rows - K_ROWS)
    key_row = ks[:, None] + np.arange(K_ROWS)[None, :]
    r0 = np.clip(r - kr // 2, 0, rows - kr)
    row_ok = (key_row[:, None, :] >= r0[:, :, None]) & (key_row[:, None, :] < r0[:, :, None] + kr)
    dr = np.clip(key_row[:, None, :] - r[:, :, None] + (WIN_ROWS - 1), 0, n_dr - 1)
    c_idx = np.arange(GRID_W)
    c0 = np.clip(c_idx - WIN_COLS // 2, 0, GRID_W - WIN_COLS)
    col_ok = (c_idx[None, :] >= c0[:, None]) & (c_idx[None, :] < c0[:, None] + WIN_COLS)
    dc = np.clip(c_idx[None, :] - c_idx[:, None] + (WIN_COLS - 1), 0, n_dc - 1)

    k_pad = -(-n_dc // LANES) * LANES
    m_pad = -(-(h * n_dr) // 8) * 8
    onehot = np.zeros((k_pad, GRID_W * GRID_W), np.float32)
    flat = np.arange(GRID_W * GRID_W).reshape(GRID_W, GRID_W)
    onehot[dc[col_ok], flat[col_ok]] = 1.0
    neg = np.where(col_ok, 0.0, NEG_INF).astype(np.float32).reshape(1, -1)
    rpb_pad = jnp.zeros((m_pad, k_pad), F32).at[:h * n_dr, :n_dc].set(
        rpb_l.reshape(h * n_dr, n_dc).astype(F32))
    col_tab = pl.pallas_call(
        _bias_rows_kernel,
        out_shape=jax.ShapeDtypeStruct((m_pad, GRID_W * GRID_W), F32),
        name="bias_rows",
    )(rpb_pad, jnp.asarray(onehot), jnp.asarray(neg))
    col_tab = col_tab[:h * n_dr].reshape(h, n_dr, GRID_W, GRID_W).transpose(0, 2, 1, 3)
    pad = K_ROWS + Q_ROWS
    col_tab = jnp.pad(col_tab, ((0, 0), (0, 0), (pad, pad), (0, 0)))
    col_tab = col_tab.reshape(h, GRID_W, (n_dr + 2 * pad) * GRID_W)
    slabs = []
    for v in range(3):
        for qr in range(Q_ROWS):
            d0 = int(ks[v] - r[v, qr]) + (WIN_ROWS - 1) + pad
            slab = col_tab[:, :, d0 * GRID_W:(d0 + K_ROWS) * GRID_W]
            lane_ok = jnp.asarray(np.repeat(row_ok[v, qr], GRID_W))
            slabs.append(jnp.where(lane_ok[None, None, :], slab, NEG_INF))
    return jnp.stack(slabs, axis=1).reshape(h, 3, Q_ROWS * GRID_W, K_ROWS * GRID_W)


def _nbr_attn(q3, k3, v3, ck3, cv3, bias, layer):
    nb, seq, width = q3.shape
    rows = seq // GRID_W
    n_blocks = rows // Q_ROWS
    n_pairs = width // LANES
    past = ck3.shape[1]
    tq = Q_ROWS * GRID_W
    tk = K_ROWS * GRID_W

    per_step = Q_BLOCKS_PER_STEP if n_blocks % Q_BLOCKS_PER_STEP == 0 else 1

    def variant(i):
        return jnp.where(i == 0, 0, jnp.where(i == n_blocks - 1, 2, 1))

    def bias_spec(j):
        return pl.BlockSpec((2, 1, tq, tk),
                            lambda p, s: (layer * n_pairs + p, variant(s * per_step + j), 0, 0))

    return pl.pallas_call(
        functools.partial(_nbr_attn_kernel, rows=rows),
        grid=(n_pairs, n_blocks // per_step),
        in_specs=[
            pl.BlockSpec((nb, per_step * tq, LANES), lambda p, s: (0, s, p)),
            pl.BlockSpec((nb, seq, LANES), lambda p, s: (0, 0, p)),
            pl.BlockSpec((nb, seq, LANES), lambda p, s: (0, 0, p)),
            pl.BlockSpec((nb, past, LANES), lambda p, s: (0, 0, p)),
            pl.BlockSpec((nb, past, LANES), lambda p, s: (0, 0, p)),
        ] + [bias_spec(j) for j in range(per_step)],
        out_specs=pl.BlockSpec((nb, per_step * tq, LANES), lambda p, s: (0, s, p)),
        out_shape=jax.ShapeDtypeStruct((nb, seq, width), BF16),
        compiler_params=_params(("arbitrary", "arbitrary")),
        name="nbr_attn",
    )(q3, k3, v3, ck3, cv3, *([bias] * per_step))


def _fourier_kernel(x_ref, cdft_ref, d1_ref, d2_ref, twc_ref, tws_ref, o_ref,
                    zr_ref, zi_ref, tr_ref, ti_ref, *, n1, n2):
    gd = cdft_ref.shape[0]
    n_groups = x_ref.shape[2] // gd
    groups = range(n_groups)
    cdft = cdft_ref[...].astype(BF16)
    d1 = d1_ref[...].astype(BF16)
    d2 = d2_ref[...].astype(BF16)
    for g in groups:
        zc = _bdot(x_ref[0, :, g * gd:(g + 1) * gd].astype(BF16), cdft)
        zr_ref[g] = zc[:, :gd]
        zi_ref[g] = zc[:, gd:]

    def stacked(re_ref, im_ref, rows):
        return jnp.concatenate(
            [jnp.concatenate([re_ref[g, rows, :], im_ref[g, rows, :]], axis=0) for g in groups],
            axis=1)

    def stage1(c, carry):
        t = _bdot(d1, stacked(zr_ref, zi_ref, pl.ds(c, n1, stride=n2)).astype(BF16))
        cs = twc_ref[c]
        sn = tws_ref[c]
        row = pl.multiple_of(c * n1, n1)
        for g in groups:
            tr = t[:n1, g * gd:(g + 1) * gd]
            ti = t[n1:, g * gd:(g + 1) * gd]
            tr_ref[g, pl.ds(row, n1), :] = tr * cs + ti * sn
            ti_ref[g, pl.ds(row, n1), :] = ti * cs - tr * sn
        return carry

    lax.fori_loop(0, n2, stage1, 0, unroll=LOOP_UNROLL)

    def stage2(k1, carry):
        rows = pl.ds(k1, n2, stride=n1)
        y = _bdot(d2, stacked(tr_ref, ti_ref, rows).astype(BF16))
        for g in groups:
            zr_ref[g, rows, :] = y[:, g * gd:(g + 1) * gd]
        return carry

    lax.fori_loop(0, n1, stage2, 0, unroll=LOOP_UNROLL)
    for g in groups:
        o_ref[0, :, g * gd:(g + 1) * gd] = zr_ref[g]


def _fourier(f3):
    nb, seq, width = f3.shape
    n2 = GRID_W
    n1 = seq // n2
    gd = FNET_GROUP_DIM
    cc, cs = _dft_mats(gd)
    c1, s1 = _dft_mats(n1)
    c2, s2 = _dft_mats(n2)
    norm = 1.0 / math.sqrt(seq * gd)
    cdft = jnp.asarray(np.concatenate([cc, -cs], axis=1) * norm, F32)
    d1 = jnp.asarray(np.block([[c1, s1], [-s1, c1]]), F32)
    d2 = jnp.asarray(np.concatenate([c2, s2], axis=1), F32)
    ang = 2.0 * np.pi * ((np.arange(n2)[:, None] * np.arange(n1)[None, :]) % seq) / seq
    twc = jnp.asarray(np.broadcast_to(np.cos(ang)[:, :, None], (n2, n1, gd)), F32)
    tws = jnp.asarray(np.broadcast_to(np.sin(ang)[:, :, None], (n2, n1, gd)), F32)
    return pl.pallas_call(
        functools.partial(_fourier_kernel, n1=n1, n2=n2),
        grid=(nb, width // FOURIER_COLS),
        in_specs=[
            pl.BlockSpec((1, seq, FOURIER_COLS), lambda b, g: (b, 0, g)),
            _const_spec(cdft.shape),
            _const_spec(d1.shape),
            _const_spec(d2.shape),
            _const_spec(twc.shape),
            _const_spec(tws.shape),
        ],
        out_specs=pl.BlockSpec((1, seq, FOURIER_COLS), lambda b, g: (b, 0, g)),
        out_shape=jax.ShapeDtypeStruct((nb, seq, width), F32),
        scratch_shapes=[pltpu.VMEM((FOURIER_COLS // gd, seq, gd), F32)] * 4,
        compiler_params=_params(("arbitrary", "arbitrary")),
        name="fourier",
    )(f3, cdft, d1, d2, twc, tws)


def _merge_kernel(fm_ref, o_ref_in, x_ref, m_ref, wgate_ref, wf_ref, wna_ref, wout_ref,
                  g_ref, b_ref, out_ref, *, alpha):
    tm, d = x_ref.shape
    shift = m_ref[0, 3:4, :]
    scale = m_ref[0, 4:5, :]
    gate = m_ref[0, 5:6, :]
    sr = min(tm, ROW_TILE)
    for r0 in range(0, tm, sr):
        rs = slice(r0, r0 + sr)
        x = x_ref[rs, :]
        u = (x * (1.0 + scale) + shift).astype(BF16)
        ga = _bdot(u, wgate_ref[:, :d])
        gb = _bdot(u, wgate_ref[:, d:])
        out_ref[rs, :] = _merge_out(fm_ref[rs, :], o_ref_in[rs, :], ga, gb, x, gate, wf_ref,
                                    wna_ref, wout_ref, g_ref[1:2, :], b_ref[1:2, :], alpha)


def _merge(fm2, o2, x2, mods_l, rows_per_group, group0, win, wf, wna, wout, layer, ln_g, ln_b,
           alpha):
    rows, d = x2.shape
    tm = min(MERGE_ROW_TILE, rows_per_group)
    tiles_per_group = rows_per_group // tm
    row_spec = lambda w: pl.BlockSpec((tm, w), lambda i: (i, 0))
    assert win.shape[2] == 2 * (2 * d), "gate columns must be the second half of w_in"
    return pl.pallas_call(
        functools.partial(_merge_kernel, alpha=alpha),
        grid=(rows // tm,),
        in_specs=[
            row_spec(fm2.shape[1]), row_spec(o2.shape[1]), row_spec(d),
            pl.BlockSpec((1, N_MOD, d), lambda i: (group0 + i // tiles_per_group, 0, 0)),
            _layer_spec(win, layer, cols=2 * d, col_block=1),
            _layer_spec(wf, layer), _layer_spec(wna, layer), _layer_spec(wout, layer),
            _const_spec(ln_g.shape), _const_spec(ln_b.shape),
        ],
        out_specs=row_spec(d),
        out_shape=jax.ShapeDtypeStruct((rows, d), F32),
        compiler_params=_params(("arbitrary",)),
        name="merge",
    )(fm2, o2, x2, mods_l, win, wf, wna, wout, ln_g, ln_b)


def kernel(x_prompt, x_sample, cache_k, cache_v, c, c_ctx, w_ada, b_ada, ln_g, ln_b, w_ff1_up,
           w_ff1_down, w_in, rpb, w_fourier, w_na_out, w_out, w_ff2_up, w_ff2_down):
    batch, seq, d = x_prompt.shape
    dec_batch, dec_seq, _ = x_sample.shape
    depth = w_ada.shape[0]
    alpha = (2 * depth) ** 0.25
    rows_lat = dec_seq // GRID_W

    cvec = jnp.zeros((MOD_ROWS, d), F32).at[0].set(c_ctx).at[1:1 + dec_batch].set(c)
    mods = _mods(cvec, w_ada, b_ada).reshape(depth, MOD_ROWS, N_MOD, d)

    y_p = x_prompt.reshape(batch * seq, d)
    y_s = x_sample.reshape(dec_batch * dec_seq, d)
    up1, dn1 = w_ff1_up.astype(BF16), w_ff1_down.astype(BF16)
    up2, dn2 = w_ff2_up.astype(BF16), w_ff2_down.astype(BF16)
    win = w_in.astype(BF16)
    wf = w_fourier.astype(BF16)
    wna = w_na_out.astype(BF16)
    wout = w_out.astype(BF16)
    ck = cache_k.transpose(1, 0, 3, 2, 4).reshape(depth, dec_batch, -1, NA_WIDTH).astype(BF16)
    cv = cache_v.transpose(1, 0, 3, 2, 4).reshape(depth, dec_batch, -1, NA_WIDTH).astype(BF16)

    bias = _nbr_bias_table(rpb.reshape((-1,) + rpb.shape[2:]), rows_lat)

    kv = None
    for l in range(depth):
        m_l = mods[l]
        g_l, b_l = ln_g[l], ln_b[l]

        y_p = _ffn(y_p, m_l, batch * seq, 0, up1, dn1, l, g_l, b_l, 0, alpha)
        y_p3, new_k, new_v = _ctx_mix(y_p.reshape(batch, seq, d), m_l, win, wf, wna, wout, g_l, b_l,
                                      alpha, l, depth, kv)
        kv = (new_k, new_v)
        y_p = _ffn(y_p3.reshape(batch * seq, d), m_l, batch * seq, 0, up2, dn2, l, g_l, b_l, 2, alpha)

        y_s = _ffn(y_s, m_l, dec_seq, 1, up1, dn1, l, g_l, b_l, 0, alpha)
        q2, k2, v2, f2 = _in_proj(y_s, m_l, dec_seq, 1, win, l)
        to3 = lambda t: t.reshape(dec_batch, dec_seq, t.shape[-1])
        o3 = _nbr_attn(to3(q2), to3(k2), to3(v2), ck[l], cv[l], bias, l)
        fm3 = _fourier(to3(f2))
        y_s = _merge(fm3.reshape(-1, FNET_WIDTH), o3.reshape(-1, NA_WIDTH), y_s, m_l,
                     dec_seq, 1, win, wf, wna, wout, l, g_l, b_l, alpha)
        y_s = _ffn(y_s, m_l, dec_seq, 1, up2, dn2, l, g_l, b_l, 2, alpha)

    return (y_p.reshape(batch, seq, d), y_s.reshape(dec_batch, dec_seq, d), new_k, new_v)
```

```python
import functools
import math

import numpy as np
import jax
import jax.numpy as jnp
from jax import lax
from jax.experimental import pallas as pl
from jax.experimental.pallas import tpu as pltpu

F32 = jnp.float32
BF16 = jnp.bfloat16

HEAD_DIM = 64
NA_HEADS = 8
NA_WIDTH = NA_HEADS * HEAD_DIM
FNET_GROUPS = 4
FNET_GROUP_DIM = 128
FNET_WIDTH = FNET_GROUPS * FNET_GROUP_DIM
GRID_W = 64
WIN_ROWS = 8
WIN_COLS = 16
N_SUB = 3
N_MOD = 3 * N_SUB
ATTN_SCALE = HEAD_DIM ** -0.5
LN_EPS = 1e-5
NEG_INF = -1e30

LANES = 128
MXU_DIM = 256
VMEM_LIMIT_BYTES = 56 * 1024 * 1024

ROW_TILE = 512
FFN_ROW_TILE = 1024
FFN_SUB_ROWS = 512
MERGE_ROW_TILE = 1024
FF_CHUNK = MXU_DIM
Q_ROWS = 4
K_ROWS = Q_ROWS + WIN_ROWS
KEY_BLOCK = GRID_W * math.gcd(Q_ROWS, WIN_ROWS // 2)
Q_BLOCKS_PER_STEP = 4
IN_CHUNK = 2 * MXU_DIM
MOD_ROWS = 8
MODS_K_ROWS = 256
FOURIER_COLS = 2 * FNET_GROUP_DIM
LOOP_UNROLL = 8
HI = lax.Precision.HIGHEST


def _params(sem):
    return pltpu.CompilerParams(dimension_semantics=sem, vmem_limit_bytes=VMEM_LIMIT_BYTES)


def _const_spec(shape):
    nd = len(shape)
    return pl.BlockSpec(shape, lambda *_: (0,) * nd, pipeline_mode=pl.Buffered(1))


def _layer_spec(w, layer, cols=None, col_block=0):
    _, r, n = w.shape
    cols = n if cols is None else cols
    return pl.BlockSpec((None, r, cols), lambda *_: (layer, 0, col_block),
                        pipeline_mode=pl.Buffered(1))


def _layer_norm(r, g, b):
    mu = jnp.mean(r, axis=-1, keepdims=True)
    d = r - mu
    var = jnp.mean(d * d, axis=-1, keepdims=True)
    return d * lax.rsqrt(var + LN_EPS) * g + b


def _bdot(a, b):
    return jnp.dot(a, b, preferred_element_type=F32)


def _dot_nt(a, b):
    return lax.dot_general(a, b, (((1,), (1,)), ((), ())), preferred_element_type=F32)


def _dft_mats(n):
    k = np.arange(n)
    ang = 2.0 * np.pi * ((k[:, None] * k[None, :]) % n) / n
    return np.cos(ang), np.sin(ang)


def _mods_kernel(c_ref, w_ref, b_ref, o_ref):
    k = pl.program_id(1)
    c = c_ref[...]
    s = (c * jax.nn.sigmoid(c)).astype(BF16)
    part = _bdot(s, w_ref[0].astype(BF16))

    @pl.when(k == 0)
    def _():
        o_ref[0] = part + b_ref[0]

    @pl.when(k > 0)
    def _():
        o_ref[0] += part


def _mods(cvec, w_ada, b_ada):
    depth, d, n = w_ada.shape
    tk = MODS_K_ROWS
    return pl.pallas_call(
        _mods_kernel,
        grid=(depth, d // tk),
        in_specs=[
            pl.BlockSpec((MOD_ROWS, tk), lambda l, k: (0, k)),
            pl.BlockSpec((1, tk, n), lambda l, k: (l, k, 0)),
            pl.BlockSpec((1, 1, n), lambda l, k: (l, 0, 0)),
        ],
        out_specs=pl.BlockSpec((1, MOD_ROWS, n), lambda l, k: (l, 0, 0)),
        out_shape=jax.ShapeDtypeStruct((depth, MOD_ROWS, n), F32),
        compiler_params=_params(("arbitrary", "arbitrary")),
        name="mods",
    )(cvec, w_ada, b_ada.reshape(depth, 1, n))


def _ffn_kernel(x_ref, m_ref, wup_ref, wdn_ref, g_ref, b_ref, o_ref, act_ref, *, sub, alpha):
    shift = m_ref[0, 3 * sub:3 * sub + 1, :]
    scale = m_ref[0, 3 * sub + 1:3 * sub + 2, :]
    gate = m_ref[0, 3 * sub + 2:3 * sub + 3, :]
    ff = wdn_ref.shape[0]
    tm = x_ref.shape[0]
    sr = min(tm, FFN_SUB_ROWS)
    slabs = [slice(r0, r0 + sr) for r0 in range(0, tm, sr)]
    us = [(x_ref[rs, :] * (1.0 + scale) + shift).astype(BF16) for rs in slabs]
    for c0 in range(0, ff, FF_CHUNK):
        for rs, u in zip(slabs, us):
            a = _bdot(u, wup_ref[:, c0:c0 + FF_CHUNK])
            g = _bdot(u, wup_ref[:, ff + c0:ff + c0 + FF_CHUNK])
            act_ref[rs, c0:c0 + FF_CHUNK] = ((g * jax.nn.sigmoid(g)) * a).astype(BF16)
    for rs in slabs:
        y = _bdot(act_ref[rs, :], wdn_ref[...])
        r = alpha * x_ref[rs, :] + (0.5 * gate) * y
        o_ref[rs, :] = _layer_norm(r, g_ref[sub:sub + 1, :], b_ref[sub:sub + 1, :])


def _ffn(x2, mods_l, rows_per_group, group0, wup, wdn, layer, ln_g, ln_b, sub, alpha):
    rows, d = x2.shape
    tm = min(FFN_ROW_TILE, rows_per_group)
    tiles_per_group = rows_per_group // tm
    return pl.pallas_call(
        functools.partial(_ffn_kernel, sub=sub, alpha=alpha),
        grid=(rows // tm,),
        in_specs=[
            pl.BlockSpec((tm, d), lambda i: (i, 0)),
            pl.BlockSpec((1, N_MOD, d), lambda i: (group0 + i // tiles_per_group, 0, 0)),
            _layer_spec(wup, layer),
            _layer_spec(wdn, layer),
            _const_spec(ln_g.shape),
            _const_spec(ln_b.shape),
        ],
        out_specs=pl.BlockSpec((tm, d), lambda i: (i, 0)),
        out_shape=jax.ShapeDtypeStruct((rows, d), F32),
        scratch_shapes=[pltpu.VMEM((tm, wdn.shape[1]), BF16)],
        compiler_params=_params(("arbitrary",)),
        name=f"ffn{sub}",
    )(x2, mods_l, wup, wdn, ln_g, ln_b)


def _row_reduce(arrays, combine, reduce):
    acc = None
    for a in arrays:
        for c0 in range(0, a.shape[1], LANES):
            chunk = a[:, c0:c0 + LANES]
            acc = chunk if acc is None else combine(acc, chunk)
    return reduce(acc, axis=-1, keepdims=True)


def _stacked_pair_attention(problems):
    lane = lax.broadcasted_iota(jnp.int32, (1, LANES), 1)
    scores = []
    for q, k_list, _, bias_list in problems:
        zero = jnp.zeros_like(q)
        q2 = jnp.concatenate([jnp.where(lane < HEAD_DIM, q, zero),
                              jnp.where(lane >= HEAD_DIM, q, zero)], axis=0)
        s = []
        for k, bias in zip(k_list, bias_list):
            sk = _dot_nt(q2, k)
            if bias is not None:
                sk = sk + jnp.concatenate(bias, axis=0)
            s.append(sk)
        scores.append(s)
    maxes = [_row_reduce(s, jnp.maximum, jnp.max) for s in scores]
    probs = [[jnp.exp(sk - m) for sk in s] for s, m in zip(scores, maxes)]
    dens = [_row_reduce(p, jnp.add, jnp.sum) for p in probs]
    outs = []
    for (q, _, v_list, _), p, den in zip(problems, probs, dens):
        o = None
        for pk, v in zip(p, v_list):
            pv = _bdot(pk.astype(BF16), v)
            o = pv if o is None else o + pv
        o = o / den
        m_rows = q.shape[0]
        outs.append(jnp.where(lane < HEAD_DIM, o[:m_rows], o[m_rows:]))
    return outs


def _merge_out(slabs, gate, wf_ref, wna_ref, wout_ref, ln_g, ln_b, alpha):
    branches = [(_bdot(fm.astype(BF16), wf_ref[...]), _bdot(o, wna_ref[...]))
                for fm, o, _, _, _ in slabs]
    merged = [(jax.nn.sigmoid(ga) * a + jax.nn.sigmoid(gb) * b).astype(BF16)
              for (_, _, ga, gb, _), (a, b) in zip(slabs, branches)]
    mixes = [_bdot(mg, wout_ref[...]) for mg in merged]
    return [_layer_norm(alpha * x + gate * mix, ln_g, ln_b)
            for (_, _, _, _, x), mix in zip(slabs, mixes)]


def _ctx_mix_kernel(x_ref, m_ref, win_ref, wf_ref, wna_ref, wout_ref, g_ref, b_ref,
                    cdft_ref, pdft_ref, *rest, alpha, n_alias, kv_slot):
    o_ref, k_ref, v_ref, z_ref, oatt_ref, fm_ref = rest[n_alias:]
    nb, seq, d = x_ref.shape
    rows = nb * seq
    x = x_ref[...].reshape(rows, d)
    shift = m_ref[0, 3:4, :]
    scale = m_ref[0, 4:5, :]
    gate = m_ref[0, 5:6, :]
    u = (x * (1.0 + scale) + shift).astype(BF16)
    for c0 in range(0, win_ref.shape[1], IN_CHUNK):
        z_ref[:, c0:c0 + IN_CHUNK] = _bdot(u, win_ref[:, c0:c0 + IN_CHUNK])

    cdft = cdft_ref[...].astype(BF16)
    pdft = pdft_ref[...].astype(BF16)
    q_off, k_off, v_off, f_off = 0, NA_WIDTH, 2 * NA_WIDTH, 3 * NA_WIDTH
    ga_off = f_off + FNET_WIDTH
    gb_off = ga_off + d
    for b in range(nb):
        r0 = b * seq
        for h in range(NA_HEADS):
            k_ref[b, kv_slot, h] = z_ref[r0:r0 + seq, k_off + h * HEAD_DIM:k_off + (h + 1) * HEAD_DIM]
            v_ref[b, kv_slot, h] = z_ref[r0:r0 + seq, v_off + h * HEAD_DIM:v_off + (h + 1) * HEAD_DIM]
        for slot in range(k_ref.shape[1]):
            if slot != kv_slot:
                k_ref[b, slot] = jnp.zeros(k_ref.shape[2:], F32)
                v_ref[b, slot] = jnp.zeros(v_ref.shape[2:], F32)
        zc = [_bdot(z_ref[r0:r0 + seq, f_off + g * FNET_GROUP_DIM:f_off + (g + 1) * FNET_GROUP_DIM]
                    .astype(BF16), cdft).astype(BF16) for g in range(FNET_GROUPS)]
        stacked = jnp.concatenate(
            [jnp.concatenate([z[:, :FNET_GROUP_DIM] for z in zc], axis=1),
             jnp.concatenate([z[:, FNET_GROUP_DIM:] for z in zc], axis=1)], axis=0)
        fm_ref[r0:r0 + seq, :] = _bdot(pdft, stacked)

    problems = []
    for b in range(nb):
        r0 = b * seq
        for hp in range(NA_WIDTH // LANES):
            c0 = hp * LANES
            q = (z_ref[r0:r0 + seq, q_off + c0:q_off + c0 + LANES] * ATTN_SCALE).astype(BF16)
            k = z_ref[r0:r0 + seq, k_off + c0:k_off + c0 + LANES].astype(BF16)
            v = z_ref[r0:r0 + seq, v_off + c0:v_off + c0 + LANES].astype(BF16)
            problems.append((b * seq, c0, (q, [k], [v], [None])))
    outs = _stacked_pair_attention([p for _, _, p in problems])
    for (r0, c0, _), o in zip(problems, outs):
        oatt_ref[r0:r0 + seq, c0:c0 + LANES] = o.astype(BF16)

    slab = (fm_ref[...], oatt_ref[...], z_ref[:, ga_off:ga_off + d], z_ref[:, gb_off:gb_off + d], x)
    (out,) = _merge_out([slab], gate, wf_ref, wna_ref, wout_ref, g_ref[1:2, :], b_ref[1:2, :],
                        alpha)
    o_ref[...] = out.reshape(nb, seq, d)


def _ctx_mix(x3, mods_l, win, wf, wna, wout, ln_g, ln_b, alpha, layer, depth, kv_prev):
    batch, seq, d = x3.shape
    nb = 2 if batch % 2 == 0 else 1
    in_width = win.shape[2]
    cc, cs = _dft_mats(FNET_GROUP_DIM)
    pc, ps = _dft_mats(seq)
    norm = 1.0 / math.sqrt(seq * FNET_GROUP_DIM)
    cdft = jnp.asarray(np.concatenate([cc, -cs], axis=1) * norm, F32)
    pdft = jnp.asarray(np.concatenate([pc, ps], axis=1), F32)
    kv_shape = jax.ShapeDtypeStruct((batch, depth, NA_HEADS, seq, HEAD_DIM), F32)
    if kv_prev is None:
        kv_spec = pl.BlockSpec((nb, depth, NA_HEADS, seq, HEAD_DIM), lambda i: (i, 0, 0, 0, 0))
        kv_slot = layer
    else:
        kv_spec = pl.BlockSpec((nb, 1, NA_HEADS, seq, HEAD_DIM), lambda i: (i, layer, 0, 0, 0))
        kv_slot = 0
    operands = [x3, mods_l, win, wf, wna, wout, ln_g, ln_b, cdft, pdft]
    in_specs = [
        pl.BlockSpec((nb, seq, d), lambda i: (i, 0, 0)),
        pl.BlockSpec((1, N_MOD, d), lambda i: (0, 0, 0)),
        _layer_spec(win, layer),
        _layer_spec(wf, layer),
        _layer_spec(wna, layer),
        _layer_spec(wout, layer),
        _const_spec(ln_g.shape),
        _const_spec(ln_b.shape),
        _const_spec(cdft.shape),
        _const_spec(pdft.shape),
    ]
    aliases = {}
    if kv_prev is not None:
        aliases = {len(operands): 1, len(operands) + 1: 2}
        operands += list(kv_prev)
        in_specs += [pl.BlockSpec(memory_space=pl.ANY)] * 2
    return pl.pallas_call(
        functools.partial(_ctx_mix_kernel, alpha=alpha, n_alias=len(aliases), kv_slot=kv_slot),
        grid=(batch // nb,),
        in_specs=in_specs,
        out_specs=[pl.BlockSpec((nb, seq, d), lambda i: (i, 0, 0)), kv_spec, kv_spec],
        out_shape=[jax.ShapeDtypeStruct((batch, seq, d), F32), kv_shape, kv_shape],
        input_output_aliases=aliases,
        scratch_shapes=[
            pltpu.VMEM((nb * seq, in_width), F32),
            pltpu.VMEM((nb * seq, NA_WIDTH), BF16),
            pltpu.VMEM((nb * seq, FNET_WIDTH), F32),
        ],
        compiler_params=_params(("arbitrary",)),
        name="ctx_mix",
    )(*operands)


def _in_proj_kernel(x_ref, m_ref, win_ref, q_ref, k_ref, v_ref, f_ref):
    x = x_ref[...]
    shift = m_ref[0, 3:4, :]
    scale = m_ref[0, 4:5, :]
    u = (x * (1.0 + scale) + shift).astype(BF16)
    cw = NA_WIDTH
    q_ref[...] = (_bdot(u, win_ref[:, 0:cw]) * ATTN_SCALE).astype(BF16)
    k_ref[...] = _bdot(u, win_ref[:, cw:2 * cw]).astype(BF16)
    v_ref[...] = _bdot(u, win_ref[:, 2 * cw:3 * cw]).astype(BF16)
    f_ref[...] = _bdot(u, win_ref[:, 3 * cw:3 * cw + FNET_WIDTH])


def _in_proj(x2, mods_l, rows_per_group, group0, win, layer):
    rows, d = x2.shape
    tm = min(ROW_TILE, rows_per_group)
    tiles_per_group = rows_per_group // tm
    cw = NA_WIDTH
    qkvf = 3 * NA_WIDTH + FNET_WIDTH
    row_spec = lambda w: pl.BlockSpec((tm, w), lambda i: (i, 0))
    return pl.pallas_call(
        _in_proj_kernel,
        grid=(rows // tm,),
        in_specs=[
            row_spec(d),
            pl.BlockSpec((1, N_MOD, d), lambda i: (group0 + i // tiles_per_group, 0, 0)),
            _layer_spec(win, layer, cols=qkvf, col_block=0),
        ],
        out_specs=[row_spec(cw), row_spec(cw), row_spec(cw), row_spec(FNET_WIDTH)],
        out_shape=[
            jax.ShapeDtypeStruct((rows, cw), BF16),
            jax.ShapeDtypeStruct((rows, cw), BF16),
            jax.ShapeDtypeStruct((rows, cw), BF16),
            jax.ShapeDtypeStruct((rows, FNET_WIDTH), F32),
        ],
        compiler_params=_params(("arbitrary",)),
        name="in_proj",
    )(x2, mods_l, win)


def _window_start_row(i, rows):
    return jnp.clip(Q_ROWS * i - WIN_ROWS // 2, 0, rows - K_ROWS)


def _nbr_attn_kernel(q_ref, k_ref, v_ref, ck_ref, cv_ref, *rest, rows):
    bias_refs, o_ref = rest[:-1], rest[-1]
    step = pl.program_id(1)
    n_keys = K_ROWS * GRID_W
    tq = Q_ROWS * GRID_W
    problems, places = [], []
    for j, bias_ref in enumerate(bias_refs):
        i = step * len(bias_refs) + j
        start = pl.multiple_of(_window_start_row(i, rows) * GRID_W, KEY_BLOCK)
        bias = (bias_ref[0, 0], bias_ref[1, 0])
        for b in range(q_ref.shape[0]):
            problems.append(
                (q_ref[b, j * tq:(j + 1) * tq, :],
                 [k_ref[b, pl.ds(start, n_keys), :], ck_ref[b]],
                 [v_ref[b, pl.ds(start, n_keys), :], cv_ref[b]],
                 [bias, None]))
            places.append((b, j))
    for (b, j), o in zip(places, _stacked_pair_attention(problems)):
        o_ref[b, j * tq:(j + 1) * tq, :] = o.astype(BF16)


def _bias_rows_kernel(rpb_ref, onehot_ref, neg_ref, o_ref):
    o_ref[...] = jnp.dot(rpb_ref[...], onehot_ref[...], precision=HI,
                         preferred_element_type=F32) + neg_ref[...]


def _nbr_bias_table(rpb_l, rows):
    h, n_dr, n_dc = rpb_l.shape
    n_blocks = rows // Q_ROWS
    kr = min(WIN_ROWS, rows)
    variants = np.array([0, min(1, n_blocks - 1), n_blocks - 1])
    r = Q_ROWS * variants[:, None] + np.arange(Q_ROWS)[None, :]
    ks = np.clip(Q_ROWS * variants - WIN_ROWS // 2, 0, rows - K_ROWS)
    key_row = ks[:, None] + np.arange(K_ROWS)[None, :]
    r0 = np.clip(r - kr // 2, 0, rows - kr)
    row_ok = (key_row[:, None, :] >= r0[:, :, None]) & (key_row[:, None, :] < r0[:, :, None] + kr)
    dr = np.clip(key_row[:, None, :] - r[:, :, None] + (WIN_ROWS - 1), 0, n_dr - 1)
    c_idx = np.arange(GRID_W)
    c0 = np.clip(c_idx - WIN_COLS // 2, 0, GRID_W - WIN_COLS)
    col_ok = (c_idx[None, :] >= c0[:, None]) & (c_idx[None, :] < c0[:, None] + WIN_COLS)
    dc = np.clip(c_idx[None, :] - c_idx[:, None] + (WIN_COLS - 1), 0, n_dc - 1)

    k_pad = -(-n_dc // LANES) * LANES
    m_pad = -(-(h * n_dr) // 8) * 8
    onehot = np.zeros((k_pad, GRID_W * GRID_W), np.float32)
    flat = np.arange(GRID_W * GRID_W).reshape(GRID_W, GRID_W)
    onehot[dc[col_ok], flat[col_ok]] = 1.0
    neg = np.where(col_ok, 0.0, NEG_INF).astype(np.float32).reshape(1, -1)
    rpb_pad = jnp.zeros((m_pad, k_pad), F32).at[:h * n_dr, :n_dc].set(
        rpb_l.reshape(h * n_dr, n_dc).astype(F32))
    col_tab = pl.pallas_call(
        _bias_rows_kernel,
        out_shape=jax.ShapeDtypeStruct((m_pad, GRID_W * GRID_W), F32),
        name="bias_rows",
    )(rpb_pad, jnp.asarray(onehot), jnp.asarray(neg))
    col_tab = col_tab[:h * n_dr].reshape(h, n_dr, GRID_W, GRID_W).transpose(0, 2, 1, 3)
    pad = K_ROWS + Q_ROWS
    col_tab = jnp.pad(col_tab, ((0, 0), (0, 0), (pad, pad), (0, 0)))
    col_tab = col_tab.reshape(h, GRID_W, (n_dr + 2 * pad) * GRID_W)
    slabs = []
    for v in range(3):
        for qr in range(Q_ROWS):
            d0 = int(ks[v] - r[v, qr]) + (WIN_ROWS - 1) + pad
            slab = col_tab[:, :, d0 * GRID_W:(d0 + K_ROWS) * GRID_W]
            lane_ok = jnp.asarray(np.repeat(row_ok[v, qr], GRID_W))
            slabs.append(jnp.where(lane_ok[None, None, :], slab, NEG_INF))
    return jnp.stack(slabs, axis=1).reshape(h, 3, Q_ROWS * GRID_W, K_ROWS * GRID_W)


def _nbr_attn(q3, k3, v3, ck3, cv3, bias, layer):
    nb, seq, width = q3.shape
    rows = seq // GRID_W
    n_blocks = rows // Q_ROWS
    n_pairs = width // LANES
    past = ck3.shape[1]
    tq = Q_ROWS * GRID_W
    tk = K_ROWS * GRID_W

    per_step = Q_BLOCKS_PER_STEP if n_blocks % Q_BLOCKS_PER_STEP == 0 else 1

    def variant(i):
        return jnp.where(i == 0, 0, jnp.where(i == n_blocks - 1, 2, 1))

    def bias_spec(j):
        return pl.BlockSpec((2, 1, tq, tk),
                            lambda p, s: (layer * n_pairs + p, variant(s * per_step + j), 0, 0))

    return pl.pallas_call(
        functools.partial(_nbr_attn_kernel, rows=rows),
        grid=(n_pairs, n_blocks // per_step),
        in_specs=[
            pl.BlockSpec((nb, per_step * tq, LANES), lambda p, s: (0, s, p)),
            pl.BlockSpec((nb, seq, LANES), lambda p, s: (0, 0, p)),
            pl.BlockSpec((nb, seq, LANES), lambda p, s: (0, 0, p)),
            pl.BlockSpec((nb, past, LANES), lambda p, s: (0, 0, p)),
            pl.BlockSpec((nb, past, LANES), lambda p, s: (0, 0, p)),
        ] + [bias_spec(j) for j in range(per_step)],
        out_specs=pl.BlockSpec((nb, per_step * tq, LANES), lambda p, s: (0, s, p)),
        out_shape=jax.ShapeDtypeStruct((nb, seq, width), BF16),
        compiler_params=_params(("arbitrary", "arbitrary")),
        name="nbr_attn",
    )(q3, k3, v3, ck3, cv3, *([bias] * per_step))


def _fourier_kernel(x_ref, cdft_ref, d1_ref, d2_ref, twc_ref, tws_ref, o_ref,
                    zr_ref, zi_ref, tr_ref, ti_ref, *, n1, n2):
    gd = cdft_ref.shape[0]
    n_groups = x_ref.shape[2] // gd
    groups = range(n_groups)
    cdft = cdft_ref[...].astype(BF16)
    d1 = d1_ref[...].astype(BF16)
    d2 = d2_ref[...].astype(BF16)
    for g in groups:
        zc = _bdot(x_ref[0, :, g * gd:(g + 1) * gd].astype(BF16), cdft)
        zr_ref[g] = zc[:, :gd]
        zi_ref[g] = zc[:, gd:]

    def stacked(re_ref, im_ref, rows):
        return jnp.concatenate(
            [jnp.concatenate([re_ref[g, rows, :], im_ref[g, rows, :]], axis=0) for g in groups],
            axis=1)

    def stage1(c, carry):
        t = _bdot(d1, stacked(zr_ref, zi_ref, pl.ds(c, n1, stride=n2)).astype(BF16))
        cs = twc_ref[c]
        sn = tws_ref[c]
        row = pl.multiple_of(c * n1, n1)
        for g in groups:
            tr = t[:n1, g * gd:(g + 1) * gd]
            ti = t[n1:, g * gd:(g + 1) * gd]
            tr_ref[g, pl.ds(row, n1), :] = tr * cs + ti * sn
            ti_ref[g, pl.ds(row, n1), :] = ti * cs - tr * sn
        return carry

    lax.fori_loop(0, n2, stage1, 0, unroll=LOOP_UNROLL)

    def stage2(k1, carry):
        rows = pl.ds(k1, n2, stride=n1)
        y = _bdot(d2, stacked(tr_ref, ti_ref, rows).astype(BF16))
        for g in groups:
            zr_ref[g, rows, :] = y[:, g * gd:(g + 1) * gd]
        return carry

    lax.fori_loop(0, n1, stage2, 0, unroll=LOOP_UNROLL)
    for g in groups:
        o_ref[0, :, g * gd:(g + 1) * gd] = zr_ref[g]


def _fourier(f3):
    nb, seq, width = f3.shape
    n2 = GRID_W
    n1 = seq // n2
    gd = FNET_GROUP_DIM
    cc, cs = _dft_mats(gd)
    c1, s1 = _dft_mats(n1)
    c2, s2 = _dft_mats(n2)
    norm = 1.0 / math.sqrt(seq * gd)
    cdft = jnp.asarray(np.concatenate([cc, -cs], axis=1) * norm, F32)
    d1 = jnp.asarray(np.block([[c1, s1], [-s1, c1]]), F32)
    d2 = jnp.asarray(np.concatenate([c2, s2], axis=1), F32)
    ang = 2.0 * np.pi * ((np.arange(n2)[:, None] * np.arange(n1)[None, :]) % seq) / seq
    twc = jnp.asarray(np.broadcast_to(np.cos(ang)[:, :, None], (n2, n1, gd)), F32)
    tws = jnp.asarray(np.broadcast_to(np.sin(ang)[:, :, None], (n2, n1, gd)), F32)
    return pl.pallas_call(
        functools.partial(_fourier_kernel, n1=n1, n2=n2),
        grid=(nb, width // FOURIER_COLS),
        in_specs=[
            pl.BlockSpec((1, seq, FOURIER_COLS), lambda b, g: (b, 0, g)),
            _const_spec(cdft.shape),
            _const_spec(d1.shape),
            _const_spec(d2.shape),
            _const_spec(twc.shape),
            _const_spec(tws.shape),
        ],
        out_specs=pl.BlockSpec((1, seq, FOURIER_COLS), lambda b, g: (b, 0, g)),
        out_shape=jax.ShapeDtypeStruct((nb, seq, width), F32),
        scratch_shapes=[pltpu.VMEM((FOURIER_COLS // gd, seq, gd), F32)] * 4,
        compiler_params=_params(("arbitrary", "arbitrary")),
        name="fourier",
    )(f3, cdft, d1, d2, twc, tws)


def _merge_kernel(fm_ref, o_ref_in, x_ref, m_ref, wgate_ref, wf_ref, wna_ref, wout_ref,
                  g_ref, b_ref, out_ref, *, alpha):
    tm, d = x_ref.shape
    shift = m_ref[0, 3:4, :]
    scale = m_ref[0, 4:5, :]
    gate = m_ref[0, 5:6, :]
    sr = min(tm, ROW_TILE)
    rows = [slice(r0, r0 + sr) for r0 in range(0, tm, sr)]
    xs = [x_ref[rs, :] for rs in rows]
    us = [(x * (1.0 + scale) + shift).astype(BF16) for x in xs]
    slabs = [(fm_ref[rs, :], o_ref_in[rs, :], _bdot(u, wgate_ref[:, :d]), _bdot(u, wgate_ref[:, d:]), x)
             for rs, u, x in zip(rows, us, xs)]
    outs = _merge_out(slabs, gate, wf_ref, wna_ref, wout_ref, g_ref[1:2, :], b_ref[1:2, :], alpha)
    for rs, out in zip(rows, outs):
        out_ref[rs, :] = out


def _merge(fm2, o2, x2, mods_l, rows_per_group, group0, win, wf, wna, wout, layer, ln_g, ln_b,
           alpha):
    rows, d = x2.shape
    tm = min(MERGE_ROW_TILE, rows_per_group)
    tiles_per_group = rows_per_group // tm
    row_spec = lambda w: pl.BlockSpec((tm, w), lambda i: (i, 0))
    assert win.shape[2] == 2 * (2 * d), "gate columns must be the second half of w_in"
    return pl.pallas_call(
        functools.partial(_merge_kernel, alpha=alpha),
        grid=(rows // tm,),
        in_specs=[
            row_spec(fm2.shape[1]), row_spec(o2.shape[1]), row_spec(d),
            pl.BlockSpec((1, N_MOD, d), lambda i: (group0 + i // tiles_per_group, 0, 0)),
            _layer_spec(win, layer, cols=2 * d, col_block=1),
            _layer_spec(wf, layer), _layer_spec(wna, layer), _layer_spec(wout, layer),
            _const_spec(ln_g.shape), _const_spec(ln_b.shape),
        ],
        out_specs=row_spec(d),
        out_shape=jax.ShapeDtypeStruct((rows, d), F32),
        compiler_params=_params(("arbitrary",)),
        name="merge",
    )(fm2, o2, x2, mods_l, win, wf, wna, wout, ln_g, ln_b)


def kernel(x_prompt, x_sample, cache_k, cache_v, c, c_ctx, w_ada, b_ada, ln_g, ln_b, w_ff1_up,
           w_ff1_down, w_in, rpb, w_fourier, w_na_out, w_out, w_ff2_up, w_ff2_down):
    batch, seq, d = x_prompt.shape
    dec_batch, dec_seq, _ = x_sample.shape
    depth = w_ada.shape[0]
    alpha = (2 * depth) ** 0.25
    rows_lat = dec_seq // GRID_W

    cvec = jnp.zeros((MOD_ROWS, d), F32).at[0].set(c_ctx).at[1:1 + dec_batch].set(c)
    mods = _mods(cvec, w_ada, b_ada).reshape(depth, MOD_ROWS, N_MOD, d)

    y_p = x_prompt.reshape(batch * seq, d)
    y_s = x_sample.reshape(dec_batch * dec_seq, d)
    up1, dn1 = w_ff1_up.astype(BF16), w_ff1_down.astype(BF16)
    up2, dn2 = w_ff2_up.astype(BF16), w_ff2_down.astype(BF16)
    win = w_in.astype(BF16)
    wf = w_fourier.astype(BF16)
    wna = w_na_out.astype(BF16)
    wout = w_out.astype(BF16)
    ck = cache_k.transpose(1, 0, 3, 2, 4).reshape(depth, dec_batch, -1, NA_WIDTH).astype(BF16)
    cv = cache_v.transpose(1, 0, 3, 2, 4).reshape(depth, dec_batch, -1, NA_WIDTH).astype(BF16)

    bias = _nbr_bias_table(rpb.reshape((-1,) + rpb.shape[2:]), rows_lat)

    kv = None
    for l in range(depth):
        m_l = mods[l]
        g_l, b_l = ln_g[l], ln_b[l]

        y_p = _ffn(y_p, m_l, batch * seq, 0, up1, dn1, l, g_l, b_l, 0, alpha)
        y_p3, new_k, new_v = _ctx_mix(y_p.reshape(batch, seq, d), m_l, win, wf, wna, wout, g_l, b_l,
                                      alpha, l, depth, kv)
        kv = (new_k, new_v)
        y_p = _ffn(y_p3.reshape(batch * seq, d), m_l, batch * seq, 0, up2, dn2, l, g_l, b_l, 2, alpha)

        y_s = _ffn(y_s, m_l, dec_seq, 1, up1, dn1, l, g_l, b_l, 0, alpha)
        q2, k2, v2, f2 = _in_proj(y_s, m_l, dec_seq, 1, win, l)
        to3 = lambda t: t.reshape(dec_batch, dec_seq, t.shape[-1])
        o3 = _nbr_attn(to3(q2), to3(k2), to3(v2), ck[l], cv[l], bias, l)
        fm3 = _fourier(to3(f2))
        y_s = _merge(fm3.reshape(-1, FNET_WIDTH), o3.reshape(-1, NA_WIDTH), y_s, m_l,
                     dec_seq, 1, win, wf, wna, wout, l, g_l, b_l, alpha)
        y_s = _ffn(y_s, m_l, dec_seq, 1, up2, dn2, l, g_l, b_l, 2, alpha)

    return (y_p.reshape(batch, seq, d), y_s.reshape(dec_batch, dec_seq, d), new_k, new_v)
```

```python
import functools
import math

import numpy as np
import jax
import jax.numpy as jnp
from jax import lax
from jax.experimental import pallas as pl
from jax.experimental.pallas import tpu as pltpu

F32 = jnp.float32
BF16 = jnp.bfloat16

HEAD_DIM = 64
NA_HEADS = 8
NA_WIDTH = NA_HEADS * HEAD_DIM
FNET_GROUPS = 4
FNET_GROUP_DIM = 128
FNET_WIDTH = FNET_GROUPS * FNET_GROUP_DIM
GRID_W = 64
WIN_ROWS = 8
WIN_COLS = 16
N_SUB = 3
N_MOD = 3 * N_SUB
ATTN_SCALE = HEAD_DIM ** -0.5
LN_EPS = 1e-5
NEG_INF = -1e30

LANES = 128
MXU_DIM = 256
VMEM_LIMIT_BYTES = 56 * 1024 * 1024

ROW_TILE = 512
FFN_ROW_TILE = 1024
FFN_SUB_ROWS = 512
MERGE_ROW_TILE = 1024
FF_CHUNK = MXU_DIM
Q_ROWS = 4
K_ROWS = Q_ROWS + WIN_ROWS
KEY_BLOCK = GRID_W * math.gcd(Q_ROWS, WIN_ROWS // 2)
Q_BLOCKS_PER_STEP = 4
IN_CHUNK = 2 * MXU_DIM
MOD_ROWS = 8
MODS_K_ROWS = 256
FOURIER_COLS = 2 * FNET_GROUP_DIM
LOOP_UNROLL = 8
HI = lax.Precision.HIGHEST


def _params(sem):
    return pltpu.CompilerParams(dimension_semantics=sem, vmem_limit_bytes=VMEM_LIMIT_BYTES)


def _const_spec(shape):
    nd = len(shape)
    return pl.BlockSpec(shape, lambda *_: (0,) * nd, pipeline_mode=pl.Buffered(1))


def _layer_spec(w, layer, cols=None, col_block=0):
    _, r, n = w.shape
    cols = n if cols is None else cols
    return pl.BlockSpec((None, r, cols), lambda *_: (layer, 0, col_block),
                        pipeline_mode=pl.Buffered(1))


def _layer_norm(r, g, b):
    mu = jnp.mean(r, axis=-1, keepdims=True)
    d = r - mu
    var = jnp.mean(d * d, axis=-1, keepdims=True)
    return d * lax.rsqrt(var + LN_EPS) * g + b


def _bdot(a, b):
    return jnp.dot(a, b, preferred_element_type=F32)


def _dot_nt(a, b):
    return lax.dot_general(a, b, (((1,), (1,)), ((), ())), preferred_element_type=F32)


def _dft_mats(n):
    k = np.arange(n)
    ang = 2.0 * np.pi * ((k[:, None] * k[None, :]) % n) / n
    return np.cos(ang), np.sin(ang)


def _mods_kernel(c_ref, w_ref, b_ref, o_ref):
    k = pl.program_id(1)
    c = c_ref[...]
    s = (c * jax.nn.sigmoid(c)).astype(BF16)
    part = _bdot(s, w_ref[0].astype(BF16))

    @pl.when(k == 0)
    def _():
        o_ref[0] = part + b_ref[0]

    @pl.when(k > 0)
    def _():
        o_ref[0] += part


def _mods(cvec, w_ada, b_ada):
    depth, d, n = w_ada.shape
    tk = MODS_K_ROWS
    return pl.pallas_call(
        _mods_kernel,
        grid=(depth, d // tk),
        in_specs=[
            pl.BlockSpec((MOD_ROWS, tk), lambda l, k: (0, k)),
            pl.BlockSpec((1, tk, n), lambda l, k: (l, k, 0)),
            pl.BlockSpec((1, 1, n), lambda l, k: (l, 0, 0)),
        ],
        out_specs=pl.BlockSpec((1, MOD_ROWS, n), lambda l, k: (l, 0, 0)),
        out_shape=jax.ShapeDtypeStruct((depth, MOD_ROWS, n), F32),
        compiler_params=_params(("arbitrary", "arbitrary")),
        name="mods",
    )(cvec, w_ada, b_ada.reshape(depth, 1, n))


def _ffn_kernel(*refs, sub, alpha, seg_starts):
    n_seg = len(seg_starts)
    x_refs = refs[:n_seg]
    m_ref, wup_ref, wdn_ref, g_ref, b_ref, o_ref, act_ref = refs[n_seg:]
    step = pl.program_id(0)

    def read_x(rs):
        x = x_refs[-1][rs, :]
        for s in range(n_seg - 2, -1, -1):
            x = jnp.where(step < seg_starts[s + 1], x_refs[s][rs, :], x)
        return x

    shift = m_ref[0, 3 * sub:3 * sub + 1, :]
    scale = m_ref[0, 3 * sub + 1:3 * sub + 2, :]
    gate = m_ref[0, 3 * sub + 2:3 * sub + 3, :]
    ff = wdn_ref.shape[0]
    tm = o_ref.shape[0]
    sr = min(tm, FFN_SUB_ROWS)
    slabs = [slice(r0, r0 + sr) for r0 in range(0, tm, sr)]
    us = [(read_x(rs) * (1.0 + scale) + shift).astype(BF16) for rs in slabs]
    for c0 in range(0, ff, FF_CHUNK):
        for rs, u in zip(slabs, us):
            a = _bdot(u, wup_ref[:, c0:c0 + FF_CHUNK])
            g = _bdot(u, wup_ref[:, ff + c0:ff + c0 + FF_CHUNK])
            act_ref[rs, c0:c0 + FF_CHUNK] = ((g * jax.nn.sigmoid(g)) * a).astype(BF16)
    for rs in slabs:
        y = _bdot(act_ref[rs, :], wdn_ref[...])
        r = alpha * read_x(rs) + (0.5 * gate) * y
        o_ref[rs, :] = _layer_norm(r, g_ref[sub:sub + 1, :], b_ref[sub:sub + 1, :])


def _ffn(segments, mods_l, wup, wdn, layer, ln_g, ln_b, sub, alpha):
    d = segments[0][0].shape[1]
    tm = FFN_ROW_TILE
    for _, first_row, n_rows, _, rows_per_group in segments:
        tm = math.gcd(math.gcd(tm, rows_per_group), math.gcd(n_rows, first_row))
    starts, n_steps = [], 0
    for _, _, n_rows, _, _ in segments:
        starts.append(n_steps)
        n_steps += n_rows // tm

    def x_spec(s):
        _, first_row, n_rows, _, _ = segments[s]
        first, count = first_row // tm, n_rows // tm
        return pl.BlockSpec((tm, d), lambda i: (first + jnp.clip(i - starts[s], 0, count - 1), 0))

    def group(i):
        g = None
        for s in range(len(segments) - 1, -1, -1):
            _, _, _, group0, rows_per_group = segments[s]
            g_s = group0 + jnp.maximum(i - starts[s], 0) // (rows_per_group // tm)
            g = g_s if g is None else jnp.where(i < starts[s + 1], g_s, g)
        return g

    return pl.pallas_call(
        functools.partial(_ffn_kernel, sub=sub, alpha=alpha, seg_starts=tuple(starts)),
        grid=(n_steps,),
        in_specs=[x_spec(s) for s in range(len(segments))] + [
            pl.BlockSpec((1, N_MOD, d), lambda i: (group(i), 0, 0)),
            _layer_spec(wup, layer),
            _layer_spec(wdn, layer),
            _const_spec(ln_g.shape),
            _const_spec(ln_b.shape),
        ],
        out_specs=pl.BlockSpec((tm, d), lambda i: (i, 0)),
        out_shape=jax.ShapeDtypeStruct((n_steps * tm, d), F32),
        scratch_shapes=[pltpu.VMEM((tm, wdn.shape[1]), BF16)],
        compiler_params=_params(("arbitrary",)),
        name=f"ffn{sub}",
    )(*[seg[0] for seg in segments], mods_l, wup, wdn, ln_g, ln_b)


def _row_reduce(arrays, combine, reduce):
    acc = None
    for a in arrays:
        for c0 in range(0, a.shape[1], LANES):
            chunk = a[:, c0:c0 + LANES]
            acc = chunk if acc is None else combine(acc, chunk)
    return reduce(acc, axis=-1, keepdims=True)


def _stacked_pair_attention(problems):
    lane = lax.broadcasted_iota(jnp.int32, (1, LANES), 1)
    scores = []
    for q, k_list, _, bias_list in problems:
        zero = jnp.zeros_like(q)
        q2 = jnp.concatenate([jnp.where(lane < HEAD_DIM, q, zero),
                              jnp.where(lane >= HEAD_DIM, q, zero)], axis=0)
        s = []
        for k, bias in zip(k_list, bias_list):
            sk = _dot_nt(q2, k)
            if bias is not None:
                sk = sk + jnp.concatenate(bias, axis=0)
            s.append(sk)
        scores.append(s)
    maxes = [_row_reduce(s, jnp.maximum, jnp.max) for s in scores]
    probs = [[jnp.exp(sk - m) for sk in s] for s, m in zip(scores, maxes)]
    dens = [_row_reduce(p, jnp.add, jnp.sum) for p in probs]
    outs = []
    for (q, _, v_list, _), p, den in zip(problems, probs, dens):
        o = None
        for pk, v in zip(p, v_list):
            pv = _bdot(pk.astype(BF16), v)
            o = pv if o is None else o + pv
        o = o / den
        m_rows = q.shape[0]
        outs.append(jnp.where(lane < HEAD_DIM, o[:m_rows], o[m_rows:]))
    return outs


def _merge_out(slabs, gate, wf_ref, wna_ref, wout_ref, ln_g, ln_b, alpha):
    branches = [(_bdot(fm.astype(BF16), wf_ref[...]), _bdot(o, wna_ref[...]))
                for fm, o, _, _, _ in slabs]
    merged = [(jax.nn.sigmoid(ga) * a + jax.nn.sigmoid(gb) * b).astype(BF16)
              for (_, _, ga, gb, _), (a, b) in zip(slabs, branches)]
    mixes = [_bdot(mg, wout_ref[...]) for mg in merged]
    return [_layer_norm(alpha * x + gate * mix, ln_g, ln_b)
            for (_, _, _, _, x), mix in zip(slabs, mixes)]


def _ctx_mix_kernel(x_ref, m_ref, win_ref, wf_ref, wna_ref, wout_ref, g_ref, b_ref,
                    cdft_ref, pdft_ref, *rest, alpha, n_alias, kv_slot):
    o_ref, k_ref, v_ref, z_ref, oatt_ref, fm_ref = rest[n_alias:]
    nb, seq, d = x_ref.shape
    rows = nb * seq
    x = x_ref[...].reshape(rows, d)
    shift = m_ref[0, 3:4, :]
    scale = m_ref[0, 4:5, :]
    gate = m_ref[0, 5:6, :]
    u = (x * (1.0 + scale) + shift).astype(BF16)
    for c0 in range(0, win_ref.shape[1], IN_CHUNK):
        z_ref[:, c0:c0 + IN_CHUNK] = _bdot(u, win_ref[:, c0:c0 + IN_CHUNK])

    cdft = cdft_ref[...].astype(BF16)
    pdft = pdft_ref[...].astype(BF16)
    q_off, k_off, v_off, f_off = 0, NA_WIDTH, 2 * NA_WIDTH, 3 * NA_WIDTH
    ga_off = f_off + FNET_WIDTH
    gb_off = ga_off + d
    for b in range(nb):
        r0 = b * seq
        for h in range(NA_HEADS):
            k_ref[b, kv_slot, h] = z_ref[r0:r0 + seq, k_off + h * HEAD_DIM:k_off + (h + 1) * HEAD_DIM]
            v_ref[b, kv_slot, h] = z_ref[r0:r0 + seq, v_off + h * HEAD_DIM:v_off + (h + 1) * HEAD_DIM]
        for slot in range(k_ref.shape[1]):
            if slot != kv_slot:
                k_ref[b, slot] = jnp.zeros(k_ref.shape[2:], F32)
                v_ref[b, slot] = jnp.zeros(v_ref.shape[2:], F32)
        zc = [_bdot(z_ref[r0:r0 + seq, f_off + g * FNET_GROUP_DIM:f_off + (g + 1) * FNET_GROUP_DIM]
                    .astype(BF16), cdft).astype(BF16) for g in range(FNET_GROUPS)]
        stacked = jnp.concatenate(
            [jnp.concatenate([z[:, :FNET_GROUP_DIM] for z in zc], axis=1),
             jnp.concatenate([z[:, FNET_GROUP_DIM:] for z in zc], axis=1)], axis=0)
        fm_ref[r0:r0 + seq, :] = _bdot(pdft, stacked)

    problems = []
    for b in range(nb):
        r0 = b * seq
        for hp in range(NA_WIDTH // LANES):
            c0 = hp * LANES
            q = (z_ref[r0:r0 + seq, q_off + c0:q_off + c0 + LANES] * ATTN_SCALE).astype(BF16)
            k = z_ref[r0:r0 + seq, k_off + c0:k_off + c0 + LANES].astype(BF16)
            v = z_ref[r0:r0 + seq, v_off + c0:v_off + c0 + LANES].astype(BF16)
            problems.append((b * seq, c0, (q, [k], [v], [None])))
    outs = _stacked_pair_attention([p for _, _, p in problems])
    for (r0, c0, _), o in zip(problems, outs):
        oatt_ref[r0:r0 + seq, c0:c0 + LANES] = o.astype(BF16)

    slab = (fm_ref[...], oatt_ref[...], z_ref[:, ga_off:ga_off + d], z_ref[:, gb_off:gb_off + d], x)
    (out,) = _merge_out([slab], gate, wf_ref, wna_ref, wout_ref, g_ref[1:2, :], b_ref[1:2, :],
                        alpha)
    o_ref[...] = out.reshape(nb, seq, d)


def _ctx_mix(x3, batch, mods_l, win, wf, wna, wout, ln_g, ln_b, alpha, layer, depth, kv_prev):
    _, seq, d = x3.shape
    nb = 2 if batch % 2 == 0 else 1
    in_width = win.shape[2]
    cc, cs = _dft_mats(FNET_GROUP_DIM)
    pc, ps = _dft_mats(seq)
    norm = 1.0 / math.sqrt(seq * FNET_GROUP_DIM)
    cdft = jnp.asarray(np.concatenate([cc, -cs], axis=1) * norm, F32)
    pdft = jnp.asarray(np.concatenate([pc, ps], axis=1), F32)
    kv_shape = jax.ShapeDtypeStruct((batch, depth, NA_HEADS, seq, HEAD_DIM), F32)
    if kv_prev is None:
        kv_spec = pl.BlockSpec((nb, depth, NA_HEADS, seq, HEAD_DIM), lambda i: (i, 0, 0, 0, 0))
        kv_slot = layer
    else:
        kv_spec = pl.BlockSpec((nb, 1, NA_HEADS, seq, HEAD_DIM), lambda i: (i, layer, 0, 0, 0))
        kv_slot = 0
    operands = [x3, mods_l, win, wf, wna, wout, ln_g, ln_b, cdft, pdft]
    in_specs = [
        pl.BlockSpec((nb, seq, d), lambda i: (i, 0, 0)),
        pl.BlockSpec((1, N_MOD, d), lambda i: (0, 0, 0)),
        _layer_spec(win, layer),
        _layer_spec(wf, layer),
        _layer_spec(wna, layer),
        _layer_spec(wout, layer),
        _const_spec(ln_g.shape),
        _const_spec(ln_b.shape),
        _const_spec(cdft.shape),
        _const_spec(pdft.shape),
    ]
    aliases = {}
    if kv_prev is not None:
        aliases = {len(operands): 1, len(operands) + 1: 2}
        operands += list(kv_prev)
        in_specs += [pl.BlockSpec(memory_space=pl.ANY)] * 2
    return pl.pallas_call(
        functools.partial(_ctx_mix_kernel, alpha=alpha, n_alias=len(aliases), kv_slot=kv_slot),
        grid=(batch // nb,),
        in_specs=in_specs,
        out_specs=[pl.BlockSpec((nb, seq, d), lambda i: (i, 0, 0)), kv_spec, kv_spec],
        out_shape=[jax.ShapeDtypeStruct((batch, seq, d), F32), kv_shape, kv_shape],
        input_output_aliases=aliases,
        scratch_shapes=[
            pltpu.VMEM((nb * seq, in_width), F32),
            pltpu.VMEM((nb * seq, NA_WIDTH), BF16),
            pltpu.VMEM((nb * seq, FNET_WIDTH), F32),
        ],
        compiler_params=_params(("arbitrary",)),
        name="ctx_mix",
    )(*operands)


def _in_proj_kernel(x_ref, m_ref, win_ref, q_ref, k_ref, v_ref, f_ref):
    x = x_ref[...]
    shift = m_ref[0, 3:4, :]
    scale = m_ref[0, 4:5, :]
    u = (x * (1.0 + scale) + shift).astype(BF16)
    cw = NA_WIDTH
    q_ref[...] = (_bdot(u, win_ref[:, 0:cw]) * ATTN_SCALE).astype(BF16)
    k_ref[...] = _bdot(u, win_ref[:, cw:2 * cw]).astype(BF16)
    v_ref[...] = _bdot(u, win_ref[:, 2 * cw:3 * cw]).astype(BF16)
    f_ref[...] = _bdot(u, win_ref[:, 3 * cw:3 * cw + FNET_WIDTH])


def _in_proj(x2, first_row, rows, mods_l, rows_per_group, group0, win, layer):
    d = x2.shape[1]
    tm = math.gcd(math.gcd(ROW_TILE, rows_per_group), first_row)
    tiles_per_group = rows_per_group // tm
    first_tile = first_row // tm
    cw = NA_WIDTH
    qkvf = 3 * NA_WIDTH + FNET_WIDTH
    row_spec = lambda w: pl.BlockSpec((tm, w), lambda i: (i, 0))
    return pl.pallas_call(
        _in_proj_kernel,
        grid=(rows // tm,),
        in_specs=[
            pl.BlockSpec((tm, d), lambda i: (first_tile + i, 0)),
            pl.BlockSpec((1, N_MOD, d), lambda i: (group0 + i // tiles_per_group, 0, 0)),
            _layer_spec(win, layer, cols=qkvf, col_block=0),
        ],
        out_specs=[row_spec(cw), row_spec(cw), row_spec(cw), row_spec(FNET_WIDTH)],
        out_shape=[
            jax.ShapeDtypeStruct((rows, cw), BF16),
            jax.ShapeDtypeStruct((rows, cw), BF16),
            jax.ShapeDtypeStruct((rows, cw), BF16),
            jax.ShapeDtypeStruct((rows, FNET_WIDTH), F32),
        ],
        compiler_params=_params(("arbitrary",)),
        name="in_proj",
    )(x2, mods_l, win)


def _window_start_row(i, rows):
    return jnp.clip(Q_ROWS * i - WIN_ROWS // 2, 0, rows - K_ROWS)


def _nbr_attn_kernel(q_ref, k_ref, v_ref, ck_ref, cv_ref, *rest, rows):
    bias_refs, o_ref = rest[:-1], rest[-1]
    step = pl.program_id(1)
    n_keys = K_ROWS * GRID_W
    tq = Q_ROWS * GRID_W
    problems, places = [], []
    for j, bias_ref in enumerate(bias_refs):
        i = step * len(bias_refs) + j
        start = pl.multiple_of(_window_start_row(i, rows) * GRID_W, KEY_BLOCK)
        bias = (bias_ref[0, 0], bias_ref[1, 0])
        for b in range(q_ref.shape[0]):
            problems.append(
                (q_ref[b, j * tq:(j + 1) * tq, :],
                 [k_ref[b, pl.ds(start, n_keys), :], ck_ref[b]],
                 [v_ref[b, pl.ds(start, n_keys), :], cv_ref[b]],
                 [bias, None]))
            places.append((b, j))
    for (b, j), o in zip(places, _stacked_pair_attention(problems)):
        o_ref[b, j * tq:(j + 1) * tq, :] = o.astype(BF16)


def _bias_rows_kernel(rpb_ref, onehot_ref, neg_ref, o_ref):
    o_ref[...] = jnp.dot(rpb_ref[...], onehot_ref[...], precision=HI,
                         preferred_element_type=F32) + neg_ref[...]


def _nbr_bias_table(rpb_l, rows):
    h, n_dr, n_dc = rpb_l.shape
    n_blocks = rows // Q_ROWS
    kr = min(WIN_ROWS, rows)
    variants = np.array([0, min(1, n_blocks - 1), n_blocks - 1])
    r = Q_ROWS * variants[:, None] + np.arange(Q_ROWS)[None, :]
    ks = np.clip(Q_ROWS * variants - WIN_ROWS // 2, 0, rows - K_ROWS)
    key_row = ks[:, None] + np.arange(K_ROWS)[None, :]
    r0 = np.clip(r - kr // 2, 0, rows - kr)
    row_ok = (key_row[:, None, :] >= r0[:, :, None]) & (key_row[:, None, :] < r0[:, :, None] + kr)
    dr = np.clip(key_row[:, None, :] - r[:, :, None] + (WIN_ROWS - 1), 0, n_dr - 1)
    c_idx = np.arange(GRID_W)
    c0 = np.clip(c_idx - WIN_COLS // 2, 0, GRID_W - WIN_COLS)
    col_ok = (c_idx[None, :] >= c0[:, None]) & (c_idx[None, :] < c0[:, None] + WIN_COLS)
    dc = np.clip(c_idx[None, :] - c_idx[:, None] + (WIN_COLS - 1), 0, n_dc - 1)

    k_pad = -(-n_dc // LANES) * LANES
    m_pad = -(-(h * n_dr) // 8) * 8
    onehot = np.zeros((k_pad, GRID_W * GRID_W), np.float32)
    flat = np.arange(GRID_W * GRID_W).reshape(GRID_W, GRID_W)
    onehot[dc[col_ok], flat[col_ok]] = 1.0
    neg = np.where(col_ok, 0.0, NEG_INF).astype(np.float32).reshape(1, -1)
    rpb_pad = jnp.zeros((m_pad, k_pad), F32).at[:h * n_dr, :n_dc].set(
        rpb_l.reshape(h * n_dr, n_dc).astype(F32))
    col_tab = pl.pallas_call(
        _bias_rows_kernel,
        out_shape=jax.ShapeDtypeStruct((m_pad, GRID_W * GRID_W), F32),
        name="bias_rows",
    )(rpb_pad, jnp.asarray(onehot), jnp.asarray(neg))
    col_tab = col_tab[:h * n_dr].reshape(h, n_dr, GRID_W, GRID_W).transpose(0, 2, 1, 3)
    pad = K_ROWS + Q_ROWS
    col_tab = jnp.pad(col_tab, ((0, 0), (0, 0), (pad, pad), (0, 0)))
    col_tab = col_tab.reshape(h, GRID_W, (n_dr + 2 * pad) * GRID_W)
    slabs = []
    for v in range(3):
        for qr in range(Q_ROWS):
            d0 = int(ks[v] - r[v, qr]) + (WIN_ROWS - 1) + pad
            slab = col_tab[:, :, d0 * GRID_W:(d0 + K_ROWS) * GRID_W]
            lane_ok = jnp.asarray(np.repeat(row_ok[v, qr], GRID_W))
            slabs.append(jnp.where(lane_ok[None, None, :], slab, NEG_INF))
    return jnp.stack(slabs, axis=1).reshape(h, 3, Q_ROWS * GRID_W, K_ROWS * GRID_W)


def _nbr_attn(q3, k3, v3, ck3, cv3, bias, layer):
    nb, seq, width = q3.shape
    rows = seq // GRID_W
    n_blocks = rows // Q_ROWS
    n_pairs = width // LANES
    past = ck3.shape[1]
    tq = Q_ROWS * GRID_W
    tk = K_ROWS * GRID_W

    per_step = Q_BLOCKS_PER_STEP if n_blocks % Q_BLOCKS_PER_STEP == 0 else 1

    def variant(i):
        return jnp.where(i == 0, 0, jnp.where(i == n_blocks - 1, 2, 1))

    def bias_spec(j):
        return pl.BlockSpec((2, 1, tq, tk),
                            lambda p, s: (layer * n_pairs + p, variant(s * per_step + j), 0, 0))

    return pl.pallas_call(
        functools.partial(_nbr_attn_kernel, rows=rows),
        grid=(n_pairs, n_blocks // per_step),
        in_specs=[
            pl.BlockSpec((nb, per_step * tq, LANES), lambda p, s: (0, s, p)),
            pl.BlockSpec((nb, seq, LANES), lambda p, s: (0, 0, p)),
            pl.BlockSpec((nb, seq, LANES), lambda p, s: (0, 0, p)),
            pl.BlockSpec((nb, past, LANES), lambda p, s: (0, 0, p)),
            pl.BlockSpec((nb, past, LANES), lambda p, s: (0, 0, p)),
        ] + [bias_spec(j) for j in range(per_step)],
        out_specs=pl.BlockSpec((nb, per_step * tq, LANES), lambda p, s: (0, s, p)),
        out_shape=jax.ShapeDtypeStruct((nb, seq, width), BF16),
        compiler_params=_params(("arbitrary", "arbitrary")),
        name="nbr_attn",
    )(q3, k3, v3, ck3, cv3, *([bias] * per_step))


def _fourier_kernel(x_ref, cdft_ref, d1_ref, d2_ref, twc_ref, tws_ref, o_ref,
                    zr_ref, zi_ref, tr_ref, ti_ref, *, n1, n2):
    gd = cdft_ref.shape[0]
    n_groups = x_ref.shape[2] // gd
    groups = range(n_groups)
    cdft = cdft_ref[...].astype(BF16)
    d1 = d1_ref[...].astype(BF16)
    d2 = d2_ref[...].astype(BF16)
    for g in groups:
        zc = _bdot(x_ref[0, :, g * gd:(g + 1) * gd].astype(BF16), cdft)
        zr_ref[g] = zc[:, :gd]
        zi_ref[g] = zc[:, gd:]

    def stacked(re_ref, im_ref, rows):
        return jnp.concatenate(
            [jnp.concatenate([re_ref[g, rows, :], im_ref[g, rows, :]], axis=0) for g in groups],
            axis=1)

    def stage1(c, carry):
        t = _bdot(d1, stacked(zr_ref, zi_ref, pl.ds(c, n1, stride=n2)).astype(BF16))
        cs = twc_ref[c]
        sn = tws_ref[c]
        row = pl.multiple_of(c * n1, n1)
        for g in groups:
            tr = t[:n1, g * gd:(g + 1) * gd]
            ti = t[n1:, g * gd:(g + 1) * gd]
            tr_ref[g, pl.ds(row, n1), :] = tr * cs + ti * sn
            ti_ref[g, pl.ds(row, n1), :] = ti * cs - tr * sn
        return carry

    lax.fori_loop(0, n2, stage1, 0, unroll=LOOP_UNROLL)

    def stage2(k1, carry):
        rows = pl.ds(k1, n2, stride=n1)
        y = _bdot(d2, stacked(tr_ref, ti_ref, rows).astype(BF16))
        for g in groups:
            zr_ref[g, rows, :] = y[:, g * gd:(g + 1) * gd]
        return carry

    lax.fori_loop(0, n1, stage2, 0, unroll=LOOP_UNROLL)
    for g in groups:
        o_ref[0, :, g * gd:(g + 1) * gd] = zr_ref[g]


def _fourier(f3):
    nb, seq, width = f3.shape
    n2 = GRID_W
    n1 = seq // n2
    gd = FNET_GROUP_DIM
    cc, cs = _dft_mats(gd)
    c1, s1 = _dft_mats(n1)
    c2, s2 = _dft_mats(n2)
    norm = 1.0 / math.sqrt(seq * gd)
    cdft = jnp.asarray(np.concatenate([cc, -cs], axis=1) * norm, F32)
    d1 = jnp.asarray(np.block([[c1, s1], [-s1, c1]]), F32)
    d2 = jnp.asarray(np.concatenate([c2, s2], axis=1), F32)
    ang = 2.0 * np.pi * ((np.arange(n2)[:, None] * np.arange(n1)[None, :]) % seq) / seq
    twc = jnp.asarray(np.broadcast_to(np.cos(ang)[:, :, None], (n2, n1, gd)), F32)
    tws = jnp.asarray(np.broadcast_to(np.sin(ang)[:, :, None], (n2, n1, gd)), F32)
    return pl.pallas_call(
        functools.partial(_fourier_kernel, n1=n1, n2=n2),
        grid=(nb, width // FOURIER_COLS),
        in_specs=[
            pl.BlockSpec((1, seq, FOURIER_COLS), lambda b, g: (b, 0, g)),
            _const_spec(cdft.shape),
            _const_spec(d1.shape),
            _const_spec(d2.shape),
            _const_spec(twc.shape),
            _const_spec(tws.shape),
        ],
        out_specs=pl.BlockSpec((1, seq, FOURIER_COLS), lambda b, g: (b, 0, g)),
        out_shape=jax.ShapeDtypeStruct((nb, seq, width), F32),
        scratch_shapes=[pltpu.VMEM((FOURIER_COLS // gd, seq, gd), F32)] * 4,
        compiler_params=_params(("arbitrary", "arbitrary")),
        name="fourier",
    )(f3, cdft, d1, d2, twc, tws)


def _merge_kernel(fm_ref, o_ref_in, x_ref, m_ref, wgate_ref, wf_ref, wna_ref, wout_ref,
                  g_ref, b_ref, out_ref, *, alpha):
    tm, d = x_ref.shape
    shift = m_ref[0, 3:4, :]
    scale = m_ref[0, 4:5, :]
    gate = m_ref[0, 5:6, :]
    sr = min(tm, ROW_TILE)
    rows = [slice(r0, r0 + sr) for r0 in range(0, tm, sr)]
    xs = [x_ref[rs, :] for rs in rows]
    us = [(x * (1.0 + scale) + shift).astype(BF16) for x in xs]
    slabs = [(fm_ref[rs, :], o_ref_in[rs, :], _bdot(u, wgate_ref[:, :d]), _bdot(u, wgate_ref[:, d:]), x)
             for rs, u, x in zip(rows, us, xs)]
    outs = _merge_out(slabs, gate, wf_ref, wna_ref, wout_ref, g_ref[1:2, :], b_ref[1:2, :], alpha)
    for rs, out in zip(rows, outs):
        out_ref[rs, :] = out


def _merge(fm2, o2, x2, x_first_row, mods_l, rows_per_group, group0, win, wf, wna, wout, layer,
           ln_g, ln_b, alpha):
    rows, d = fm2.shape[0], x2.shape[1]
    tm = math.gcd(math.gcd(MERGE_ROW_TILE, rows_per_group), x_first_row)
    tiles_per_group = rows_per_group // tm
    x_first_tile = x_first_row // tm
    row_spec = lambda w: pl.BlockSpec((tm, w), lambda i: (i, 0))
    assert win.shape[2] == 2 * (2 * d), "gate columns must be the second half of w_in"
    return pl.pallas_call(
        functools.partial(_merge_kernel, alpha=alpha),
        grid=(rows // tm,),
        in_specs=[
            row_spec(fm2.shape[1]), row_spec(o2.shape[1]),
            pl.BlockSpec((tm, d), lambda i: (x_first_tile + i, 0)),
            pl.BlockSpec((1, N_MOD, d), lambda i: (group0 + i // tiles_per_group, 0, 0)),
            _layer_spec(win, layer, cols=2 * d, col_block=1),
            _layer_spec(wf, layer), _layer_spec(wna, layer), _layer_spec(wout, layer),
            _const_spec(ln_g.shape), _const_spec(ln_b.shape),
        ],
        out_specs=row_spec(d),
        out_shape=jax.ShapeDtypeStruct((rows, d), F32),
        compiler_params=_params(("arbitrary",)),
        name="merge",
    )(fm2, o2, x2, mods_l, win, wf, wna, wout, ln_g, ln_b)


def kernel(x_prompt, x_sample, cache_k, cache_v, c, c_ctx, w_ada, b_ada, ln_g, ln_b, w_ff1_up,
           w_ff1_down, w_in, rpb, w_fourier, w_na_out, w_out, w_ff2_up, w_ff2_down):
    batch, seq, d = x_prompt.shape
    dec_batch, dec_seq, _ = x_sample.shape
    depth = w_ada.shape[0]
    alpha = (2 * depth) ** 0.25
    rows_lat = dec_seq // GRID_W

    cvec = jnp.zeros((MOD_ROWS, d), F32).at[0].set(c_ctx).at[1:1 + dec_batch].set(c)
    mods = _mods(cvec, w_ada, b_ada).reshape(depth, MOD_ROWS, N_MOD, d)

    y_p = x_prompt.reshape(batch * seq, d)
    y_s = x_sample.reshape(dec_batch * dec_seq, d)
    up1, dn1 = w_ff1_up.astype(BF16), w_ff1_down.astype(BF16)
    up2, dn2 = w_ff2_up.astype(BF16), w_ff2_down.astype(BF16)
    win = w_in.astype(BF16)
    wf = w_fourier.astype(BF16)
    wna = w_na_out.astype(BF16)
    wout = w_out.astype(BF16)
    ck = cache_k.transpose(1, 0, 3, 2, 4).reshape(depth, dec_batch, -1, NA_WIDTH).astype(BF16)
    cv = cache_v.transpose(1, 0, 3, 2, 4).reshape(depth, dec_batch, -1, NA_WIDTH).astype(BF16)

    bias = _nbr_bias_table(rpb.reshape((-1,) + rpb.shape[2:]), rows_lat)

    rows_p, rows_s = batch * seq, dec_batch * dec_seq

    def segments(src_p, first_p, src_s, first_s):
        return [(src_p, first_p, rows_p, 0, rows_p), (src_s, first_s, rows_s, 1, dec_seq)]

    kv = None
    pending = segments(y_p, 0, y_s, 0)
    for l in range(depth):
        m_l = mods[l]
        g_l, b_l = ln_g[l], ln_b[l]
        y = _ffn(pending, m_l, up1, dn1, l, g_l, b_l, 0, alpha)

        y_p3, new_k, new_v = _ctx_mix(y.reshape(-1, seq, d), batch, m_l, win, wf, wna, wout,
                                      g_l, b_l, alpha, l, depth, kv)
        kv = (new_k, new_v)

        q2, k2, v2, f2 = _in_proj(y, rows_p, rows_s, m_l, dec_seq, 1, win, l)
        to3 = lambda t: t.reshape(dec_batch, dec_seq, t.shape[-1])
        o3 = _nbr_attn(to3(q2), to3(k2), to3(v2), ck[l], cv[l], bias, l)
        fm3 = _fourier(to3(f2))
        y_s = _merge(fm3.reshape(-1, FNET_WIDTH), o3.reshape(-1, NA_WIDTH), y, rows_p, m_l,
                     dec_seq, 1, win, wf, wna, wout, l, g_l, b_l, alpha)

        mixed = segments(y_p3.reshape(rows_p, d), 0, y_s, 0)
        if l + 1 < depth:
            y = _ffn(mixed, m_l, up2, dn2, l, g_l, b_l, 2, alpha)
            pending = segments(y, 0, y, rows_p)
        else:
            y_p = _ffn(mixed[:1], m_l, up2, dn2, l, g_l, b_l, 2, alpha)
            y_s = _ffn(mixed[1:], m_l, up2, dn2, l, g_l, b_l, 2, alpha)

    return (y_p.reshape(batch, seq, d), y_s.reshape(dec_batch, dec_seq, d), new_k, new_v)
```

```python
import functools
import math

import numpy as np
import jax
import jax.numpy as jnp
from jax import lax
from jax.experimental import pallas as pl
from jax.experimental.pallas import tpu as pltpu

F32 = jnp.float32
BF16 = jnp.bfloat16

HEAD_DIM = 64
NA_HEADS = 8
NA_WIDTH = NA_HEADS * HEAD_DIM
FNET_GROUPS = 4
FNET_GROUP_DIM = 128
FNET_WIDTH = FNET_GROUPS * FNET_GROUP_DIM
GRID_W = 64
WIN_ROWS = 8
WIN_COLS = 16
N_SUB = 3
N_MOD = 3 * N_SUB
ATTN_SCALE = HEAD_DIM ** -0.5
LN_EPS = 1e-5
NEG_INF = -1e30

LANES = 128
MXU_DIM = 256
VMEM_LIMIT_BYTES = 56 * 1024 * 1024

ROW_TILE = 512
FFN_ROW_TILE = 1024
FFN_SUB_ROWS = 512
MERGE_ROW_TILE = 1024
FF_CHUNK = MXU_DIM
Q_ROWS = 4
K_ROWS = Q_ROWS + WIN_ROWS
KEY_BLOCK = GRID_W * math.gcd(Q_ROWS, WIN_ROWS // 2)
Q_BLOCKS_PER_STEP = 4
IN_CHUNK = 2 * MXU_DIM
MOD_ROWS = 8
MODS_K_ROWS = 256
FOURIER_COLS = 2 * FNET_GROUP_DIM
LOOP_UNROLL = 8
HI = lax.Precision.HIGHEST


def _params(sem):
    return pltpu.CompilerParams(dimension_semantics=sem, vmem_limit_bytes=VMEM_LIMIT_BYTES)


def _const_spec(shape):
    nd = len(shape)
    return pl.BlockSpec(shape, lambda *_: (0,) * nd, pipeline_mode=pl.Buffered(1))


def _layer_spec(w, layer, cols=None, col_block=0):
    _, r, n = w.shape
    cols = n if cols is None else cols
    return pl.BlockSpec((None, r, cols), lambda *_: (layer, 0, col_block),
                        pipeline_mode=pl.Buffered(1))


def _layer_norm(r, g, b):
    mu = jnp.mean(r, axis=-1, keepdims=True)
    d = r - mu
    var = jnp.mean(d * d, axis=-1, keepdims=True)
    return d * lax.rsqrt(var + LN_EPS) * g + b


def _bdot(a, b):
    return jnp.dot(a, b, preferred_element_type=F32)


def _dot_nt(a, b):
    return lax.dot_general(a, b, (((1,), (1,)), ((), ())), preferred_element_type=F32)


def _dft_mats(n):
    k = np.arange(n)
    ang = 2.0 * np.pi * ((k[:, None] * k[None, :]) % n) / n
    return np.cos(ang), np.sin(ang)


def _mods_kernel(c_ref, w_ref, b_ref, o_ref):
    k = pl.program_id(1)
    c = c_ref[...]
    s = (c * jax.nn.sigmoid(c)).astype(BF16)
    part = _bdot(s, w_ref[0].astype(BF16))

    @pl.when(k == 0)
    def _():
        o_ref[0] = part + b_ref[0]

    @pl.when(k > 0)
    def _():
        o_ref[0] += part


def _mods(cvec, w_ada, b_ada):
    depth, d, n = w_ada.shape
    tk = MODS_K_ROWS
    return pl.pallas_call(
        _mods_kernel,
        grid=(depth, d // tk),
        in_specs=[
            pl.BlockSpec((MOD_ROWS, tk), lambda l, k: (0, k)),
            pl.BlockSpec((1, tk, n), lambda l, k: (l, k, 0)),
            pl.BlockSpec((1, 1, n), lambda l, k: (l, 0, 0)),
        ],
        out_specs=pl.BlockSpec((1, MOD_ROWS, n), lambda l, k: (l, 0, 0)),
        out_shape=jax.ShapeDtypeStruct((depth, MOD_ROWS, n), F32),
        compiler_params=_params(("arbitrary", "arbitrary")),
        name="mods",
    )(cvec, w_ada, b_ada.reshape(depth, 1, n))


def _ffn_kernel(*refs, sub, alpha, seg_starts):
    n_seg = len(seg_starts)
    x_refs = refs[:n_seg]
    m_ref, wup_ref, wdn_ref, g_ref, b_ref, o_ref, act_ref = refs[n_seg:]
    step = pl.program_id(0)

    def read_x(rs):
        x = x_refs[-1][rs, :]
        for s in range(n_seg - 2, -1, -1):
            x = jnp.where(step < seg_starts[s + 1], x_refs[s][rs, :], x)
        return x

    shift = m_ref[0, 3 * sub:3 * sub + 1, :]
    scale = m_ref[0, 3 * sub + 1:3 * sub + 2, :]
    gate = m_ref[0, 3 * sub + 2:3 * sub + 3, :]
    ff = wdn_ref.shape[0]
    tm = o_ref.shape[0]
    sr = min(tm, FFN_SUB_ROWS)
    slabs = [slice(r0, r0 + sr) for r0 in range(0, tm, sr)]
    us = [(read_x(rs) * (1.0 + scale) + shift).astype(BF16) for rs in slabs]
    for c0 in range(0, ff, FF_CHUNK):
        for rs, u in zip(slabs, us):
            a = _bdot(u, wup_ref[:, c0:c0 + FF_CHUNK])
            g = _bdot(u, wup_ref[:, ff + c0:ff + c0 + FF_CHUNK])
            act_ref[rs, c0:c0 + FF_CHUNK] = ((g * jax.nn.sigmoid(g)) * a).astype(BF16)
    for rs in slabs:
        y = _bdot(act_ref[rs, :], wdn_ref[...])
        r = alpha * read_x(rs) + (0.5 * gate) * y
        o_ref[rs, :] = _layer_norm(r, g_ref[sub:sub + 1, :], b_ref[sub:sub + 1, :])


def _ffn(segments, mods_l, wup, wdn, layer, ln_g, ln_b, sub, alpha):
    d = segments[0][0].shape[1]
    tm = FFN_ROW_TILE
    for _, first_row, n_rows, _, rows_per_group in segments:
        tm = math.gcd(math.gcd(tm, rows_per_group), math.gcd(n_rows, first_row))
    starts, n_steps = [], 0
    for _, _, n_rows, _, _ in segments:
        starts.append(n_steps)
        n_steps += n_rows // tm

    def x_spec(s):
        _, first_row, n_rows, _, _ = segments[s]
        first, count = first_row // tm, n_rows // tm
        return pl.BlockSpec((tm, d), lambda i: (first + jnp.clip(i - starts[s], 0, count - 1), 0))

    def group(i):
        g = None
        for s in range(len(segments) - 1, -1, -1):
            _, _, _, group0, rows_per_group = segments[s]
            g_s = group0 + jnp.maximum(i - starts[s], 0) // (rows_per_group // tm)
            g = g_s if g is None else jnp.where(i < starts[s + 1], g_s, g)
        return g

    return pl.pallas_call(
        functools.partial(_ffn_kernel, sub=sub, alpha=alpha, seg_starts=tuple(starts)),
        grid=(n_steps,),
        in_specs=[x_spec(s) for s in range(len(segments))] + [
            pl.BlockSpec((1, N_MOD, d), lambda i: (group(i), 0, 0)),
            _layer_spec(wup, layer),
            _layer_spec(wdn, layer),
            _const_spec(ln_g.shape),
            _const_spec(ln_b.shape),
        ],
        out_specs=pl.BlockSpec((tm, d), lambda i: (i, 0)),
        out_shape=jax.ShapeDtypeStruct((n_steps * tm, d), F32),
        scratch_shapes=[pltpu.VMEM((tm, wdn.shape[1]), BF16)],
        compiler_params=_params(("arbitrary",)),
        name=f"ffn{sub}",
    )(*[seg[0] for seg in segments], mods_l, wup, wdn, ln_g, ln_b)


def _row_reduce(arrays, combine, reduce):
    acc = None
    for a in arrays:
        for c0 in range(0, a.shape[1], LANES):
            chunk = a[:, c0:c0 + LANES]
            acc = chunk if acc is None else combine(acc, chunk)
    return reduce(acc, axis=-1, keepdims=True)


def _stacked_pair_attention(problems):
    lane = lax.broadcasted_iota(jnp.int32, (1, LANES), 1)
    scores = []
    for q, k_list, _, bias_list in problems:
        zero = jnp.zeros_like(q)
        q2 = jnp.concatenate([jnp.where(lane < HEAD_DIM, q, zero),
                              jnp.where(lane >= HEAD_DIM, q, zero)], axis=0)
        s = []
        for k, bias in zip(k_list, bias_list):
            sk = _dot_nt(q2, k)
            if bias is not None:
                sk = sk + jnp.concatenate(bias, axis=0)
            s.append(sk)
        scores.append(s)
    maxes = [_row_reduce(s, jnp.maximum, jnp.max) for s in scores]
    probs = [[jnp.exp(sk - m) for sk in s] for s, m in zip(scores, maxes)]
    dens = [_row_reduce(p, jnp.add, jnp.sum) for p in probs]
    outs = []
    for (q, _, v_list, _), p, den in zip(problems, probs, dens):
        o = None
        for pk, v in zip(p, v_list):
            pv = _bdot(pk.astype(BF16), v)
            o = pv if o is None else o + pv
        o = o / den
        m_rows = q.shape[0]
        outs.append(jnp.where(lane < HEAD_DIM, o[:m_rows], o[m_rows:]))
    return outs


def _merge_out(slabs, gate, wf_ref, wna_ref, wout_ref, ln_g, ln_b, alpha):
    branches = [(_bdot(fm.astype(BF16), wf_ref[...]), _bdot(o, wna_ref[...]))
                for fm, o, _, _, _ in slabs]
    merged = [(jax.nn.sigmoid(ga) * a + jax.nn.sigmoid(gb) * b).astype(BF16)
              for (_, _, ga, gb, _), (a, b) in zip(slabs, branches)]
    mixes = [_bdot(mg, wout_ref[...]) for mg in merged]
    return [_layer_norm(alpha * x + gate * mix, ln_g, ln_b)
            for (_, _, _, _, x), mix in zip(slabs, mixes)]


def _ctx_mix_kernel(x_ref, m_ref, win_ref, wf_ref, wna_ref, wout_ref, g_ref, b_ref,
                    cdft_ref, pdft_ref, *rest, alpha, n_alias, kv_slot):
    o_ref, k_ref, v_ref, z_ref, oatt_ref, fm_ref = rest[n_alias:]
    nb, seq, d = x_ref.shape
    rows = nb * seq
    x = x_ref[...].reshape(rows, d)
    shift = m_ref[0, 3:4, :]
    scale = m_ref[0, 4:5, :]
    gate = m_ref[0, 5:6, :]
    u = (x * (1.0 + scale) + shift).astype(BF16)
    for c0 in range(0, win_ref.shape[1], IN_CHUNK):
        z_ref[:, c0:c0 + IN_CHUNK] = _bdot(u, win_ref[:, c0:c0 + IN_CHUNK])

    cdft = cdft_ref[...].astype(BF16)
    pdft = pdft_ref[...].astype(BF16)
    q_off, k_off, v_off, f_off = 0, NA_WIDTH, 2 * NA_WIDTH, 3 * NA_WIDTH
    ga_off = f_off + FNET_WIDTH
    gb_off = ga_off + d
    for b in range(nb):
        r0 = b * seq
        for h in range(NA_HEADS):
            k_ref[b, kv_slot, h] = z_ref[r0:r0 + seq, k_off + h * HEAD_DIM:k_off + (h + 1) * HEAD_DIM]
            v_ref[b, kv_slot, h] = z_ref[r0:r0 + seq, v_off + h * HEAD_DIM:v_off + (h + 1) * HEAD_DIM]
        for slot in range(k_ref.shape[1]):
            if slot != kv_slot:
                k_ref[b, slot] = jnp.zeros(k_ref.shape[2:], F32)
                v_ref[b, slot] = jnp.zeros(v_ref.shape[2:], F32)
        zc = [_bdot(z_ref[r0:r0 + seq, f_off + g * FNET_GROUP_DIM:f_off + (g + 1) * FNET_GROUP_DIM]
                    .astype(BF16), cdft).astype(BF16) for g in range(FNET_GROUPS)]
        stacked = jnp.concatenate(
            [jnp.concatenate([z[:, :FNET_GROUP_DIM] for z in zc], axis=1),
             jnp.concatenate([z[:, FNET_GROUP_DIM:] for z in zc], axis=1)], axis=0)
        fm_ref[r0:r0 + seq, :] = _bdot(pdft, stacked)

    problems = []
    for b in range(nb):
        r0 = b * seq
        for hp in range(NA_WIDTH // LANES):
            c0 = hp * LANES
            q = (z_ref[r0:r0 + seq, q_off + c0:q_off + c0 + LANES] * ATTN_SCALE).astype(BF16)
            k = z_ref[r0:r0 + seq, k_off + c0:k_off + c0 + LANES].astype(BF16)
            v = z_ref[r0:r0 + seq, v_off + c0:v_off + c0 + LANES].astype(BF16)
            problems.append((b * seq, c0, (q, [k], [v], [None])))
    outs = _stacked_pair_attention([p for _, _, p in problems])
    for (r0, c0, _), o in zip(problems, outs):
        oatt_ref[r0:r0 + seq, c0:c0 + LANES] = o.astype(BF16)

    slab = (fm_ref[...], oatt_ref[...], z_ref[:, ga_off:ga_off + d], z_ref[:, gb_off:gb_off + d], x)
    (out,) = _merge_out([slab], gate, wf_ref, wna_ref, wout_ref, g_ref[1:2, :], b_ref[1:2, :],
                        alpha)
    o_ref[...] = out.reshape(nb, seq, d)


def _ctx_mix(x3, batch, mods_l, win, wf, wna, wout, ln_g, ln_b, alpha, layer, depth, kv_prev):
    _, seq, d = x3.shape
    nb = 2 if batch % 2 == 0 else 1
    in_width = win.shape[2]
    cc, cs = _dft_mats(FNET_GROUP_DIM)
    pc, ps = _dft_mats(seq)
    norm = 1.0 / math.sqrt(seq * FNET_GROUP_DIM)
    cdft = jnp.asarray(np.concatenate([cc, -cs], axis=1) * norm, F32)
    pdft = jnp.asarray(np.concatenate([pc, ps], axis=1), F32)
    kv_shape = jax.ShapeDtypeStruct((batch, depth, NA_HEADS, seq, HEAD_DIM), F32)
    if kv_prev is None:
        kv_spec = pl.BlockSpec((nb, depth, NA_HEADS, seq, HEAD_DIM), lambda i: (i, 0, 0, 0, 0))
        kv_slot = layer
    else:
        kv_spec = pl.BlockSpec((nb, 1, NA_HEADS, seq, HEAD_DIM), lambda i: (i, layer, 0, 0, 0))
        kv_slot = 0
    operands = [x3, mods_l, win, wf, wna, wout, ln_g, ln_b, cdft, pdft]
    in_specs = [
        pl.BlockSpec((nb, seq, d), lambda i: (i, 0, 0)),
        pl.BlockSpec((1, N_MOD, d), lambda i: (0, 0, 0)),
        _layer_spec(win, layer),
        _layer_spec(wf, layer),
        _layer_spec(wna, layer),
        _layer_spec(wout, layer),
        _const_spec(ln_g.shape),
        _const_spec(ln_b.shape),
        _const_spec(cdft.shape),
        _const_spec(pdft.shape),
    ]
    aliases = {}
    if kv_prev is not None:
        aliases = {len(operands): 1, len(operands) + 1: 2}
        operands += list(kv_prev)
        in_specs += [pl.BlockSpec(memory_space=pl.ANY)] * 2
    return pl.pallas_call(
        functools.partial(_ctx_mix_kernel, alpha=alpha, n_alias=len(aliases), kv_slot=kv_slot),
        grid=(batch // nb,),
        in_specs=in_specs,
        out_specs=[pl.BlockSpec((nb, seq, d), lambda i: (i, 0, 0)), kv_spec, kv_spec],
        out_shape=[jax.ShapeDtypeStruct((batch, seq, d), F32), kv_shape, kv_shape],
        input_output_aliases=aliases,
        scratch_shapes=[
            pltpu.VMEM((nb * seq, in_width), F32),
            pltpu.VMEM((nb * seq, NA_WIDTH), BF16),
            pltpu.VMEM((nb * seq, FNET_WIDTH), F32),
        ],
        compiler_params=_params(("arbitrary",)),
        name="ctx_mix",
    )(*operands)


def _in_proj_kernel(x_ref, m_ref, win_ref, q_ref, k_ref, v_ref, f_ref):
    x = x_ref[...]
    shift = m_ref[0, 3:4, :]
    scale = m_ref[0, 4:5, :]
    u = (x * (1.0 + scale) + shift).astype(BF16)
    cw = NA_WIDTH
    q_ref[...] = (_bdot(u, win_ref[:, 0:cw]) * ATTN_SCALE).astype(BF16)
    k_ref[...] = _bdot(u, win_ref[:, cw:2 * cw]).astype(BF16)
    v_ref[...] = _bdot(u, win_ref[:, 2 * cw:3 * cw]).astype(BF16)
    f_ref[...] = _bdot(u, win_ref[:, 3 * cw:3 * cw + FNET_WIDTH])


def _in_proj(x2, first_row, rows, mods_l, rows_per_group, group0, win, layer):
    d = x2.shape[1]
    tm = math.gcd(math.gcd(ROW_TILE, rows_per_group), first_row)
    tiles_per_group = rows_per_group // tm
    first_tile = first_row // tm
    cw = NA_WIDTH
    qkvf = 3 * NA_WIDTH + FNET_WIDTH
    row_spec = lambda w: pl.BlockSpec((tm, w), lambda i: (i, 0))
    return pl.pallas_call(
        _in_proj_kernel,
        grid=(rows // tm,),
        in_specs=[
            pl.BlockSpec((tm, d), lambda i: (first_tile + i, 0)),
            pl.BlockSpec((1, N_MOD, d), lambda i: (group0 + i // tiles_per_group, 0, 0)),
            _layer_spec(win, layer, cols=qkvf, col_block=0),
        ],
        out_specs=[row_spec(cw), row_spec(cw), row_spec(cw), row_spec(FNET_WIDTH)],
        out_shape=[
            jax.ShapeDtypeStruct((rows, cw), BF16),
            jax.ShapeDtypeStruct((rows, cw), BF16),
            jax.ShapeDtypeStruct((rows, cw), BF16),
            jax.ShapeDtypeStruct((rows, FNET_WIDTH), F32),
        ],
        compiler_params=_params(("arbitrary",)),
        name="in_proj",
    )(x2, mods_l, win)


def _window_start_row(i, rows):
    return jnp.clip(Q_ROWS * i - WIN_ROWS // 2, 0, rows - K_ROWS)


def _nbr_attn_kernel(q_ref, k_ref, v_ref, ck_ref, cv_ref, *rest, rows):
    bias_refs, o_ref = rest[:-1], rest[-1]
    step = pl.program_id(1)
    n_keys = K_ROWS * GRID_W
    tq = Q_ROWS * GRID_W
    problems, places = [], []
    for j, bias_ref in enumerate(bias_refs):
        i = step * len(bias_refs) + j
        start = pl.multiple_of(_window_start_row(i, rows) * GRID_W, KEY_BLOCK)
        bias = (bias_ref[0, 0], bias_ref[1, 0])
        for b in range(q_ref.shape[0]):
            problems.append(
                (q_ref[b, j * tq:(j + 1) * tq, :],
                 [k_ref[b, pl.ds(start, n_keys), :], ck_ref[b]],
                 [v_ref[b, pl.ds(start, n_keys), :], cv_ref[b]],
                 [bias, None]))
            places.append((b, j))
    for (b, j), o in zip(places, _stacked_pair_attention(problems)):
        o_ref[b, j * tq:(j + 1) * tq, :] = o.astype(BF16)


def _bias_rows_kernel(rpb_ref, onehot_ref, neg_ref, o_ref):
    o_ref[...] = jnp.dot(rpb_ref[...], onehot_ref[...], precision=HI,
                         preferred_element_type=F32) + neg_ref[...]


def _nbr_bias_table(rpb_l, rows):
    h, n_dr, n_dc = rpb_l.shape
    n_blocks = rows // Q_ROWS
    kr = min(WIN_ROWS, rows)
    variants = np.array([0, min(1, n_blocks - 1), n_blocks - 1])
    r = Q_ROWS * variants[:, None] + np.arange(Q_ROWS)[None, :]
    ks = np.clip(Q_ROWS * variants - WIN_ROWS // 2, 0, rows - K_ROWS)
    key_row = ks[:, None] + np.arange(K_ROWS)[None, :]
    r0 = np.clip(r - kr // 2, 0, rows - kr)
    row_ok = (key_row[:, None, :] >= r0[:, :, None]) & (key_row[:, None, :] < r0[:, :, None] + kr)
    dr = np.clip(key_row[:, None, :] - r[:, :, None] + (WIN_ROWS - 1), 0, n_dr - 1)
    c_idx = np.arange(GRID_W)
    c0 = np.clip(c_idx - WIN_COLS // 2, 0, GRID_W - WIN_COLS)
    col_ok = (c_idx[None, :] >= c0[:, None]) & (c_idx[None, :] < c0[:, None] + WIN_COLS)
    dc = np.clip(c_idx[None, :] - c_idx[:, None] + (WIN_COLS - 1), 0, n_dc - 1)

    k_pad = -(-n_dc // LANES) * LANES
    m_pad = -(-(h * n_dr) // 8) * 8
    onehot = np.zeros((k_pad, GRID_W * GRID_W), np.float32)
    flat = np.arange(GRID_W * GRID_W).reshape(GRID_W, GRID_W)
    onehot[dc[col_ok], flat[col_ok]] = 1.0
    neg = np.where(col_ok, 0.0, NEG_INF).astype(np.float32).reshape(1, -1)
    rpb_pad = jnp.zeros((m_pad, k_pad), F32).at[:h * n_dr, :n_dc].set(
        rpb_l.reshape(h * n_dr, n_dc).astype(F32))
    col_tab = pl.pallas_call(
        _bias_rows_kernel,
        out_shape=jax.ShapeDtypeStruct((m_pad, GRID_W * GRID_W), F32),
        name="bias_rows",
    )(rpb_pad, jnp.asarray(onehot), jnp.asarray(neg))
    col_tab = col_tab[:h * n_dr].reshape(h, n_dr, GRID_W, GRID_W).transpose(0, 2, 1, 3)
    pad = K_ROWS + Q_ROWS
    col_tab = jnp.pad(col_tab, ((0, 0), (0, 0), (pad, pad), (0, 0)))
    col_tab = col_tab.reshape(h, GRID_W, (n_dr + 2 * pad) * GRID_W)
    slabs = []
    for v in range(3):
        for qr in range(Q_ROWS):
            d0 = int(ks[v] - r[v, qr]) + (WIN_ROWS - 1) + pad
            slab = col_tab[:, :, d0 * GRID_W:(d0 + K_ROWS) * GRID_W]
            lane_ok = jnp.asarray(np.repeat(row_ok[v, qr], GRID_W))
            slabs.append(jnp.where(lane_ok[None, None, :], slab, NEG_INF))
    return jnp.stack(slabs, axis=1).reshape(h, 3, Q_ROWS * GRID_W, K_ROWS * GRID_W)


def _nbr_attn(q3, k3, v3, ck3, cv3, bias, layer):
    nb, seq, width = q3.shape
    rows = seq // GRID_W
    n_blocks = rows // Q_ROWS
    n_pairs = width // LANES
    past = ck3.shape[1]
    tq = Q_ROWS * GRID_W
    tk = K_ROWS * GRID_W

    per_step = Q_BLOCKS_PER_STEP if n_blocks % Q_BLOCKS_PER_STEP == 0 else 1

    def variant(i):
        return jnp.where(i == 0, 0, jnp.where(i == n_blocks - 1, 2, 1))

    def bias_spec(j):
        return pl.BlockSpec((2, 1, tq, tk),
                            lambda p, s: (layer * n_pairs + p, variant(s * per_step + j), 0, 0))

    return pl.pallas_call(
        functools.partial(_nbr_attn_kernel, rows=rows),
        grid=(n_pairs, n_blocks // per_step),
        in_specs=[
            pl.BlockSpec((nb, per_step * tq, LANES), lambda p, s: (0, s, p)),
            pl.BlockSpec((nb, seq, LANES), lambda p, s: (0, 0, p)),
            pl.BlockSpec((nb, seq, LANES), lambda p, s: (0, 0, p)),
            pl.BlockSpec((nb, past, LANES), lambda p, s: (0, 0, p)),
            pl.BlockSpec((nb, past, LANES), lambda p, s: (0, 0, p)),
        ] + [bias_spec(j) for j in range(per_step)],
        out_specs=pl.BlockSpec((nb, per_step * tq, LANES), lambda p, s: (0, s, p)),
        out_shape=jax.ShapeDtypeStruct((nb, seq, width), BF16),
        compiler_params=_params(("arbitrary", "arbitrary")),
        name="nbr_attn",
    )(q3, k3, v3, ck3, cv3, *([bias] * per_step))


def _fourier_kernel(x_ref, cdft_ref, d1_ref, d2_ref, twc_ref, tws_ref, o_ref,
                    zr_ref, zi_ref, tr_ref, ti_ref, *, n1, n2):
    gd = cdft_ref.shape[0]
    n_groups = x_ref.shape[2] // gd
    groups = range(n_groups)
    cdft = cdft_ref[...].astype(BF16)
    d1 = d1_ref[...].astype(BF16)
    d2 = d2_ref[...].astype(BF16)
    for g in groups:
        zc = _bdot(x_ref[0, :, g * gd:(g + 1) * gd].astype(BF16), cdft)
        zr_ref[g] = zc[:, :gd]
        zi_ref[g] = zc[:, gd:]

    def stacked(re_ref, im_ref, rows):
        return jnp.concatenate(
            [jnp.concatenate([re_ref[g, rows, :], im_ref[g, rows, :]], axis=0) for g in groups],
            axis=1)

    def stage1(c, carry):
        t = _bdot(d1, stacked(zr_ref, zi_ref, pl.ds(c, n1, stride=n2)).astype(BF16))
        cs = twc_ref[c]
        sn = tws_ref[c]
        row = pl.multiple_of(c * n1, n1)
        for g in groups:
            tr = t[:n1, g * gd:(g + 1) * gd]
            ti = t[n1:, g * gd:(g + 1) * gd]
            tr_ref[g, pl.ds(row, n1), :] = tr * cs + ti * sn
            ti_ref[g, pl.ds(row, n1), :] = ti * cs - tr * sn
        return carry

    lax.fori_loop(0, n2, stage1, 0, unroll=LOOP_UNROLL)

    def stage2(k1, carry):
        rows = pl.ds(k1, n2, stride=n1)
        y = _bdot(d2, stacked(tr_ref, ti_ref, rows).astype(BF16))
        for g in groups:
            zr_ref[g, rows, :] = y[:, g * gd:(g + 1) * gd]
        return carry

    lax.fori_loop(0, n1, stage2, 0, unroll=LOOP_UNROLL)
    for g in groups:
        o_ref[0, :, g * gd:(g + 1) * gd] = zr_ref[g]


def _fourier(f3):
    nb, seq, width = f3.shape
    n2 = GRID_W
    n1 = seq // n2
    gd = FNET_GROUP_DIM
    cc, cs = _dft_mats(gd)
    c1, s1 = _dft_mats(n1)
    c2, s2 = _dft_mats(n2)
    norm = 1.0 / math.sqrt(seq * gd)
    cdft = jnp.asarray(np.concatenate([cc, -cs], axis=1) * norm, F32)
    d1 = jnp.asarray(np.block([[c1, s1], [-s1, c1]]), F32)
    d2 = jnp.asarray(np.concatenate([c2, s2], axis=1), F32)
    ang = 2.0 * np.pi * ((np.arange(n2)[:, None] * np.arange(n1)[None, :]) % seq) / seq
    twc = jnp.asarray(np.broadcast_to(np.cos(ang)[:, :, None], (n2, n1, gd)), F32)
    tws = jnp.asarray(np.broadcast_to(np.sin(ang)[:, :, None], (n2, n1, gd)), F32)
    return pl.pallas_call(
        functools.partial(_fourier_kernel, n1=n1, n2=n2),
        grid=(nb, width // FOURIER_COLS),
        in_specs=[
            pl.BlockSpec((1, seq, FOURIER_COLS), lambda b, g: (b, 0, g)),
            _const_spec(cdft.shape),
            _const_spec(d1.shape),
            _const_spec(d2.shape),
            _const_spec(twc.shape),
            _const_spec(tws.shape),
        ],
        out_specs=pl.BlockSpec((1, seq, FOURIER_COLS), lambda b, g: (b, 0, g)),
        out_shape=jax.ShapeDtypeStruct((nb, seq, width), F32),
        scratch_shapes=[pltpu.VMEM((FOURIER_COLS // gd, seq, gd), F32)] * 4,
        compiler_params=_params(("arbitrary", "arbitrary")),
        name="fourier",
    )(f3, cdft, d1, d2, twc, tws)


def _merge_kernel(fm_ref, o_ref_in, x_ref, m_ref, wgate_ref, wf_ref, wna_ref, wout_ref,
                  g_ref, b_ref, out_ref, *, alpha):
    tm, d = x_ref.shape
    shift = m_ref[0, 3:4, :]
    scale = m_ref[0, 4:5, :]
    gate = m_ref[0, 5:6, :]
    sr = min(tm, ROW_TILE)
    rows = [slice(r0, r0 + sr) for r0 in range(0, tm, sr)]
    xs = [x_ref[rs, :] for rs in rows]
    us = [(x * (1.0 + scale) + shift).astype(BF16) for x in xs]
    slabs = [(fm_ref[rs, :], o_ref_in[rs, :], _bdot(u, wgate_ref[:, :d]), _bdot(u, wgate_ref[:, d:]), x)
             for rs, u, x in zip(rows, us, xs)]
    outs = _merge_out(slabs, gate, wf_ref, wna_ref, wout_ref, g_ref[1:2, :], b_ref[1:2, :], alpha)
    for rs, out in zip(rows, outs):
        out_ref[rs, :] = out


def _merge(fm2, o2, x2, x_first_row, mods_l, rows_per_group, group0, win, wf, wna, wout, layer,
           ln_g, ln_b, alpha):
    rows, d = fm2.shape[0], x2.shape[1]
    tm = math.gcd(math.gcd(MERGE_ROW_TILE, rows_per_group), x_first_row)
    tiles_per_group = rows_per_group // tm
    x_first_tile = x_first_row // tm
    row_spec = lambda w: pl.BlockSpec((tm, w), lambda i: (i, 0))
    assert win.shape[2] == 2 * (2 * d), "gate columns must be the second half of w_in"
    return pl.pallas_call(
        functools.partial(_merge_kernel, alpha=alpha),
        grid=(rows // tm,),
        in_specs=[
            row_spec(fm2.shape[1]), row_spec(o2.shape[1]),
            pl.BlockSpec((tm, d), lambda i: (x_first_tile + i, 0)),
            pl.BlockSpec((1, N_MOD, d), lambda i: (group0 + i // tiles_per_group, 0, 0)),
            _layer_spec(win, layer, cols=2 * d, col_block=1),
            _layer_spec(wf, layer), _layer_spec(wna, layer), _layer_spec(wout, layer),
            _const_spec(ln_g.shape), _const_spec(ln_b.shape),
        ],
        out_specs=row_spec(d),
        out_shape=jax.ShapeDtypeStruct((rows, d), F32),
        compiler_params=_params(("arbitrary",)),
        name="merge",
    )(fm2, o2, x2, mods_l, win, wf, wna, wout, ln_g, ln_b)


def kernel(x_prompt, x_sample, cache_k, cache_v, c, c_ctx, w_ada, b_ada, ln_g, ln_b, w_ff1_up,
           w_ff1_down, w_in, rpb, w_fourier, w_na_out, w_out, w_ff2_up, w_ff2_down):
    batch, seq, d = x_prompt.shape
    dec_batch, dec_seq, _ = x_sample.shape
    depth = w_ada.shape[0]
    alpha = (2 * depth) ** 0.25
    rows_lat = dec_seq // GRID_W

    cvec = jnp.zeros((MOD_ROWS, d), F32).at[0].set(c_ctx).at[1:1 + dec_batch].set(c)
    mods = _mods(cvec, w_ada, b_ada).reshape(depth, MOD_ROWS, N_MOD, d)

    y_p = x_prompt.reshape(batch * seq, d)
    y_s = x_sample.reshape(dec_batch * dec_seq, d)
    up1, dn1 = w_ff1_up.astype(BF16), w_ff1_down.astype(BF16)
    up2, dn2 = w_ff2_up.astype(BF16), w_ff2_down.astype(BF16)
    win = w_in.astype(BF16)
    wf = w_fourier.astype(BF16)
    wna = w_na_out.astype(BF16)
    wout = w_out.astype(BF16)
    ck = cache_k.transpose(1, 0, 3, 2, 4).reshape(depth, dec_batch, -1, NA_WIDTH).astype(BF16)
    cv = cache_v.transpose(1, 0, 3, 2, 4).reshape(depth, dec_batch, -1, NA_WIDTH).astype(BF16)

    bias = _nbr_bias_table(rpb.reshape((-1,) + rpb.shape[2:]), rows_lat)

    rows_p, rows_s = batch * seq, dec_batch * dec_seq

    def segments(src_p, first_p, src_s, first_s):
        return [(src_p, first_p, rows_p, 0, rows_p), (src_s, first_s, rows_s, 1, dec_seq)]

    kv = None
    pending = segments(y_p, 0, y_s, 0)
    for l in range(depth):
        m_l = mods[l]
        g_l, b_l = ln_g[l], ln_b[l]
        y = _ffn(pending, m_l, up1, dn1, l, g_l, b_l, 0, alpha)

        q2, k2, v2, f2 = _in_proj(y, rows_p, rows_s, m_l, dec_seq, 1, win, l)
        to3 = lambda t: t.reshape(dec_batch, dec_seq, t.shape[-1])
        o3 = _nbr_attn(to3(q2), to3(k2), to3(v2), ck[l], cv[l], bias, l)
        fm3 = _fourier(to3(f2))
        y_s = _merge(fm3.reshape(-1, FNET_WIDTH), o3.reshape(-1, NA_WIDTH), y, rows_p, m_l,
                     dec_seq, 1, win, wf, wna, wout, l, g_l, b_l, alpha)

        y_p3, new_k, new_v = _ctx_mix(y.reshape(-1, seq, d), batch, m_l, win, wf, wna, wout,
                                      g_l, b_l, alpha, l, depth, kv)
        kv = (new_k, new_v)

        mixed = segments(y_p3.reshape(rows_p, d), 0, y_s, 0)
        if l + 1 < depth:
            y = _ffn(mixed, m_l, up2, dn2, l, g_l, b_l, 2, alpha)
            pending = segments(y, 0, y, rows_p)
        else:
            y_p = _ffn(mixed[:1], m_l, up2, dn2, l, g_l, b_l, 2, alpha)
            y_s = _ffn(mixed[1:], m_l, up2, dn2, l, g_l, b_l, 2, alpha)

    return (y_p.reshape(batch, seq, d), y_s.reshape(dec_batch, dec_seq, d), new_k, new_v)
```

```python
import functools
import math

import numpy as np
import jax
import jax.numpy as jnp
from jax import lax
from jax.experimental import pallas as pl
from jax.experimental.pallas import tpu as pltpu

F32 = jnp.float32
BF16 = jnp.bfloat16

HEAD_DIM = 64
NA_HEADS = 8
NA_WIDTH = NA_HEADS * HEAD_DIM
FNET_GROUPS = 4
FNET_GROUP_DIM = 128
FNET_WIDTH = FNET_GROUPS * FNET_GROUP_DIM
GRID_W = 64
WIN_ROWS = 8
WIN_COLS = 16
N_SUB = 3
N_MOD = 3 * N_SUB
ATTN_SCALE = HEAD_DIM ** -0.5
LN_EPS = 1e-5
NEG_INF = -1e30

LANES = 128
MXU_DIM = 256
VMEM_LIMIT_BYTES = 56 * 1024 * 1024

ROW_TILE = 512
FFN_ROW_TILE = 1024
FFN_SUB_ROWS = 512
MERGE_ROW_TILE = 1024
FF_CHUNK = MXU_DIM
Q_ROWS = 4
K_ROWS = Q_ROWS + WIN_ROWS
KEY_BLOCK = GRID_W * math.gcd(Q_ROWS, WIN_ROWS // 2)
Q_BLOCKS_PER_STEP = 4
IN_CHUNK = 2 * MXU_DIM
MOD_ROWS = 8
MODS_K_ROWS = 256
FOURIER_COLS = 2 * FNET_GROUP_DIM
LOOP_UNROLL = 8
HI = lax.Precision.HIGHEST


def _params(sem):
    return pltpu.CompilerParams(dimension_semantics=sem, vmem_limit_bytes=VMEM_LIMIT_BYTES)


def _const_spec(shape):
    nd = len(shape)
    return pl.BlockSpec(shape, lambda *_: (0,) * nd, pipeline_mode=pl.Buffered(1))


def _layer_spec(w, layer, cols=None, col_block=0):
    _, r, n = w.shape
    cols = n if cols is None else cols
    return pl.BlockSpec((None, r, cols), lambda *_: (layer, 0, col_block),
                        pipeline_mode=pl.Buffered(1))


def _layer_norm(r, g, b):
    mu = jnp.mean(r, axis=-1, keepdims=True)
    d = r - mu
    var = jnp.mean(d * d, axis=-1, keepdims=True)
    return d * lax.rsqrt(var + LN_EPS) * g + b


def _bdot(a, b):
    return jnp.dot(a, b, preferred_element_type=F32)


def _dot_nt(a, b):
    return lax.dot_general(a, b, (((1,), (1,)), ((), ())), preferred_element_type=F32)


def _dft_mats(n):
    k = np.arange(n)
    ang = 2.0 * np.pi * ((k[:, None] * k[None, :]) % n) / n
    return np.cos(ang), np.sin(ang)


def _mods_kernel(c_ref, w_ref, b_ref, o_ref):
    k = pl.program_id(1)
    c = c_ref[...]
    s = (c * jax.nn.sigmoid(c)).astype(BF16)
    part = _bdot(s, w_ref[0].astype(BF16))

    @pl.when(k == 0)
    def _():
        o_ref[0] = part + b_ref[0]

    @pl.when(k > 0)
    def _():
        o_ref[0] += part


def _mods(cvec, w_ada, b_ada):
    depth, d, n = w_ada.shape
    tk = MODS_K_ROWS
    return pl.pallas_call(
        _mods_kernel,
        grid=(depth, d // tk),
        in_specs=[
            pl.BlockSpec((MOD_ROWS, tk), lambda l, k: (0, k)),
            pl.BlockSpec((1, tk, n), lambda l, k: (l, k, 0)),
            pl.BlockSpec((1, 1, n), lambda l, k: (l, 0, 0)),
        ],
        out_specs=pl.BlockSpec((1, MOD_ROWS, n), lambda l, k: (l, 0, 0)),
        out_shape=jax.ShapeDtypeStruct((depth, MOD_ROWS, n), F32),
        compiler_params=_params(("arbitrary", "arbitrary")),
        name="mods",
    )(cvec, w_ada, b_ada.reshape(depth, 1, n))


def _ffn_kernel(*refs, sub, alpha, seg_starts):
    n_seg = len(seg_starts)
    x_refs = refs[:n_seg]
    m_ref, wup_ref, wdn_ref, g_ref, b_ref, o_ref, act_ref = refs[n_seg:]
    step = pl.program_id(0)

    def read_x(rs):
        x = x_refs[-1][rs, :]
        for s in range(n_seg - 2, -1, -1):
            x = jnp.where(step < seg_starts[s + 1], x_refs[s][rs, :], x)
        return x

    shift = m_ref[0, 3 * sub:3 * sub + 1, :]
    scale = m_ref[0, 3 * sub + 1:3 * sub + 2, :]
    gate = m_ref[0, 3 * sub + 2:3 * sub + 3, :]
    ff = wdn_ref.shape[0]
    tm = o_ref.shape[0]
    sr = min(tm, FFN_SUB_ROWS)
    slabs = [slice(r0, r0 + sr) for r0 in range(0, tm, sr)]
    us = [(read_x(rs) * (1.0 + scale) + shift).astype(BF16) for rs in slabs]
    for c0 in range(0, ff, FF_CHUNK):
        for rs, u in zip(slabs, us):
            a = _bdot(u, wup_ref[:, c0:c0 + FF_CHUNK])
            g = _bdot(u, wup_ref[:, ff + c0:ff + c0 + FF_CHUNK])
            act_ref[rs, c0:c0 + FF_CHUNK] = ((g * jax.nn.sigmoid(g)) * a).astype(BF16)
    for rs in slabs:
        y = _bdot(act_ref[rs, :], wdn_ref[...])
        r = alpha * read_x(rs) + (0.5 * gate) * y
        o_ref[rs, :] = _layer_norm(r, g_ref[sub:sub + 1, :], b_ref[sub:sub + 1, :])


def _ffn(segments, mods_l, wup, wdn, layer, ln_g, ln_b, sub, alpha):
    d = segments[0][0].shape[1]
    tm = FFN_ROW_TILE
    for _, first_row, n_rows, _, rows_per_group in segments:
        tm = math.gcd(math.gcd(tm, rows_per_group), math.gcd(n_rows, first_row))
    starts, n_steps = [], 0
    for _, _, n_rows, _, _ in segments:
        starts.append(n_steps)
        n_steps += n_rows // tm

    def x_spec(s):
        _, first_row, n_rows, _, _ = segments[s]
        first, count = first_row // tm, n_rows // tm
        return pl.BlockSpec((tm, d), lambda i: (first + jnp.clip(i - starts[s], 0, count - 1), 0))

    def group(i):
        g = None
        for s in range(len(segments) - 1, -1, -1):
            _, _, _, group0, rows_per_group = segments[s]
            g_s = group0 + jnp.maximum(i - starts[s], 0) // (rows_per_group // tm)
            g = g_s if g is None else jnp.where(i < starts[s + 1], g_s, g)
        return g

    return pl.pallas_call(
        functools.partial(_ffn_kernel, sub=sub, alpha=alpha, seg_starts=tuple(starts)),
        grid=(n_steps,),
        in_specs=[x_spec(s) for s in range(len(segments))] + [
            pl.BlockSpec((1, N_MOD, d), lambda i: (group(i), 0, 0)),
            _layer_spec(wup, layer),
            _layer_spec(wdn, layer),
            _const_spec(ln_g.shape),
            _const_spec(ln_b.shape),
        ],
        out_specs=pl.BlockSpec((tm, d), lambda i: (i, 0)),
        out_shape=jax.ShapeDtypeStruct((n_steps * tm, d), F32),
        scratch_shapes=[pltpu.VMEM((tm, wdn.shape[1]), BF16)],
        compiler_params=_params(("arbitrary",)),
        name=f"ffn{sub}",
    )(*[seg[0] for seg in segments], mods_l, wup, wdn, ln_g, ln_b)


def _row_reduce(arrays, combine, reduce):
    acc = None
    for a in arrays:
        for c0 in range(0, a.shape[1], LANES):
            chunk = a[:, c0:c0 + LANES]
            acc = chunk if acc is None else combine(acc, chunk)
    return reduce(acc, axis=-1, keepdims=True)


def _stacked_pair_attention(problems):
    lane = lax.broadcasted_iota(jnp.int32, (1, LANES), 1)
    scores = []
    for q, k_list, _, bias_list in problems:
        zero = jnp.zeros_like(q)
        q2 = jnp.concatenate([jnp.where(lane < HEAD_DIM, q, zero),
                              jnp.where(lane >= HEAD_DIM, q, zero)], axis=0)
        s = []
        for k, bias in zip(k_list, bias_list):
            sk = _dot_nt(q2, k)
            if bias is not None:
                sk = sk + jnp.concatenate(bias, axis=0)
            s.append(sk)
        scores.append(s)
    maxes = [_row_reduce(s, jnp.maximum, jnp.max) for s in scores]
    probs = [[jnp.exp(sk - m) for sk in s] for s, m in zip(scores, maxes)]
    dens = [_row_reduce(p, jnp.add, jnp.sum) for p in probs]
    outs = []
    for (q, _, v_list, _), p, den in zip(problems, probs, dens):
        o = None
        for pk, v in zip(p, v_list):
            pv = _bdot(pk.astype(BF16), v)
            o = pv if o is None else o + pv
        o = o / den
        m_rows = q.shape[0]
        outs.append(jnp.where(lane < HEAD_DIM, o[:m_rows], o[m_rows:]))
    return outs


def _merge_out(slabs, gate, wf_ref, wna_ref, wout_ref, ln_g, ln_b, alpha):
    branches = [(_bdot(fm.astype(BF16), wf_ref[...]), _bdot(o, wna_ref[...]))
                for fm, o, _, _, _ in slabs]
    merged = [(jax.nn.sigmoid(ga) * a + jax.nn.sigmoid(gb) * b).astype(BF16)
              for (_, _, ga, gb, _), (a, b) in zip(slabs, branches)]
    mixes = [_bdot(mg, wout_ref[...]) for mg in merged]
    return [_layer_norm(alpha * x + gate * mix, ln_g, ln_b)
            for (_, _, _, _, x), mix in zip(slabs, mixes)]


def _ctx_mix_kernel(x_ref, m_ref, win_ref, wf_ref, wna_ref, wout_ref, g_ref, b_ref,
                    cdft_ref, pdft_ref, *rest, alpha, n_alias, kv_slot):
    o_ref, k_ref, v_ref, z_ref, oatt_ref, fm_ref = rest[n_alias:]
    nb, seq, d = x_ref.shape
    rows = nb * seq
    x = x_ref[...].reshape(rows, d)
    shift = m_ref[0, 3:4, :]
    scale = m_ref[0, 4:5, :]
    gate = m_ref[0, 5:6, :]
    u = (x * (1.0 + scale) + shift).astype(BF16)
    for c0 in range(0, win_ref.shape[1], IN_CHUNK):
        z_ref[:, c0:c0 + IN_CHUNK] = _bdot(u, win_ref[:, c0:c0 + IN_CHUNK])

    cdft = cdft_ref[...].astype(BF16)
    pdft = pdft_ref[...].astype(BF16)
    q_off, k_off, v_off, f_off = 0, NA_WIDTH, 2 * NA_WIDTH, 3 * NA_WIDTH
    ga_off = f_off + FNET_WIDTH
    gb_off = ga_off + d
    for b in range(nb):
        r0 = b * seq
        for h in range(NA_HEADS):
            k_ref[b, kv_slot, h] = z_ref[r0:r0 + seq, k_off + h * HEAD_DIM:k_off + (h + 1) * HEAD_DIM]
            v_ref[b, kv_slot, h] = z_ref[r0:r0 + seq, v_off + h * HEAD_DIM:v_off + (h + 1) * HEAD_DIM]
        for slot in range(k_ref.shape[1]):
            if slot != kv_slot:
                k_ref[b, slot] = jnp.zeros(k_ref.shape[2:], F32)
                v_ref[b, slot] = jnp.zeros(v_ref.shape[2:], F32)
        zc = [_bdot(z_ref[r0:r0 + seq, f_off + g * FNET_GROUP_DIM:f_off + (g + 1) * FNET_GROUP_DIM]
                    .astype(BF16), cdft).astype(BF16) for g in range(FNET_GROUPS)]
        stacked = jnp.concatenate(
            [jnp.concatenate([z[:, :FNET_GROUP_DIM] for z in zc], axis=1),
             jnp.concatenate([z[:, FNET_GROUP_DIM:] for z in zc], axis=1)], axis=0)
        fm_ref[r0:r0 + seq, :] = _bdot(pdft, stacked)

    problems = []
    for b in range(nb):
        r0 = b * seq
        for hp in range(NA_WIDTH // LANES):
            c0 = hp * LANES
            q = (z_ref[r0:r0 + seq, q_off + c0:q_off + c0 + LANES] * ATTN_SCALE).astype(BF16)
            k = z_ref[r0:r0 + seq, k_off + c0:k_off + c0 + LANES].astype(BF16)
            v = z_ref[r0:r0 + seq, v_off + c0:v_off + c0 + LANES].astype(BF16)
            problems.append((b * seq, c0, (q, [k], [v], [None])))
    outs = _stacked_pair_attention([p for _, _, p in problems])
    for (r0, c0, _), o in zip(problems, outs):
        oatt_ref[r0:r0 + seq, c0:c0 + LANES] = o.astype(BF16)

    slab = (fm_ref[...], oatt_ref[...], z_ref[:, ga_off:ga_off + d], z_ref[:, gb_off:gb_off + d], x)
    (out,) = _merge_out([slab], gate, wf_ref, wna_ref, wout_ref, g_ref[1:2, :], b_ref[1:2, :],
                        alpha)
    o_ref[...] = out.reshape(nb, seq, d)


def _ctx_mix(x3, batch, mods_l, win, wf, wna, wout, ln_g, ln_b, alpha, layer, depth, kv_prev):
    _, seq, d = x3.shape
    nb = 2 if batch % 2 == 0 else 1
    in_width = win.shape[2]
    cc, cs = _dft_mats(FNET_GROUP_DIM)
    pc, ps = _dft_mats(seq)
    norm = 1.0 / math.sqrt(seq * FNET_GROUP_DIM)
    cdft = jnp.asarray(np.concatenate([cc, -cs], axis=1) * norm, F32)
    pdft = jnp.asarray(np.concatenate([pc, ps], axis=1), F32)
    kv_shape = jax.ShapeDtypeStruct((batch, depth, NA_HEADS, seq, HEAD_DIM), F32)
    if kv_prev is None:
        kv_spec = pl.BlockSpec((nb, depth, NA_HEADS, seq, HEAD_DIM), lambda i: (i, 0, 0, 0, 0))
        kv_slot = layer
    else:
        kv_spec = pl.BlockSpec((nb, 1, NA_HEADS, seq, HEAD_DIM), lambda i: (i, layer, 0, 0, 0))
        kv_slot = 0
    operands = [x3, mods_l, win, wf, wna, wout, ln_g, ln_b, cdft, pdft]
    in_specs = [
        pl.BlockSpec((nb, seq, d), lambda i: (i, 0, 0)),
        pl.BlockSpec((1, N_MOD, d), lambda i: (0, 0, 0)),
        _layer_spec(win, layer),
        _layer_spec(wf, layer),
        _layer_spec(wna, layer),
        _layer_spec(wout, layer),
        _const_spec(ln_g.shape),
        _const_spec(ln_b.shape),
        _const_spec(cdft.shape),
        _const_spec(pdft.shape),
    ]
    aliases = {}
    if kv_prev is not None:
        aliases = {len(operands): 1, len(operands) + 1: 2}
        operands += list(kv_prev)
        in_specs += [pl.BlockSpec(memory_space=pl.ANY)] * 2
    return pl.pallas_call(
        functools.partial(_ctx_mix_kernel, alpha=alpha, n_alias=len(aliases), kv_slot=kv_slot),
        grid=(batch // nb,),
        in_specs=in_specs,
        out_specs=[pl.BlockSpec((nb, seq, d), lambda i: (i, 0, 0)), kv_spec, kv_spec],
        out_shape=[jax.ShapeDtypeStruct((batch, seq, d), F32), kv_shape, kv_shape],
        input_output_aliases=aliases,
        scratch_shapes=[
            pltpu.VMEM((nb * seq, in_width), F32),
            pltpu.VMEM((nb * seq, NA_WIDTH), BF16),
            pltpu.VMEM((nb * seq, FNET_WIDTH), F32),
        ],
        compiler_params=_params(("arbitrary",)),
        name="ctx_mix",
    )(*operands)


def _in_proj_kernel(x_ref, m_ref, win_ref, q_ref, k_ref, v_ref, f_ref):
    x = x_ref[...]
    shift = m_ref[0, 3:4, :]
    scale = m_ref[0, 4:5, :]
    u = (x * (1.0 + scale) + shift).astype(BF16)
    cw = NA_WIDTH
    q_ref[...] = (_bdot(u, win_ref[:, 0:cw]) * ATTN_SCALE).astype(BF16)
    k_ref[...] = _bdot(u, win_ref[:, cw:2 * cw]).astype(BF16)
    v_ref[...] = _bdot(u, win_ref[:, 2 * cw:3 * cw]).astype(BF16)
    f_ref[...] = _bdot(u, win_ref[:, 3 * cw:3 * cw + FNET_WIDTH])


def _in_proj(x2, first_row, rows, mods_l, rows_per_group, group0, win, layer):
    d = x2.shape[1]
    tm = math.gcd(math.gcd(ROW_TILE, rows_per_group), first_row)
    tiles_per_group = rows_per_group // tm
    first_tile = first_row // tm
    cw = NA_WIDTH
    qkvf = 3 * NA_WIDTH + FNET_WIDTH
    row_spec = lambda w: pl.BlockSpec((tm, w), lambda i: (i, 0))
    return pl.pallas_call(
        _in_proj_kernel,
        grid=(rows // tm,),
        in_specs=[
            pl.BlockSpec((tm, d), lambda i: (first_tile + i, 0)),
            pl.BlockSpec((1, N_MOD, d), lambda i: (group0 + i // tiles_per_group, 0, 0)),
            _layer_spec(win, layer, cols=qkvf, col_block=0),
        ],
        out_specs=[row_spec(cw), row_spec(cw), row_spec(cw), row_spec(FNET_WIDTH)],
        out_shape=[
            jax.ShapeDtypeStruct((rows, cw), BF16),
            jax.ShapeDtypeStruct((rows, cw), BF16),
            jax.ShapeDtypeStruct((rows, cw), BF16),
            jax.ShapeDtypeStruct((rows, FNET_WIDTH), F32),
        ],
        compiler_params=_params(("arbitrary",)),
        name="in_proj",
    )(x2, mods_l, win)


def _window_start_row(i, rows):
    return jnp.clip(Q_ROWS * i - WIN_ROWS // 2, 0, rows - K_ROWS)


def _nbr_attn_kernel(q_ref, k_ref, v_ref, ck_ref, cv_ref, *rest, rows):
    bias_refs, o_ref = rest[:-1], rest[-1]
    step = pl.program_id(1)
    n_keys = K_ROWS * GRID_W
    tq = Q_ROWS * GRID_W
    problems, places = [], []
    for j, bias_ref in enumerate(bias_refs):
        i = step * len(bias_refs) + j
        start = pl.multiple_of(_window_start_row(i, rows) * GRID_W, KEY_BLOCK)
        bias = (bias_ref[0, 0], bias_ref[1, 0])
        for b in range(q_ref.shape[0]):
            problems.append(
                (q_ref[b, j * tq:(j + 1) * tq, :],
                 [k_ref[b, pl.ds(start, n_keys), :], ck_ref[b]],
                 [v_ref[b, pl.ds(start, n_keys), :], cv_ref[b]],
                 [bias, None]))
            places.append((b, j))
    for (b, j), o in zip(places, _stacked_pair_attention(problems)):
        o_ref[b, j * tq:(j + 1) * tq, :] = o.astype(BF16)


def _bias_rows_kernel(rpb_ref, onehot_ref, neg_ref, o_ref):
    o_ref[...] = jnp.dot(rpb_ref[...], onehot_ref[...], precision=HI,
                         preferred_element_type=F32) + neg_ref[...]


def _nbr_bias_table(rpb_l, rows):
    h, n_dr, n_dc = rpb_l.shape
    n_blocks = rows // Q_ROWS
    kr = min(WIN_ROWS, rows)
    variants = np.array([0, min(1, n_blocks - 1), n_blocks - 1])
    r = Q_ROWS * variants[:, None] + np.arange(Q_ROWS)[None, :]
    ks = np.clip(Q_ROWS * variants - WIN_ROWS // 2, 0, rows - K_ROWS)
    key_row = ks[:, None] + np.arange(K_ROWS)[None, :]
    r0 = np.clip(r - kr // 2, 0, rows - kr)
    row_ok = (key_row[:, None, :] >= r0[:, :, None]) & (key_row[:, None, :] < r0[:, :, None] + kr)
    dr = np.clip(key_row[:, None, :] - r[:, :, None] + (WIN_ROWS - 1), 0, n_dr - 1)
    c_idx = np.arange(GRID_W)
    c0 = np.clip(c_idx - WIN_COLS // 2, 0, GRID_W - WIN_COLS)
    col_ok = (c_idx[None, :] >= c0[:, None]) & (c_idx[None, :] < c0[:, None] + WIN_COLS)
    dc = np.clip(c_idx[None, :] - c_idx[:, None] + (WIN_COLS - 1), 0, n_dc - 1)

    k_pad = -(-n_dc // LANES) * LANES
    m_pad = -(-(h * n_dr) // 8) * 8
    onehot = np.zeros((k_pad, GRID_W * GRID_W), np.float32)
    flat = np.arange(GRID_W * GRID_W).reshape(GRID_W, GRID_W)
    onehot[dc[col_ok], flat[col_ok]] = 1.0
    neg = np.where(col_ok, 0.0, NEG_INF).astype(np.float32).reshape(1, -1)
    rpb_pad = jnp.zeros((m_pad, k_pad), F32).at[:h * n_dr, :n_dc].set(
        rpb_l.reshape(h * n_dr, n_dc).astype(F32))
    col_tab = pl.pallas_call(
        _bias_rows_kernel,
        out_shape=jax.ShapeDtypeStruct((m_pad, GRID_W * GRID_W), F32),
        name="bias_rows",
    )(rpb_pad, jnp.asarray(onehot), jnp.asarray(neg))
    col_tab = col_tab[:h * n_dr].reshape(h, n_dr, GRID_W, GRID_W).transpose(0, 2, 1, 3)
    pad = K_ROWS + Q_ROWS
    col_tab = jnp.pad(col_tab, ((0, 0), (0, 0), (pad, pad), (0, 0)))
    col_tab = col_tab.reshape(h, GRID_W, (n_dr + 2 * pad) * GRID_W)
    slabs = []
    for v in range(3):
        for qr in range(Q_ROWS):
            d0 = int(ks[v] - r[v, qr]) + (WIN_ROWS - 1) + pad
            slab = col_tab[:, :, d0 * GRID_W:(d0 + K_ROWS) * GRID_W]
            lane_ok = jnp.asarray(np.repeat(row_ok[v, qr], GRID_W))
            slabs.append(jnp.where(lane_ok[None, None, :], slab, NEG_INF))
    return jnp.stack(slabs, axis=1).reshape(h, 3, Q_ROWS * GRID_W, K_ROWS * GRID_W)


def _nbr_attn(q3, k3, v3, ck3, cv3, bias, layer):
    nb, seq, width = q3.shape
    rows = seq // GRID_W
    n_blocks = rows // Q_ROWS
    n_pairs = width // LANES
    past = ck3.shape[1]
    tq = Q_ROWS * GRID_W
    tk = K_ROWS * GRID_W

    per_step = Q_BLOCKS_PER_STEP if n_blocks % Q_BLOCKS_PER_STEP == 0 else 1

    def variant(i):
        return jnp.where(i == 0, 0, jnp.where(i == n_blocks - 1, 2, 1))

    def bias_spec(j):
        return pl.BlockSpec((2, 1, tq, tk),
                            lambda p, s: (layer * n_pairs + p, variant(s * per_step + j), 0, 0))

    return pl.pallas_call(
        functools.partial(_nbr_attn_kernel, rows=rows),
        grid=(n_pairs, n_blocks // per_step),
        in_specs=[
            pl.BlockSpec((nb, per_step * tq, LANES), lambda p, s: (0, s, p)),
            pl.BlockSpec((nb, seq, LANES), lambda p, s: (0, 0, p)),
            pl.BlockSpec((nb, seq, LANES), lambda p, s: (0, 0, p)),
            pl.BlockSpec((nb, past, LANES), lambda p, s: (0, 0, p)),
            pl.BlockSpec((nb, past, LANES), lambda p, s: (0, 0, p)),
        ] + [bias_spec(j) for j in range(per_step)],
        out_specs=pl.BlockSpec((nb, per_step * tq, LANES), lambda p, s: (0, s, p)),
        out_shape=jax.ShapeDtypeStruct((nb, seq, width), BF16),
        compiler_params=_params(("arbitrary", "arbitrary")),
        name="nbr_attn",
    )(q3, k3, v3, ck3, cv3, *([bias] * per_step))


def _fourier_kernel(x_ref, cdft_ref, d1_ref, d2_ref, twc_ref, tws_ref, o_ref,
                    zr_ref, zi_ref, tr_ref, ti_ref, *, n1, n2):
    gd = cdft_ref.shape[0]
    n_groups = x_ref.shape[2] // gd
    groups = range(n_groups)
    cdft = cdft_ref[...].astype(BF16)
    d1 = d1_ref[...].astype(BF16)
    d2 = d2_ref[...].astype(BF16)
    for g in groups:
        zc = _bdot(x_ref[0, :, g * gd:(g + 1) * gd].astype(BF16), cdft)
        zr_ref[g] = zc[:, :gd]
        zi_ref[g] = zc[:, gd:]

    def stacked(re_ref, im_ref, rows):
        return jnp.concatenate(
            [jnp.concatenate([re_ref[g, rows, :], im_ref[g, rows, :]], axis=0) for g in groups],
            axis=1)

    def stage1(c, carry):
        t = _bdot(d1, stacked(zr_ref, zi_ref, pl.ds(c, n1, stride=n2)).astype(BF16))
        cs = twc_ref[c]
        sn = tws_ref[c]
        row = pl.multiple_of(c * n1, n1)
        for g in groups:
            tr = t[:n1, g * gd:(g + 1) * gd]
            ti = t[n1:, g * gd:(g + 1) * gd]
            tr_ref[g, pl.ds(row, n1), :] = tr * cs + ti * sn
            ti_ref[g, pl.ds(row, n1), :] = ti * cs - tr * sn
        return carry

    lax.fori_loop(0, n2, stage1, 0, unroll=LOOP_UNROLL)

    def stage2(k1, carry):
        rows = pl.ds(k1, n2, stride=n1)
        y = _bdot(d2, stacked(tr_ref, ti_ref, rows).astype(BF16))
        for g in groups:
            zr_ref[g, rows, :] = y[:, g * gd:(g + 1) * gd]
        return carry

    lax.fori_loop(0, n1, stage2, 0, unroll=LOOP_UNROLL)
    for g in groups:
        o_ref[0, :, g * gd:(g + 1) * gd] = zr_ref[g]


def _fourier(f3):
    nb, seq, width = f3.shape
    n2 = GRID_W
    n1 = seq // n2
    gd = FNET_GROUP_DIM
    cc, cs = _dft_mats(gd)
    c1, s1 = _dft_mats(n1)
    c2, s2 = _dft_mats(n2)
    norm = 1.0 / math.sqrt(seq * gd)
    cdft = jnp.asarray(np.concatenate([cc, -cs], axis=1) * norm, F32)
    d1 = jnp.asarray(np.block([[c1, s1], [-s1, c1]]), F32)
    d2 = jnp.asarray(np.concatenate([c2, s2], axis=1), F32)
    ang = 2.0 * np.pi * ((np.arange(n2)[:, None] * np.arange(n1)[None, :]) % seq) / seq
    twc = jnp.asarray(np.broadcast_to(np.cos(ang)[:, :, None], (n2, n1, gd)), F32)
    tws = jnp.asarray(np.broadcast_to(np.sin(ang)[:, :, None], (n2, n1, gd)), F32)
    return pl.pallas_call(
        functools.partial(_fourier_kernel, n1=n1, n2=n2),
        grid=(nb, width // FOURIER_COLS),
        in_specs=[
            pl.BlockSpec((1, seq, FOURIER_COLS), lambda b, g: (b, 0, g)),
            _const_spec(cdft.shape),
            _const_spec(d1.shape),
            _const_spec(d2.shape),
            _const_spec(twc.shape),
            _const_spec(tws.shape),
        ],
        out_specs=pl.BlockSpec((1, seq, FOURIER_COLS), lambda b, g: (b, 0, g)),
        out_shape=jax.ShapeDtypeStruct((nb, seq, width), F32),
        scratch_shapes=[pltpu.VMEM((FOURIER_COLS // gd, seq, gd), F32)] * 4,
        compiler_params=_params(("arbitrary", "arbitrary")),
        name="fourier",
    )(f3, cdft, d1, d2, twc, tws)


def _merge_kernel(fm_ref, o_ref_in, x_ref, m_ref, wgate_ref, wf_ref, wna_ref, wout_ref,
                  g_ref, b_ref, out_ref, *, alpha):
    tm, d = x_ref.shape
    shift = m_ref[0, 3:4, :]
    scale = m_ref[0, 4:5, :]
    gate = m_ref[0, 5:6, :]
    sr = min(tm, ROW_TILE)
    rows = [slice(r0, r0 + sr) for r0 in range(0, tm, sr)]
    xs = [x_ref[rs, :] for rs in rows]
    us = [(x * (1.0 + scale) + shift).astype(BF16) for x in xs]
    slabs = [(fm_ref[rs, :], o_ref_in[rs, :], _bdot(u, wgate_ref[:, :d]), _bdot(u, wgate_ref[:, d:]), x)
             for rs, u, x in zip(rows, us, xs)]
    outs = _merge_out(slabs, gate, wf_ref, wna_ref, wout_ref, g_ref[1:2, :], b_ref[1:2, :], alpha)
    for rs, out in zip(rows, outs):
        out_ref[rs, :] = out


def _merge(fm2, o2, x2, x_first_row, mods_l, rows_per_group, group0, win, wf, wna, wout, layer,
           ln_g, ln_b, alpha):
    rows, d = fm2.shape[0], x2.shape[1]
    tm = math.gcd(math.gcd(MERGE_ROW_TILE, rows_per_group), x_first_row)
    tiles_per_group = rows_per_group // tm
    x_first_tile = x_first_row // tm
    row_spec = lambda w: pl.BlockSpec((tm, w), lambda i: (i, 0))
    assert win.shape[2] == 2 * (2 * d), "gate columns must be the second half of w_in"
    return pl.pallas_call(
        functools.partial(_merge_kernel, alpha=alpha),
        grid=(rows // tm,),
        in_specs=[
            row_spec(fm2.shape[1]), row_spec(o2.shape[1]),
            pl.BlockSpec((tm, d), lambda i: (x_first_tile + i, 0)),
            pl.BlockSpec((1, N_MOD, d), lambda i: (group0 + i // tiles_per_group, 0, 0)),
            _layer_spec(win, layer, cols=2 * d, col_block=1),
            _layer_spec(wf, layer), _layer_spec(wna, layer), _layer_spec(wout, layer),
            _const_spec(ln_g.shape), _const_spec(ln_b.shape),
        ],
        out_specs=row_spec(d),
        out_shape=jax.ShapeDtypeStruct((rows, d), F32),
        compiler_params=_params(("arbitrary",)),
        name="merge",
    )(fm2, o2, x2, mods_l, win, wf, wna, wout, ln_g, ln_b)


def kernel(x_prompt, x_sample, cache_k, cache_v, c, c_ctx, w_ada, b_ada, ln_g, ln_b, w_ff1_up,
           w_ff1_down, w_in, rpb, w_fourier, w_na_out, w_out, w_ff2_up, w_ff2_down):
    batch, seq, d = x_prompt.shape
    dec_batch, dec_seq, _ = x_sample.shape
    depth = w_ada.shape[0]
    alpha = (2 * depth) ** 0.25
    rows_lat = dec_seq // GRID_W

    cvec = jnp.zeros((MOD_ROWS, d), F32).at[0].set(c_ctx).at[1:1 + dec_batch].set(c)
    mods = _mods(cvec, w_ada, b_ada).reshape(depth, MOD_ROWS, N_MOD, d)

    y_p = x_prompt.reshape(batch * seq, d)
    y_s = x_sample.reshape(dec_batch * dec_seq, d)
    up1, dn1 = w_ff1_up.astype(BF16), w_ff1_down.astype(BF16)
    up2, dn2 = w_ff2_up.astype(BF16), w_ff2_down.astype(BF16)
    win = w_in.astype(BF16)
    wf = w_fourier.astype(BF16)
    wna = w_na_out.astype(BF16)
    wout = w_out.astype(BF16)
    ck = cache_k.transpose(1, 0, 3, 2, 4).reshape(depth, dec_batch, -1, NA_WIDTH).astype(BF16)
    cv = cache_v.transpose(1, 0, 3, 2, 4).reshape(depth, dec_batch, -1, NA_WIDTH).astype(BF16)

    bias = _nbr_bias_table(rpb.reshape((-1,) + rpb.shape[2:]), rows_lat)

    rows_p, rows_s = batch * seq, dec_batch * dec_seq

    def segments(src_p, first_p, src_s, first_s):
        return [(src_p, first_p, rows_p, 0, rows_p), (src_s, first_s, rows_s, 1, dec_seq)]

    kv = None
    pending = segments(y_p, 0, y_s, 0)
    for l in range(depth):
        m_l = mods[l]
        g_l, b_l = ln_g[l], ln_b[l]
        y = _ffn(pending, m_l, up1, dn1, l, g_l, b_l, 0, alpha)

        q2, k2, v2, f2 = _in_proj(y, rows_p, rows_s, m_l, dec_seq, 1, win, l)
        to3 = lambda t: t.reshape(dec_batch, dec_seq, t.shape[-1])
        o3 = _nbr_attn(to3(q2), to3(k2), to3(v2), ck[l], cv[l], bias, l)
        fm3 = _fourier(to3(f2))
        y_s = _merge(fm3.reshape(-1, FNET_WIDTH), o3.reshape(-1, NA_WIDTH), y, rows_p, m_l,
                     dec_seq, 1, win, wf, wna, wout, l, g_l, b_l, alpha)

        y, y_s = lax.optimization_barrier((y, y_s))
        y_p3, new_k, new_v = _ctx_mix(y.reshape(-1, seq, d), batch, m_l, win, wf, wna, wout,
                                      g_l, b_l, alpha, l, depth, kv)
        kv = (new_k, new_v)

        mixed = segments(y_p3.reshape(rows_p, d), 0, y_s, 0)
        if l + 1 < depth:
            y = _ffn(mixed, m_l, up2, dn2, l, g_l, b_l, 2, alpha)
            pending = segments(y, 0, y, rows_p)
        else:
            y_p = _ffn(mixed[:1], m_l, up2, dn2, l, g_l, b_l, 2, alpha)
            y_s = _ffn(mixed[1:], m_l, up2, dn2, l, g_l, b_l, 2, alpha)

    return (y_p.reshape(batch, seq, d), y_s.reshape(dec_batch, dec_seq, d), new_k, new_v)
```

```python
import functools
import math

import numpy as np
import jax
import jax.numpy as jnp
from jax import lax
from jax.experimental import pallas as pl
from jax.experimental.pallas import tpu as pltpu

F32 = jnp.float32
BF16 = jnp.bfloat16

HEAD_DIM = 64
NA_HEADS = 8
NA_WIDTH = NA_HEADS * HEAD_DIM
FNET_GROUPS = 4
FNET_GROUP_DIM = 128
FNET_WIDTH = FNET_GROUPS * FNET_GROUP_DIM
GRID_W = 64
WIN_ROWS = 8
WIN_COLS = 16
N_SUB = 3
N_MOD = 3 * N_SUB
ATTN_SCALE = HEAD_DIM ** -0.5
LN_EPS = 1e-5
NEG_INF = -1e30

LANES = 128
MXU_DIM = 256
VMEM_LIMIT_BYTES = 56 * 1024 * 1024

ROW_TILE = 512
FFN_ROW_TILE = 1024
FFN_SUB_ROWS = 512
MERGE_ROW_TILE = 1024
FF_CHUNK = MXU_DIM
Q_ROWS = 4
K_ROWS = Q_ROWS + WIN_ROWS
KEY_BLOCK = GRID_W * math.gcd(Q_ROWS, WIN_ROWS // 2)
Q_BLOCKS_PER_STEP = 4
IN_CHUNK = 2 * MXU_DIM
MOD_ROWS = 8
MODS_K_ROWS = 256
FOURIER_COLS = 2 * FNET_GROUP_DIM
LOOP_UNROLL = 8


def _params(sem):
    return pltpu.CompilerParams(dimension_semantics=sem, vmem_limit_bytes=VMEM_LIMIT_BYTES)


def _const_spec(shape):
    nd = len(shape)
    return pl.BlockSpec(shape, lambda *_: (0,) * nd, pipeline_mode=pl.Buffered(1))


def _layer_spec(w, layer, cols=None, col_block=0):
    _, r, n = w.shape
    cols = n if cols is None else cols
    return pl.BlockSpec((None, r, cols), lambda *_: (layer, 0, col_block),
                        pipeline_mode=pl.Buffered(1))


def _layer_norm(r, g, b):
    mu = jnp.mean(r, axis=-1, keepdims=True)
    d = r - mu
    var = jnp.mean(d * d, axis=-1, keepdims=True)
    return d * lax.rsqrt(var + LN_EPS) * g + b


def _bdot(a, b):
    return jnp.dot(a, b, preferred_element_type=F32)


def _dot_nt(a, b):
    return lax.dot_general(a, b, (((1,), (1,)), ((), ())), preferred_element_type=F32)


def _dft_mats(n):
    k = np.arange(n)
    ang = 2.0 * np.pi * ((k[:, None] * k[None, :]) % n) / n
    return np.cos(ang), np.sin(ang)


def _mods_kernel(c_ref, w_ref, b_ref, o_ref):
    k = pl.program_id(1)
    c = c_ref[...]
    s = (c * jax.nn.sigmoid(c)).astype(BF16)
    part = _bdot(s, w_ref[0].astype(BF16))

    @pl.when(k == 0)
    def _():
        o_ref[0] = part + b_ref[0]

    @pl.when(k > 0)
    def _():
        o_ref[0] += part


def _mods(cvec, w_ada, b_ada):
    depth, d, n = w_ada.shape
    tk = MODS_K_ROWS
    return pl.pallas_call(
        _mods_kernel,
        grid=(depth, d // tk),
        in_specs=[
            pl.BlockSpec((MOD_ROWS, tk), lambda l, k: (0, k)),
            pl.BlockSpec((1, tk, n), lambda l, k: (l, k, 0)),
            pl.BlockSpec((1, 1, n), lambda l, k: (l, 0, 0)),
        ],
        out_specs=pl.BlockSpec((1, MOD_ROWS, n), lambda l, k: (l, 0, 0)),
        out_shape=jax.ShapeDtypeStruct((depth, MOD_ROWS, n), F32),
        compiler_params=_params(("arbitrary", "arbitrary")),
        name="mods",
    )(cvec, w_ada, b_ada.reshape(depth, 1, n))


def _ffn_kernel(*refs, sub, alpha, seg_starts):
    n_seg = len(seg_starts)
    x_refs = refs[:n_seg]
    m_ref, wup_ref, wdn_ref, g_ref, b_ref, o_ref, act_ref = refs[n_seg:]
    step = pl.program_id(0)

    def read_x(rs):
        x = x_refs[-1][rs, :]
        for s in range(n_seg - 2, -1, -1):
            x = jnp.where(step < seg_starts[s + 1], x_refs[s][rs, :], x)
        return x

    shift = m_ref[0, 3 * sub:3 * sub + 1, :]
    scale = m_ref[0, 3 * sub + 1:3 * sub + 2, :]
    gate = m_ref[0, 3 * sub + 2:3 * sub + 3, :]
    ff = wdn_ref.shape[0]
    tm = o_ref.shape[0]
    sr = min(tm, FFN_SUB_ROWS)
    slabs = [slice(r0, r0 + sr) for r0 in range(0, tm, sr)]
    us = [(read_x(rs) * (1.0 + scale) + shift).astype(BF16) for rs in slabs]
    for c0 in range(0, ff, FF_CHUNK):
        for rs, u in zip(slabs, us):
            a = _bdot(u, wup_ref[:, c0:c0 + FF_CHUNK])
            g = _bdot(u, wup_ref[:, ff + c0:ff + c0 + FF_CHUNK])
            act_ref[rs, c0:c0 + FF_CHUNK] = ((g * jax.nn.sigmoid(g)) * a).astype(BF16)
    for rs in slabs:
        y = _bdot(act_ref[rs, :], wdn_ref[...])
        r = alpha * read_x(rs) + (0.5 * gate) * y
        o_ref[rs, :] = _layer_norm(r, g_ref[sub:sub + 1, :], b_ref[sub:sub + 1, :])


def _ffn(segments, mods_l, wup, wdn, layer, ln_g, ln_b, sub, alpha):
    d = segments[0][0].shape[1]
    tm = FFN_ROW_TILE
    for _, first_row, n_rows, _, rows_per_group in segments:
        tm = math.gcd(math.gcd(tm, rows_per_group), math.gcd(n_rows, first_row))
    starts, n_steps = [], 0
    for _, _, n_rows, _, _ in segments:
        starts.append(n_steps)
        n_steps += n_rows // tm

    def x_spec(s):
        _, first_row, n_rows, _, _ = segments[s]
        first, count = first_row // tm, n_rows // tm
        return pl.BlockSpec((tm, d), lambda i: (first + jnp.clip(i - starts[s], 0, count - 1), 0))

    def group(i):
        g = None
        for s in range(len(segments) - 1, -1, -1):
            _, _, _, group0, rows_per_group = segments[s]
            g_s = group0 + jnp.maximum(i - starts[s], 0) // (rows_per_group // tm)
            g = g_s if g is None else jnp.where(i < starts[s + 1], g_s, g)
        return g

    return pl.pallas_call(
        functools.partial(_ffn_kernel, sub=sub, alpha=alpha, seg_starts=tuple(starts)),
        grid=(n_steps,),
        in_specs=[x_spec(s) for s in range(len(segments))] + [
            pl.BlockSpec((1, N_MOD, d), lambda i: (group(i), 0, 0)),
            _layer_spec(wup, layer),
            _layer_spec(wdn, layer),
            _const_spec(ln_g.shape),
            _const_spec(ln_b.shape),
        ],
        out_specs=pl.BlockSpec((tm, d), lambda i: (i, 0)),
        out_shape=jax.ShapeDtypeStruct((n_steps * tm, d), F32),
        scratch_shapes=[pltpu.VMEM((tm, wdn.shape[1]), BF16)],
        compiler_params=_params(("arbitrary",)),
        name=f"ffn{sub}",
    )(*[seg[0] for seg in segments], mods_l, wup, wdn, ln_g, ln_b)


def _row_reduce(arrays, combine, reduce):
    acc = None
    for a in arrays:
        for c0 in range(0, a.shape[1], LANES):
            chunk = a[:, c0:c0 + LANES]
            acc = chunk if acc is None else combine(acc, chunk)
    return reduce(acc, axis=-1, keepdims=True)


def _stacked_pair_attention(problems):
    lane = lax.broadcasted_iota(jnp.int32, (1, LANES), 1)
    scores = []
    for q, k_list, _, bias_list in problems:
        zero = jnp.zeros_like(q)
        q2 = jnp.concatenate([jnp.where(lane < HEAD_DIM, q, zero),
                              jnp.where(lane >= HEAD_DIM, q, zero)], axis=0)
        s = []
        for k, bias in zip(k_list, bias_list):
            sk = _dot_nt(q2, k)
            if bias is not None:
                sk = sk + jnp.concatenate(bias, axis=0)
            s.append(sk)
        scores.append(s)
    maxes = [_row_reduce(s, jnp.maximum, jnp.max) for s in scores]
    probs = [[jnp.exp(sk - m) for sk in s] for s, m in zip(scores, maxes)]
    dens = [_row_reduce(p, jnp.add, jnp.sum) for p in probs]
    outs = []
    for (q, _, v_list, _), p, den in zip(problems, probs, dens):
        o = None
        for pk, v in zip(p, v_list):
            pv = _bdot(pk.astype(BF16), v)
            o = pv if o is None else o + pv
        o = o / den
        m_rows = q.shape[0]
        outs.append(jnp.where(lane < HEAD_DIM, o[:m_rows], o[m_rows:]))
    return outs


def _merge_out(slabs, gate, wf_ref, wna_ref, wout_ref, ln_g, ln_b, alpha):
    branches = [(_bdot(fm.astype(BF16), wf_ref[...]), _bdot(o, wna_ref[...]))
                for fm, o, _, _, _ in slabs]
    merged = [(jax.nn.sigmoid(ga) * a + jax.nn.sigmoid(gb) * b).astype(BF16)
              for (_, _, ga, gb, _), (a, b) in zip(slabs, branches)]
    mixes = [_bdot(mg, wout_ref[...]) for mg in merged]
    return [_layer_norm(alpha * x + gate * mix, ln_g, ln_b)
            for (_, _, _, _, x), mix in zip(slabs, mixes)]


def _ctx_mix_kernel(x_ref, m_ref, win_ref, wf_ref, wna_ref, wout_ref, g_ref, b_ref,
                    cdft_ref, pdft_ref, *rest, alpha, n_alias, kv_slot):
    o_ref, k_ref, v_ref, z_ref, oatt_ref, fm_ref = rest[n_alias:]
    nb, seq, d = x_ref.shape
    rows = nb * seq
    x = x_ref[...].reshape(rows, d)
    shift = m_ref[0, 3:4, :]
    scale = m_ref[0, 4:5, :]
    gate = m_ref[0, 5:6, :]
    u = (x * (1.0 + scale) + shift).astype(BF16)
    for c0 in range(0, win_ref.shape[1], IN_CHUNK):
        z_ref[:, c0:c0 + IN_CHUNK] = _bdot(u, win_ref[:, c0:c0 + IN_CHUNK])

    cdft = cdft_ref[...].astype(BF16)
    pdft = pdft_ref[...].astype(BF16)
    q_off, k_off, v_off, f_off = 0, NA_WIDTH, 2 * NA_WIDTH, 3 * NA_WIDTH
    ga_off = f_off + FNET_WIDTH
    gb_off = ga_off + d
    for b in range(nb):
        r0 = b * seq
        for h in range(NA_HEADS):
            k_ref[b, kv_slot, h] = z_ref[r0:r0 + seq, k_off + h * HEAD_DIM:k_off + (h + 1) * HEAD_DIM]
            v_ref[b, kv_slot, h] = z_ref[r0:r0 + seq, v_off + h * HEAD_DIM:v_off + (h + 1) * HEAD_DIM]
        for slot in range(k_ref.shape[1]):
            if slot != kv_slot:
                k_ref[b, slot] = jnp.zeros(k_ref.shape[2:], F32)
                v_ref[b, slot] = jnp.zeros(v_ref.shape[2:], F32)
        zc = [_bdot(z_ref[r0:r0 + seq, f_off + g * FNET_GROUP_DIM:f_off + (g + 1) * FNET_GROUP_DIM]
                    .astype(BF16), cdft).astype(BF16) for g in range(FNET_GROUPS)]
        stacked = jnp.concatenate(
            [jnp.concatenate([z[:, :FNET_GROUP_DIM] for z in zc], axis=1),
             jnp.concatenate([z[:, FNET_GROUP_DIM:] for z in zc], axis=1)], axis=0)
        fm_ref[r0:r0 + seq, :] = _bdot(pdft, stacked)

    problems = []
    for b in range(nb):
        r0 = b * seq
        for hp in range(NA_WIDTH // LANES):
            c0 = hp * LANES
            q = (z_ref[r0:r0 + seq, q_off + c0:q_off + c0 + LANES] * ATTN_SCALE).astype(BF16)
            k = z_ref[r0:r0 + seq, k_off + c0:k_off + c0 + LANES].astype(BF16)
            v = z_ref[r0:r0 + seq, v_off + c0:v_off + c0 + LANES].astype(BF16)
            problems.append((b * seq, c0, (q, [k], [v], [None])))
    outs = _stacked_pair_attention([p for _, _, p in problems])
    for (r0, c0, _), o in zip(problems, outs):
        oatt_ref[r0:r0 + seq, c0:c0 + LANES] = o.astype(BF16)

    slab = (fm_ref[...], oatt_ref[...], z_ref[:, ga_off:ga_off + d], z_ref[:, gb_off:gb_off + d], x)
    (out,) = _merge_out([slab], gate, wf_ref, wna_ref, wout_ref, g_ref[1:2, :], b_ref[1:2, :],
                        alpha)
    o_ref[...] = out.reshape(nb, seq, d)


def _ctx_mix(x3, batch, mods_l, win, wf, wna, wout, ln_g, ln_b, alpha, layer, depth, kv_prev):
    _, seq, d = x3.shape
    nb = 2 if batch % 2 == 0 else 1
    in_width = win.shape[2]
    cc, cs = _dft_mats(FNET_GROUP_DIM)
    pc, ps = _dft_mats(seq)
    norm = 1.0 / math.sqrt(seq * FNET_GROUP_DIM)
    cdft = jnp.asarray(np.concatenate([cc, -cs], axis=1) * norm, F32)
    pdft = jnp.asarray(np.concatenate([pc, ps], axis=1), F32)
    kv_shape = jax.ShapeDtypeStruct((batch, depth, NA_HEADS, seq, HEAD_DIM), F32)
    if kv_prev is None:
        kv_spec = pl.BlockSpec((nb, depth, NA_HEADS, seq, HEAD_DIM), lambda i: (i, 0, 0, 0, 0))
        kv_slot = layer
    else:
        kv_spec = pl.BlockSpec((nb, 1, NA_HEADS, seq, HEAD_DIM), lambda i: (i, layer, 0, 0, 0))
        kv_slot = 0
    operands = [x3, mods_l, win, wf, wna, wout, ln_g, ln_b, cdft, pdft]
    in_specs = [
        pl.BlockSpec((nb, seq, d), lambda i: (i, 0, 0)),
        pl.BlockSpec((1, N_MOD, d), lambda i: (0, 0, 0)),
        _layer_spec(win, layer),
        _layer_spec(wf, layer),
        _layer_spec(wna, layer),
        _layer_spec(wout, layer),
        _const_spec(ln_g.shape),
        _const_spec(ln_b.shape),
        _const_spec(cdft.shape),
        _const_spec(pdft.shape),
    ]
    aliases = {}
    if kv_prev is not None:
        aliases = {len(operands): 1, len(operands) + 1: 2}
        operands += list(kv_prev)
        in_specs += [pl.BlockSpec(memory_space=pl.ANY)] * 2
    return pl.pallas_call(
        functools.partial(_ctx_mix_kernel, alpha=alpha, n_alias=len(aliases), kv_slot=kv_slot),
        grid=(batch // nb,),
        in_specs=in_specs,
        out_specs=[pl.BlockSpec((nb, seq, d), lambda i: (i, 0, 0)), kv_spec, kv_spec],
        out_shape=[jax.ShapeDtypeStruct((batch, seq, d), F32), kv_shape, kv_shape],
        input_output_aliases=aliases,
        scratch_shapes=[
            pltpu.VMEM((nb * seq, in_width), F32),
            pltpu.VMEM((nb * seq, NA_WIDTH), BF16),
            pltpu.VMEM((nb * seq, FNET_WIDTH), F32),
        ],
        compiler_params=_params(("arbitrary",)),
        name="ctx_mix",
    )(*operands)


def _in_proj_kernel(x_ref, m_ref, win_ref, q_ref, k_ref, v_ref, f_ref):
    x = x_ref[...]
    shift = m_ref[0, 3:4, :]
    scale = m_ref[0, 4:5, :]
    u = (x * (1.0 + scale) + shift).astype(BF16)
    cw = NA_WIDTH
    q_ref[...] = (_bdot(u, win_ref[:, 0:cw]) * ATTN_SCALE).astype(BF16)
    k_ref[...] = _bdot(u, win_ref[:, cw:2 * cw]).astype(BF16)
    v_ref[...] = _bdot(u, win_ref[:, 2 * cw:3 * cw]).astype(BF16)
    f_ref[...] = _bdot(u, win_ref[:, 3 * cw:3 * cw + FNET_WIDTH])


def _in_proj(x2, first_row, rows, mods_l, rows_per_group, group0, win, layer):
    d = x2.shape[1]
    tm = math.gcd(math.gcd(ROW_TILE, rows_per_group), first_row)
    tiles_per_group = rows_per_group // tm
    first_tile = first_row // tm
    cw = NA_WIDTH
    qkvf = 3 * NA_WIDTH + FNET_WIDTH
    row_spec = lambda w: pl.BlockSpec((tm, w), lambda i: (i, 0))
    return pl.pallas_call(
        _in_proj_kernel,
        grid=(rows // tm,),
        in_specs=[
            pl.BlockSpec((tm, d), lambda i: (first_tile + i, 0)),
            pl.BlockSpec((1, N_MOD, d), lambda i: (group0 + i // tiles_per_group, 0, 0)),
            _layer_spec(win, layer, cols=qkvf, col_block=0),
        ],
        out_specs=[row_spec(cw), row_spec(cw), row_spec(cw), row_spec(FNET_WIDTH)],
        out_shape=[
            jax.ShapeDtypeStruct((rows, cw), BF16),
            jax.ShapeDtypeStruct((rows, cw), BF16),
            jax.ShapeDtypeStruct((rows, cw), BF16),
            jax.ShapeDtypeStruct((rows, FNET_WIDTH), F32),
        ],
        compiler_params=_params(("arbitrary",)),
        name="in_proj",
    )(x2, mods_l, win)


def _window_start_row(i, rows):
    return jnp.clip(Q_ROWS * i - WIN_ROWS // 2, 0, rows - K_ROWS)


def _nbr_attn_kernel(q_ref, k_ref, v_ref, ck_ref, cv_ref, *rest, rows):
    bias_refs, o_ref = rest[:-1], rest[-1]
    step = pl.program_id(1)
    n_keys = K_ROWS * GRID_W
    tq = Q_ROWS * GRID_W
    problems, places = [], []
    for j, bias_ref in enumerate(bias_refs):
        i = step * len(bias_refs) + j
        start = pl.multiple_of(_window_start_row(i, rows) * GRID_W, KEY_BLOCK)
        bias = (bias_ref[0, 0], bias_ref[1, 0])
        for b in range(q_ref.shape[0]):
            problems.append(
                (q_ref[b, j * tq:(j + 1) * tq, :],
                 [k_ref[b, pl.ds(start, n_keys), :], ck_ref[b]],
                 [v_ref[b, pl.ds(start, n_keys), :], cv_ref[b]],
                 [bias, None]))
            places.append((b, j))
    for (b, j), o in zip(places, _stacked_pair_attention(problems)):
        o_ref[b, j * tq:(j + 1) * tq, :] = o.astype(BF16)


def _bias_table_kernel(rpb_ref, ok_ref, o_ref, *, n_dr, n_dc, slot0, n_cols, starts):
    h = pl.program_id(0)
    base = h * (n_dr * n_dc)
    row = lax.broadcasted_iota(jnp.int32, (GRID_W, LANES), 0)
    lane = lax.broadcasted_iota(jnp.int32, (GRID_W, LANES), 1)
    rel = (lane & (GRID_W - 1)) - row + (WIN_COLS - 1)
    low = lane < GRID_W

    def column(dr_low, dr_high):
        tile = jnp.zeros((GRID_W, LANES), F32)
        valid = [0 <= dr < n_dr for dr in (dr_low, dr_high)]
        if not any(valid):
            return tile
        for d in range(n_dc):
            lo = rpb_ref[base + dr_low * n_dc + d] if valid[0] else 0.0
            hi = rpb_ref[base + dr_high * n_dc + d] if valid[1] else 0.0
            tile = jnp.where(rel == d, jnp.where(low, lo, hi), tile)
        return tile

    builds = [[column(2 * p - shift - slot0, 2 * p + 1 - shift - slot0) for p in range(n_cols)]
              for shift in (0, 1)]
    cols_per_slab = K_ROWS * GRID_W // LANES
    for t, start in enumerate(starts):
        shift = start % 2
        c0 = (start + shift) // 2
        window = jnp.concatenate(builds[shift][c0:c0 + cols_per_slab], axis=1)
        o_ref[0, t] = jnp.where(ok_ref[t] > 0.0, window, NEG_INF)


def _nbr_bias_table(rpb_all, rows):
    h, n_dr, n_dc = rpb_all.shape
    n_blocks = rows // Q_ROWS
    kr = min(WIN_ROWS, rows)
    variants = np.array([0, min(1, n_blocks - 1), n_blocks - 1])
    r = Q_ROWS * variants[:, None] + np.arange(Q_ROWS)[None, :]
    ks = np.clip(Q_ROWS * variants - WIN_ROWS // 2, 0, rows - K_ROWS)
    key_row = ks[:, None] + np.arange(K_ROWS)[None, :]
    r0 = np.clip(r - kr // 2, 0, rows - kr)
    row_ok = (key_row[:, None, :] >= r0[:, :, None]) & (key_row[:, None, :] < r0[:, :, None] + kr)
    c_idx = np.arange(GRID_W)
    c0 = np.clip(c_idx - WIN_COLS // 2, 0, GRID_W - WIN_COLS)
    col_ok = (c_idx[None, :] >= c0[:, None]) & (c_idx[None, :] < c0[:, None] + WIN_COLS)
    rel = c_idx[None, :] - c_idx[:, None] + (WIN_COLS - 1)
    assert rel[col_ok].min() >= 0 and rel[col_ok].max() < n_dc

    first_dr = (ks[:, None] - r + (WIN_ROWS - 1)).reshape(-1)
    slot0 = int(max(0, -first_dr.min()))
    starts = tuple(int(d) + slot0 for d in first_dr)
    n_cols = (max(starts) + 1 + K_ROWS + 1) // 2
    ok = (row_ok[:, :, None, :, None] & col_ok[None, None, :, None, :]).reshape(
        3 * Q_ROWS, GRID_W, K_ROWS * GRID_W).astype(np.float32)
    n_slabs = 3 * Q_ROWS
    table = pl.pallas_call(
        functools.partial(_bias_table_kernel, n_dr=n_dr, n_dc=n_dc, slot0=slot0, n_cols=n_cols,
                          starts=starts),
        grid=(h,),
        in_specs=[
            pl.BlockSpec(memory_space=pltpu.SMEM),
            _const_spec(ok.shape),
        ],
        out_specs=pl.BlockSpec((1, n_slabs, GRID_W, K_ROWS * GRID_W), lambda i: (i, 0, 0, 0)),
        out_shape=jax.ShapeDtypeStruct((h, n_slabs, GRID_W, K_ROWS * GRID_W), F32),
        compiler_params=_params(("arbitrary",)),
        name="bias_table",
    )(rpb_all.reshape(-1).astype(F32), jnp.asarray(ok))
    return table.reshape(h, 3, Q_ROWS * GRID_W, K_ROWS * GRID_W)


def _nbr_attn(q3, k3, v3, ck3, cv3, bias, layer):
    nb, seq, width = q3.shape
    rows = seq // GRID_W
    n_blocks = rows // Q_ROWS
    n_pairs = width // LANES
    past = ck3.shape[1]
    tq = Q_ROWS * GRID_W
    tk = K_ROWS * GRID_W

    per_step = Q_BLOCKS_PER_STEP if n_blocks % Q_BLOCKS_PER_STEP == 0 else 1

    def variant(i):
        return jnp.where(i == 0, 0, jnp.where(i == n_blocks - 1, 2, 1))

    def bias_spec(j):
        return pl.BlockSpec((2, 1, tq, tk),
                            lambda p, s: (layer * n_pairs + p, variant(s * per_step + j), 0, 0))

    return pl.pallas_call(
        functools.partial(_nbr_attn_kernel, rows=rows),
        grid=(n_pairs, n_blocks // per_step),
        in_specs=[
            pl.BlockSpec((nb, per_step * tq, LANES), lambda p, s: (0, s, p)),
            pl.BlockSpec((nb, seq, LANES), lambda p, s: (0, 0, p)),
            pl.BlockSpec((nb, seq, LANES), lambda p, s: (0, 0, p)),
            pl.BlockSpec((nb, past, LANES), lambda p, s: (0, 0, p)),
            pl.BlockSpec((nb, past, LANES), lambda p, s: (0, 0, p)),
        ] + [bias_spec(j) for j in range(per_step)],
        out_specs=pl.BlockSpec((nb, per_step * tq, LANES), lambda p, s: (0, s, p)),
        out_shape=jax.ShapeDtypeStruct((nb, seq, width), BF16),
        compiler_params=_params(("arbitrary", "arbitrary")),
        name="nbr_attn",
    )(q3, k3, v3, ck3, cv3, *([bias] * per_step))


def _fourier_kernel(x_ref, cdft_ref, d1_ref, d2_ref, twc_ref, tws_ref, o_ref,
                    zr_ref, zi_ref, tr_ref, ti_ref, *, n1, n2):
    gd = cdft_ref.shape[0]
    n_groups = x_ref.shape[2] // gd
    groups = range(n_groups)
    cdft = cdft_ref[...].astype(BF16)
    d1 = d1_ref[...].astype(BF16)
    d2 = d2_ref[...].astype(BF16)
    for g in groups:
        zc = _bdot(x_ref[0, :, g * gd:(g + 1) * gd].astype(BF16), cdft)
        zr_ref[g] = zc[:, :gd]
        zi_ref[g] = zc[:, gd:]

    def stacked(re_ref, im_ref, rows):
        return jnp.concatenate(
            [jnp.concatenate([re_ref[g, rows, :], im_ref[g, rows, :]], axis=0) for g in groups],
            axis=1)

    def stage1(c, carry):
        t = _bdot(d1, stacked(zr_ref, zi_ref, pl.ds(c, n1, stride=n2)).astype(BF16))
        cs = twc_ref[c]
        sn = tws_ref[c]
        row = pl.multiple_of(c * n1, n1)
        for g in groups:
            tr = t[:n1, g * gd:(g + 1) * gd]
            ti = t[n1:, g * gd:(g + 1) * gd]
            tr_ref[g, pl.ds(row, n1), :] = tr * cs + ti * sn
            ti_ref[g, pl.ds(row, n1), :] = ti * cs - tr * sn
        return carry

    lax.fori_loop(0, n2, stage1, 0, unroll=LOOP_UNROLL)

    def stage2(k1, carry):
        rows = pl.ds(k1, n2, stride=n1)
        y = _bdot(d2, stacked(tr_ref, ti_ref, rows).astype(BF16))
        for g in groups:
            zr_ref[g, rows, :] = y[:, g * gd:(g + 1) * gd]
        return carry

    lax.fori_loop(0, n1, stage2, 0, unroll=LOOP_UNROLL)
    for g in groups:
        o_ref[0, :, g * gd:(g + 1) * gd] = zr_ref[g]


def _fourier(f3):
    nb, seq, width = f3.shape
    n2 = GRID_W
    n1 = seq // n2
    gd = FNET_GROUP_DIM
    cc, cs = _dft_mats(gd)
    c1, s1 = _dft_mats(n1)
    c2, s2 = _dft_mats(n2)
    norm = 1.0 / math.sqrt(seq * gd)
    cdft = jnp.asarray(np.concatenate([cc, -cs], axis=1) * norm, F32)
    d1 = jnp.asarray(np.block([[c1, s1], [-s1, c1]]), F32)
    d2 = jnp.asarray(np.concatenate([c2, s2], axis=1), F32)
    ang = 2.0 * np.pi * ((np.arange(n2)[:, None] * np.arange(n1)[None, :]) % seq) / seq
    twc = jnp.asarray(np.broadcast_to(np.cos(ang)[:, :, None], (n2, n1, gd)), F32)
    tws = jnp.asarray(np.broadcast_to(np.sin(ang)[:, :, None], (n2, n1, gd)), F32)
    return pl.pallas_call(
        functools.partial(_fourier_kernel, n1=n1, n2=n2),
        grid=(nb, width // FOURIER_COLS),
        in_specs=[
            pl.BlockSpec((1, seq, FOURIER_COLS), lambda b, g: (b, 0, g)),
            _const_spec(cdft.shape),
            _const_spec(d1.shape),
            _const_spec(d2.shape),
            _const_spec(twc.shape),
            _const_spec(tws.shape),
        ],
        out_specs=pl.BlockSpec((1, seq, FOURIER_COLS), lambda b, g: (b, 0, g)),
        out_shape=jax.ShapeDtypeStruct((nb, seq, width), F32),
        scratch_shapes=[pltpu.VMEM((FOURIER_COLS // gd, seq, gd), F32)] * 4,
        compiler_params=_params(("arbitrary", "arbitrary")),
        name="fourier",
    )(f3, cdft, d1, d2, twc, tws)


def _merge_kernel(fm_ref, o_ref_in, x_ref, m_ref, wgate_ref, wf_ref, wna_ref, wout_ref,
                  g_ref, b_ref, out_ref, *, alpha):
    tm, d = x_ref.shape
    shift = m_ref[0, 3:4, :]
    scale = m_ref[0, 4:5, :]
    gate = m_ref[0, 5:6, :]
    sr = min(tm, ROW_TILE)
    rows = [slice(r0, r0 + sr) for r0 in range(0, tm, sr)]
    xs = [x_ref[rs, :] for rs in rows]
    us = [(x * (1.0 + scale) + shift).astype(BF16) for x in xs]
    slabs = [(fm_ref[rs, :], o_ref_in[rs, :], _bdot(u, wgate_ref[:, :d]), _bdot(u, wgate_ref[:, d:]), x)
             for rs, u, x in zip(rows, us, xs)]
    outs = _merge_out(slabs, gate, wf_ref, wna_ref, wout_ref, g_ref[1:2, :], b_ref[1:2, :], alpha)
    for rs, out in zip(rows, outs):
        out_ref[rs, :] = out


def _merge(fm2, o2, x2, x_first_row, mods_l, rows_per_group, group0, win, wf, wna, wout, layer,
           ln_g, ln_b, alpha):
    rows, d = fm2.shape[0], x2.shape[1]
    tm = math.gcd(math.gcd(MERGE_ROW_TILE, rows_per_group), x_first_row)
    tiles_per_group = rows_per_group // tm
    x_first_tile = x_first_row // tm
    row_spec = lambda w: pl.BlockSpec((tm, w), lambda i: (i, 0))
    assert win.shape[2] == 2 * (2 * d), "gate columns must be the second half of w_in"
    return pl.pallas_call(
        functools.partial(_merge_kernel, alpha=alpha),
        grid=(rows // tm,),
        in_specs=[
            row_spec(fm2.shape[1]), row_spec(o2.shape[1]),
            pl.BlockSpec((tm, d), lambda i: (x_first_tile + i, 0)),
            pl.BlockSpec((1, N_MOD, d), lambda i: (group0 + i // tiles_per_group, 0, 0)),
            _layer_spec(win, layer, cols=2 * d, col_block=1),
            _layer_spec(wf, layer), _layer_spec(wna, layer), _layer_spec(wout, layer),
            _const_spec(ln_g.shape), _const_spec(ln_b.shape),
        ],
        out_specs=row_spec(d),
        out_shape=jax.ShapeDtypeStruct((rows, d), F32),
        compiler_params=_params(("arbitrary",)),
        name="merge",
    )(fm2, o2, x2, mods_l, win, wf, wna, wout, ln_g, ln_b)


def kernel(x_prompt, x_sample, cache_k, cache_v, c, c_ctx, w_ada, b_ada, ln_g, ln_b, w_ff1_up,
           w_ff1_down, w_in, rpb, w_fourier, w_na_out, w_out, w_ff2_up, w_ff2_down):
    batch, seq, d = x_prompt.shape
    dec_batch, dec_seq, _ = x_sample.shape
    depth = w_ada.shape[0]
    alpha = (2 * depth) ** 0.25
    rows_lat = dec_seq // GRID_W

    cvec = jnp.zeros((MOD_ROWS, d), F32).at[0].set(c_ctx).at[1:1 + dec_batch].set(c)
    mods = _mods(cvec, w_ada, b_ada).reshape(depth, MOD_ROWS, N_MOD, d)

    y_p = x_prompt.reshape(batch * seq, d)
    y_s = x_sample.reshape(dec_batch * dec_seq, d)
    up1, dn1 = w_ff1_up.astype(BF16), w_ff1_down.astype(BF16)
    up2, dn2 = w_ff2_up.astype(BF16), w_ff2_down.astype(BF16)
    win = w_in.astype(BF16)
    wf = w_fourier.astype(BF16)
    wna = w_na_out.astype(BF16)
    wout = w_out.astype(BF16)
    ck = cache_k.transpose(1, 0, 3, 2, 4).reshape(depth, dec_batch, -1, NA_WIDTH).astype(BF16)
    cv = cache_v.transpose(1, 0, 3, 2, 4).reshape(depth, dec_batch, -1, NA_WIDTH).astype(BF16)

    bias = _nbr_bias_table(rpb.reshape((-1,) + rpb.shape[2:]), rows_lat)

    rows_p, rows_s = batch * seq, dec_batch * dec_seq

    def segments(src_p, first_p, src_s, first_s):
        return [(src_p, first_p, rows_p, 0, rows_p), (src_s, first_s, rows_s, 1, dec_seq)]

    kv = None
    pending = segments(y_p, 0, y_s, 0)
    for l in range(depth):
        m_l = mods[l]
        g_l, b_l = ln_g[l], ln_b[l]
        y = _ffn(pending, m_l, up1, dn1, l, g_l, b_l, 0, alpha)

        q2, k2, v2, f2 = _in_proj(y, rows_p, rows_s, m_l, dec_seq, 1, win, l)
        to3 = lambda t: t.reshape(dec_batch, dec_seq, t.shape[-1])
        o3 = _nbr_attn(to3(q2), to3(k2), to3(v2), ck[l], cv[l], bias, l)
        fm3 = _fourier(to3(f2))
        y_s = _merge(fm3.reshape(-1, FNET_WIDTH), o3.reshape(-1, NA_WIDTH), y, rows_p, m_l,
                     dec_seq, 1, win, wf, wna, wout, l, g_l, b_l, alpha)

        y, y_s = lax.optimization_barrier((y, y_s))
        y_p3, new_k, new_v = _ctx_mix(y.reshape(-1, seq, d), batch, m_l, win, wf, wna, wout,
                                      g_l, b_l, alpha, l, depth, kv)
        kv = (new_k, new_v)

        mixed = segments(y_p3.reshape(rows_p, d), 0, y_s, 0)
        if l + 1 < depth:
            y = _ffn(mixed, m_l, up2, dn2, l, g_l, b_l, 2, alpha)
            pending = segments(y, 0, y, rows_p)
        else:
            y_p = _ffn(mixed[:1], m_l, up2, dn2, l, g_l, b_l, 2, alpha)
            y_s = _ffn(mixed[1:], m_l, up2, dn2, l, g_l, b_l, 2, alpha)

    return (y_p.reshape(batch, seq, d), y_s.reshape(dec_batch, dec_seq, d), new_k, new_v)
```

```python
import functools
import math

import numpy as np
import jax
import jax.numpy as jnp
from jax import lax
from jax.experimental import pallas as pl
from jax.experimental.pallas import tpu as pltpu

F32 = jnp.float32
BF16 = jnp.bfloat16

HEAD_DIM = 64
NA_HEADS = 8
NA_WIDTH = NA_HEADS * HEAD_DIM
FNET_GROUPS = 4
FNET_GROUP_DIM = 128
FNET_WIDTH = FNET_GROUPS * FNET_GROUP_DIM
GRID_W = 64
WIN_ROWS = 8
WIN_COLS = 16
N_SUB = 3
N_MOD = 3 * N_SUB
ATTN_SCALE = HEAD_DIM ** -0.5
LN_EPS = 1e-5
NEG_INF = -1e30

LANES = 128
MXU_DIM = 256
VMEM_LIMIT_BYTES = 56 * 1024 * 1024

ROW_TILE = 512
FFN_ROW_TILE = 1024
FFN_SUB_ROWS = 512
MERGE_ROW_TILE = 1024
FF_CHUNK = MXU_DIM
Q_ROWS = 4
K_ROWS = Q_ROWS + WIN_ROWS
KEY_BLOCK = GRID_W * math.gcd(Q_ROWS, WIN_ROWS // 2)
Q_BLOCKS_PER_STEP = 4
IN_CHUNK = 2 * MXU_DIM
MOD_ROWS = 8
MODS_K_ROWS = 256
FOURIER_COLS = 2 * FNET_GROUP_DIM
SLAB_BLOCK = 8


def _params(sem):
    return pltpu.CompilerParams(dimension_semantics=sem, vmem_limit_bytes=VMEM_LIMIT_BYTES)


def _const_spec(shape):
    nd = len(shape)
    return pl.BlockSpec(shape, lambda *_: (0,) * nd, pipeline_mode=pl.Buffered(1))


def _layer_spec(w, layer, cols=None, col_block=0):
    _, r, n = w.shape
    cols = n if cols is None else cols
    return pl.BlockSpec((None, r, cols), lambda *_: (layer, 0, col_block),
                        pipeline_mode=pl.Buffered(1))


def _layer_norm(r, g, b):
    mu = jnp.mean(r, axis=-1, keepdims=True)
    d = r - mu
    var = jnp.mean(d * d, axis=-1, keepdims=True)
    return d * lax.rsqrt(var + LN_EPS) * g + b


def _bdot(a, b):
    return jnp.dot(a, b, preferred_element_type=F32)


def _dot_nt(a, b):
    return lax.dot_general(a, b, (((1,), (1,)), ((), ())), preferred_element_type=F32)


def _dft_mats(n):
    k = np.arange(n)
    ang = 2.0 * np.pi * ((k[:, None] * k[None, :]) % n) / n
    return np.cos(ang), np.sin(ang)


def _mods_kernel(c_ref, w_ref, b_ref, o_ref):
    k = pl.program_id(1)
    c = c_ref[...]
    s = (c * jax.nn.sigmoid(c)).astype(BF16)
    part = _bdot(s, w_ref[0].astype(BF16))

    @pl.when(k == 0)
    def _():
        o_ref[0] = part + b_ref[0]

    @pl.when(k > 0)
    def _():
        o_ref[0] += part


def _mods(cvec, w_ada, b_ada):
    depth, d, n = w_ada.shape
    tk = MODS_K_ROWS
    return pl.pallas_call(
        _mods_kernel,
        grid=(depth, d // tk),
        in_specs=[
            pl.BlockSpec((MOD_ROWS, tk), lambda l, k: (0, k)),
            pl.BlockSpec((1, tk, n), lambda l, k: (l, k, 0)),
            pl.BlockSpec((1, 1, n), lambda l, k: (l, 0, 0)),
        ],
        out_specs=pl.BlockSpec((1, MOD_ROWS, n), lambda l, k: (l, 0, 0)),
        out_shape=jax.ShapeDtypeStruct((depth, MOD_ROWS, n), F32),
        compiler_params=_params(("arbitrary", "arbitrary")),
        name="mods",
    )(cvec, w_ada, b_ada.reshape(depth, 1, n))


def _ffn_kernel(*refs, sub, alpha, seg_starts):
    n_seg = len(seg_starts)
    x_refs = refs[:n_seg]
    m_ref, wup_ref, wdn_ref, g_ref, b_ref, o_ref, act_ref = refs[n_seg:]
    step = pl.program_id(0)

    def read_x(rs):
        x = x_refs[-1][rs, :]
        for s in range(n_seg - 2, -1, -1):
            x = jnp.where(step < seg_starts[s + 1], x_refs[s][rs, :], x)
        return x

    shift = m_ref[0, 3 * sub:3 * sub + 1, :]
    scale = m_ref[0, 3 * sub + 1:3 * sub + 2, :]
    gate = m_ref[0, 3 * sub + 2:3 * sub + 3, :]
    ff = wdn_ref.shape[0]
    tm = o_ref.shape[0]
    sr = min(tm, FFN_SUB_ROWS)
    slabs = [slice(r0, r0 + sr) for r0 in range(0, tm, sr)]
    us = [(read_x(rs) * (1.0 + scale) + shift).astype(BF16) for rs in slabs]
    for c0 in range(0, ff, FF_CHUNK):
        for rs, u in zip(slabs, us):
            a = _bdot(u, wup_ref[:, c0:c0 + FF_CHUNK])
            g = _bdot(u, wup_ref[:, ff + c0:ff + c0 + FF_CHUNK])
            act_ref[rs, c0:c0 + FF_CHUNK] = ((g * jax.nn.sigmoid(g)) * a).astype(BF16)
    for rs in slabs:
        y = _bdot(act_ref[rs, :], wdn_ref[...])
        r = alpha * read_x(rs) + (0.5 * gate) * y
        o_ref[rs, :] = _layer_norm(r, g_ref[sub:sub + 1, :], b_ref[sub:sub + 1, :])


def _ffn(segments, mods_l, wup, wdn, layer, ln_g, ln_b, sub, alpha):
    d = segments[0][0].shape[1]
    tm = FFN_ROW_TILE
    for _, first_row, n_rows, _, rows_per_group in segments:
        tm = math.gcd(math.gcd(tm, rows_per_group), math.gcd(n_rows, first_row))
    starts, n_steps = [], 0
    for _, _, n_rows, _, _ in segments:
        starts.append(n_steps)
        n_steps += n_rows // tm

    def x_spec(s):
        _, first_row, n_rows, _, _ = segments[s]
        first, count = first_row // tm, n_rows // tm
        return pl.BlockSpec((tm, d), lambda i: (first + jnp.clip(i - starts[s], 0, count - 1), 0))

    def group(i):
        g = None
        for s in range(len(segments) - 1, -1, -1):
            _, _, _, group0, rows_per_group = segments[s]
            g_s = group0 + jnp.maximum(i - starts[s], 0) // (rows_per_group // tm)
            g = g_s if g is None else jnp.where(i < starts[s + 1], g_s, g)
        return g

    return pl.pallas_call(
        functools.partial(_ffn_kernel, sub=sub, alpha=alpha, seg_starts=tuple(starts)),
        grid=(n_steps,),
        in_specs=[x_spec(s) for s in range(len(segments))] + [
            pl.BlockSpec((1, N_MOD, d), lambda i: (group(i), 0, 0)),
            _layer_spec(wup, layer),
            _layer_spec(wdn, layer),
            _const_spec(ln_g.shape),
            _const_spec(ln_b.shape),
        ],
        out_specs=pl.BlockSpec((tm, d), lambda i: (i, 0)),
        out_shape=jax.ShapeDtypeStruct((n_steps * tm, d), F32),
        scratch_shapes=[pltpu.VMEM((tm, wdn.shape[1]), BF16)],
        compiler_params=_params(("arbitrary",)),
        name=f"ffn{sub}",
    )(*[seg[0] for seg in segments], mods_l, wup, wdn, ln_g, ln_b)


def _row_reduce(arrays, combine, reduce):
    acc = None
    for a in arrays:
        for c0 in range(0, a.shape[1], LANES):
            chunk = a[:, c0:c0 + LANES]
            acc = chunk if acc is None else combine(acc, chunk)
    return reduce(acc, axis=-1, keepdims=True)


def _stacked_pair_attention(problems):
    lane = lax.broadcasted_iota(jnp.int32, (1, LANES), 1)
    scores = []
    for q, k_list, _, bias_list in problems:
        zero = jnp.zeros_like(q)
        q2 = jnp.concatenate([jnp.where(lane < HEAD_DIM, q, zero),
                              jnp.where(lane >= HEAD_DIM, q, zero)], axis=0)
        s = []
        for k, bias in zip(k_list, bias_list):
            sk = _dot_nt(q2, k)
            if bias is not None:
                sk = sk + jnp.concatenate(bias, axis=0)
            s.append(sk)
        scores.append(s)
    maxes = [_row_reduce(s, jnp.maximum, jnp.max) for s in scores]
    probs = [[jnp.exp(sk - m) for sk in s] for s, m in zip(scores, maxes)]
    dens = [_row_reduce(p, jnp.add, jnp.sum) for p in probs]
    outs = []
    for (q, _, v_list, _), p, den in zip(problems, probs, dens):
        o = None
        for pk, v in zip(p, v_list):
            pv = _bdot(pk.astype(BF16), v)
            o = pv if o is None else o + pv
        o = o / den
        m_rows = q.shape[0]
        outs.append(jnp.where(lane < HEAD_DIM, o[:m_rows], o[m_rows:]))
    return outs


def _merge_out(slabs, gate, wf_ref, wna_ref, wout_ref, ln_g, ln_b, alpha):
    branches = [(_bdot(fm.astype(BF16), wf_ref[...]), _bdot(o, wna_ref[...]))
                for fm, o, _, _, _ in slabs]
    merged = [(jax.nn.sigmoid(ga) * a + jax.nn.sigmoid(gb) * b).astype(BF16)
              for (_, _, ga, gb, _), (a, b) in zip(slabs, branches)]
    mixes = [_bdot(mg, wout_ref[...]) for mg in merged]
    return [_layer_norm(alpha * x + gate * mix, ln_g, ln_b)
            for (_, _, _, _, x), mix in zip(slabs, mixes)]


def _ctx_mix_kernel(x_ref, m_ref, win_ref, wf_ref, wna_ref, wout_ref, g_ref, b_ref,
                    cdft_ref, pdft_ref, *rest, alpha, n_alias, kv_slot):
    o_ref, k_ref, v_ref, z_ref, oatt_ref, fm_ref = rest[n_alias:]
    nb, seq, d = x_ref.shape
    rows = nb * seq
    x = x_ref[...].reshape(rows, d)
    shift = m_ref[0, 3:4, :]
    scale = m_ref[0, 4:5, :]
    gate = m_ref[0, 5:6, :]
    u = (x * (1.0 + scale) + shift).astype(BF16)
    for c0 in range(0, win_ref.shape[1], IN_CHUNK):
        z_ref[:, c0:c0 + IN_CHUNK] = _bdot(u, win_ref[:, c0:c0 + IN_CHUNK])

    cdft = cdft_ref[...].astype(BF16)
    pdft = pdft_ref[...].astype(BF16)
    q_off, k_off, v_off, f_off = 0, NA_WIDTH, 2 * NA_WIDTH, 3 * NA_WIDTH
    ga_off = f_off + FNET_WIDTH
    gb_off = ga_off + d
    for b in range(nb):
        r0 = b * seq
        for h in range(NA_HEADS):
            k_ref[b, kv_slot, h] = z_ref[r0:r0 + seq, k_off + h * HEAD_DIM:k_off + (h + 1) * HEAD_DIM]
            v_ref[b, kv_slot, h] = z_ref[r0:r0 + seq, v_off + h * HEAD_DIM:v_off + (h + 1) * HEAD_DIM]
        for slot in range(k_ref.shape[1]):
            if slot != kv_slot:
                k_ref[b, slot] = jnp.zeros(k_ref.shape[2:], F32)
                v_ref[b, slot] = jnp.zeros(v_ref.shape[2:], F32)
        zc = [_bdot(z_ref[r0:r0 + seq, f_off + g * FNET_GROUP_DIM:f_off + (g + 1) * FNET_GROUP_DIM]
                    .astype(BF16), cdft).astype(BF16) for g in range(FNET_GROUPS)]
        stacked = jnp.concatenate(
            [jnp.concatenate([z[:, :FNET_GROUP_DIM] for z in zc], axis=1),
             jnp.concatenate([z[:, FNET_GROUP_DIM:] for z in zc], axis=1)], axis=0)
        fm_ref[r0:r0 + seq, :] = _bdot(pdft, stacked)

    problems = []
    for b in range(nb):
        r0 = b * seq
        for hp in range(NA_WIDTH // LANES):
            c0 = hp * LANES
            q = (z_ref[r0:r0 + seq, q_off + c0:q_off + c0 + LANES] * ATTN_SCALE).astype(BF16)
            k = z_ref[r0:r0 + seq, k_off + c0:k_off + c0 + LANES].astype(BF16)
            v = z_ref[r0:r0 + seq, v_off + c0:v_off + c0 + LANES].astype(BF16)
            problems.append((b * seq, c0, (q, [k], [v], [None])))
    outs = _stacked_pair_attention([p for _, _, p in problems])
    for (r0, c0, _), o in zip(problems, outs):
        oatt_ref[r0:r0 + seq, c0:c0 + LANES] = o.astype(BF16)

    slab = (fm_ref[...], oatt_ref[...], z_ref[:, ga_off:ga_off + d], z_ref[:, gb_off:gb_off + d], x)
    (out,) = _merge_out([slab], gate, wf_ref, wna_ref, wout_ref, g_ref[1:2, :], b_ref[1:2, :],
                        alpha)
    o_ref[...] = out.reshape(nb, seq, d)


def _ctx_mix(x3, batch, mods_l, win, wf, wna, wout, ln_g, ln_b, alpha, layer, depth, kv_prev):
    _, seq, d = x3.shape
    nb = 2 if batch % 2 == 0 else 1
    in_width = win.shape[2]
    cc, cs = _dft_mats(FNET_GROUP_DIM)
    pc, ps = _dft_mats(seq)
    norm = 1.0 / math.sqrt(seq * FNET_GROUP_DIM)
    cdft = jnp.asarray(np.concatenate([cc, -cs], axis=1) * norm, F32)
    pdft = jnp.asarray(np.concatenate([pc, ps], axis=1), F32)
    kv_shape = jax.ShapeDtypeStruct((batch, depth, NA_HEADS, seq, HEAD_DIM), F32)
    if kv_prev is None:
        kv_spec = pl.BlockSpec((nb, depth, NA_HEADS, seq, HEAD_DIM), lambda i: (i, 0, 0, 0, 0))
        kv_slot = layer
    else:
        kv_spec = pl.BlockSpec((nb, 1, NA_HEADS, seq, HEAD_DIM), lambda i: (i, layer, 0, 0, 0))
        kv_slot = 0
    operands = [x3, mods_l, win, wf, wna, wout, ln_g, ln_b, cdft, pdft]
    in_specs = [
        pl.BlockSpec((nb, seq, d), lambda i: (i, 0, 0)),
        pl.BlockSpec((1, N_MOD, d), lambda i: (0, 0, 0)),
        _layer_spec(win, layer),
        _layer_spec(wf, layer),
        _layer_spec(wna, layer),
        _layer_spec(wout, layer),
        _const_spec(ln_g.shape),
        _const_spec(ln_b.shape),
        _const_spec(cdft.shape),
        _const_spec(pdft.shape),
    ]
    aliases = {}
    if kv_prev is not None:
        aliases = {len(operands): 1, len(operands) + 1: 2}
        operands += list(kv_prev)
        in_specs += [pl.BlockSpec(memory_space=pl.ANY)] * 2
    return pl.pallas_call(
        functools.partial(_ctx_mix_kernel, alpha=alpha, n_alias=len(aliases), kv_slot=kv_slot),
        grid=(batch // nb,),
        in_specs=in_specs,
        out_specs=[pl.BlockSpec((nb, seq, d), lambda i: (i, 0, 0)), kv_spec, kv_spec],
        out_shape=[jax.ShapeDtypeStruct((batch, seq, d), F32), kv_shape, kv_shape],
        input_output_aliases=aliases,
        scratch_shapes=[
            pltpu.VMEM((nb * seq, in_width), F32),
            pltpu.VMEM((nb * seq, NA_WIDTH), BF16),
            pltpu.VMEM((nb * seq, FNET_WIDTH), F32),
        ],
        compiler_params=_params(("arbitrary",)),
        name="ctx_mix",
    )(*operands)


def _in_proj_kernel(x_ref, m_ref, win_ref, q_ref, k_ref, v_ref, f_ref):
    x = x_ref[...]
    shift = m_ref[0, 3:4, :]
    scale = m_ref[0, 4:5, :]
    u = (x * (1.0 + scale) + shift).astype(BF16)
    cw = NA_WIDTH
    q_ref[...] = (_bdot(u, win_ref[:, 0:cw]) * ATTN_SCALE).astype(BF16)
    k_ref[...] = _bdot(u, win_ref[:, cw:2 * cw]).astype(BF16)
    v_ref[...] = _bdot(u, win_ref[:, 2 * cw:3 * cw]).astype(BF16)
    f_ref[...] = _bdot(u, win_ref[:, 3 * cw:3 * cw + FNET_WIDTH])


def _in_proj(x2, first_row, rows, mods_l, rows_per_group, group0, win, layer):
    d = x2.shape[1]
    tm = math.gcd(math.gcd(ROW_TILE, rows_per_group), first_row)
    tiles_per_group = rows_per_group // tm
    first_tile = first_row // tm
    cw = NA_WIDTH
    qkvf = 3 * NA_WIDTH + FNET_WIDTH
    row_spec = lambda w: pl.BlockSpec((tm, w), lambda i: (i, 0))
    return pl.pallas_call(
        _in_proj_kernel,
        grid=(rows // tm,),
        in_specs=[
            pl.BlockSpec((tm, d), lambda i: (first_tile + i, 0)),
            pl.BlockSpec((1, N_MOD, d), lambda i: (group0 + i // tiles_per_group, 0, 0)),
            _layer_spec(win, layer, cols=qkvf, col_block=0),
        ],
        out_specs=[row_spec(cw), row_spec(cw), row_spec(cw), row_spec(FNET_WIDTH)],
        out_shape=[
            jax.ShapeDtypeStruct((rows, cw), BF16),
            jax.ShapeDtypeStruct((rows, cw), BF16),
            jax.ShapeDtypeStruct((rows, cw), BF16),
            jax.ShapeDtypeStruct((rows, FNET_WIDTH), F32),
        ],
        compiler_params=_params(("arbitrary",)),
        name="in_proj",
    )(x2, mods_l, win)


def _window_start_row(i, rows):
    return jnp.clip(Q_ROWS * i - WIN_ROWS // 2, 0, rows - K_ROWS)


def _nbr_attn_kernel(q_ref, k_ref, v_ref, ck_ref, cv_ref, *rest, rows):
    bias_refs, o_ref = rest[:-1], rest[-1]
    step = pl.program_id(1)
    n_keys = K_ROWS * GRID_W
    tq = Q_ROWS * GRID_W
    problems, places = [], []
    for j, bias_ref in enumerate(bias_refs):
        i = step * len(bias_refs) + j
        start = pl.multiple_of(_window_start_row(i, rows) * GRID_W, KEY_BLOCK)
        bias = (bias_ref[0, 0], bias_ref[1, 0])
        for b in range(q_ref.shape[0]):
            problems.append(
                (q_ref[b, j * tq:(j + 1) * tq, :],
                 [k_ref[b, pl.ds(start, n_keys), :], ck_ref[b]],
                 [v_ref[b, pl.ds(start, n_keys), :], cv_ref[b]],
                 [bias, None]))
            places.append((b, j))
    for (b, j), o in zip(places, _stacked_pair_attention(problems)):
        o_ref[b, j * tq:(j + 1) * tq, :] = o.astype(BF16)


def _bias_table_kernel(rpb_ref, ok_ref, o_ref, *, n_dr, n_dc, slot0, n_cols, starts):
    h = pl.program_id(0)
    base = h * (n_dr * n_dc)
    row = lax.broadcasted_iota(jnp.int32, (GRID_W, LANES), 0)
    lane = lax.broadcasted_iota(jnp.int32, (GRID_W, LANES), 1)
    rel = (lane & (GRID_W - 1)) - row + (WIN_COLS - 1)
    low = lane < GRID_W

    def column(dr_low, dr_high):
        tile = jnp.zeros((GRID_W, LANES), F32)
        valid = [0 <= dr < n_dr for dr in (dr_low, dr_high)]
        if not any(valid):
            return tile
        for d in range(n_dc):
            lo = rpb_ref[base + dr_low * n_dc + d] if valid[0] else 0.0
            hi = rpb_ref[base + dr_high * n_dc + d] if valid[1] else 0.0
            tile = jnp.where(rel == d, jnp.where(low, lo, hi), tile)
        return tile

    builds = [[column(2 * p - shift - slot0, 2 * p + 1 - shift - slot0) for p in range(n_cols)]
              for shift in (0, 1)]
    cols_per_slab = K_ROWS * GRID_W // LANES
    for t, start in enumerate(starts):
        shift = start % 2
        c0 = (start + shift) // 2
        window = jnp.concatenate(builds[shift][c0:c0 + cols_per_slab], axis=1)
        o_ref[0, t] = jnp.where(ok_ref[t] > 0.0, window, NEG_INF)


def _nbr_bias_table(rpb_all, rows):
    h, n_dr, n_dc = rpb_all.shape
    n_blocks = rows // Q_ROWS
    kr = min(WIN_ROWS, rows)
    variants = np.array([0, min(1, n_blocks - 1), n_blocks - 1])
    r = Q_ROWS * variants[:, None] + np.arange(Q_ROWS)[None, :]
    ks = np.clip(Q_ROWS * variants - WIN_ROWS // 2, 0, rows - K_ROWS)
    key_row = ks[:, None] + np.arange(K_ROWS)[None, :]
    r0 = np.clip(r - kr // 2, 0, rows - kr)
    row_ok = (key_row[:, None, :] >= r0[:, :, None]) & (key_row[:, None, :] < r0[:, :, None] + kr)
    c_idx = np.arange(GRID_W)
    c0 = np.clip(c_idx - WIN_COLS // 2, 0, GRID_W - WIN_COLS)
    col_ok = (c_idx[None, :] >= c0[:, None]) & (c_idx[None, :] < c0[:, None] + WIN_COLS)
    rel = c_idx[None, :] - c_idx[:, None] + (WIN_COLS - 1)
    assert rel[col_ok].min() >= 0 and rel[col_ok].max() < n_dc

    first_dr = (ks[:, None] - r + (WIN_ROWS - 1)).reshape(-1)
    slot0 = int(max(0, -first_dr.min()))
    starts = tuple(int(d) + slot0 for d in first_dr)
    n_cols = (max(starts) + 1 + K_ROWS + 1) // 2
    ok = (row_ok[:, :, None, :, None] & col_ok[None, None, :, None, :]).reshape(
        3 * Q_ROWS, GRID_W, K_ROWS * GRID_W).astype(np.float32)
    n_slabs = 3 * Q_ROWS
    table = pl.pallas_call(
        functools.partial(_bias_table_kernel, n_dr=n_dr, n_dc=n_dc, slot0=slot0, n_cols=n_cols,
                          starts=starts),
        grid=(h,),
        in_specs=[
            pl.BlockSpec(memory_space=pltpu.SMEM),
            _const_spec(ok.shape),
        ],
        out_specs=pl.BlockSpec((1, n_slabs, GRID_W, K_ROWS * GRID_W), lambda i: (i, 0, 0, 0)),
        out_shape=jax.ShapeDtypeStruct((h, n_slabs, GRID_W, K_ROWS * GRID_W), F32),
        compiler_params=_params(("arbitrary",)),
        name="bias_table",
    )(rpb_all.reshape(-1).astype(F32), jnp.asarray(ok))
    return table.reshape(h, 3, Q_ROWS * GRID_W, K_ROWS * GRID_W)


def _nbr_attn(q3, k3, v3, ck3, cv3, bias, layer):
    nb, seq, width = q3.shape
    rows = seq // GRID_W
    n_blocks = rows // Q_ROWS
    n_pairs = width // LANES
    past = ck3.shape[1]
    tq = Q_ROWS * GRID_W
    tk = K_ROWS * GRID_W

    per_step = Q_BLOCKS_PER_STEP if n_blocks % Q_BLOCKS_PER_STEP == 0 else 1

    def variant(i):
        return jnp.where(i == 0, 0, jnp.where(i == n_blocks - 1, 2, 1))

    def bias_spec(j):
        return pl.BlockSpec((2, 1, tq, tk),
                            lambda p, s: (layer * n_pairs + p, variant(s * per_step + j), 0, 0))

    return pl.pallas_call(
        functools.partial(_nbr_attn_kernel, rows=rows),
        grid=(n_pairs, n_blocks // per_step),
        in_specs=[
            pl.BlockSpec((nb, per_step * tq, LANES), lambda p, s: (0, s, p)),
            pl.BlockSpec((nb, seq, LANES), lambda p, s: (0, 0, p)),
            pl.BlockSpec((nb, seq, LANES), lambda p, s: (0, 0, p)),
            pl.BlockSpec((nb, past, LANES), lambda p, s: (0, 0, p)),
            pl.BlockSpec((nb, past, LANES), lambda p, s: (0, 0, p)),
        ] + [bias_spec(j) for j in range(per_step)],
        out_specs=pl.BlockSpec((nb, per_step * tq, LANES), lambda p, s: (0, s, p)),
        out_shape=jax.ShapeDtypeStruct((nb, seq, width), BF16),
        compiler_params=_params(("arbitrary", "arbitrary")),
        name="nbr_attn",
    )(q3, k3, v3, ck3, cv3, *([bias] * per_step))


def _fourier_kernel(x_ref, cdft_ref, d1_ref, d2_ref, twc_ref, tws_ref, o_ref,
                    zr_ref, zi_ref, tr_ref, ti_ref, *, n1, n2):
    gd = cdft_ref.shape[0]
    n_groups = x_ref.shape[3] // gd
    groups = range(n_groups)
    blk = range(SLAB_BLOCK)
    cdft = cdft_ref[...].astype(BF16)
    d1 = d1_ref[...].astype(BF16)
    d2 = d2_ref[...].astype(BF16)
    x = x_ref[0].reshape(n1 * n2, n_groups * gd)
    for g in groups:
        zc = _bdot(x[:, g * gd:(g + 1) * gd].astype(BF16), cdft)
        zr_ref[g] = zc[:, :gd].reshape(n1, n2, gd)
        zi_ref[g] = zc[:, gd:].reshape(n1, n2, gd)

    def slabs(re_ref, im_ref, first):
        cols = pl.ds(pl.multiple_of(first, SLAB_BLOCK), SLAB_BLOCK)
        re = [pltpu.einshape("rsl->srl", re_ref[g, :, cols, :]) for g in groups]
        im = [pltpu.einshape("rsl->srl", im_ref[g, :, cols, :]) for g in groups]
        return jnp.concatenate(
            [jnp.concatenate([re[g][j], im[g][j]], axis=0) for j in blk for g in groups], axis=1)

    def stage1(cb, carry):
        first = cb * SLAB_BLOCK
        t = _bdot(d1, slabs(zr_ref, zi_ref, first).astype(BF16))
        for j in blk:
            cs = twc_ref[first + j]
            sn = tws_ref[first + j]
            for g in groups:
                col = (j * n_groups + g) * gd
                tr = t[:n1, col:col + gd]
                ti = t[n1:, col:col + gd]
                tr_ref[g, first + j] = tr * cs + ti * sn
                ti_ref[g, first + j] = ti * cs - tr * sn
        return carry

    lax.fori_loop(0, n2 // SLAB_BLOCK, stage1, 0)

    def stage2(kb, carry):
        first = kb * SLAB_BLOCK
        y = _bdot(d2, slabs(tr_ref, ti_ref, first).astype(BF16))
        rows = pl.ds(pl.multiple_of(first, SLAB_BLOCK), SLAB_BLOCK)
        for g in groups:
            yg = jnp.stack([y[:, (j * n_groups + g) * gd:(j * n_groups + g + 1) * gd] for j in blk])
            o_ref[0, :, rows, g * gd:(g + 1) * gd] = pltpu.einshape("srl->rsl", yg)
        return carry

    lax.fori_loop(0, n1 // SLAB_BLOCK, stage2, 0)


def _fourier(f3):
    nb, seq, width = f3.shape
    n2 = GRID_W
    n1 = seq // n2
    gd = FNET_GROUP_DIM
    cc, cs = _dft_mats(gd)
    c1, s1 = _dft_mats(n1)
    c2, s2 = _dft_mats(n2)
    norm = 1.0 / math.sqrt(seq * gd)
    cdft = jnp.asarray(np.concatenate([cc, -cs], axis=1) * norm, F32)
    d1 = jnp.asarray(np.block([[c1, s1], [-s1, c1]]), F32)
    d2 = jnp.asarray(np.concatenate([c2, s2], axis=1), F32)
    ang = 2.0 * np.pi * ((np.arange(n2)[:, None] * np.arange(n1)[None, :]) % seq) / seq
    twc = jnp.asarray(np.broadcast_to(np.cos(ang)[:, :, None], (n2, n1, gd)), F32)
    tws = jnp.asarray(np.broadcast_to(np.sin(ang)[:, :, None], (n2, n1, gd)), F32)
    n_groups = FOURIER_COLS // gd
    out = pl.pallas_call(
        functools.partial(_fourier_kernel, n1=n1, n2=n2),
        grid=(nb, width // FOURIER_COLS),
        in_specs=[
            pl.BlockSpec((1, n1, n2, FOURIER_COLS), lambda b, g: (b, 0, 0, g)),
            _const_spec(cdft.shape),
            _const_spec(d1.shape),
            _const_spec(d2.shape),
            _const_spec(twc.shape),
            _const_spec(tws.shape),
        ],
        out_specs=pl.BlockSpec((1, n2, n1, FOURIER_COLS), lambda b, g: (b, 0, 0, g)),
        out_shape=jax.ShapeDtypeStruct((nb, n2, n1, width), F32),
        scratch_shapes=[pltpu.VMEM((n_groups, n1, n2, gd), F32)] * 2
        + [pltpu.VMEM((n_groups, n2, n1, gd), F32)] * 2,
        compiler_params=_params(("arbitrary", "arbitrary")),
        name="fourier",
    )(f3.reshape(nb, n1, n2, width), cdft, d1, d2, twc, tws)
    return out.reshape(nb, seq, width)


def _merge_kernel(fm_ref, o_ref_in, x_ref, m_ref, wgate_ref, wf_ref, wna_ref, wout_ref,
                  g_ref, b_ref, out_ref, *, alpha):
    tm, d = x_ref.shape
    shift = m_ref[0, 3:4, :]
    scale = m_ref[0, 4:5, :]
    gate = m_ref[0, 5:6, :]
    sr = min(tm, ROW_TILE)
    rows = [slice(r0, r0 + sr) for r0 in range(0, tm, sr)]
    xs = [x_ref[rs, :] for rs in rows]
    us = [(x * (1.0 + scale) + shift).astype(BF16) for x in xs]
    slabs = [(fm_ref[rs, :], o_ref_in[rs, :], _bdot(u, wgate_ref[:, :d]), _bdot(u, wgate_ref[:, d:]), x)
             for rs, u, x in zip(rows, us, xs)]
    outs = _merge_out(slabs, gate, wf_ref, wna_ref, wout_ref, g_ref[1:2, :], b_ref[1:2, :], alpha)
    for rs, out in zip(rows, outs):
        out_ref[rs, :] = out


def _merge(fm2, o2, x2, x_first_row, mods_l, rows_per_group, group0, win, wf, wna, wout, layer,
           ln_g, ln_b, alpha):
    rows, d = fm2.shape[0], x2.shape[1]
    tm = math.gcd(math.gcd(MERGE_ROW_TILE, rows_per_group), x_first_row)
    tiles_per_group = rows_per_group // tm
    x_first_tile = x_first_row // tm
    row_spec = lambda w: pl.BlockSpec((tm, w), lambda i: (i, 0))
    assert win.shape[2] == 2 * (2 * d), "gate columns must be the second half of w_in"
    return pl.pallas_call(
        functools.partial(_merge_kernel, alpha=alpha),
        grid=(rows // tm,),
        in_specs=[
            row_spec(fm2.shape[1]), row_spec(o2.shape[1]),
            pl.BlockSpec((tm, d), lambda i: (x_first_tile + i, 0)),
            pl.BlockSpec((1, N_MOD, d), lambda i: (group0 + i // tiles_per_group, 0, 0)),
            _layer_spec(win, layer, cols=2 * d, col_block=1),
            _layer_spec(wf, layer), _layer_spec(wna, layer), _layer_spec(wout, layer),
            _const_spec(ln_g.shape), _const_spec(ln_b.shape),
        ],
        out_specs=row_spec(d),
        out_shape=jax.ShapeDtypeStruct((rows, d), F32),
        compiler_params=_params(("arbitrary",)),
        name="merge",
    )(fm2, o2, x2, mods_l, win, wf, wna, wout, ln_g, ln_b)


def kernel(x_prompt, x_sample, cache_k, cache_v, c, c_ctx, w_ada, b_ada, ln_g, ln_b, w_ff1_up,
           w_ff1_down, w_in, rpb, w_fourier, w_na_out, w_out, w_ff2_up, w_ff2_down):
    batch, seq, d = x_prompt.shape
    dec_batch, dec_seq, _ = x_sample.shape
    depth = w_ada.shape[0]
    alpha = (2 * depth) ** 0.25
    rows_lat = dec_seq // GRID_W

    cvec = jnp.zeros((MOD_ROWS, d), F32).at[0].set(c_ctx).at[1:1 + dec_batch].set(c)
    mods = _mods(cvec, w_ada, b_ada).reshape(depth, MOD_ROWS, N_MOD, d)

    y_p = x_prompt.reshape(batch * seq, d)
    y_s = x_sample.reshape(dec_batch * dec_seq, d)
    up1, dn1 = w_ff1_up.astype(BF16), w_ff1_down.astype(BF16)
    up2, dn2 = w_ff2_up.astype(BF16), w_ff2_down.astype(BF16)
    win = w_in.astype(BF16)
    wf = w_fourier.astype(BF16)
    wna = w_na_out.astype(BF16)
    wout = w_out.astype(BF16)
    ck = cache_k.transpose(1, 0, 3, 2, 4).reshape(depth, dec_batch, -1, NA_WIDTH).astype(BF16)
    cv = cache_v.transpose(1, 0, 3, 2, 4).reshape(depth, dec_batch, -1, NA_WIDTH).astype(BF16)

    bias = _nbr_bias_table(rpb.reshape((-1,) + rpb.shape[2:]), rows_lat)

    rows_p, rows_s = batch * seq, dec_batch * dec_seq

    def segments(src_p, first_p, src_s, first_s):
        return [(src_p, first_p, rows_p, 0, rows_p), (src_s, first_s, rows_s, 1, dec_seq)]

    kv = None
    pending = segments(y_p, 0, y_s, 0)
    for l in range(depth):
        m_l = mods[l]
        g_l, b_l = ln_g[l], ln_b[l]
        y = _ffn(pending, m_l, up1, dn1, l, g_l, b_l, 0, alpha)

        q2, k2, v2, f2 = _in_proj(y, rows_p, rows_s, m_l, dec_seq, 1, win, l)
        to3 = lambda t: t.reshape(dec_batch, dec_seq, t.shape[-1])
        o3 = _nbr_attn(to3(q2), to3(k2), to3(v2), ck[l], cv[l], bias, l)
        fm3 = _fourier(to3(f2))
        y_s = _merge(fm3.reshape(-1, FNET_WIDTH), o3.reshape(-1, NA_WIDTH), y, rows_p, m_l,
                     dec_seq, 1, win, wf, wna, wout, l, g_l, b_l, alpha)

        y, y_s = lax.optimization_barrier((y, y_s))
        y_p3, new_k, new_v = _ctx_mix(y.reshape(-1, seq, d), batch, m_l, win, wf, wna, wout,
                                      g_l, b_l, alpha, l, depth, kv)
        kv = (new_k, new_v)

        mixed = segments(y_p3.reshape(rows_p, d), 0, y_s, 0)
        if l + 1 < depth:
            y = _ffn(mixed, m_l, up2, dn2, l, g_l, b_l, 2, alpha)
            pending = segments(y, 0, y, rows_p)
        else:
            y_p = _ffn(mixed[:1], m_l, up2, dn2, l, g_l, b_l, 2, alpha)
            y_s = _ffn(mixed[1:], m_l, up2, dn2, l, g_l, b_l, 2, alpha)

    return (y_p.reshape(batch, seq, d), y_s.reshape(dec_batch, dec_seq, d), new_k, new_v)
```

```python
import functools
import math

import numpy as np
import jax
import jax.numpy as jnp
from jax import lax
from jax.experimental import pallas as pl
from jax.experimental.pallas import tpu as pltpu

F32 = jnp.float32
BF16 = jnp.bfloat16

HEAD_DIM = 64
NA_HEADS = 8
NA_WIDTH = NA_HEADS * HEAD_DIM
FNET_GROUPS = 4
FNET_GROUP_DIM = 128
FNET_WIDTH = FNET_GROUPS * FNET_GROUP_DIM
GRID_W = 64
WIN_ROWS = 8
WIN_COLS = 16
N_SUB = 3
N_MOD = 3 * N_SUB
ATTN_SCALE = HEAD_DIM ** -0.5
LN_EPS = 1e-5
NEG_INF = -1e30

LANES = 128
MXU_DIM = 256
VMEM_LIMIT_BYTES = 56 * 1024 * 1024

ROW_TILE = 512
FFN_ROW_TILE = 1024
FFN_SUB_ROWS = 512
MERGE_ROW_TILE = 1024
FF_CHUNK = MXU_DIM
Q_ROWS = 4
K_ROWS = Q_ROWS + WIN_ROWS
KEY_BLOCK = GRID_W * math.gcd(Q_ROWS, WIN_ROWS // 2)
Q_BLOCKS_PER_STEP = 4
IN_CHUNK = 2 * MXU_DIM
MOD_ROWS = 8
MODS_K_ROWS = 256
FOURIER_COLS = 2 * FNET_GROUP_DIM
SLAB_BLOCK = 8


def _params(sem):
    return pltpu.CompilerParams(dimension_semantics=sem, vmem_limit_bytes=VMEM_LIMIT_BYTES)


def _const_spec(shape):
    nd = len(shape)
    return pl.BlockSpec(shape, lambda *_: (0,) * nd, pipeline_mode=pl.Buffered(1))


def _layer_spec(w, layer, cols=None, col_block=0):
    _, r, n = w.shape
    cols = n if cols is None else cols
    return pl.BlockSpec((None, r, cols), lambda *_: (layer, 0, col_block),
                        pipeline_mode=pl.Buffered(1))


def _layer_norm(r, g, b):
    mu = jnp.mean(r, axis=-1, keepdims=True)
    d = r - mu
    var = jnp.mean(d * d, axis=-1, keepdims=True)
    return d * lax.rsqrt(var + LN_EPS) * g + b


def _bdot(a, b):
    return jnp.dot(a, b, preferred_element_type=F32)


def _dot_nt(a, b):
    return lax.dot_general(a, b, (((1,), (1,)), ((), ())), preferred_element_type=F32)


def _dft_mats(n):
    k = np.arange(n)
    ang = 2.0 * np.pi * ((k[:, None] * k[None, :]) % n) / n
    return np.cos(ang), np.sin(ang)


def _mods_kernel(c_ref, w_ref, b_ref, o_ref):
    k = pl.program_id(1)
    c = c_ref[...]
    s = (c * jax.nn.sigmoid(c)).astype(BF16)
    part = _bdot(s, w_ref[0].astype(BF16))

    @pl.when(k == 0)
    def _():
        o_ref[0] = part + b_ref[0]

    @pl.when(k > 0)
    def _():
        o_ref[0] += part


def _mods(cvec, w_ada, b_ada):
    depth, d, n = w_ada.shape
    tk = MODS_K_ROWS
    return pl.pallas_call(
        _mods_kernel,
        grid=(depth, d // tk),
        in_specs=[
            pl.BlockSpec((MOD_ROWS, tk), lambda l, k: (0, k)),
            pl.BlockSpec((1, tk, n), lambda l, k: (l, k, 0)),
            pl.BlockSpec((1, 1, n), lambda l, k: (l, 0, 0)),
        ],
        out_specs=pl.BlockSpec((1, MOD_ROWS, n), lambda l, k: (l, 0, 0)),
        out_shape=jax.ShapeDtypeStruct((depth, MOD_ROWS, n), F32),
        compiler_params=_params(("arbitrary", "arbitrary")),
        name="mods",
    )(cvec, w_ada, b_ada.reshape(depth, 1, n))


def _ffn_kernel(*refs, sub, alpha, seg_starts):
    n_seg = len(seg_starts)
    x_refs = refs[:n_seg]
    m_ref, wup_ref, wdn_ref, g_ref, b_ref, o_ref, act_ref = refs[n_seg:]
    step = pl.program_id(0)

    def read_x(rs):
        x = x_refs[-1][rs, :]
        for s in range(n_seg - 2, -1, -1):
            x = jnp.where(step < seg_starts[s + 1], x_refs[s][rs, :], x)
        return x

    shift = m_ref[0, 3 * sub:3 * sub + 1, :]
    scale = m_ref[0, 3 * sub + 1:3 * sub + 2, :]
    gate = m_ref[0, 3 * sub + 2:3 * sub + 3, :]
    ff = wdn_ref.shape[0]
    tm = o_ref.shape[0]
    sr = min(tm, FFN_SUB_ROWS)
    slabs = [slice(r0, r0 + sr) for r0 in range(0, tm, sr)]
    us = [(read_x(rs) * (1.0 + scale) + shift).astype(BF16) for rs in slabs]
    for c0 in range(0, ff, FF_CHUNK):
        for rs, u in zip(slabs, us):
            a = _bdot(u, wup_ref[:, c0:c0 + FF_CHUNK])
            g = _bdot(u, wup_ref[:, ff + c0:ff + c0 + FF_CHUNK])
            act_ref[rs, c0:c0 + FF_CHUNK] = ((g * jax.nn.sigmoid(g)) * a).astype(BF16)
    for rs in slabs:
        y = _bdot(act_ref[rs, :], wdn_ref[...])
        r = alpha * read_x(rs) + (0.5 * gate) * y
        o_ref[rs, :] = _layer_norm(r, g_ref[sub:sub + 1, :], b_ref[sub:sub + 1, :])


def _ffn(segments, mods_l, wup, wdn, layer, ln_g, ln_b, sub, alpha):
    d = segments[0][0].shape[1]
    tm = FFN_ROW_TILE
    for _, first_row, n_rows, _, rows_per_group in segments:
        tm = math.gcd(math.gcd(tm, rows_per_group), math.gcd(n_rows, first_row))
    starts, n_steps = [], 0
    for _, _, n_rows, _, _ in segments:
        starts.append(n_steps)
        n_steps += n_rows // tm

    def x_spec(s):
        _, first_row, n_rows, _, _ = segments[s]
        first, count = first_row // tm, n_rows // tm
        return pl.BlockSpec((tm, d), lambda i: (first + jnp.clip(i - starts[s], 0, count - 1), 0))

    def group(i):
        g = None
        for s in range(len(segments) - 1, -1, -1):
            _, _, _, group0, rows_per_group = segments[s]
            g_s = group0 + jnp.maximum(i - starts[s], 0) // (rows_per_group // tm)
            g = g_s if g is None else jnp.where(i < starts[s + 1], g_s, g)
        return g

    return pl.pallas_call(
        functools.partial(_ffn_kernel, sub=sub, alpha=alpha, seg_starts=tuple(starts)),
        grid=(n_steps,),
        in_specs=[x_spec(s) for s in range(len(segments))] + [
            pl.BlockSpec((1, N_MOD, d), lambda i: (group(i), 0, 0)),
            _layer_spec(wup, layer),
            _layer_spec(wdn, layer),
            _const_spec(ln_g.shape),
            _const_spec(ln_b.shape),
        ],
        out_specs=pl.BlockSpec((tm, d), lambda i: (i, 0)),
        out_shape=jax.ShapeDtypeStruct((n_steps * tm, d), F32),
        scratch_shapes=[pltpu.VMEM((tm, wdn.shape[1]), BF16)],
        compiler_params=_params(("arbitrary",)),
        name=f"ffn{sub}",
    )(*[seg[0] for seg in segments], mods_l, wup, wdn, ln_g, ln_b)


def _row_reduce(arrays, combine, reduce):
    acc = None
    for a in arrays:
        for c0 in range(0, a.shape[1], LANES):
            chunk = a[:, c0:c0 + LANES]
            acc = chunk if acc is None else combine(acc, chunk)
    return reduce(acc, axis=-1, keepdims=True)


def _stacked_pair_attention(problems):
    lane = lax.broadcasted_iota(jnp.int32, (1, LANES), 1)
    scores = []
    for q, k_list, _, bias_list in problems:
        zero = jnp.zeros_like(q)
        q2 = jnp.concatenate([jnp.where(lane < HEAD_DIM, q, zero),
                              jnp.where(lane >= HEAD_DIM, q, zero)], axis=0)
        s = []
        for k, bias in zip(k_list, bias_list):
            sk = _dot_nt(q2, k)
            if bias is not None:
                sk = sk + jnp.concatenate(bias, axis=0)
            s.append(sk)
        scores.append(s)
    maxes = [_row_reduce(s, jnp.maximum, jnp.max) for s in scores]
    probs = [[jnp.exp(sk - m) for sk in s] for s, m in zip(scores, maxes)]
    dens = [_row_reduce(p, jnp.add, jnp.sum) for p in probs]
    outs = []
    for (q, _, v_list, _), p, den in zip(problems, probs, dens):
        o = None
        for pk, v in zip(p, v_list):
            pv = _bdot(pk.astype(BF16), v)
            o = pv if o is None else o + pv
        o = o / den
        m_rows = q.shape[0]
        outs.append(jnp.where(lane < HEAD_DIM, o[:m_rows], o[m_rows:]))
    return outs


def _merge_out(slabs, gate, wf_ref, wna_ref, wout_ref, ln_g, ln_b, alpha):
    branches = [(_bdot(fm.astype(BF16), wf_ref[...]), _bdot(o, wna_ref[...]))
                for fm, o, _, _, _ in slabs]
    merged = [(jax.nn.sigmoid(ga) * a + jax.nn.sigmoid(gb) * b).astype(BF16)
              for (_, _, ga, gb, _), (a, b) in zip(slabs, branches)]
    mixes = [_bdot(mg, wout_ref[...]) for mg in merged]
    return [_layer_norm(alpha * x + gate * mix, ln_g, ln_b)
            for (_, _, _, _, x), mix in zip(slabs, mixes)]


def _ctx_mix_kernel(x_ref, m_ref, win_ref, wf_ref, wna_ref, wout_ref, g_ref, b_ref,
                    cdft_ref, pdft_ref, *rest, alpha, n_alias, kv_slot):
    o_ref, k_ref, v_ref, z_ref, oatt_ref, fm_ref = rest[n_alias:]
    nb, seq, d = x_ref.shape
    rows = nb * seq
    x = x_ref[...].reshape(rows, d)
    shift = m_ref[0, 3:4, :]
    scale = m_ref[0, 4:5, :]
    gate = m_ref[0, 5:6, :]
    u = (x * (1.0 + scale) + shift).astype(BF16)
    for c0 in range(0, win_ref.shape[1], IN_CHUNK):
        z_ref[:, c0:c0 + IN_CHUNK] = _bdot(u, win_ref[:, c0:c0 + IN_CHUNK])

    cdft = cdft_ref[...].astype(BF16)
    pdft = pdft_ref[...].astype(BF16)
    q_off, k_off, v_off, f_off = 0, NA_WIDTH, 2 * NA_WIDTH, 3 * NA_WIDTH
    ga_off = f_off + FNET_WIDTH
    gb_off = ga_off + d
    for b in range(nb):
        r0 = b * seq
        for h in range(NA_HEADS):
            k_ref[b, kv_slot, h] = z_ref[r0:r0 + seq, k_off + h * HEAD_DIM:k_off + (h + 1) * HEAD_DIM]
            v_ref[b, kv_slot, h] = z_ref[r0:r0 + seq, v_off + h * HEAD_DIM:v_off + (h + 1) * HEAD_DIM]
        for slot in range(k_ref.shape[1]):
            if slot != kv_slot:
                k_ref[b, slot] = jnp.zeros(k_ref.shape[2:], F32)
                v_ref[b, slot] = jnp.zeros(v_ref.shape[2:], F32)
        zc = [_bdot(z_ref[r0:r0 + seq, f_off + g * FNET_GROUP_DIM:f_off + (g + 1) * FNET_GROUP_DIM]
                    .astype(BF16), cdft).astype(BF16) for g in range(FNET_GROUPS)]
        stacked = jnp.concatenate(
            [jnp.concatenate([z[:, :FNET_GROUP_DIM] for z in zc], axis=1),
             jnp.concatenate([z[:, FNET_GROUP_DIM:] for z in zc], axis=1)], axis=0)
        fm_ref[r0:r0 + seq, :] = _bdot(pdft, stacked)

    problems = []
    for b in range(nb):
        r0 = b * seq
        for hp in range(NA_WIDTH // LANES):
            c0 = hp * LANES
            q = (z_ref[r0:r0 + seq, q_off + c0:q_off + c0 + LANES] * ATTN_SCALE).astype(BF16)
            k = z_ref[r0:r0 + seq, k_off + c0:k_off + c0 + LANES].astype(BF16)
            v = z_ref[r0:r0 + seq, v_off + c0:v_off + c0 + LANES].astype(BF16)
            problems.append((b * seq, c0, (q, [k], [v], [None])))
    outs = _stacked_pair_attention([p for _, _, p in problems])
    for (r0, c0, _), o in zip(problems, outs):
        oatt_ref[r0:r0 + seq, c0:c0 + LANES] = o.astype(BF16)

    slab = (fm_ref[...], oatt_ref[...], z_ref[:, ga_off:ga_off + d], z_ref[:, gb_off:gb_off + d], x)
    (out,) = _merge_out([slab], gate, wf_ref, wna_ref, wout_ref, g_ref[1:2, :], b_ref[1:2, :],
                        alpha)
    o_ref[...] = out.reshape(nb, seq, d)


def _ctx_mix(x3, batch, mods_l, win, wf, wna, wout, ln_g, ln_b, alpha, layer, depth, kv_prev):
    _, seq, d = x3.shape
    nb = 2 if batch % 2 == 0 else 1
    in_width = win.shape[2]
    cc, cs = _dft_mats(FNET_GROUP_DIM)
    pc, ps = _dft_mats(seq)
    norm = 1.0 / math.sqrt(seq * FNET_GROUP_DIM)
    cdft = jnp.asarray(np.concatenate([cc, -cs], axis=1) * norm, F32)
    pdft = jnp.asarray(np.concatenate([pc, ps], axis=1), F32)
    kv_shape = jax.ShapeDtypeStruct((batch, depth, NA_HEADS, seq, HEAD_DIM), F32)
    if kv_prev is None:
        kv_spec = pl.BlockSpec((nb, depth, NA_HEADS, seq, HEAD_DIM), lambda i: (i, 0, 0, 0, 0))
        kv_slot = layer
    else:
        kv_spec = pl.BlockSpec((nb, 1, NA_HEADS, seq, HEAD_DIM), lambda i: (i, layer, 0, 0, 0))
        kv_slot = 0
    operands = [x3, mods_l, win, wf, wna, wout, ln_g, ln_b, cdft, pdft]
    in_specs = [
        pl.BlockSpec((nb, seq, d), lambda i: (i, 0, 0)),
        pl.BlockSpec((1, N_MOD, d), lambda i: (0, 0, 0)),
        _layer_spec(win, layer),
        _layer_spec(wf, layer),
        _layer_spec(wna, layer),
        _layer_spec(wout, layer),
        _const_spec(ln_g.shape),
        _const_spec(ln_b.shape),
        _const_spec(cdft.shape),
        _const_spec(pdft.shape),
    ]
    aliases = {}
    if kv_prev is not None:
        aliases = {len(operands): 1, len(operands) + 1: 2}
        operands += list(kv_prev)
        in_specs += [pl.BlockSpec(memory_space=pl.ANY)] * 2
    return pl.pallas_call(
        functools.partial(_ctx_mix_kernel, alpha=alpha, n_alias=len(aliases), kv_slot=kv_slot),
        grid=(batch // nb,),
        in_specs=in_specs,
        out_specs=[pl.BlockSpec((nb, seq, d), lambda i: (i, 0, 0)), kv_spec, kv_spec],
        out_shape=[jax.ShapeDtypeStruct((batch, seq, d), F32), kv_shape, kv_shape],
        input_output_aliases=aliases,
        scratch_shapes=[
            pltpu.VMEM((nb * seq, in_width), F32),
            pltpu.VMEM((nb * seq, NA_WIDTH), BF16),
            pltpu.VMEM((nb * seq, FNET_WIDTH), F32),
        ],
        compiler_params=_params(("arbitrary",)),
        name="ctx_mix",
    )(*operands)


def _in_proj_kernel(x_ref, m_ref, win_ref, q_ref, k_ref, v_ref, f_ref):
    x = x_ref[...]
    shift = m_ref[0, 3:4, :]
    scale = m_ref[0, 4:5, :]
    u = (x * (1.0 + scale) + shift).astype(BF16)
    cw = NA_WIDTH
    q_ref[...] = (_bdot(u, win_ref[:, 0:cw]) * ATTN_SCALE).astype(BF16)
    k_ref[...] = _bdot(u, win_ref[:, cw:2 * cw]).astype(BF16)
    v_ref[...] = _bdot(u, win_ref[:, 2 * cw:3 * cw]).astype(BF16)
    f_ref[...] = _bdot(u, win_ref[:, 3 * cw:3 * cw + FNET_WIDTH])


def _in_proj(x2, first_row, rows, mods_l, rows_per_group, group0, win, layer):
    d = x2.shape[1]
    tm = math.gcd(math.gcd(ROW_TILE, rows_per_group), first_row)
    tiles_per_group = rows_per_group // tm
    first_tile = first_row // tm
    cw = NA_WIDTH
    qkvf = 3 * NA_WIDTH + FNET_WIDTH
    row_spec = lambda w: pl.BlockSpec((tm, w), lambda i: (i, 0))
    return pl.pallas_call(
        _in_proj_kernel,
        grid=(rows // tm,),
        in_specs=[
            pl.BlockSpec((tm, d), lambda i: (first_tile + i, 0)),
            pl.BlockSpec((1, N_MOD, d), lambda i: (group0 + i // tiles_per_group, 0, 0)),
            _layer_spec(win, layer, cols=qkvf, col_block=0),
        ],
        out_specs=[row_spec(cw), row_spec(cw), row_spec(cw), row_spec(FNET_WIDTH)],
        out_shape=[
            jax.ShapeDtypeStruct((rows, cw), BF16),
            jax.ShapeDtypeStruct((rows, cw), BF16),
            jax.ShapeDtypeStruct((rows, cw), BF16),
            jax.ShapeDtypeStruct((rows, FNET_WIDTH), F32),
        ],
        compiler_params=_params(("arbitrary",)),
        name="in_proj",
    )(x2, mods_l, win)


def _window_start_row(i, rows):
    return jnp.clip(Q_ROWS * i - WIN_ROWS // 2, 0, rows - K_ROWS)


def _nbr_attn_kernel(q_ref, k_ref, v_ref, ck_ref, cv_ref, *rest, rows):
    bias_refs, o_ref = rest[:-1], rest[-1]
    step = pl.program_id(1)
    n_keys = K_ROWS * GRID_W
    tq = Q_ROWS * GRID_W
    problems, places = [], []
    for j, bias_ref in enumerate(bias_refs):
        i = step * len(bias_refs) + j
        start = pl.multiple_of(_window_start_row(i, rows) * GRID_W, KEY_BLOCK)
        bias = (bias_ref[0, 0], bias_ref[1, 0])
        for b in range(q_ref.shape[0]):
            problems.append(
                (q_ref[b, j * tq:(j + 1) * tq, :],
                 [k_ref[b, pl.ds(start, n_keys), :], ck_ref[b]],
                 [v_ref[b, pl.ds(start, n_keys), :], cv_ref[b]],
                 [bias, None]))
            places.append((b, j))
    for (b, j), o in zip(places, _stacked_pair_attention(problems)):
        o_ref[b, j * tq:(j + 1) * tq, :] = o.astype(BF16)


def _bias_table_kernel(rpb_ref, ok_ref, o_ref, *, n_dr, n_dc, slot0, n_cols, starts):
    h = pl.program_id(0)
    base = h * (n_dr * n_dc)
    row = lax.broadcasted_iota(jnp.int32, (GRID_W, LANES), 0)
    lane = lax.broadcasted_iota(jnp.int32, (GRID_W, LANES), 1)
    rel = (lane & (GRID_W - 1)) - row + (WIN_COLS - 1)
    low = lane < GRID_W

    def column(dr_low, dr_high):
        tile = jnp.zeros((GRID_W, LANES), F32)
        valid = [0 <= dr < n_dr for dr in (dr_low, dr_high)]
        if not any(valid):
            return tile
        for d in range(n_dc):
            lo = rpb_ref[base + dr_low * n_dc + d] if valid[0] else 0.0
            hi = rpb_ref[base + dr_high * n_dc + d] if valid[1] else 0.0
            tile = jnp.where(rel == d, jnp.where(low, lo, hi), tile)
        return tile

    builds = [[column(2 * p - shift - slot0, 2 * p + 1 - shift - slot0) for p in range(n_cols)]
              for shift in (0, 1)]
    cols_per_slab = K_ROWS * GRID_W // LANES
    for t, start in enumerate(starts):
        shift = start % 2
        c0 = (start + shift) // 2
        window = jnp.concatenate(builds[shift][c0:c0 + cols_per_slab], axis=1)
        o_ref[0, t] = jnp.where(ok_ref[t] > 0.0, window, NEG_INF)


def _nbr_bias_table(rpb_all, rows):
    h, n_dr, n_dc = rpb_all.shape
    n_blocks = rows // Q_ROWS
    kr = min(WIN_ROWS, rows)
    variants = np.array([0, min(1, n_blocks - 1), n_blocks - 1])
    r = Q_ROWS * variants[:, None] + np.arange(Q_ROWS)[None, :]
    ks = np.clip(Q_ROWS * variants - WIN_ROWS // 2, 0, rows - K_ROWS)
    key_row = ks[:, None] + np.arange(K_ROWS)[None, :]
    r0 = np.clip(r - kr // 2, 0, rows - kr)
    row_ok = (key_row[:, None, :] >= r0[:, :, None]) & (key_row[:, None, :] < r0[:, :, None] + kr)
    c_idx = np.arange(GRID_W)
    c0 = np.clip(c_idx - WIN_COLS // 2, 0, GRID_W - WIN_COLS)
    col_ok = (c_idx[None, :] >= c0[:, None]) & (c_idx[None, :] < c0[:, None] + WIN_COLS)
    rel = c_idx[None, :] - c_idx[:, None] + (WIN_COLS - 1)
    assert rel[col_ok].min() >= 0 and rel[col_ok].max() < n_dc

    first_dr = (ks[:, None] - r + (WIN_ROWS - 1)).reshape(-1)
    slot0 = int(max(0, -first_dr.min()))
    starts = tuple(int(d) + slot0 for d in first_dr)
    n_cols = (max(starts) + 1 + K_ROWS + 1) // 2
    ok = (row_ok[:, :, None, :, None] & col_ok[None, None, :, None, :]).reshape(
        3 * Q_ROWS, GRID_W, K_ROWS * GRID_W).astype(np.float32)
    n_slabs = 3 * Q_ROWS
    table = pl.pallas_call(
        functools.partial(_bias_table_kernel, n_dr=n_dr, n_dc=n_dc, slot0=slot0, n_cols=n_cols,
                          starts=starts),
        grid=(h,),
        in_specs=[
            pl.BlockSpec(memory_space=pltpu.SMEM),
            _const_spec(ok.shape),
        ],
        out_specs=pl.BlockSpec((1, n_slabs, GRID_W, K_ROWS * GRID_W), lambda i: (i, 0, 0, 0)),
        out_shape=jax.ShapeDtypeStruct((h, n_slabs, GRID_W, K_ROWS * GRID_W), F32),
        compiler_params=_params(("arbitrary",)),
        name="bias_table",
    )(rpb_all.reshape(-1).astype(F32), jnp.asarray(ok))
    return table.reshape(h, 3, Q_ROWS * GRID_W, K_ROWS * GRID_W)


def _nbr_attn(q3, k3, v3, ck3, cv3, bias, layer):
    nb, seq, width = q3.shape
    rows = seq // GRID_W
    n_blocks = rows // Q_ROWS
    n_pairs = width // LANES
    past = ck3.shape[1]
    tq = Q_ROWS * GRID_W
    tk = K_ROWS * GRID_W

    per_step = Q_BLOCKS_PER_STEP if n_blocks % Q_BLOCKS_PER_STEP == 0 else 1

    def variant(i):
        return jnp.where(i == 0, 0, jnp.where(i == n_blocks - 1, 2, 1))

    def bias_spec(j):
        return pl.BlockSpec((2, 1, tq, tk),
                            lambda p, s: (layer * n_pairs + p, variant(s * per_step + j), 0, 0))

    return pl.pallas_call(
        functools.partial(_nbr_attn_kernel, rows=rows),
        grid=(n_pairs, n_blocks // per_step),
        in_specs=[
            pl.BlockSpec((nb, per_step * tq, LANES), lambda p, s: (0, s, p)),
            pl.BlockSpec((nb, seq, LANES), lambda p, s: (0, 0, p)),
            pl.BlockSpec((nb, seq, LANES), lambda p, s: (0, 0, p)),
            pl.BlockSpec((nb, past, LANES), lambda p, s: (0, 0, p)),
            pl.BlockSpec((nb, past, LANES), lambda p, s: (0, 0, p)),
        ] + [bias_spec(j) for j in range(per_step)],
        out_specs=pl.BlockSpec((nb, per_step * tq, LANES), lambda p, s: (0, s, p)),
        out_shape=jax.ShapeDtypeStruct((nb, seq, width), BF16),
        compiler_params=_params(("arbitrary", "arbitrary")),
        name="nbr_attn",
    )(q3, k3, v3, ck3, cv3, *([bias] * per_step))


def _fourier_kernel(x_ref, cdft_ref, d1_ref, d2_ref, twc_ref, tws_ref, o_ref,
                    zr_ref, zi_ref, tr_ref, ti_ref, *, n1, n2):
    gd = cdft_ref.shape[0]
    n_groups = x_ref.shape[3] // gd
    groups = range(n_groups)
    blk = range(SLAB_BLOCK)
    cdft = cdft_ref[...].astype(BF16)
    d1 = d1_ref[...].astype(BF16)
    d2 = d2_ref[...].astype(BF16)
    x = x_ref[0].reshape(n1 * n2, n_groups * gd)
    for g in groups:
        zc = _bdot(x[:, g * gd:(g + 1) * gd].astype(BF16), cdft)
        zr_ref[g] = zc[:, :gd].reshape(n1, n2, gd)
        zi_ref[g] = zc[:, gd:].reshape(n1, n2, gd)

    def slabs(re_ref, im_ref, first):
        cols = pl.ds(pl.multiple_of(first, SLAB_BLOCK), SLAB_BLOCK)
        re = [jnp.swapaxes(re_ref[g, :, cols, :], 0, 1) for g in groups]
        im = [jnp.swapaxes(im_ref[g, :, cols, :], 0, 1) for g in groups]
        return jnp.concatenate(
            [jnp.concatenate([re[g][j], im[g][j]], axis=0) for j in blk for g in groups], axis=1)

    def stage1(cb, carry):
        first = cb * SLAB_BLOCK
        t = _bdot(d1, slabs(zr_ref, zi_ref, first).astype(BF16))
        for j in blk:
            cs = twc_ref[first + j]
            sn = tws_ref[first + j]
            for g in groups:
                col = (j * n_groups + g) * gd
                tr = t[:n1, col:col + gd]
                ti = t[n1:, col:col + gd]
                tr_ref[g, first + j] = tr * cs + ti * sn
                ti_ref[g, first + j] = ti * cs - tr * sn
        return carry

    lax.fori_loop(0, n2 // SLAB_BLOCK, stage1, 0)

    def stage2(kb, carry):
        first = kb * SLAB_BLOCK
        y = _bdot(d2, slabs(tr_ref, ti_ref, first).astype(BF16))
        rows = pl.ds(pl.multiple_of(first, SLAB_BLOCK), SLAB_BLOCK)
        for g in groups:
            yg = jnp.stack([y[:, (j * n_groups + g) * gd:(j * n_groups + g + 1) * gd] for j in blk])
            o_ref[0, :, rows, g * gd:(g + 1) * gd] = jnp.swapaxes(yg, 0, 1)
        return carry

    lax.fori_loop(0, n1 // SLAB_BLOCK, stage2, 0)


def _fourier(f3):
    nb, seq, width = f3.shape
    n2 = GRID_W
    n1 = seq // n2
    gd = FNET_GROUP_DIM
    cc, cs = _dft_mats(gd)
    c1, s1 = _dft_mats(n1)
    c2, s2 = _dft_mats(n2)
    norm = 1.0 / math.sqrt(seq * gd)
    cdft = jnp.asarray(np.concatenate([cc, -cs], axis=1) * norm, F32)
    d1 = jnp.asarray(np.block([[c1, s1], [-s1, c1]]), F32)
    d2 = jnp.asarray(np.concatenate([c2, s2], axis=1), F32)
    ang = 2.0 * np.pi * ((np.arange(n2)[:, None] * np.arange(n1)[None, :]) % seq) / seq
    twc = jnp.asarray(np.broadcast_to(np.cos(ang)[:, :, None], (n2, n1, gd)), F32)
    tws = jnp.asarray(np.broadcast_to(np.sin(ang)[:, :, None], (n2, n1, gd)), F32)
    n_groups = FOURIER_COLS // gd
    out = pl.pallas_call(
        functools.partial(_fourier_kernel, n1=n1, n2=n2),
        grid=(nb, width // FOURIER_COLS),
        in_specs=[
            pl.BlockSpec((1, n1, n2, FOURIER_COLS), lambda b, g: (b, 0, 0, g)),
            _const_spec(cdft.shape),
            _const_spec(d1.shape),
            _const_spec(d2.shape),
            _const_spec(twc.shape),
            _const_spec(tws.shape),
        ],
        out_specs=pl.BlockSpec((1, n2, n1, FOURIER_COLS), lambda b, g: (b, 0, 0, g)),
        out_shape=jax.ShapeDtypeStruct((nb, n2, n1, width), F32),
        scratch_shapes=[pltpu.VMEM((n_groups, n1, n2, gd), F32)] * 2
        + [pltpu.VMEM((n_groups, n2, n1, gd), F32)] * 2,
        compiler_params=_params(("arbitrary", "arbitrary")),
        name="fourier",
    )(f3.reshape(nb, n1, n2, width), cdft, d1, d2, twc, tws)
    return out.reshape(nb, seq, width)


def _merge_kernel(fm_ref, o_ref_in, x_ref, m_ref, wgate_ref, wf_ref, wna_ref, wout_ref,
                  g_ref, b_ref, out_ref, *, alpha):
    tm, d = x_ref.shape
    shift = m_ref[0, 3:4, :]
    scale = m_ref[0, 4:5, :]
    gate = m_ref[0, 5:6, :]
    sr = min(tm, ROW_TILE)
    rows = [slice(r0, r0 + sr) for r0 in range(0, tm, sr)]
    xs = [x_ref[rs, :] for rs in rows]
    us = [(x * (1.0 + scale) + shift).astype(BF16) for x in xs]
    slabs = [(fm_ref[rs, :], o_ref_in[rs, :], _bdot(u, wgate_ref[:, :d]), _bdot(u, wgate_ref[:, d:]), x)
             for rs, u, x in zip(rows, us, xs)]
    outs = _merge_out(slabs, gate, wf_ref, wna_ref, wout_ref, g_ref[1:2, :], b_ref[1:2, :], alpha)
    for rs, out in zip(rows, outs):
        out_ref[rs, :] = out


def _merge(fm2, o2, x2, x_first_row, mods_l, rows_per_group, group0, win, wf, wna, wout, layer,
           ln_g, ln_b, alpha):
    rows, d = fm2.shape[0], x2.shape[1]
    tm = math.gcd(math.gcd(MERGE_ROW_TILE, rows_per_group), x_first_row)
    tiles_per_group = rows_per_group // tm
    x_first_tile = x_first_row // tm
    row_spec = lambda w: pl.BlockSpec((tm, w), lambda i: (i, 0))
    assert win.shape[2] == 2 * (2 * d), "gate columns must be the second half of w_in"
    return pl.pallas_call(
        functools.partial(_merge_kernel, alpha=alpha),
        grid=(rows // tm,),
        in_specs=[
            row_spec(fm2.shape[1]), row_spec(o2.shape[1]),
            pl.BlockSpec((tm, d), lambda i: (x_first_tile + i, 0)),
            pl.BlockSpec((1, N_MOD, d), lambda i: (group0 + i // tiles_per_group, 0, 0)),
            _layer_spec(win, layer, cols=2 * d, col_block=1),
            _layer_spec(wf, layer), _layer_spec(wna, layer), _layer_spec(wout, layer),
            _const_spec(ln_g.shape), _const_spec(ln_b.shape),
        ],
        out_specs=row_spec(d),
        out_shape=jax.ShapeDtypeStruct((rows, d), F32),
        compiler_params=_params(("arbitrary",)),
        name="merge",
    )(fm2, o2, x2, mods_l, win, wf, wna, wout, ln_g, ln_b)


def kernel(x_prompt, x_sample, cache_k, cache_v, c, c_ctx, w_ada, b_ada, ln_g, ln_b, w_ff1_up,
           w_ff1_down, w_in, rpb, w_fourier, w_na_out, w_out, w_ff2_up, w_ff2_down):
    batch, seq, d = x_prompt.shape
    dec_batch, dec_seq, _ = x_sample.shape
    depth = w_ada.shape[0]
    alpha = (2 * depth) ** 0.25
    rows_lat = dec_seq // GRID_W

    cvec = jnp.zeros((MOD_ROWS, d), F32).at[0].set(c_ctx).at[1:1 + dec_batch].set(c)
    mods = _mods(cvec, w_ada, b_ada).reshape(depth, MOD_ROWS, N_MOD, d)

    y_p = x_prompt.reshape(batch * seq, d)
    y_s = x_sample.reshape(dec_batch * dec_seq, d)
    up1, dn1 = w_ff1_up.astype(BF16), w_ff1_down.astype(BF16)
    up2, dn2 = w_ff2_up.astype(BF16), w_ff2_down.astype(BF16)
    win = w_in.astype(BF16)
    wf = w_fourier.astype(BF16)
    wna = w_na_out.astype(BF16)
    wout = w_out.astype(BF16)
    ck = cache_k.transpose(1, 0, 3, 2, 4).reshape(depth, dec_batch, -1, NA_WIDTH).astype(BF16)
    cv = cache_v.transpose(1, 0, 3, 2, 4).reshape(depth, dec_batch, -1, NA_WIDTH).astype(BF16)

    bias = _nbr_bias_table(rpb.reshape((-1,) + rpb.shape[2:]), rows_lat)

    rows_p, rows_s = batch * seq, dec_batch * dec_seq

    def segments(src_p, first_p, src_s, first_s):
        return [(src_p, first_p, rows_p, 0, rows_p), (src_s, first_s, rows_s, 1, dec_seq)]

    kv = None
    pending = segments(y_p, 0, y_s, 0)
    for l in range(depth):
        m_l = mods[l]
        g_l, b_l = ln_g[l], ln_b[l]
        y = _ffn(pending, m_l, up1, dn1, l, g_l, b_l, 0, alpha)

        q2, k2, v2, f2 = _in_proj(y, rows_p, rows_s, m_l, dec_seq, 1, win, l)
        to3 = lambda t: t.reshape(dec_batch, dec_seq, t.shape[-1])
        o3 = _nbr_attn(to3(q2), to3(k2), to3(v2), ck[l], cv[l], bias, l)
        fm3 = _fourier(to3(f2))
        y_s = _merge(fm3.reshape(-1, FNET_WIDTH), o3.reshape(-1, NA_WIDTH), y, rows_p, m_l,
                     dec_seq, 1, win, wf, wna, wout, l, g_l, b_l, alpha)

        y, y_s = lax.optimization_barrier((y, y_s))
        y_p3, new_k, new_v = _ctx_mix(y.reshape(-1, seq, d), batch, m_l, win, wf, wna, wout,
                                      g_l, b_l, alpha, l, depth, kv)
        kv = (new_k, new_v)

        mixed = segments(y_p3.reshape(rows_p, d), 0, y_s, 0)
        if l + 1 < depth:
            y = _ffn(mixed, m_l, up2, dn2, l, g_l, b_l, 2, alpha)
            pending = segments(y, 0, y, rows_p)
        else:
            y_p = _ffn(mixed[:1], m_l, up2, dn2, l, g_l, b_l, 2, alpha)
            y_s = _ffn(mixed[1:], m_l, up2, dn2, l, g_l, b_l, 2, alpha)

    return (y_p.reshape(batch, seq, d), y_s.reshape(dec_batch, dec_seq, d), new_k, new_v)
```

```python
import functools
import math

import numpy as np
import jax
import jax.numpy as jnp
from jax import lax
from jax.experimental import pallas as pl
from jax.experimental.pallas import tpu as pltpu

F32 = jnp.float32
BF16 = jnp.bfloat16

HEAD_DIM = 64
NA_HEADS = 8
NA_WIDTH = NA_HEADS * HEAD_DIM
FNET_GROUPS = 4
FNET_GROUP_DIM = 128
FNET_WIDTH = FNET_GROUPS * FNET_GROUP_DIM
GRID_W = 64
WIN_ROWS = 8
WIN_COLS = 16
N_SUB = 3
N_MOD = 3 * N_SUB
ATTN_SCALE = HEAD_DIM ** -0.5
LN_EPS = 1e-5
NEG_INF = -1e30

LANES = 128
MXU_DIM = 256
V7X_VMEM_BYTES = 64 * 1024 * 1024
VMEM_LIMIT_BYTES = V7X_VMEM_BYTES * 7 // 8

ROW_TILE = 512
FFN_ROW_TILE = 1024
FFN_SUB_ROWS = 512
MERGE_ROW_TILE = 1024
FF_CHUNK = MXU_DIM
Q_ROWS = 4
K_ROWS = Q_ROWS + WIN_ROWS
KEY_BLOCK = GRID_W * math.gcd(Q_ROWS, WIN_ROWS // 2)
Q_BLOCKS_PER_STEP = 4
IN_CHUNK = 2 * MXU_DIM
MOD_ROWS = 8
MODS_K_ROWS = 256
FOURIER_COLS = 2 * FNET_GROUP_DIM
SLAB_BLOCK = 8


def _params(sem):
    return pltpu.CompilerParams(dimension_semantics=sem, vmem_limit_bytes=VMEM_LIMIT_BYTES)


def _const_spec(shape):
    nd = len(shape)
    return pl.BlockSpec(shape, lambda *_: (0,) * nd, pipeline_mode=pl.Buffered(1))


def _layer_spec(w, layer, cols=None, col_block=0):
    _, r, n = w.shape
    cols = n if cols is None else cols
    return pl.BlockSpec((None, r, cols), lambda *_: (layer, 0, col_block),
                        pipeline_mode=pl.Buffered(1))


def _layer_norm(r, g, b):
    mu = jnp.mean(r, axis=-1, keepdims=True)
    d = r - mu
    var = jnp.mean(d * d, axis=-1, keepdims=True)
    return d * lax.rsqrt(var + LN_EPS) * g + b


def _bdot(a, b):
    return jnp.dot(a, b, preferred_element_type=F32)


def _dot_nt(a, b):
    return lax.dot_general(a, b, (((1,), (1,)), ((), ())), preferred_element_type=F32)


def _dft_mats(n):
    k = np.arange(n)
    ang = 2.0 * np.pi * ((k[:, None] * k[None, :]) % n) / n
    return np.cos(ang), np.sin(ang)


def _mods_kernel(c_ref, w_ref, b_ref, o_ref):
    k = pl.program_id(1)
    c = c_ref[...]
    s = (c * jax.nn.sigmoid(c)).astype(BF16)
    part = _bdot(s, w_ref[0].astype(BF16))

    @pl.when(k == 0)
    def _():
        o_ref[0] = part + b_ref[0]

    @pl.when(k > 0)
    def _():
        o_ref[0] += part


def _mods(cvec, w_ada, b_ada):
    depth, d, n = w_ada.shape
    tk = MODS_K_ROWS
    return pl.pallas_call(
        _mods_kernel,
        grid=(depth, d // tk),
        in_specs=[
            pl.BlockSpec((MOD_ROWS, tk), lambda l, k: (0, k)),
            pl.BlockSpec((1, tk, n), lambda l, k: (l, k, 0)),
            pl.BlockSpec((1, 1, n), lambda l, k: (l, 0, 0)),
        ],
        out_specs=pl.BlockSpec((1, MOD_ROWS, n), lambda l, k: (l, 0, 0)),
        out_shape=jax.ShapeDtypeStruct((depth, MOD_ROWS, n), F32),
        compiler_params=_params(("arbitrary", "arbitrary")),
        name="mods",
    )(cvec, w_ada, b_ada.reshape(depth, 1, n))


def _ffn_kernel(*refs, sub, alpha, seg_starts):
    n_seg = len(seg_starts)
    x_refs = refs[:n_seg]
    m_ref, wup_ref, wdn_ref, g_ref, b_ref, o_ref, act_ref = refs[n_seg:]
    step = pl.program_id(0)

    def read_x(rs):
        x = x_refs[-1][rs, :]
        for s in range(n_seg - 2, -1, -1):
            x = jnp.where(step < seg_starts[s + 1], x_refs[s][rs, :], x)
        return x

    shift = m_ref[0, 3 * sub:3 * sub + 1, :]
    scale = m_ref[0, 3 * sub + 1:3 * sub + 2, :]
    gate = m_ref[0, 3 * sub + 2:3 * sub + 3, :]
    ff = wdn_ref.shape[0]
    tm = o_ref.shape[0]
    sr = min(tm, FFN_SUB_ROWS)
    slabs = [slice(r0, r0 + sr) for r0 in range(0, tm, sr)]
    us = [(read_x(rs) * (1.0 + scale) + shift).astype(BF16) for rs in slabs]
    for c0 in range(0, ff, FF_CHUNK):
        for rs, u in zip(slabs, us):
            a = _bdot(u, wup_ref[:, c0:c0 + FF_CHUNK])
            g = _bdot(u, wup_ref[:, ff + c0:ff + c0 + FF_CHUNK])
            act_ref[rs, c0:c0 + FF_CHUNK] = ((g * jax.nn.sigmoid(g)) * a).astype(BF16)
    for rs in slabs:
        y = _bdot(act_ref[rs, :], wdn_ref[...])
        r = alpha * read_x(rs) + (0.5 * gate) * y
        o_ref[rs, :] = _layer_norm(r, g_ref[sub:sub + 1, :], b_ref[sub:sub + 1, :])


def _ffn(segments, mods_l, wup, wdn, layer, ln_g, ln_b, sub, alpha):
    d = segments[0][0].shape[1]
    tm = FFN_ROW_TILE
    for _, first_row, n_rows, _, rows_per_group in segments:
        tm = math.gcd(math.gcd(tm, rows_per_group), math.gcd(n_rows, first_row))
    starts, n_steps = [], 0
    for _, _, n_rows, _, _ in segments:
        starts.append(n_steps)
        n_steps += n_rows // tm

    def x_spec(s):
        _, first_row, n_rows, _, _ = segments[s]
        first, count = first_row // tm, n_rows // tm
        return pl.BlockSpec((tm, d), lambda i: (first + jnp.clip(i - starts[s], 0, count - 1), 0))

    def group(i):
        g = None
        for s in range(len(segments) - 1, -1, -1):
            _, _, _, group0, rows_per_group = segments[s]
            g_s = group0 + jnp.maximum(i - starts[s], 0) // (rows_per_group // tm)
            g = g_s if g is None else jnp.where(i < starts[s + 1], g_s, g)
        return g

    return pl.pallas_call(
        functools.partial(_ffn_kernel, sub=sub, alpha=alpha, seg_starts=tuple(starts)),
        grid=(n_steps,),
        in_specs=[x_spec(s) for s in range(len(segments))] + [
            pl.BlockSpec((1, N_MOD, d), lambda i: (group(i), 0, 0)),
            _layer_spec(wup, layer),
            _layer_spec(wdn, layer),
            _const_spec(ln_g.shape),
            _const_spec(ln_b.shape),
        ],
        out_specs=pl.BlockSpec((tm, d), lambda i: (i, 0)),
        out_shape=jax.ShapeDtypeStruct((n_steps * tm, d), F32),
        scratch_shapes=[pltpu.VMEM((tm, wdn.shape[1]), BF16)],
        compiler_params=_params(("arbitrary",)),
        name=f"ffn{sub}",
    )(*[seg[0] for seg in segments], mods_l, wup, wdn, ln_g, ln_b)


def _row_reduce(arrays, combine, reduce):
    acc = None
    for a in arrays:
        for c0 in range(0, a.shape[1], LANES):
            chunk = a[:, c0:c0 + LANES]
            acc = chunk if acc is None else combine(acc, chunk)
    return reduce(acc, axis=-1, keepdims=True)


def _stacked_pair_attention(problems):
    lane = lax.broadcasted_iota(jnp.int32, (1, LANES), 1)
    scores = []
    for q, k_list, _, bias_list in problems:
        zero = jnp.zeros_like(q)
        q2 = jnp.concatenate([jnp.where(lane < HEAD_DIM, q, zero),
                              jnp.where(lane >= HEAD_DIM, q, zero)], axis=0)
        s = []
        for k, bias in zip(k_list, bias_list):
            sk = _dot_nt(q2, k)
            if bias is not None:
                sk = sk + jnp.concatenate(bias, axis=0)
            s.append(sk)
        scores.append(s)
    maxes = [_row_reduce(s, jnp.maximum, jnp.max) for s in scores]
    probs = [[jnp.exp(sk - m) for sk in s] for s, m in zip(scores, maxes)]
    dens = [_row_reduce(p, jnp.add, jnp.sum) for p in probs]
    outs = []
    for (q, _, v_list, _), p, den in zip(problems, probs, dens):
        o = None
        for pk, v in zip(p, v_list):
            pv = _bdot(pk.astype(BF16), v)
            o = pv if o is None else o + pv
        o = o / den
        m_rows = q.shape[0]
        outs.append(jnp.where(lane < HEAD_DIM, o[:m_rows], o[m_rows:]))
    return outs


def _merge_out(slabs, gate, wf_ref, wna_ref, wout_ref, ln_g, ln_b, alpha):
    branches = [(_bdot(fm.astype(BF16), wf_ref[...]), _bdot(o, wna_ref[...]))
                for fm, o, _, _, _ in slabs]
    merged = [(jax.nn.sigmoid(ga) * a + jax.nn.sigmoid(gb) * b).astype(BF16)
              for (_, _, ga, gb, _), (a, b) in zip(slabs, branches)]
    mixes = [_bdot(mg, wout_ref[...]) for mg in merged]
    return [_layer_norm(alpha * x + gate * mix, ln_g, ln_b)
            for (_, _, _, _, x), mix in zip(slabs, mixes)]


def _ctx_mix_kernel(x_ref, m_ref, win_ref, wf_ref, wna_ref, wout_ref, g_ref, b_ref,
                    cdft_ref, pdft_ref, *rest, alpha, n_alias, kv_slot):
    o_ref, k_ref, v_ref, z_ref, oatt_ref, fm_ref = rest[n_alias:]
    nb, seq, d = x_ref.shape
    rows = nb * seq
    x = x_ref[...].reshape(rows, d)
    shift = m_ref[0, 3:4, :]
    scale = m_ref[0, 4:5, :]
    gate = m_ref[0, 5:6, :]
    u = (x * (1.0 + scale) + shift).astype(BF16)
    for c0 in range(0, win_ref.shape[1], IN_CHUNK):
        z_ref[:, c0:c0 + IN_CHUNK] = _bdot(u, win_ref[:, c0:c0 + IN_CHUNK])

    cdft = cdft_ref[...].astype(BF16)
    pdft = pdft_ref[...].astype(BF16)
    q_off, k_off, v_off, f_off = 0, NA_WIDTH, 2 * NA_WIDTH, 3 * NA_WIDTH
    ga_off = f_off + FNET_WIDTH
    gb_off = ga_off + d
    for b in range(nb):
        r0 = b * seq
        for h in range(NA_HEADS):
            k_ref[b, kv_slot, h] = z_ref[r0:r0 + seq, k_off + h * HEAD_DIM:k_off + (h + 1) * HEAD_DIM]
            v_ref[b, kv_slot, h] = z_ref[r0:r0 + seq, v_off + h * HEAD_DIM:v_off + (h + 1) * HEAD_DIM]
        for slot in range(k_ref.shape[1]):
            if slot != kv_slot:
                k_ref[b, slot] = jnp.zeros(k_ref.shape[2:], F32)
                v_ref[b, slot] = jnp.zeros(v_ref.shape[2:], F32)
        zc = [_bdot(z_ref[r0:r0 + seq, f_off + g * FNET_GROUP_DIM:f_off + (g + 1) * FNET_GROUP_DIM]
                    .astype(BF16), cdft).astype(BF16) for g in range(FNET_GROUPS)]
        stacked = jnp.concatenate(
            [jnp.concatenate([z[:, :FNET_GROUP_DIM] for z in zc], axis=1),
             jnp.concatenate([z[:, FNET_GROUP_DIM:] for z in zc], axis=1)], axis=0)
        fm_ref[r0:r0 + seq, :] = _bdot(pdft, stacked)

    problems = []
    for b in range(nb):
        r0 = b * seq
        for hp in range(NA_WIDTH // LANES):
            c0 = hp * LANES
            q = (z_ref[r0:r0 + seq, q_off + c0:q_off + c0 + LANES] * ATTN_SCALE).astype(BF16)
            k = z_ref[r0:r0 + seq, k_off + c0:k_off + c0 + LANES].astype(BF16)
            v = z_ref[r0:r0 + seq, v_off + c0:v_off + c0 + LANES].astype(BF16)
            problems.append((b * seq, c0, (q, [k], [v], [None])))
    outs = _stacked_pair_attention([p for _, _, p in problems])
    for (r0, c0, _), o in zip(problems, outs):
        oatt_ref[r0:r0 + seq, c0:c0 + LANES] = o.astype(BF16)

    slab = (fm_ref[...], oatt_ref[...], z_ref[:, ga_off:ga_off + d], z_ref[:, gb_off:gb_off + d], x)
    (out,) = _merge_out([slab], gate, wf_ref, wna_ref, wout_ref, g_ref[1:2, :], b_ref[1:2, :],
                        alpha)
    o_ref[...] = out.reshape(nb, seq, d)


def _ctx_mix(x3, batch, mods_l, win, wf, wna, wout, ln_g, ln_b, alpha, layer, depth, kv_prev):
    _, seq, d = x3.shape
    nb = 2 if batch % 2 == 0 else 1
    in_width = win.shape[2]
    cc, cs = _dft_mats(FNET_GROUP_DIM)
    pc, ps = _dft_mats(seq)
    norm = 1.0 / math.sqrt(seq * FNET_GROUP_DIM)
    cdft = jnp.asarray(np.concatenate([cc, -cs], axis=1) * norm, F32)
    pdft = jnp.asarray(np.concatenate([pc, ps], axis=1), F32)
    kv_shape = jax.ShapeDtypeStruct((batch, depth, NA_HEADS, seq, HEAD_DIM), F32)
    if kv_prev is None:
        kv_spec = pl.BlockSpec((nb, depth, NA_HEADS, seq, HEAD_DIM), lambda i: (i, 0, 0, 0, 0))
        kv_slot = layer
    else:
        kv_spec = pl.BlockSpec((nb, 1, NA_HEADS, seq, HEAD_DIM), lambda i: (i, layer, 0, 0, 0))
        kv_slot = 0
    operands = [x3, mods_l, win, wf, wna, wout, ln_g, ln_b, cdft, pdft]
    in_specs = [
        pl.BlockSpec((nb, seq, d), lambda i: (i, 0, 0)),
        pl.BlockSpec((1, N_MOD, d), lambda i: (0, 0, 0)),
        _layer_spec(win, layer),
        _layer_spec(wf, layer),
        _layer_spec(wna, layer),
        _layer_spec(wout, layer),
        _const_spec(ln_g.shape),
        _const_spec(ln_b.shape),
        _const_spec(cdft.shape),
        _const_spec(pdft.shape),
    ]
    aliases = {}
    if kv_prev is not None:
        aliases = {len(operands): 1, len(operands) + 1: 2}
        operands += list(kv_prev)
        in_specs += [pl.BlockSpec(memory_space=pl.ANY)] * 2
    return pl.pallas_call(
        functools.partial(_ctx_mix_kernel, alpha=alpha, n_alias=len(aliases), kv_slot=kv_slot),
        grid=(batch // nb,),
        in_specs=in_specs,
        out_specs=[pl.BlockSpec((nb, seq, d), lambda i: (i, 0, 0)), kv_spec, kv_spec],
        out_shape=[jax.ShapeDtypeStruct((batch, seq, d), F32), kv_shape, kv_shape],
        input_output_aliases=aliases,
        scratch_shapes=[
            pltpu.VMEM((nb * seq, in_width), F32),
            pltpu.VMEM((nb * seq, NA_WIDTH), BF16),
            pltpu.VMEM((nb * seq, FNET_WIDTH), F32),
        ],
        compiler_params=_params(("arbitrary",)),
        name="ctx_mix",
    )(*operands)


def _in_proj_kernel(x_ref, m_ref, win_ref, q_ref, k_ref, v_ref, f_ref):
    x = x_ref[...]
    shift = m_ref[0, 3:4, :]
    scale = m_ref[0, 4:5, :]
    u = (x * (1.0 + scale) + shift).astype(BF16)
    cw = NA_WIDTH
    q_ref[...] = (_bdot(u, win_ref[:, 0:cw]) * ATTN_SCALE).astype(BF16)
    k_ref[...] = _bdot(u, win_ref[:, cw:2 * cw]).astype(BF16)
    v_ref[...] = _bdot(u, win_ref[:, 2 * cw:3 * cw]).astype(BF16)
    f_ref[...] = _bdot(u, win_ref[:, 3 * cw:3 * cw + FNET_WIDTH]).astype(BF16)


def _in_proj(x2, first_row, rows, mods_l, rows_per_group, group0, win, layer):
    d = x2.shape[1]
    tm = math.gcd(math.gcd(ROW_TILE, rows_per_group), first_row)
    tiles_per_group = rows_per_group // tm
    first_tile = first_row // tm
    cw = NA_WIDTH
    qkvf = 3 * NA_WIDTH + FNET_WIDTH
    row_spec = lambda w: pl.BlockSpec((tm, w), lambda i: (i, 0))
    return pl.pallas_call(
        _in_proj_kernel,
        grid=(rows // tm,),
        in_specs=[
            pl.BlockSpec((tm, d), lambda i: (first_tile + i, 0)),
            pl.BlockSpec((1, N_MOD, d), lambda i: (group0 + i // tiles_per_group, 0, 0)),
            _layer_spec(win, layer, cols=qkvf, col_block=0),
        ],
        out_specs=[row_spec(cw), row_spec(cw), row_spec(cw), row_spec(FNET_WIDTH)],
        out_shape=[
            jax.ShapeDtypeStruct((rows, cw), BF16),
            jax.ShapeDtypeStruct((rows, cw), BF16),
            jax.ShapeDtypeStruct((rows, cw), BF16),
            jax.ShapeDtypeStruct((rows, FNET_WIDTH), BF16),
        ],
        compiler_params=_params(("arbitrary",)),
        name="in_proj",
    )(x2, mods_l, win)


def _window_start_row(i, rows):
    return jnp.clip(Q_ROWS * i - WIN_ROWS // 2, 0, rows - K_ROWS)


def _nbr_attn_kernel(q_ref, k_ref, v_ref, ck_ref, cv_ref, *rest, rows):
    bias_refs, o_ref = rest[:-1], rest[-1]
    step = pl.program_id(1)
    n_keys = K_ROWS * GRID_W
    tq = Q_ROWS * GRID_W
    problems, places = [], []
    for j, bias_ref in enumerate(bias_refs):
        i = step * len(bias_refs) + j
        start = pl.multiple_of(_window_start_row(i, rows) * GRID_W, KEY_BLOCK)
        bias = (bias_ref[0, 0], bias_ref[1, 0])
        for b in range(q_ref.shape[0]):
            problems.append(
                (q_ref[b, j * tq:(j + 1) * tq, :],
                 [k_ref[b, pl.ds(start, n_keys), :], ck_ref[b]],
                 [v_ref[b, pl.ds(start, n_keys), :], cv_ref[b]],
                 [bias, None]))
            places.append((b, j))
    for (b, j), o in zip(places, _stacked_pair_attention(problems)):
        o_ref[b, j * tq:(j + 1) * tq, :] = o.astype(BF16)


def _bias_table_kernel(rpb_ref, ok_ref, o_ref, *, n_dr, n_dc, slot0, n_cols, starts):
    h = pl.program_id(0)
    base = h * (n_dr * n_dc)
    row = lax.broadcasted_iota(jnp.int32, (GRID_W, LANES), 0)
    lane = lax.broadcasted_iota(jnp.int32, (GRID_W, LANES), 1)
    rel = (lane & (GRID_W - 1)) - row + (WIN_COLS - 1)
    low = lane < GRID_W

    def column(dr_low, dr_high):
        tile = jnp.zeros((GRID_W, LANES), F32)
        valid = [0 <= dr < n_dr for dr in (dr_low, dr_high)]
        if not any(valid):
            return tile
        for d in range(n_dc):
            lo = rpb_ref[base + dr_low * n_dc + d] if valid[0] else 0.0
            hi = rpb_ref[base + dr_high * n_dc + d] if valid[1] else 0.0
            tile = jnp.where(rel == d, jnp.where(low, lo, hi), tile)
        return tile

    builds = [[column(2 * p - shift - slot0, 2 * p + 1 - shift - slot0) for p in range(n_cols)]
              for shift in (0, 1)]
    cols_per_slab = K_ROWS * GRID_W // LANES
    for t, start in enumerate(starts):
        shift = start % 2
        c0 = (start + shift) // 2
        window = jnp.concatenate(builds[shift][c0:c0 + cols_per_slab], axis=1)
        o_ref[0, t] = jnp.where(ok_ref[t] > 0.0, window, NEG_INF)


def _nbr_bias_table(rpb_all, rows):
    h, n_dr, n_dc = rpb_all.shape
    n_blocks = rows // Q_ROWS
    kr = min(WIN_ROWS, rows)
    variants = np.array([0, min(1, n_blocks - 1), n_blocks - 1])
    r = Q_ROWS * variants[:, None] + np.arange(Q_ROWS)[None, :]
    ks = np.clip(Q_ROWS * variants - WIN_ROWS // 2, 0, rows - K_ROWS)
    key_row = ks[:, None] + np.arange(K_ROWS)[None, :]
    r0 = np.clip(r - kr // 2, 0, rows - kr)
    row_ok = (key_row[:, None, :] >= r0[:, :, None]) & (key_row[:, None, :] < r0[:, :, None] + kr)
    c_idx = np.arange(GRID_W)
    c0 = np.clip(c_idx - WIN_COLS // 2, 0, GRID_W - WIN_COLS)
    col_ok = (c_idx[None, :] >= c0[:, None]) & (c_idx[None, :] < c0[:, None] + WIN_COLS)
    rel = c_idx[None, :] - c_idx[:, None] + (WIN_COLS - 1)
    assert rel[col_ok].min() >= 0 and rel[col_ok].max() < n_dc

    first_dr = (ks[:, None] - r + (WIN_ROWS - 1)).reshape(-1)
    slot0 = int(max(0, -first_dr.min()))
    starts = tuple(int(d) + slot0 for d in first_dr)
    n_cols = (max(starts) + 1 + K_ROWS + 1) // 2
    ok = (row_ok[:, :, None, :, None] & col_ok[None, None, :, None, :]).reshape(
        3 * Q_ROWS, GRID_W, K_ROWS * GRID_W).astype(np.float32)
    n_slabs = 3 * Q_ROWS
    table = pl.pallas_call(
        functools.partial(_bias_table_kernel, n_dr=n_dr, n_dc=n_dc, slot0=slot0, n_cols=n_cols,
                          starts=starts),
        grid=(h,),
        in_specs=[
            pl.BlockSpec(memory_space=pltpu.SMEM),
            _const_spec(ok.shape),
        ],
        out_specs=pl.BlockSpec((1, n_slabs, GRID_W, K_ROWS * GRID_W), lambda i: (i, 0, 0, 0)),
        out_shape=jax.ShapeDtypeStruct((h, n_slabs, GRID_W, K_ROWS * GRID_W), F32),
        compiler_params=_params(("arbitrary",)),
        name="bias_table",
    )(rpb_all.reshape(-1).astype(F32), jnp.asarray(ok))
    return table.reshape(h, 3, Q_ROWS * GRID_W, K_ROWS * GRID_W)


def _nbr_attn(q3, k3, v3, ck3, cv3, bias, layer):
    nb, seq, width = q3.shape
    rows = seq // GRID_W
    n_blocks = rows // Q_ROWS
    n_pairs = width // LANES
    past = ck3.shape[1]
    tq = Q_ROWS * GRID_W
    tk = K_ROWS * GRID_W

    per_step = Q_BLOCKS_PER_STEP if n_blocks % Q_BLOCKS_PER_STEP == 0 else 1

    def variant(i):
        return jnp.where(i == 0, 0, jnp.where(i == n_blocks - 1, 2, 1))

    def bias_spec(j):
        return pl.BlockSpec((2, 1, tq, tk),
                            lambda p, s: (layer * n_pairs + p, variant(s * per_step + j), 0, 0))

    return pl.pallas_call(
        functools.partial(_nbr_attn_kernel, rows=rows),
        grid=(n_pairs, n_blocks // per_step),
        in_specs=[
            pl.BlockSpec((nb, per_step * tq, LANES), lambda p, s: (0, s, p)),
            pl.BlockSpec((nb, seq, LANES), lambda p, s: (0, 0, p)),
            pl.BlockSpec((nb, seq, LANES), lambda p, s: (0, 0, p)),
            pl.BlockSpec((nb, past, LANES), lambda p, s: (0, 0, p)),
            pl.BlockSpec((nb, past, LANES), lambda p, s: (0, 0, p)),
        ] + [bias_spec(j) for j in range(per_step)],
        out_specs=pl.BlockSpec((nb, per_step * tq, LANES), lambda p, s: (0, s, p)),
        out_shape=jax.ShapeDtypeStruct((nb, seq, width), BF16),
        compiler_params=_params(("arbitrary", "arbitrary")),
        name="nbr_attn",
    )(q3, k3, v3, ck3, cv3, *([bias] * per_step))


def _fourier_kernel(x_ref, cdft_ref, d1_ref, d2_ref, twc_ref, tws_ref, o_ref,
                    zr_ref, zi_ref, tr_ref, ti_ref, *, n1, n2):
    gd = cdft_ref.shape[0]
    n_groups = x_ref.shape[3] // gd
    groups = range(n_groups)
    blk = range(SLAB_BLOCK)
    cdft = cdft_ref[...].astype(BF16)
    d1 = d1_ref[...].astype(BF16)
    d2 = d2_ref[...].astype(BF16)
    x = x_ref[0].reshape(n1 * n2, n_groups * gd)
    for g in groups:
        zc = _bdot(x[:, g * gd:(g + 1) * gd], cdft)
        zr_ref[g] = zc[:, :gd].reshape(n1, n2, gd)
        zi_ref[g] = zc[:, gd:].reshape(n1, n2, gd)

    def slabs(re_ref, im_ref, first):
        cols = pl.ds(pl.multiple_of(first, SLAB_BLOCK), SLAB_BLOCK)
        re = [jnp.swapaxes(re_ref[g, :, cols, :], 0, 1) for g in groups]
        im = [jnp.swapaxes(im_ref[g, :, cols, :], 0, 1) for g in groups]
        return jnp.concatenate(
            [jnp.concatenate([re[g][j], im[g][j]], axis=0) for j in blk for g in groups], axis=1)

    def stage1(cb, carry):
        first = cb * SLAB_BLOCK
        t = _bdot(d1, slabs(zr_ref, zi_ref, first).astype(BF16))
        for j in blk:
            cs = twc_ref[first + j]
            sn = tws_ref[first + j]
            for g in groups:
                col = (j * n_groups + g) * gd
                tr = t[:n1, col:col + gd]
                ti = t[n1:, col:col + gd]
                tr_ref[g, first + j] = tr * cs + ti * sn
                ti_ref[g, first + j] = ti * cs - tr * sn
        return carry

    lax.fori_loop(0, n2 // SLAB_BLOCK, stage1, 0)

    def stage2(kb, carry):
        first = kb * SLAB_BLOCK
        y = _bdot(d2, slabs(tr_ref, ti_ref, first).astype(BF16))
        rows = pl.ds(pl.multiple_of(first, SLAB_BLOCK), SLAB_BLOCK)
        for g in groups:
            yg = jnp.stack([y[:, (j * n_groups + g) * gd:(j * n_groups + g + 1) * gd] for j in blk])
            o_ref[0, :, rows, g * gd:(g + 1) * gd] = jnp.swapaxes(yg, 0, 1)
        return carry

    lax.fori_loop(0, n1 // SLAB_BLOCK, stage2, 0)


def _fourier(f3):
    nb, seq, width = f3.shape
    n2 = GRID_W
    n1 = seq // n2
    gd = FNET_GROUP_DIM
    cc, cs = _dft_mats(gd)
    c1, s1 = _dft_mats(n1)
    c2, s2 = _dft_mats(n2)
    norm = 1.0 / math.sqrt(seq * gd)
    cdft = jnp.asarray(np.concatenate([cc, -cs], axis=1) * norm, F32)
    d1 = jnp.asarray(np.block([[c1, s1], [-s1, c1]]), F32)
    d2 = jnp.asarray(np.concatenate([c2, s2], axis=1), F32)
    ang = 2.0 * np.pi * ((np.arange(n2)[:, None] * np.arange(n1)[None, :]) % seq) / seq
    twc = jnp.asarray(np.broadcast_to(np.cos(ang)[:, :, None], (n2, n1, gd)), F32)
    tws = jnp.asarray(np.broadcast_to(np.sin(ang)[:, :, None], (n2, n1, gd)), F32)
    n_groups = FOURIER_COLS // gd
    out = pl.pallas_call(
        functools.partial(_fourier_kernel, n1=n1, n2=n2),
        grid=(nb, width // FOURIER_COLS),
        in_specs=[
            pl.BlockSpec((1, n1, n2, FOURIER_COLS), lambda b, g: (b, 0, 0, g)),
            _const_spec(cdft.shape),
            _const_spec(d1.shape),
            _const_spec(d2.shape),
            _const_spec(twc.shape),
            _const_spec(tws.shape),
        ],
        out_specs=pl.BlockSpec((1, n2, n1, FOURIER_COLS), lambda b, g: (b, 0, 0, g)),
        out_shape=jax.ShapeDtypeStruct((nb, n2, n1, width), F32),
        scratch_shapes=[pltpu.VMEM((n_groups, n1, n2, gd), F32)] * 2
        + [pltpu.VMEM((n_groups, n2, n1, gd), F32)] * 2,
        compiler_params=_params(("arbitrary", "arbitrary")),
        name="fourier",
    )(f3.reshape(nb, n1, n2, width), cdft, d1, d2, twc, tws)
    return out.reshape(nb, seq, width)


def _merge_kernel(fm_ref, o_ref_in, x_ref, m_ref, wgate_ref, wf_ref, wna_ref, wout_ref,
                  g_ref, b_ref, out_ref, *, alpha):
    tm, d = x_ref.shape
    shift = m_ref[0, 3:4, :]
    scale = m_ref[0, 4:5, :]
    gate = m_ref[0, 5:6, :]
    sr = min(tm, ROW_TILE)
    rows = [slice(r0, r0 + sr) for r0 in range(0, tm, sr)]
    xs = [x_ref[rs, :] for rs in rows]
    us = [(x * (1.0 + scale) + shift).astype(BF16) for x in xs]
    slabs = [(fm_ref[rs, :], o_ref_in[rs, :], _bdot(u, wgate_ref[:, :d]), _bdot(u, wgate_ref[:, d:]), x)
             for rs, u, x in zip(rows, us, xs)]
    outs = _merge_out(slabs, gate, wf_ref, wna_ref, wout_ref, g_ref[1:2, :], b_ref[1:2, :], alpha)
    for rs, out in zip(rows, outs):
        out_ref[rs, :] = out


def _merge(fm2, o2, x2, x_first_row, mods_l, rows_per_group, group0, win, wf, wna, wout, layer,
           ln_g, ln_b, alpha):
    rows, d = fm2.shape[0], x2.shape[1]
    tm = math.gcd(math.gcd(MERGE_ROW_TILE, rows_per_group), x_first_row)
    tiles_per_group = rows_per_group // tm
    x_first_tile = x_first_row // tm
    row_spec = lambda w: pl.BlockSpec((tm, w), lambda i: (i, 0))
    assert win.shape[2] == 2 * (2 * d), "gate columns must be the second half of w_in"
    return pl.pallas_call(
        functools.partial(_merge_kernel, alpha=alpha),
        grid=(rows // tm,),
        in_specs=[
            row_spec(fm2.shape[1]), row_spec(o2.shape[1]),
            pl.BlockSpec((tm, d), lambda i: (x_first_tile + i, 0)),
            pl.BlockSpec((1, N_MOD, d), lambda i: (group0 + i // tiles_per_group, 0, 0)),
            _layer_spec(win, layer, cols=2 * d, col_block=1),
            _layer_spec(wf, layer), _layer_spec(wna, layer), _layer_spec(wout, layer),
            _const_spec(ln_g.shape), _const_spec(ln_b.shape),
        ],
        out_specs=row_spec(d),
        out_shape=jax.ShapeDtypeStruct((rows, d), F32),
        compiler_params=_params(("arbitrary",)),
        name="merge",
    )(fm2, o2, x2, mods_l, win, wf, wna, wout, ln_g, ln_b)


def kernel(x_prompt, x_sample, cache_k, cache_v, c, c_ctx, w_ada, b_ada, ln_g, ln_b, w_ff1_up,
           w_ff1_down, w_in, rpb, w_fourier, w_na_out, w_out, w_ff2_up, w_ff2_down):
    batch, seq, d = x_prompt.shape
    dec_batch, dec_seq, _ = x_sample.shape
    depth = w_ada.shape[0]
    alpha = (2 * depth) ** 0.25
    rows_lat = dec_seq // GRID_W

    cvec = jnp.zeros((MOD_ROWS, d), F32).at[0].set(c_ctx).at[1:1 + dec_batch].set(c)
    mods = _mods(cvec, w_ada, b_ada).reshape(depth, MOD_ROWS, N_MOD, d)

    y_p = x_prompt.reshape(batch * seq, d)
    y_s = x_sample.reshape(dec_batch * dec_seq, d)
    up1, dn1 = w_ff1_up.astype(BF16), w_ff1_down.astype(BF16)
    up2, dn2 = w_ff2_up.astype(BF16), w_ff2_down.astype(BF16)
    win = w_in.astype(BF16)
    wf = w_fourier.astype(BF16)
    wna = w_na_out.astype(BF16)
    wout = w_out.astype(BF16)
    ck = cache_k.transpose(1, 0, 3, 2, 4).reshape(depth, dec_batch, -1, NA_WIDTH).astype(BF16)
    cv = cache_v.transpose(1, 0, 3, 2, 4).reshape(depth, dec_batch, -1, NA_WIDTH).astype(BF16)

    bias = _nbr_bias_table(rpb.reshape((-1,) + rpb.shape[2:]), rows_lat)

    rows_p, rows_s = batch * seq, dec_batch * dec_seq

    def segments(src_p, first_p, src_s, first_s):
        return [(src_p, first_p, rows_p, 0, rows_p), (src_s, first_s, rows_s, 1, dec_seq)]

    kv = None
    pending = segments(y_p, 0, y_s, 0)
    for l in range(depth):
        m_l = mods[l]
        g_l, b_l = ln_g[l], ln_b[l]
        y = _ffn(pending, m_l, up1, dn1, l, g_l, b_l, 0, alpha)

        q2, k2, v2, f2 = _in_proj(y, rows_p, rows_s, m_l, dec_seq, 1, win, l)
        to3 = lambda t: t.reshape(dec_batch, dec_seq, t.shape[-1])
        o3 = _nbr_attn(to3(q2), to3(k2), to3(v2), ck[l], cv[l], bias, l)
        fm3 = _fourier(to3(f2))
        y_s = _merge(fm3.reshape(-1, FNET_WIDTH), o3.reshape(-1, NA_WIDTH), y, rows_p, m_l,
                     dec_seq, 1, win, wf, wna, wout, l, g_l, b_l, alpha)

        y, y_s = lax.optimization_barrier((y, y_s))
        y_p3, new_k, new_v = _ctx_mix(y.reshape(-1, seq, d), batch, m_l, win, wf, wna, wout,
                                      g_l, b_l, alpha, l, depth, kv)
        kv = (new_k, new_v)

        mixed = segments(y_p3.reshape(rows_p, d), 0, y_s, 0)
        if l + 1 < depth:
            y = _ffn(mixed, m_l, up2, dn2, l, g_l, b_l, 2, alpha)
            pending = segments(y, 0, y, rows_p)
        else:
            y_p = _ffn(mixed[:1], m_l, up2, dn2, l, g_l, b_l, 2, alpha)
            y_s = _ffn(mixed[1:], m_l, up2, dn2, l, g_l, b_l, 2, alpha)

    return (y_p.reshape(batch, seq, d), y_s.reshape(dec_batch, dec_seq, d), new_k, new_v)
```

```python
import functools
import math

import numpy as np
import jax
import jax.numpy as jnp
from jax import lax
from jax.experimental import pallas as pl
from jax.experimental.pallas import tpu as pltpu

F32 = jnp.float32
BF16 = jnp.bfloat16

HEAD_DIM = 64
NA_HEADS = 8
NA_WIDTH = NA_HEADS * HEAD_DIM
FNET_GROUPS = 4
FNET_GROUP_DIM = 128
FNET_WIDTH = FNET_GROUPS * FNET_GROUP_DIM
GRID_W = 64
WIN_ROWS = 8
WIN_COLS = 16
N_SUB = 3
N_MOD = 3 * N_SUB
ATTN_SCALE = HEAD_DIM ** -0.5
LN_EPS = 1e-5
NEG_INF = -1e30

LANES = 128
MXU_DIM = 256
V7X_VMEM_BYTES = 64 * 1024 * 1024
VMEM_LIMIT_BYTES = V7X_VMEM_BYTES * 7 // 8

ROW_TILE = 512
FFN_ROW_TILE = 1024
FFN_SUB_ROWS = 512
MERGE_ROW_TILE = 1024
FF_CHUNK = MXU_DIM
Q_ROWS = 4
K_ROWS = Q_ROWS + WIN_ROWS
KEY_BLOCK = GRID_W * math.gcd(Q_ROWS, WIN_ROWS // 2)
Q_BLOCKS_PER_STEP = 4
IN_CHUNK = 2 * MXU_DIM
MOD_ROWS = 8
MODS_K_ROWS = 256
FOURIER_COLS = 2 * FNET_GROUP_DIM
SLAB_BLOCK = 8


def _params(sem):
    return pltpu.CompilerParams(dimension_semantics=sem, vmem_limit_bytes=VMEM_LIMIT_BYTES)


def _const_spec(shape):
    nd = len(shape)
    return pl.BlockSpec(shape, lambda *_: (0,) * nd, pipeline_mode=pl.Buffered(1))


def _layer_spec(w, layer, cols=None, col_block=0):
    _, r, n = w.shape
    cols = n if cols is None else cols
    return pl.BlockSpec((None, r, cols), lambda *_: (layer, 0, col_block),
                        pipeline_mode=pl.Buffered(1))


def _layer_norm(r, g, b):
    mu = jnp.mean(r, axis=-1, keepdims=True)
    d = r - mu
    var = jnp.mean(d * d, axis=-1, keepdims=True)
    return d * lax.rsqrt(var + LN_EPS) * g + b


def _bdot(a, b):
    return jnp.dot(a, b, preferred_element_type=F32)


def _dot_nt(a, b):
    return lax.dot_general(a, b, (((1,), (1,)), ((), ())), preferred_element_type=F32)


def _dft_mats(n):
    k = np.arange(n)
    ang = 2.0 * np.pi * ((k[:, None] * k[None, :]) % n) / n
    return np.cos(ang), np.sin(ang)


def _mods_kernel(c_ref, w_ref, b_ref, o_ref):
    k = pl.program_id(1)
    c = c_ref[...]
    s = (c * jax.nn.sigmoid(c)).astype(BF16)
    part = _bdot(s, w_ref[0].astype(BF16))

    @pl.when(k == 0)
    def _():
        o_ref[0] = part + b_ref[0]

    @pl.when(k > 0)
    def _():
        o_ref[0] += part


def _mods(cvec, w_ada, b_ada):
    depth, d, n = w_ada.shape
    tk = MODS_K_ROWS
    return pl.pallas_call(
        _mods_kernel,
        grid=(depth, d // tk),
        in_specs=[
            pl.BlockSpec((MOD_ROWS, tk), lambda l, k: (0, k)),
            pl.BlockSpec((1, tk, n), lambda l, k: (l, k, 0)),
            pl.BlockSpec((1, 1, n), lambda l, k: (l, 0, 0)),
        ],
        out_specs=pl.BlockSpec((1, MOD_ROWS, n), lambda l, k: (l, 0, 0)),
        out_shape=jax.ShapeDtypeStruct((depth, MOD_ROWS, n), F32),
        compiler_params=_params(("arbitrary", "arbitrary")),
        name="mods",
    )(cvec, w_ada, b_ada.reshape(depth, 1, n))


def _ffn_kernel(*refs, sub, alpha, seg_starts):
    n_seg = len(seg_starts)
    x_refs = refs[:n_seg]
    m_ref, wup_ref, wdn_ref, g_ref, b_ref, o_ref, act_ref = refs[n_seg:]
    step = pl.program_id(0)

    def read_x(rs):
        x = x_refs[-1][rs, :]
        for s in range(n_seg - 2, -1, -1):
            x = jnp.where(step < seg_starts[s + 1], x_refs[s][rs, :], x)
        return x

    shift = m_ref[0, 3 * sub:3 * sub + 1, :]
    scale = m_ref[0, 3 * sub + 1:3 * sub + 2, :]
    gate = m_ref[0, 3 * sub + 2:3 * sub + 3, :]
    ff = wdn_ref.shape[0]
    tm = o_ref.shape[0]
    sr = min(tm, FFN_SUB_ROWS)
    slabs = [slice(r0, r0 + sr) for r0 in range(0, tm, sr)]
    us = [(read_x(rs) * (1.0 + scale) + shift).astype(BF16) for rs in slabs]
    for c0 in range(0, ff, FF_CHUNK):
        for rs, u in zip(slabs, us):
            a = _bdot(u, wup_ref[:, c0:c0 + FF_CHUNK])
            g = _bdot(u, wup_ref[:, ff + c0:ff + c0 + FF_CHUNK])
            act_ref[rs, c0:c0 + FF_CHUNK] = ((g * jax.nn.sigmoid(g)) * a).astype(BF16)
    for rs in slabs:
        y = _bdot(act_ref[rs, :], wdn_ref[...])
        r = alpha * read_x(rs) + (0.5 * gate) * y
        o_ref[rs, :] = _layer_norm(r, g_ref[sub:sub + 1, :], b_ref[sub:sub + 1, :])


def _ffn(segments, mods_l, wup, wdn, layer, ln_g, ln_b, sub, alpha):
    d = segments[0][0].shape[1]
    tm = FFN_ROW_TILE
    for _, first_row, n_rows, _, rows_per_group in segments:
        tm = math.gcd(math.gcd(tm, rows_per_group), math.gcd(n_rows, first_row))
    starts, n_steps = [], 0
    for _, _, n_rows, _, _ in segments:
        starts.append(n_steps)
        n_steps += n_rows // tm

    def x_spec(s):
        _, first_row, n_rows, _, _ = segments[s]
        first, count = first_row // tm, n_rows // tm
        return pl.BlockSpec((tm, d), lambda i: (first + jnp.clip(i - starts[s], 0, count - 1), 0))

    def group(i):
        g = None
        for s in range(len(segments) - 1, -1, -1):
            _, _, _, group0, rows_per_group = segments[s]
            g_s = group0 + jnp.maximum(i - starts[s], 0) // (rows_per_group // tm)
            g = g_s if g is None else jnp.where(i < starts[s + 1], g_s, g)
        return g

    return pl.pallas_call(
        functools.partial(_ffn_kernel, sub=sub, alpha=alpha, seg_starts=tuple(starts)),
        grid=(n_steps,),
        in_specs=[x_spec(s) for s in range(len(segments))] + [
            pl.BlockSpec((1, N_MOD, d), lambda i: (group(i), 0, 0)),
            _layer_spec(wup, layer),
            _layer_spec(wdn, layer),
            _const_spec(ln_g.shape),
            _const_spec(ln_b.shape),
        ],
        out_specs=pl.BlockSpec((tm, d), lambda i: (i, 0)),
        out_shape=jax.ShapeDtypeStruct((n_steps * tm, d), F32),
        scratch_shapes=[pltpu.VMEM((tm, wdn.shape[1]), BF16)],
        compiler_params=_params(("arbitrary",)),
        name=f"ffn{sub}",
    )(*[seg[0] for seg in segments], mods_l, wup, wdn, ln_g, ln_b)


def _row_reduce(arrays, combine, reduce):
    acc = None
    for a in arrays:
        for c0 in range(0, a.shape[1], LANES):
            chunk = a[:, c0:c0 + LANES]
            acc = chunk if acc is None else combine(acc, chunk)
    return reduce(acc, axis=-1, keepdims=True)


def _stacked_pair_attention(problems):
    lane = lax.broadcasted_iota(jnp.int32, (1, LANES), 1)
    scores = []
    for q, k_list, _, bias_list in problems:
        zero = jnp.zeros_like(q)
        q2 = jnp.concatenate([jnp.where(lane < HEAD_DIM, q, zero),
                              jnp.where(lane >= HEAD_DIM, q, zero)], axis=0)
        s = []
        for k, bias in zip(k_list, bias_list):
            sk = _dot_nt(q2, k)
            if bias is not None:
                sk = sk + jnp.concatenate(bias, axis=0)
            s.append(sk)
        scores.append(s)
    maxes = [_row_reduce(s, jnp.maximum, jnp.max) for s in scores]
    probs = [[jnp.exp(sk - m) for sk in s] for s, m in zip(scores, maxes)]
    dens = [_row_reduce(p, jnp.add, jnp.sum) for p in probs]
    outs = []
    for (q, _, v_list, _), p, den in zip(problems, probs, dens):
        o = None
        for pk, v in zip(p, v_list):
            pv = _bdot(pk.astype(BF16), v)
            o = pv if o is None else o + pv
        o = o / den
        m_rows = q.shape[0]
        outs.append(jnp.where(lane < HEAD_DIM, o[:m_rows], o[m_rows:]))
    return outs


def _merge_out(slabs, gate, wf_ref, wna_ref, wout_ref, ln_g, ln_b, alpha):
    branches = [(_bdot(fm.astype(BF16), wf_ref[...]), _bdot(o, wna_ref[...]))
                for fm, o, _, _, _ in slabs]
    merged = [(jax.nn.sigmoid(ga) * a + jax.nn.sigmoid(gb) * b).astype(BF16)
              for (_, _, ga, gb, _), (a, b) in zip(slabs, branches)]
    mixes = [_bdot(mg, wout_ref[...]) for mg in merged]
    return [_layer_norm(alpha * x + gate * mix, ln_g, ln_b)
            for (_, _, _, _, x), mix in zip(slabs, mixes)]


def _ctx_mix_kernel(x_ref, m_ref, win_ref, wf_ref, wna_ref, wout_ref, g_ref, b_ref,
                    cdft_ref, pdft_ref, *rest, alpha, n_alias, kv_slot):
    o_ref, k_ref, v_ref, z_ref, oatt_ref, fm_ref = rest[n_alias:]
    nb, seq, d = x_ref.shape
    rows = nb * seq
    x = x_ref[...].reshape(rows, d)
    shift = m_ref[0, 3:4, :]
    scale = m_ref[0, 4:5, :]
    gate = m_ref[0, 5:6, :]
    u = (x * (1.0 + scale) + shift).astype(BF16)
    for c0 in range(0, win_ref.shape[1], IN_CHUNK):
        z_ref[:, c0:c0 + IN_CHUNK] = _bdot(u, win_ref[:, c0:c0 + IN_CHUNK])

    cdft = cdft_ref[...].astype(BF16)
    pdft = pdft_ref[...].astype(BF16)
    q_off, k_off, v_off, f_off = 0, NA_WIDTH, 2 * NA_WIDTH, 3 * NA_WIDTH
    ga_off = f_off + FNET_WIDTH
    gb_off = ga_off + d
    for b in range(nb):
        r0 = b * seq
        for ref, off in ((k_ref, k_off), (v_ref, v_off)):
            t = z_ref[r0:r0 + seq, off:off + NA_WIDTH].T
            ref[b, kv_slot] = t.reshape(NA_HEADS, HEAD_DIM, seq)
        for slot in range(k_ref.shape[1]):
            if slot != kv_slot:
                k_ref[b, slot] = jnp.zeros(k_ref.shape[2:], F32)
                v_ref[b, slot] = jnp.zeros(v_ref.shape[2:], F32)
        zc = [_bdot(z_ref[r0:r0 + seq, f_off + g * FNET_GROUP_DIM:f_off + (g + 1) * FNET_GROUP_DIM]
                    .astype(BF16), cdft).astype(BF16) for g in range(FNET_GROUPS)]
        stacked = jnp.concatenate(
            [jnp.concatenate([z[:, :FNET_GROUP_DIM] for z in zc], axis=1),
             jnp.concatenate([z[:, FNET_GROUP_DIM:] for z in zc], axis=1)], axis=0)
        fm_ref[r0:r0 + seq, :] = _bdot(pdft, stacked)

    problems = []
    for b in range(nb):
        r0 = b * seq
        for hp in range(NA_WIDTH // LANES):
            c0 = hp * LANES
            q = (z_ref[r0:r0 + seq, q_off + c0:q_off + c0 + LANES] * ATTN_SCALE).astype(BF16)
            k = z_ref[r0:r0 + seq, k_off + c0:k_off + c0 + LANES].astype(BF16)
            v = z_ref[r0:r0 + seq, v_off + c0:v_off + c0 + LANES].astype(BF16)
            problems.append((b * seq, c0, (q, [k], [v], [None])))
    outs = _stacked_pair_attention([p for _, _, p in problems])
    for (r0, c0, _), o in zip(problems, outs):
        oatt_ref[r0:r0 + seq, c0:c0 + LANES] = o.astype(BF16)

    slab = (fm_ref[...], oatt_ref[...], z_ref[:, ga_off:ga_off + d], z_ref[:, gb_off:gb_off + d], x)
    (out,) = _merge_out([slab], gate, wf_ref, wna_ref, wout_ref, g_ref[1:2, :], b_ref[1:2, :],
                        alpha)
    o_ref[...] = out.reshape(nb, seq, d)


def _ctx_mix(x3, batch, mods_l, win, wf, wna, wout, ln_g, ln_b, alpha, layer, depth, kv_prev):
    _, seq, d = x3.shape
    nb = 2 if batch % 2 == 0 else 1
    in_width = win.shape[2]
    cc, cs = _dft_mats(FNET_GROUP_DIM)
    pc, ps = _dft_mats(seq)
    norm = 1.0 / math.sqrt(seq * FNET_GROUP_DIM)
    cdft = jnp.asarray(np.concatenate([cc, -cs], axis=1) * norm, F32)
    pdft = jnp.asarray(np.concatenate([pc, ps], axis=1), F32)
    kv_shape = jax.ShapeDtypeStruct((batch, depth, NA_HEADS, HEAD_DIM, seq), F32)
    if kv_prev is None:
        kv_spec = pl.BlockSpec((nb, depth, NA_HEADS, HEAD_DIM, seq), lambda i: (i, 0, 0, 0, 0))
        kv_slot = layer
    else:
        kv_spec = pl.BlockSpec((nb, 1, NA_HEADS, HEAD_DIM, seq), lambda i: (i, layer, 0, 0, 0))
        kv_slot = 0
    operands = [x3, mods_l, win, wf, wna, wout, ln_g, ln_b, cdft, pdft]
    in_specs = [
        pl.BlockSpec((nb, seq, d), lambda i: (i, 0, 0)),
        pl.BlockSpec((1, N_MOD, d), lambda i: (0, 0, 0)),
        _layer_spec(win, layer),
        _layer_spec(wf, layer),
        _layer_spec(wna, layer),
        _layer_spec(wout, layer),
        _const_spec(ln_g.shape),
        _const_spec(ln_b.shape),
        _const_spec(cdft.shape),
        _const_spec(pdft.shape),
    ]
    aliases = {}
    if kv_prev is not None:
        aliases = {len(operands): 1, len(operands) + 1: 2}
        operands += list(kv_prev)
        in_specs += [pl.BlockSpec(memory_space=pl.ANY)] * 2
    return pl.pallas_call(
        functools.partial(_ctx_mix_kernel, alpha=alpha, n_alias=len(aliases), kv_slot=kv_slot),
        grid=(batch // nb,),
        in_specs=in_specs,
        out_specs=[pl.BlockSpec((nb, seq, d), lambda i: (i, 0, 0)), kv_spec, kv_spec],
        out_shape=[jax.ShapeDtypeStruct((batch, seq, d), F32), kv_shape, kv_shape],
        input_output_aliases=aliases,
        scratch_shapes=[
            pltpu.VMEM((nb * seq, in_width), F32),
            pltpu.VMEM((nb * seq, NA_WIDTH), BF16),
            pltpu.VMEM((nb * seq, FNET_WIDTH), F32),
        ],
        compiler_params=_params(("arbitrary",)),
        name="ctx_mix",
    )(*operands)


def _in_proj_kernel(x_ref, m_ref, win_ref, q_ref, k_ref, v_ref, f_ref):
    x = x_ref[...]
    shift = m_ref[0, 3:4, :]
    scale = m_ref[0, 4:5, :]
    u = (x * (1.0 + scale) + shift).astype(BF16)
    cw = NA_WIDTH
    q_ref[...] = (_bdot(u, win_ref[:, 0:cw]) * ATTN_SCALE).astype(BF16)
    k_ref[...] = _bdot(u, win_ref[:, cw:2 * cw]).astype(BF16)
    v_ref[...] = _bdot(u, win_ref[:, 2 * cw:3 * cw]).astype(BF16)
    f_ref[...] = _bdot(u, win_ref[:, 3 * cw:3 * cw + FNET_WIDTH]).astype(BF16)


def _in_proj(x2, first_row, rows, mods_l, rows_per_group, group0, win, layer):
    d = x2.shape[1]
    tm = math.gcd(math.gcd(ROW_TILE, rows_per_group), first_row)
    tiles_per_group = rows_per_group // tm
    first_tile = first_row // tm
    cw = NA_WIDTH
    qkvf = 3 * NA_WIDTH + FNET_WIDTH
    row_spec = lambda w: pl.BlockSpec((tm, w), lambda i: (i, 0))
    return pl.pallas_call(
        _in_proj_kernel,
        grid=(rows // tm,),
        in_specs=[
            pl.BlockSpec((tm, d), lambda i: (first_tile + i, 0)),
            pl.BlockSpec((1, N_MOD, d), lambda i: (group0 + i // tiles_per_group, 0, 0)),
            _layer_spec(win, layer, cols=qkvf, col_block=0),
        ],
        out_specs=[row_spec(cw), row_spec(cw), row_spec(cw), row_spec(FNET_WIDTH)],
        out_shape=[
            jax.ShapeDtypeStruct((rows, cw), BF16),
            jax.ShapeDtypeStruct((rows, cw), BF16),
            jax.ShapeDtypeStruct((rows, cw), BF16),
            jax.ShapeDtypeStruct((rows, FNET_WIDTH), BF16),
        ],
        compiler_params=_params(("arbitrary",)),
        name="in_proj",
    )(x2, mods_l, win)


def _window_start_row(i, rows):
    return jnp.clip(Q_ROWS * i - WIN_ROWS // 2, 0, rows - K_ROWS)


def _nbr_attn_kernel(q_ref, k_ref, v_ref, ck_ref, cv_ref, *rest, rows):
    bias_refs, o_ref = rest[:-1], rest[-1]
    step = pl.program_id(1)
    n_keys = K_ROWS * GRID_W
    tq = Q_ROWS * GRID_W
    problems, places = [], []
    for j, bias_ref in enumerate(bias_refs):
        i = step * len(bias_refs) + j
        start = pl.multiple_of(_window_start_row(i, rows) * GRID_W, KEY_BLOCK)
        bias = (bias_ref[0, 0], bias_ref[1, 0])
        for b in range(q_ref.shape[0]):
            problems.append(
                (q_ref[b, j * tq:(j + 1) * tq, :],
                 [k_ref[b, pl.ds(start, n_keys), :], ck_ref[b]],
                 [v_ref[b, pl.ds(start, n_keys), :], cv_ref[b]],
                 [bias, None]))
            places.append((b, j))
    for (b, j), o in zip(places, _stacked_pair_attention(problems)):
        o_ref[b, j * tq:(j + 1) * tq, :] = o.astype(BF16)


def _bias_table_kernel(rpb_ref, ok_ref, o_ref, *, n_dr, n_dc, slot0, n_cols, starts):
    h = pl.program_id(0)
    base = h * (n_dr * n_dc)
    row = lax.broadcasted_iota(jnp.int32, (GRID_W, LANES), 0)
    lane = lax.broadcasted_iota(jnp.int32, (GRID_W, LANES), 1)
    rel = (lane & (GRID_W - 1)) - row + (WIN_COLS - 1)
    low = lane < GRID_W

    def column(dr_low, dr_high):
        tile = jnp.zeros((GRID_W, LANES), F32)
        valid = [0 <= dr < n_dr for dr in (dr_low, dr_high)]
        if not any(valid):
            return tile
        for d in range(n_dc):
            lo = rpb_ref[base + dr_low * n_dc + d] if valid[0] else 0.0
            hi = rpb_ref[base + dr_high * n_dc + d] if valid[1] else 0.0
            tile = jnp.where(rel == d, jnp.where(low, lo, hi), tile)
        return tile

    builds = [[column(2 * p - shift - slot0, 2 * p + 1 - shift - slot0) for p in range(n_cols)]
              for shift in (0, 1)]
    cols_per_slab = K_ROWS * GRID_W // LANES
    for t, start in enumerate(starts):
        shift = start % 2
        c0 = (start + shift) // 2
        window = jnp.concatenate(builds[shift][c0:c0 + cols_per_slab], axis=1)
        o_ref[0, t] = jnp.where(ok_ref[t] > 0.0, window, NEG_INF)


def _nbr_bias_table(rpb_all, rows):
    h, n_dr, n_dc = rpb_all.shape
    n_blocks = rows // Q_ROWS
    kr = min(WIN_ROWS, rows)
    variants = np.array([0, min(1, n_blocks - 1), n_blocks - 1])
    r = Q_ROWS * variants[:, None] + np.arange(Q_ROWS)[None, :]
    ks = np.clip(Q_ROWS * variants - WIN_ROWS // 2, 0, rows - K_ROWS)
    key_row = ks[:, None] + np.arange(K_ROWS)[None, :]
    r0 = np.clip(r - kr // 2, 0, rows - kr)
    row_ok = (key_row[:, None, :] >= r0[:, :, None]) & (key_row[:, None, :] < r0[:, :, None] + kr)
    c_idx = np.arange(GRID_W)
    c0 = np.clip(c_idx - WIN_COLS // 2, 0, GRID_W - WIN_COLS)
    col_ok = (c_idx[None, :] >= c0[:, None]) & (c_idx[None, :] < c0[:, None] + WIN_COLS)
    rel = c_idx[None, :] - c_idx[:, None] + (WIN_COLS - 1)
    assert rel[col_ok].min() >= 0 and rel[col_ok].max() < n_dc

    first_dr = (ks[:, None] - r + (WIN_ROWS - 1)).reshape(-1)
    slot0 = int(max(0, -first_dr.min()))
    starts = tuple(int(d) + slot0 for d in first_dr)
    n_cols = (max(starts) + 1 + K_ROWS + 1) // 2
    ok = (row_ok[:, :, None, :, None] & col_ok[None, None, :, None, :]).reshape(
        3 * Q_ROWS, GRID_W, K_ROWS * GRID_W).astype(np.float32)
    n_slabs = 3 * Q_ROWS
    table = pl.pallas_call(
        functools.partial(_bias_table_kernel, n_dr=n_dr, n_dc=n_dc, slot0=slot0, n_cols=n_cols,
                          starts=starts),
        grid=(h,),
        in_specs=[
            pl.BlockSpec(memory_space=pltpu.SMEM),
            _const_spec(ok.shape),
        ],
        out_specs=pl.BlockSpec((1, n_slabs, GRID_W, K_ROWS * GRID_W), lambda i: (i, 0, 0, 0)),
        out_shape=jax.ShapeDtypeStruct((h, n_slabs, GRID_W, K_ROWS * GRID_W), F32),
        compiler_params=_params(("arbitrary",)),
        name="bias_table",
    )(rpb_all.reshape(-1).astype(F32), jnp.asarray(ok))
    return table.reshape(h, 3, Q_ROWS * GRID_W, K_ROWS * GRID_W)


def _nbr_attn(q3, k3, v3, ck3, cv3, bias, layer):
    nb, seq, width = q3.shape
    rows = seq // GRID_W
    n_blocks = rows // Q_ROWS
    n_pairs = width // LANES
    past = ck3.shape[1]
    tq = Q_ROWS * GRID_W
    tk = K_ROWS * GRID_W

    per_step = Q_BLOCKS_PER_STEP if n_blocks % Q_BLOCKS_PER_STEP == 0 else 1

    def variant(i):
        return jnp.where(i == 0, 0, jnp.where(i == n_blocks - 1, 2, 1))

    def bias_spec(j):
        return pl.BlockSpec((2, 1, tq, tk),
                            lambda p, s: (layer * n_pairs + p, variant(s * per_step + j), 0, 0))

    return pl.pallas_call(
        functools.partial(_nbr_attn_kernel, rows=rows),
        grid=(n_pairs, n_blocks // per_step),
        in_specs=[
            pl.BlockSpec((nb, per_step * tq, LANES), lambda p, s: (0, s, p)),
            pl.BlockSpec((nb, seq, LANES), lambda p, s: (0, 0, p)),
            pl.BlockSpec((nb, seq, LANES), lambda p, s: (0, 0, p)),
            pl.BlockSpec((nb, past, LANES), lambda p, s: (0, 0, p)),
            pl.BlockSpec((nb, past, LANES), lambda p, s: (0, 0, p)),
        ] + [bias_spec(j) for j in range(per_step)],
        out_specs=pl.BlockSpec((nb, per_step * tq, LANES), lambda p, s: (0, s, p)),
        out_shape=jax.ShapeDtypeStruct((nb, seq, width), BF16),
        compiler_params=_params(("arbitrary", "arbitrary")),
        name="nbr_attn",
    )(q3, k3, v3, ck3, cv3, *([bias] * per_step))


def _fourier_kernel(x_ref, cdft_ref, d1_ref, d2_ref, twc_ref, tws_ref, o_ref,
                    zr_ref, zi_ref, tr_ref, ti_ref, *, n1, n2):
    gd = cdft_ref.shape[0]
    n_groups = x_ref.shape[3] // gd
    groups = range(n_groups)
    blk = range(SLAB_BLOCK)
    cdft = cdft_ref[...].astype(BF16)
    d1 = d1_ref[...].astype(BF16)
    d2 = d2_ref[...].astype(BF16)
    x = x_ref[0].reshape(n1 * n2, n_groups * gd)
    for g in groups:
        zc = _bdot(x[:, g * gd:(g + 1) * gd], cdft)
        zr_ref[g] = zc[:, :gd].reshape(n1, n2, gd)
        zi_ref[g] = zc[:, gd:].reshape(n1, n2, gd)

    def slabs(re_ref, im_ref, first):
        cols = pl.ds(pl.multiple_of(first, SLAB_BLOCK), SLAB_BLOCK)
        re = [jnp.swapaxes(re_ref[g, :, cols, :], 0, 1) for g in groups]
        im = [jnp.swapaxes(im_ref[g, :, cols, :], 0, 1) for g in groups]
        return jnp.concatenate(
            [jnp.concatenate([re[g][j], im[g][j]], axis=0) for j in blk for g in groups], axis=1)

    def stage1(cb, carry):
        first = cb * SLAB_BLOCK
        t = _bdot(d1, slabs(zr_ref, zi_ref, first).astype(BF16))
        for j in blk:
            cs = twc_ref[first + j]
            sn = tws_ref[first + j]
            for g in groups:
                col = (j * n_groups + g) * gd
                tr = t[:n1, col:col + gd]
                ti = t[n1:, col:col + gd]
                tr_ref[g, first + j] = tr * cs + ti * sn
                ti_ref[g, first + j] = ti * cs - tr * sn
        return carry

    lax.fori_loop(0, n2 // SLAB_BLOCK, stage1, 0)

    def stage2(kb, carry):
        first = kb * SLAB_BLOCK
        y = _bdot(d2, slabs(tr_ref, ti_ref, first).astype(BF16))
        rows = pl.ds(pl.multiple_of(first, SLAB_BLOCK), SLAB_BLOCK)
        for g in groups:
            yg = jnp.stack([y[:, (j * n_groups + g) * gd:(j * n_groups + g + 1) * gd] for j in blk])
            o_ref[0, :, rows, g * gd:(g + 1) * gd] = jnp.swapaxes(yg, 0, 1)
        return carry

    lax.fori_loop(0, n1 // SLAB_BLOCK, stage2, 0)


def _fourier(f3):
    nb, seq, width = f3.shape
    n2 = GRID_W
    n1 = seq // n2
    gd = FNET_GROUP_DIM
    cc, cs = _dft_mats(gd)
    c1, s1 = _dft_mats(n1)
    c2, s2 = _dft_mats(n2)
    norm = 1.0 / math.sqrt(seq * gd)
    cdft = jnp.asarray(np.concatenate([cc, -cs], axis=1) * norm, F32)
    d1 = jnp.asarray(np.block([[c1, s1], [-s1, c1]]), F32)
    d2 = jnp.asarray(np.concatenate([c2, s2], axis=1), F32)
    ang = 2.0 * np.pi * ((np.arange(n2)[:, None] * np.arange(n1)[None, :]) % seq) / seq
    twc = jnp.asarray(np.broadcast_to(np.cos(ang)[:, :, None], (n2, n1, gd)), F32)
    tws = jnp.asarray(np.broadcast_to(np.sin(ang)[:, :, None], (n2, n1, gd)), F32)
    n_groups = FOURIER_COLS // gd
    out = pl.pallas_call(
        functools.partial(_fourier_kernel, n1=n1, n2=n2),
        grid=(nb, width // FOURIER_COLS),
        in_specs=[
            pl.BlockSpec((1, n1, n2, FOURIER_COLS), lambda b, g: (b, 0, 0, g)),
            _const_spec(cdft.shape),
            _const_spec(d1.shape),
            _const_spec(d2.shape),
            _const_spec(twc.shape),
            _const_spec(tws.shape),
        ],
        out_specs=pl.BlockSpec((1, n2, n1, FOURIER_COLS), lambda b, g: (b, 0, 0, g)),
        out_shape=jax.ShapeDtypeStruct((nb, n2, n1, width), F32),
        scratch_shapes=[pltpu.VMEM((n_groups, n1, n2, gd), F32)] * 2
        + [pltpu.VMEM((n_groups, n2, n1, gd), F32)] * 2,
        compiler_params=_params(("arbitrary", "arbitrary")),
        name="fourier",
    )(f3.reshape(nb, n1, n2, width), cdft, d1, d2, twc, tws)
    return out.reshape(nb, seq, width)


def _merge_kernel(fm_ref, o_ref_in, x_ref, m_ref, wgate_ref, wf_ref, wna_ref, wout_ref,
                  g_ref, b_ref, out_ref, *, alpha):
    tm, d = x_ref.shape
    shift = m_ref[0, 3:4, :]
    scale = m_ref[0, 4:5, :]
    gate = m_ref[0, 5:6, :]
    sr = min(tm, ROW_TILE)
    rows = [slice(r0, r0 + sr) for r0 in range(0, tm, sr)]
    xs = [x_ref[rs, :] for rs in rows]
    us = [(x * (1.0 + scale) + shift).astype(BF16) for x in xs]
    slabs = [(fm_ref[rs, :], o_ref_in[rs, :], _bdot(u, wgate_ref[:, :d]), _bdot(u, wgate_ref[:, d:]), x)
             for rs, u, x in zip(rows, us, xs)]
    outs = _merge_out(slabs, gate, wf_ref, wna_ref, wout_ref, g_ref[1:2, :], b_ref[1:2, :], alpha)
    for rs, out in zip(rows, outs):
        out_ref[rs, :] = out


def _merge(fm2, o2, x2, x_first_row, mods_l, rows_per_group, group0, win, wf, wna, wout, layer,
           ln_g, ln_b, alpha):
    rows, d = fm2.shape[0], x2.shape[1]
    tm = math.gcd(math.gcd(MERGE_ROW_TILE, rows_per_group), x_first_row)
    tiles_per_group = rows_per_group // tm
    x_first_tile = x_first_row // tm
    row_spec = lambda w: pl.BlockSpec((tm, w), lambda i: (i, 0))
    assert win.shape[2] == 2 * (2 * d), "gate columns must be the second half of w_in"
    return pl.pallas_call(
        functools.partial(_merge_kernel, alpha=alpha),
        grid=(rows // tm,),
        in_specs=[
            row_spec(fm2.shape[1]), row_spec(o2.shape[1]),
            pl.BlockSpec((tm, d), lambda i: (x_first_tile + i, 0)),
            pl.BlockSpec((1, N_MOD, d), lambda i: (group0 + i // tiles_per_group, 0, 0)),
            _layer_spec(win, layer, cols=2 * d, col_block=1),
            _layer_spec(wf, layer), _layer_spec(wna, layer), _layer_spec(wout, layer),
            _const_spec(ln_g.shape), _const_spec(ln_b.shape),
        ],
        out_specs=row_spec(d),
        out_shape=jax.ShapeDtypeStruct((rows, d), F32),
        compiler_params=_params(("arbitrary",)),
        name="merge",
    )(fm2, o2, x2, mods_l, win, wf, wna, wout, ln_g, ln_b)


def kernel(x_prompt, x_sample, cache_k, cache_v, c, c_ctx, w_ada, b_ada, ln_g, ln_b, w_ff1_up,
           w_ff1_down, w_in, rpb, w_fourier, w_na_out, w_out, w_ff2_up, w_ff2_down):
    batch, seq, d = x_prompt.shape
    dec_batch, dec_seq, _ = x_sample.shape
    depth = w_ada.shape[0]
    alpha = (2 * depth) ** 0.25
    rows_lat = dec_seq // GRID_W

    cvec = jnp.zeros((MOD_ROWS, d), F32).at[0].set(c_ctx).at[1:1 + dec_batch].set(c)
    mods = _mods(cvec, w_ada, b_ada).reshape(depth, MOD_ROWS, N_MOD, d)

    y_p = x_prompt.reshape(batch * seq, d)
    y_s = x_sample.reshape(dec_batch * dec_seq, d)
    up1, dn1 = w_ff1_up.astype(BF16), w_ff1_down.astype(BF16)
    up2, dn2 = w_ff2_up.astype(BF16), w_ff2_down.astype(BF16)
    win = w_in.astype(BF16)
    wf = w_fourier.astype(BF16)
    wna = w_na_out.astype(BF16)
    wout = w_out.astype(BF16)
    ck = cache_k.transpose(1, 0, 3, 2, 4).reshape(depth, dec_batch, -1, NA_WIDTH).astype(BF16)
    cv = cache_v.transpose(1, 0, 3, 2, 4).reshape(depth, dec_batch, -1, NA_WIDTH).astype(BF16)

    bias = _nbr_bias_table(rpb.reshape((-1,) + rpb.shape[2:]), rows_lat)

    rows_p, rows_s = batch * seq, dec_batch * dec_seq

    def segments(src_p, first_p, src_s, first_s):
        return [(src_p, first_p, rows_p, 0, rows_p), (src_s, first_s, rows_s, 1, dec_seq)]

    kv = None
    pending = segments(y_p, 0, y_s, 0)
    for l in range(depth):
        m_l = mods[l]
        g_l, b_l = ln_g[l], ln_b[l]
        y = _ffn(pending, m_l, up1, dn1, l, g_l, b_l, 0, alpha)

        q2, k2, v2, f2 = _in_proj(y, rows_p, rows_s, m_l, dec_seq, 1, win, l)
        to3 = lambda t: t.reshape(dec_batch, dec_seq, t.shape[-1])
        o3 = _nbr_attn(to3(q2), to3(k2), to3(v2), ck[l], cv[l], bias, l)
        fm3 = _fourier(to3(f2))
        y_s = _merge(fm3.reshape(-1, FNET_WIDTH), o3.reshape(-1, NA_WIDTH), y, rows_p, m_l,
                     dec_seq, 1, win, wf, wna, wout, l, g_l, b_l, alpha)

        y, y_s = lax.optimization_barrier((y, y_s))
        y_p3, new_k, new_v = _ctx_mix(y.reshape(-1, seq, d), batch, m_l, win, wf, wna, wout,
                                      g_l, b_l, alpha, l, depth, kv)
        kv = (new_k, new_v)

        mixed = segments(y_p3.reshape(rows_p, d), 0, y_s, 0)
        if l + 1 < depth:
            y = _ffn(mixed, m_l, up2, dn2, l, g_l, b_l, 2, alpha)
            pending = segments(y, 0, y, rows_p)
        else:
            y_p = _ffn(mixed[:1], m_l, up2, dn2, l, g_l, b_l, 2, alpha)
            y_s = _ffn(mixed[1:], m_l, up2, dn2, l, g_l, b_l, 2, alpha)

    new_k, new_v = (t.transpose(0, 1, 2, 4, 3) for t in kv)
    return (y_p.reshape(batch, seq, d), y_s.reshape(dec_batch, dec_seq, d), new_k, new_v)
```

```python
import functools
import math

import numpy as np
import jax
import jax.numpy as jnp
from jax import lax
from jax.experimental import pallas as pl
from jax.experimental.pallas import tpu as pltpu

F32 = jnp.float32
BF16 = jnp.bfloat16

HEAD_DIM = 64
NA_HEADS = 8
NA_WIDTH = NA_HEADS * HEAD_DIM
FNET_GROUPS = 4
FNET_GROUP_DIM = 128
FNET_WIDTH = FNET_GROUPS * FNET_GROUP_DIM
GRID_W = 64
WIN_ROWS = 8
WIN_COLS = 16
N_SUB = 3
N_MOD = 3 * N_SUB
ATTN_SCALE = HEAD_DIM ** -0.5
LN_EPS = 1e-5
NEG_INF = -1e30

LANES = 128
MXU_DIM = 256
V7X_VMEM_BYTES = 64 * 1024 * 1024
VMEM_LIMIT_BYTES = V7X_VMEM_BYTES * 7 // 8

ROW_TILE = 512
FFN_ROW_TILE = 1024
FFN_SUB_ROWS = 512
MERGE_ROW_TILE = 1024
MERGE_SUB_ROWS = 256
FF_CHUNK = MXU_DIM
Q_ROWS = 4
K_ROWS = Q_ROWS + WIN_ROWS
KEY_BLOCK = GRID_W * math.gcd(Q_ROWS, WIN_ROWS // 2)
Q_BLOCKS_PER_STEP = 4
IN_CHUNK = 2 * MXU_DIM
MOD_ROWS = 8
MODS_K_ROWS = 256
FOURIER_COLS = 2 * FNET_GROUP_DIM
SLAB_BLOCK = 8


def _params(sem):
    return pltpu.CompilerParams(dimension_semantics=sem, vmem_limit_bytes=VMEM_LIMIT_BYTES)


def _const_spec(shape):
    nd = len(shape)
    return pl.BlockSpec(shape, lambda *_: (0,) * nd, pipeline_mode=pl.Buffered(1))


def _layer_spec(w, layer, cols=None, col_block=0):
    _, r, n = w.shape
    cols = n if cols is None else cols
    return pl.BlockSpec((None, r, cols), lambda *_: (layer, 0, col_block),
                        pipeline_mode=pl.Buffered(1))


def _layer_norm(r, g, b):
    mu = jnp.mean(r, axis=-1, keepdims=True)
    d = r - mu
    var = jnp.mean(d * d, axis=-1, keepdims=True)
    return d * lax.rsqrt(var + LN_EPS) * g + b


def _bdot(a, b):
    return jnp.dot(a, b, preferred_element_type=F32)


def _dot_nt(a, b):
    return lax.dot_general(a, b, (((1,), (1,)), ((), ())), preferred_element_type=F32)


def _dft_mats(n):
    k = np.arange(n)
    ang = 2.0 * np.pi * ((k[:, None] * k[None, :]) % n) / n
    return np.cos(ang), np.sin(ang)


def _mods_kernel(c_ref, w_ref, b_ref, o_ref):
    k = pl.program_id(1)
    c = c_ref[...]
    s = (c * jax.nn.sigmoid(c)).astype(BF16)
    part = _bdot(s, w_ref[0].astype(BF16))

    @pl.when(k == 0)
    def _():
        o_ref[0] = part + b_ref[0]

    @pl.when(k > 0)
    def _():
        o_ref[0] += part


def _mods(cvec, w_ada, b_ada):
    depth, d, n = w_ada.shape
    tk = MODS_K_ROWS
    return pl.pallas_call(
        _mods_kernel,
        grid=(depth, d // tk),
        in_specs=[
            pl.BlockSpec((MOD_ROWS, tk), lambda l, k: (0, k)),
            pl.BlockSpec((1, tk, n), lambda l, k: (l, k, 0)),
            pl.BlockSpec((1, 1, n), lambda l, k: (l, 0, 0)),
        ],
        out_specs=pl.BlockSpec((1, MOD_ROWS, n), lambda l, k: (l, 0, 0)),
        out_shape=jax.ShapeDtypeStruct((depth, MOD_ROWS, n), F32),
        compiler_params=_params(("arbitrary", "arbitrary")),
        name="mods",
    )(cvec, w_ada, b_ada.reshape(depth, 1, n))


def _ffn_kernel(*refs, sub, alpha, seg_starts):
    n_seg = len(seg_starts)
    x_refs = refs[:n_seg]
    m_ref, wup_ref, wdn_ref, g_ref, b_ref, o_ref, act_ref = refs[n_seg:]
    step = pl.program_id(0)

    def read_x(rs):
        x = x_refs[-1][rs, :]
        for s in range(n_seg - 2, -1, -1):
            x = jnp.where(step < seg_starts[s + 1], x_refs[s][rs, :], x)
        return x

    shift = m_ref[0, 3 * sub:3 * sub + 1, :]
    scale = m_ref[0, 3 * sub + 1:3 * sub + 2, :]
    gate = m_ref[0, 3 * sub + 2:3 * sub + 3, :]
    ff = wdn_ref.shape[0]
    tm = o_ref.shape[0]
    sr = min(tm, FFN_SUB_ROWS)
    slabs = [slice(r0, r0 + sr) for r0 in range(0, tm, sr)]
    us = [(read_x(rs) * (1.0 + scale) + shift).astype(BF16) for rs in slabs]
    for c0 in range(0, ff, FF_CHUNK):
        for rs, u in zip(slabs, us):
            a = _bdot(u, wup_ref[:, c0:c0 + FF_CHUNK])
            g = _bdot(u, wup_ref[:, ff + c0:ff + c0 + FF_CHUNK])
            act_ref[rs, c0:c0 + FF_CHUNK] = ((g * jax.nn.sigmoid(g)) * a).astype(BF16)
    for rs in slabs:
        y = _bdot(act_ref[rs, :], wdn_ref[...])
        r = alpha * read_x(rs) + (0.5 * gate) * y
        o_ref[rs, :] = _layer_norm(r, g_ref[sub:sub + 1, :], b_ref[sub:sub + 1, :])


def _ffn(segments, mods_l, wup, wdn, layer, ln_g, ln_b, sub, alpha):
    d = segments[0][0].shape[1]
    tm = FFN_ROW_TILE
    for _, first_row, n_rows, _, rows_per_group in segments:
        tm = math.gcd(math.gcd(tm, rows_per_group), math.gcd(n_rows, first_row))
    starts, n_steps = [], 0
    for _, _, n_rows, _, _ in segments:
        starts.append(n_steps)
        n_steps += n_rows // tm

    def x_spec(s):
        _, first_row, n_rows, _, _ = segments[s]
        first, count = first_row // tm, n_rows // tm
        return pl.BlockSpec((tm, d), lambda i: (first + jnp.clip(i - starts[s], 0, count - 1), 0))

    def group(i):
        g = None
        for s in range(len(segments) - 1, -1, -1):
            _, _, _, group0, rows_per_group = segments[s]
            g_s = group0 + jnp.maximum(i - starts[s], 0) // (rows_per_group // tm)
            g = g_s if g is None else jnp.where(i < starts[s + 1], g_s, g)
        return g

    return pl.pallas_call(
        functools.partial(_ffn_kernel, sub=sub, alpha=alpha, seg_starts=tuple(starts)),
        grid=(n_steps,),
        in_specs=[x_spec(s) for s in range(len(segments))] + [
            pl.BlockSpec((1, N_MOD, d), lambda i: (group(i), 0, 0)),
            _layer_spec(wup, layer),
            _layer_spec(wdn, layer),
            _const_spec(ln_g.shape),
            _const_spec(ln_b.shape),
        ],
        out_specs=pl.BlockSpec((tm, d), lambda i: (i, 0)),
        out_shape=jax.ShapeDtypeStruct((n_steps * tm, d), F32),
        scratch_shapes=[pltpu.VMEM((tm, wdn.shape[1]), BF16)],
        compiler_params=_params(("arbitrary",)),
        name=f"ffn{sub}",
    )(*[seg[0] for seg in segments], mods_l, wup, wdn, ln_g, ln_b)


def _row_reduce(arrays, combine, reduce):
    acc = None
    for a in arrays:
        for c0 in range(0, a.shape[1], LANES):
            chunk = a[:, c0:c0 + LANES]
            acc = chunk if acc is None else combine(acc, chunk)
    return reduce(acc, axis=-1, keepdims=True)


def _stacked_pair_attention(problems):
    lane = lax.broadcasted_iota(jnp.int32, (1, LANES), 1)
    scores = []
    for q, k_list, _, bias_list in problems:
        zero = jnp.zeros_like(q)
        q2 = jnp.concatenate([jnp.where(lane < HEAD_DIM, q, zero),
                              jnp.where(lane >= HEAD_DIM, q, zero)], axis=0)
        s = []
        for k, bias in zip(k_list, bias_list):
            sk = _dot_nt(q2, k)
            if bias is not None:
                sk = sk + jnp.concatenate(bias, axis=0)
            s.append(sk)
        scores.append(s)
    maxes = [_row_reduce(s, jnp.maximum, jnp.max) for s in scores]
    probs = [[jnp.exp(sk - m) for sk in s] for s, m in zip(scores, maxes)]
    dens = [_row_reduce(p, jnp.add, jnp.sum) for p in probs]
    outs = []
    for (q, _, v_list, _), p, den in zip(problems, probs, dens):
        o = None
        for pk, v in zip(p, v_list):
            pv = _bdot(pk.astype(BF16), v)
            o = pv if o is None else o + pv
        o = o / den
        m_rows = q.shape[0]
        outs.append(jnp.where(lane < HEAD_DIM, o[:m_rows], o[m_rows:]))
    return outs


def _merge_out(slabs, gate, wf_ref, wna_ref, wout_ref, ln_g, ln_b, alpha):
    branches = [(_bdot(fm.astype(BF16), wf_ref[...]), _bdot(o, wna_ref[...]))
                for fm, o, _, _, _ in slabs]
    merged = [(jax.nn.sigmoid(ga) * a + jax.nn.sigmoid(gb) * b).astype(BF16)
              for (_, _, ga, gb, _), (a, b) in zip(slabs, branches)]
    mixes = [_bdot(mg, wout_ref[...]) for mg in merged]
    return [_layer_norm(alpha * x + gate * mix, ln_g, ln_b)
            for (_, _, _, _, x), mix in zip(slabs, mixes)]


def _ctx_mix_kernel(x_ref, m_ref, win_ref, wf_ref, wna_ref, wout_ref, g_ref, b_ref,
                    cdft_ref, pdft_ref, *rest, alpha, n_alias, kv_slot):
    o_ref, k_ref, v_ref, z_ref, oatt_ref, fm_ref = rest[n_alias:]
    nb, seq, d = x_ref.shape
    rows = nb * seq
    x = x_ref[...].reshape(rows, d)
    shift = m_ref[0, 3:4, :]
    scale = m_ref[0, 4:5, :]
    gate = m_ref[0, 5:6, :]
    u = (x * (1.0 + scale) + shift).astype(BF16)
    for c0 in range(0, win_ref.shape[1], IN_CHUNK):
        z_ref[:, c0:c0 + IN_CHUNK] = _bdot(u, win_ref[:, c0:c0 + IN_CHUNK])

    cdft = cdft_ref[...].astype(BF16)
    pdft = pdft_ref[...].astype(BF16)
    q_off, k_off, v_off, f_off = 0, NA_WIDTH, 2 * NA_WIDTH, 3 * NA_WIDTH
    ga_off = f_off + FNET_WIDTH
    gb_off = ga_off + d
    for b in range(nb):
        r0 = b * seq
        for ref, off in ((k_ref, k_off), (v_ref, v_off)):
            t = z_ref[r0:r0 + seq, off:off + NA_WIDTH].T
            ref[b, kv_slot] = t.reshape(NA_HEADS, HEAD_DIM, seq)
        for slot in range(k_ref.shape[1]):
            if slot != kv_slot:
                k_ref[b, slot] = jnp.zeros(k_ref.shape[2:], F32)
                v_ref[b, slot] = jnp.zeros(v_ref.shape[2:], F32)
        zc = [_bdot(z_ref[r0:r0 + seq, f_off + g * FNET_GROUP_DIM:f_off + (g + 1) * FNET_GROUP_DIM]
                    .astype(BF16), cdft).astype(BF16) for g in range(FNET_GROUPS)]
        stacked = jnp.concatenate(
            [jnp.concatenate([z[:, :FNET_GROUP_DIM] for z in zc], axis=1),
             jnp.concatenate([z[:, FNET_GROUP_DIM:] for z in zc], axis=1)], axis=0)
        fm_ref[r0:r0 + seq, :] = _bdot(pdft, stacked)

    problems = []
    for b in range(nb):
        r0 = b * seq
        for hp in range(NA_WIDTH // LANES):
            c0 = hp * LANES
            q = (z_ref[r0:r0 + seq, q_off + c0:q_off + c0 + LANES] * ATTN_SCALE).astype(BF16)
            k = z_ref[r0:r0 + seq, k_off + c0:k_off + c0 + LANES].astype(BF16)
            v = z_ref[r0:r0 + seq, v_off + c0:v_off + c0 + LANES].astype(BF16)
            problems.append((b * seq, c0, (q, [k], [v], [None])))
    outs = _stacked_pair_attention([p for _, _, p in problems])
    for (r0, c0, _), o in zip(problems, outs):
        oatt_ref[r0:r0 + seq, c0:c0 + LANES] = o.astype(BF16)

    rows = [slice(b * seq, (b + 1) * seq) for b in range(nb)]
    slabs = [(fm_ref[rs, :], oatt_ref[rs, :], z_ref[rs, ga_off:ga_off + d],
              z_ref[rs, gb_off:gb_off + d], x_ref[b]) for b, rs in enumerate(rows)]
    outs = _merge_out(slabs, gate, wf_ref, wna_ref, wout_ref, g_ref[1:2, :], b_ref[1:2, :], alpha)
    for b, out in enumerate(outs):
        o_ref[b] = out


def _ctx_mix(x3, batch, mods_l, win, wf, wna, wout, ln_g, ln_b, alpha, layer, depth, kv_prev):
    _, seq, d = x3.shape
    nb = 2 if batch % 2 == 0 else 1
    in_width = win.shape[2]
    cc, cs = _dft_mats(FNET_GROUP_DIM)
    pc, ps = _dft_mats(seq)
    norm = 1.0 / math.sqrt(seq * FNET_GROUP_DIM)
    cdft = jnp.asarray(np.concatenate([cc, -cs], axis=1) * norm, F32)
    pdft = jnp.asarray(np.concatenate([pc, ps], axis=1), F32)
    kv_shape = jax.ShapeDtypeStruct((batch, depth, NA_HEADS, HEAD_DIM, seq), F32)
    if kv_prev is None:
        kv_spec = pl.BlockSpec((nb, depth, NA_HEADS, HEAD_DIM, seq), lambda i: (i, 0, 0, 0, 0))
        kv_slot = layer
    else:
        kv_spec = pl.BlockSpec((nb, 1, NA_HEADS, HEAD_DIM, seq), lambda i: (i, layer, 0, 0, 0))
        kv_slot = 0
    operands = [x3, mods_l, win, wf, wna, wout, ln_g, ln_b, cdft, pdft]
    in_specs = [
        pl.BlockSpec((nb, seq, d), lambda i: (i, 0, 0)),
        pl.BlockSpec((1, N_MOD, d), lambda i: (0, 0, 0)),
        _layer_spec(win, layer),
        _layer_spec(wf, layer),
        _layer_spec(wna, layer),
        _layer_spec(wout, layer),
        _const_spec(ln_g.shape),
        _const_spec(ln_b.shape),
        _const_spec(cdft.shape),
        _const_spec(pdft.shape),
    ]
    aliases = {}
    if kv_prev is not None:
        aliases = {len(operands): 1, len(operands) + 1: 2}
        operands += list(kv_prev)
        in_specs += [pl.BlockSpec(memory_space=pl.ANY)] * 2
    return pl.pallas_call(
        functools.partial(_ctx_mix_kernel, alpha=alpha, n_alias=len(aliases), kv_slot=kv_slot),
        grid=(batch // nb,),
        in_specs=in_specs,
        out_specs=[pl.BlockSpec((nb, seq, d), lambda i: (i, 0, 0)), kv_spec, kv_spec],
        out_shape=[jax.ShapeDtypeStruct((batch, seq, d), F32), kv_shape, kv_shape],
        input_output_aliases=aliases,
        scratch_shapes=[
            pltpu.VMEM((nb * seq, in_width), F32),
            pltpu.VMEM((nb * seq, NA_WIDTH), BF16),
            pltpu.VMEM((nb * seq, FNET_WIDTH), F32),
        ],
        compiler_params=_params(("arbitrary",)),
        name="ctx_mix",
    )(*operands)


def _in_proj_kernel(x_ref, m_ref, win_ref, q_ref, k_ref, v_ref, f_ref):
    x = x_ref[...]
    shift = m_ref[0, 3:4, :]
    scale = m_ref[0, 4:5, :]
    u = (x * (1.0 + scale) + shift).astype(BF16)
    cw = NA_WIDTH
    q_ref[...] = (_bdot(u, win_ref[:, 0:cw]) * ATTN_SCALE).astype(BF16)
    k_ref[...] = _bdot(u, win_ref[:, cw:2 * cw]).astype(BF16)
    v_ref[...] = _bdot(u, win_ref[:, 2 * cw:3 * cw]).astype(BF16)
    f_ref[...] = _bdot(u, win_ref[:, 3 * cw:3 * cw + FNET_WIDTH]).astype(BF16)


def _in_proj(x2, first_row, rows, mods_l, rows_per_group, group0, win, layer):
    d = x2.shape[1]
    tm = math.gcd(math.gcd(ROW_TILE, rows_per_group), first_row)
    tiles_per_group = rows_per_group // tm
    first_tile = first_row // tm
    cw = NA_WIDTH
    qkvf = 3 * NA_WIDTH + FNET_WIDTH
    row_spec = lambda w: pl.BlockSpec((tm, w), lambda i: (i, 0))
    return pl.pallas_call(
        _in_proj_kernel,
        grid=(rows // tm,),
        in_specs=[
            pl.BlockSpec((tm, d), lambda i: (first_tile + i, 0)),
            pl.BlockSpec((1, N_MOD, d), lambda i: (group0 + i // tiles_per_group, 0, 0)),
            _layer_spec(win, layer, cols=qkvf, col_block=0),
        ],
        out_specs=[row_spec(cw), row_spec(cw), row_spec(cw), row_spec(FNET_WIDTH)],
        out_shape=[
            jax.ShapeDtypeStruct((rows, cw), BF16),
            jax.ShapeDtypeStruct((rows, cw), BF16),
            jax.ShapeDtypeStruct((rows, cw), BF16),
            jax.ShapeDtypeStruct((rows, FNET_WIDTH), BF16),
        ],
        compiler_params=_params(("arbitrary",)),
        name="in_proj",
    )(x2, mods_l, win)


def _window_start_row(i, rows):
    return jnp.clip(Q_ROWS * i - WIN_ROWS // 2, 0, rows - K_ROWS)


def _nbr_attn_kernel(q_ref, k_ref, v_ref, ck_ref, cv_ref, *rest, rows):
    bias_refs, o_ref = rest[:-1], rest[-1]
    step = pl.program_id(1)
    n_keys = K_ROWS * GRID_W
    tq = Q_ROWS * GRID_W
    problems, places = [], []
    for j, bias_ref in enumerate(bias_refs):
        i = step * len(bias_refs) + j
        start = pl.multiple_of(_window_start_row(i, rows) * GRID_W, KEY_BLOCK)
        bias = (bias_ref[0, 0], bias_ref[1, 0])
        for b in range(q_ref.shape[0]):
            problems.append(
                (q_ref[b, j * tq:(j + 1) * tq, :],
                 [k_ref[b, pl.ds(start, n_keys), :], ck_ref[b]],
                 [v_ref[b, pl.ds(start, n_keys), :], cv_ref[b]],
                 [bias, None]))
            places.append((b, j))
    for (b, j), o in zip(places, _stacked_pair_attention(problems)):
        o_ref[b, j * tq:(j + 1) * tq, :] = o.astype(BF16)


def _bias_table_kernel(rpb_ref, ok_ref, o_ref, *, n_dr, n_dc, slot0, n_cols, starts):
    h = pl.program_id(0)
    base = h * (n_dr * n_dc)
    row = lax.broadcasted_iota(jnp.int32, (GRID_W, LANES), 0)
    lane = lax.broadcasted_iota(jnp.int32, (GRID_W, LANES), 1)
    rel = (lane & (GRID_W - 1)) - row + (WIN_COLS - 1)
    low = lane < GRID_W

    def column(dr_low, dr_high):
        tile = jnp.zeros((GRID_W, LANES), F32)
        valid = [0 <= dr < n_dr for dr in (dr_low, dr_high)]
        if not any(valid):
            return tile
        for d in range(n_dc):
            lo = rpb_ref[base + dr_low * n_dc + d] if valid[0] else 0.0
            hi = rpb_ref[base + dr_high * n_dc + d] if valid[1] else 0.0
            tile = jnp.where(rel == d, jnp.where(low, lo, hi), tile)
        return tile

    builds = [[column(2 * p - shift - slot0, 2 * p + 1 - shift - slot0) for p in range(n_cols)]
              for shift in (0, 1)]
    cols_per_slab = K_ROWS * GRID_W // LANES
    for t, start in enumerate(starts):
        shift = start % 2
        c0 = (start + shift) // 2
        window = jnp.concatenate(builds[shift][c0:c0 + cols_per_slab], axis=1)
        o_ref[0, t] = jnp.where(ok_ref[t] > 0.0, window, NEG_INF)


def _nbr_bias_table(rpb_all, rows):
    h, n_dr, n_dc = rpb_all.shape
    n_blocks = rows // Q_ROWS
    kr = min(WIN_ROWS, rows)
    variants = np.array([0, min(1, n_blocks - 1), n_blocks - 1])
    r = Q_ROWS * variants[:, None] + np.arange(Q_ROWS)[None, :]
    ks = np.clip(Q_ROWS * variants - WIN_ROWS // 2, 0, rows - K_ROWS)
    key_row = ks[:, None] + np.arange(K_ROWS)[None, :]
    r0 = np.clip(r - kr // 2, 0, rows - kr)
    row_ok = (key_row[:, None, :] >= r0[:, :, None]) & (key_row[:, None, :] < r0[:, :, None] + kr)
    c_idx = np.arange(GRID_W)
    c0 = np.clip(c_idx - WIN_COLS // 2, 0, GRID_W - WIN_COLS)
    col_ok = (c_idx[None, :] >= c0[:, None]) & (c_idx[None, :] < c0[:, None] + WIN_COLS)
    rel = c_idx[None, :] - c_idx[:, None] + (WIN_COLS - 1)
    assert rel[col_ok].min() >= 0 and rel[col_ok].max() < n_dc

    first_dr = (ks[:, None] - r + (WIN_ROWS - 1)).reshape(-1)
    slot0 = int(max(0, -first_dr.min()))
    starts = tuple(int(d) + slot0 for d in first_dr)
    n_cols = (max(starts) + 1 + K_ROWS + 1) // 2
    ok = (row_ok[:, :, None, :, None] & col_ok[None, None, :, None, :]).reshape(
        3 * Q_ROWS, GRID_W, K_ROWS * GRID_W).astype(np.float32)
    n_slabs = 3 * Q_ROWS
    table = pl.pallas_call(
        functools.partial(_bias_table_kernel, n_dr=n_dr, n_dc=n_dc, slot0=slot0, n_cols=n_cols,
                          starts=starts),
        grid=(h,),
        in_specs=[
            pl.BlockSpec(memory_space=pltpu.SMEM),
            _const_spec(ok.shape),
        ],
        out_specs=pl.BlockSpec((1, n_slabs, GRID_W, K_ROWS * GRID_W), lambda i: (i, 0, 0, 0)),
        out_shape=jax.ShapeDtypeStruct((h, n_slabs, GRID_W, K_ROWS * GRID_W), F32),
        compiler_params=_params(("arbitrary",)),
        name="bias_table",
    )(rpb_all.reshape(-1).astype(F32), jnp.asarray(ok))
    return table.reshape(h, 3, Q_ROWS * GRID_W, K_ROWS * GRID_W)


def _nbr_attn(q3, k3, v3, ck3, cv3, bias, layer):
    nb, seq, width = q3.shape
    rows = seq // GRID_W
    n_blocks = rows // Q_ROWS
    n_pairs = width // LANES
    past = ck3.shape[1]
    tq = Q_ROWS * GRID_W
    tk = K_ROWS * GRID_W

    per_step = Q_BLOCKS_PER_STEP if n_blocks % Q_BLOCKS_PER_STEP == 0 else 1

    def variant(i):
        return jnp.where(i == 0, 0, jnp.where(i == n_blocks - 1, 2, 1))

    def bias_spec(j):
        return pl.BlockSpec((2, 1, tq, tk),
                            lambda p, s: (layer * n_pairs + p, variant(s * per_step + j), 0, 0))

    return pl.pallas_call(
        functools.partial(_nbr_attn_kernel, rows=rows),
        grid=(n_pairs, n_blocks // per_step),
        in_specs=[
            pl.BlockSpec((nb, per_step * tq, LANES), lambda p, s: (0, s, p)),
            pl.BlockSpec((nb, seq, LANES), lambda p, s: (0, 0, p)),
            pl.BlockSpec((nb, seq, LANES), lambda p, s: (0, 0, p)),
            pl.BlockSpec((nb, past, LANES), lambda p, s: (0, 0, p)),
            pl.BlockSpec((nb, past, LANES), lambda p, s: (0, 0, p)),
        ] + [bias_spec(j) for j in range(per_step)],
        out_specs=pl.BlockSpec((nb, per_step * tq, LANES), lambda p, s: (0, s, p)),
        out_shape=jax.ShapeDtypeStruct((nb, seq, width), BF16),
        compiler_params=_params(("arbitrary", "arbitrary")),
        name="nbr_attn",
    )(q3, k3, v3, ck3, cv3, *([bias] * per_step))


def _fourier_kernel(x_ref, cdft_ref, d1_ref, d2_ref, twc_ref, tws_ref, o_ref,
                    zr_ref, zi_ref, tr_ref, ti_ref, *, n1, n2):
    gd = cdft_ref.shape[0]
    n_groups = x_ref.shape[3] // gd
    groups = range(n_groups)
    blk = range(SLAB_BLOCK)
    cdft = cdft_ref[...].astype(BF16)
    d1 = d1_ref[...].astype(BF16)
    d2 = d2_ref[...].astype(BF16)
    x = x_ref[0].reshape(n1 * n2, n_groups * gd)
    for g in groups:
        zc = _bdot(x[:, g * gd:(g + 1) * gd], cdft)
        zr_ref[g] = zc[:, :gd].reshape(n1, n2, gd)
        zi_ref[g] = zc[:, gd:].reshape(n1, n2, gd)

    def slabs(re_ref, im_ref, first):
        cols = pl.ds(pl.multiple_of(first, SLAB_BLOCK), SLAB_BLOCK)
        re = [jnp.swapaxes(re_ref[g, :, cols, :], 0, 1) for g in groups]
        im = [jnp.swapaxes(im_ref[g, :, cols, :], 0, 1) for g in groups]
        return jnp.concatenate(
            [jnp.concatenate([re[g][j], im[g][j]], axis=0) for j in blk for g in groups], axis=1)

    def stage1(cb, carry):
        first = cb * SLAB_BLOCK
        t = _bdot(d1, slabs(zr_ref, zi_ref, first).astype(BF16))
        for j in blk:
            cs = twc_ref[first + j]
            sn = tws_ref[first + j]
            for g in groups:
                col = (j * n_groups + g) * gd
                tr = t[:n1, col:col + gd]
                ti = t[n1:, col:col + gd]
                tr_ref[g, first + j] = tr * cs + ti * sn
                ti_ref[g, first + j] = ti * cs - tr * sn
        return carry

    lax.fori_loop(0, n2 // SLAB_BLOCK, stage1, 0)

    def stage2(kb, carry):
        first = kb * SLAB_BLOCK
        y = _bdot(d2, slabs(tr_ref, ti_ref, first).astype(BF16))
        rows = pl.ds(pl.multiple_of(first, SLAB_BLOCK), SLAB_BLOCK)
        for g in groups:
            yg = jnp.stack([y[:, (j * n_groups + g) * gd:(j * n_groups + g + 1) * gd] for j in blk])
            o_ref[0, :, rows, g * gd:(g + 1) * gd] = jnp.swapaxes(yg, 0, 1)
        return carry

    lax.fori_loop(0, n1 // SLAB_BLOCK, stage2, 0)


def _fourier(f3):
    nb, seq, width = f3.shape
    n2 = GRID_W
    n1 = seq // n2
    gd = FNET_GROUP_DIM
    cc, cs = _dft_mats(gd)
    c1, s1 = _dft_mats(n1)
    c2, s2 = _dft_mats(n2)
    norm = 1.0 / math.sqrt(seq * gd)
    cdft = jnp.asarray(np.concatenate([cc, -cs], axis=1) * norm, F32)
    d1 = jnp.asarray(np.block([[c1, s1], [-s1, c1]]), F32)
    d2 = jnp.asarray(np.concatenate([c2, s2], axis=1), F32)
    ang = 2.0 * np.pi * ((np.arange(n2)[:, None] * np.arange(n1)[None, :]) % seq) / seq
    twc = jnp.asarray(np.broadcast_to(np.cos(ang)[:, :, None], (n2, n1, gd)), F32)
    tws = jnp.asarray(np.broadcast_to(np.sin(ang)[:, :, None], (n2, n1, gd)), F32)
    n_groups = FOURIER_COLS // gd
    out = pl.pallas_call(
        functools.partial(_fourier_kernel, n1=n1, n2=n2),
        grid=(nb, width // FOURIER_COLS),
        in_specs=[
            pl.BlockSpec((1, n1, n2, FOURIER_COLS), lambda b, g: (b, 0, 0, g)),
            _const_spec(cdft.shape),
            _const_spec(d1.shape),
            _const_spec(d2.shape),
            _const_spec(twc.shape),
            _const_spec(tws.shape),
        ],
        out_specs=pl.BlockSpec((1, n2, n1, FOURIER_COLS), lambda b, g: (b, 0, 0, g)),
        out_shape=jax.ShapeDtypeStruct((nb, n2, n1, width), F32),
        scratch_shapes=[pltpu.VMEM((n_groups, n1, n2, gd), F32)] * 2
        + [pltpu.VMEM((n_groups, n2, n1, gd), F32)] * 2,
        compiler_params=_params(("arbitrary", "arbitrary")),
        name="fourier",
    )(f3.reshape(nb, n1, n2, width), cdft, d1, d2, twc, tws)
    return out.reshape(nb, seq, width)


def _merge_kernel(fm_ref, o_ref_in, x_ref, m_ref, wgate_ref, wf_ref, wna_ref, wout_ref,
                  g_ref, b_ref, out_ref, *, alpha):
    tm, d = x_ref.shape
    shift = m_ref[0, 3:4, :]
    scale = m_ref[0, 4:5, :]
    gate = m_ref[0, 5:6, :]
    sr = min(tm, MERGE_SUB_ROWS)
    rows = [slice(r0, r0 + sr) for r0 in range(0, tm, sr)]
    xs = [x_ref[rs, :] for rs in rows]
    us = [(x * (1.0 + scale) + shift).astype(BF16) for x in xs]
    slabs = [(fm_ref[rs, :], o_ref_in[rs, :], _bdot(u, wgate_ref[:, :d]), _bdot(u, wgate_ref[:, d:]), x)
             for rs, u, x in zip(rows, us, xs)]
    outs = _merge_out(slabs, gate, wf_ref, wna_ref, wout_ref, g_ref[1:2, :], b_ref[1:2, :], alpha)
    for rs, out in zip(rows, outs):
        out_ref[rs, :] = out


def _merge(fm2, o2, x2, x_first_row, mods_l, rows_per_group, group0, win, wf, wna, wout, layer,
           ln_g, ln_b, alpha):
    rows, d = fm2.shape[0], x2.shape[1]
    tm = math.gcd(math.gcd(MERGE_ROW_TILE, rows_per_group), x_first_row)
    tiles_per_group = rows_per_group // tm
    x_first_tile = x_first_row // tm
    row_spec = lambda w: pl.BlockSpec((tm, w), lambda i: (i, 0))
    assert win.shape[2] == 2 * (2 * d), "gate columns must be the second half of w_in"
    return pl.pallas_call(
        functools.partial(_merge_kernel, alpha=alpha),
        grid=(rows // tm,),
        in_specs=[
            row_spec(fm2.shape[1]), row_spec(o2.shape[1]),
            pl.BlockSpec((tm, d), lambda i: (x_first_tile + i, 0)),
            pl.BlockSpec((1, N_MOD, d), lambda i: (group0 + i // tiles_per_group, 0, 0)),
            _layer_spec(win, layer, cols=2 * d, col_block=1),
            _layer_spec(wf, layer), _layer_spec(wna, layer), _layer_spec(wout, layer),
            _const_spec(ln_g.shape), _const_spec(ln_b.shape),
        ],
        out_specs=row_spec(d),
        out_shape=jax.ShapeDtypeStruct((rows, d), F32),
        compiler_params=_params(("arbitrary",)),
        name="merge",
    )(fm2, o2, x2, mods_l, win, wf, wna, wout, ln_g, ln_b)


def kernel(x_prompt, x_sample, cache_k, cache_v, c, c_ctx, w_ada, b_ada, ln_g, ln_b, w_ff1_up,
           w_ff1_down, w_in, rpb, w_fourier, w_na_out, w_out, w_ff2_up, w_ff2_down):
    batch, seq, d = x_prompt.shape
    dec_batch, dec_seq, _ = x_sample.shape
    depth = w_ada.shape[0]
    alpha = (2 * depth) ** 0.25
    rows_lat = dec_seq // GRID_W

    cvec = jnp.zeros((MOD_ROWS, d), F32).at[0].set(c_ctx).at[1:1 + dec_batch].set(c)
    mods = _mods(cvec, w_ada, b_ada).reshape(depth, MOD_ROWS, N_MOD, d)

    y_p = x_prompt.reshape(batch * seq, d)
    y_s = x_sample.reshape(dec_batch * dec_seq, d)
    up1, dn1 = w_ff1_up.astype(BF16), w_ff1_down.astype(BF16)
    up2, dn2 = w_ff2_up.astype(BF16), w_ff2_down.astype(BF16)
    win = w_in.astype(BF16)
    wf = w_fourier.astype(BF16)
    wna = w_na_out.astype(BF16)
    wout = w_out.astype(BF16)
    ck = cache_k.transpose(1, 0, 3, 2, 4).reshape(depth, dec_batch, -1, NA_WIDTH).astype(BF16)
    cv = cache_v.transpose(1, 0, 3, 2, 4).reshape(depth, dec_batch, -1, NA_WIDTH).astype(BF16)

    bias = _nbr_bias_table(rpb.reshape((-1,) + rpb.shape[2:]), rows_lat)

    rows_p, rows_s = batch * seq, dec_batch * dec_seq

    def segments(src_p, first_p, src_s, first_s):
        return [(src_p, first_p, rows_p, 0, rows_p), (src_s, first_s, rows_s, 1, dec_seq)]

    kv = None
    pending = segments(y_p, 0, y_s, 0)
    for l in range(depth):
        m_l = mods[l]
        g_l, b_l = ln_g[l], ln_b[l]
        y = _ffn(pending, m_l, up1, dn1, l, g_l, b_l, 0, alpha)

        q2, k2, v2, f2 = _in_proj(y, rows_p, rows_s, m_l, dec_seq, 1, win, l)
        to3 = lambda t: t.reshape(dec_batch, dec_seq, t.shape[-1])
        o3 = _nbr_attn(to3(q2), to3(k2), to3(v2), ck[l], cv[l], bias, l)
        fm3 = _fourier(to3(f2))
        y_s = _merge(fm3.reshape(-1, FNET_WIDTH), o3.reshape(-1, NA_WIDTH), y, rows_p, m_l,
                     dec_seq, 1, win, wf, wna, wout, l, g_l, b_l, alpha)

        y, y_s = lax.optimization_barrier((y, y_s))
        y_p3, new_k, new_v = _ctx_mix(y.reshape(-1, seq, d), batch, m_l, win, wf, wna, wout,
                                      g_l, b_l, alpha, l, depth, kv)
        kv = (new_k, new_v)

        mixed = segments(y_p3.reshape(rows_p, d), 0, y_s, 0)
        if l + 1 < depth:
            y = _ffn(mixed, m_l, up2, dn2, l, g_l, b_l, 2, alpha)
            pending = segments(y, 0, y, rows_p)
        else:
            y_p = _ffn(mixed[:1], m_l, up2, dn2, l, g_l, b_l, 2, alpha)
            y_s = _ffn(mixed[1:], m_l, up2, dn2, l, g_l, b_l, 2, alpha)

    new_k, new_v = (t.transpose(0, 1, 2, 4, 3) for t in kv)
    return (y_p.reshape(batch, seq, d), y_s.reshape(dec_batch, dec_seq, d), new_k, new_v)
```

```python
import functools
import math

import numpy as np
import jax
import jax.numpy as jnp
from jax import lax
from jax.experimental import pallas as pl
from jax.experimental.pallas import tpu as pltpu

F32 = jnp.float32
BF16 = jnp.bfloat16

HEAD_DIM = 64
NA_HEADS = 8
NA_WIDTH = NA_HEADS * HEAD_DIM
FNET_GROUPS = 4
FNET_GROUP_DIM = 128
FNET_WIDTH = FNET_GROUPS * FNET_GROUP_DIM
GRID_W = 64
WIN_ROWS = 8
WIN_COLS = 16
N_SUB = 3
N_MOD = 3 * N_SUB
ATTN_SCALE = HEAD_DIM ** -0.5
LN_EPS = 1e-5
NEG_INF = -1e30

LANES = 128
MXU_DIM = 256
V7X_VMEM_BYTES = 64 * 1024 * 1024
VMEM_LIMIT_BYTES = V7X_VMEM_BYTES * 7 // 8

ROW_TILE = 512
FFN_ROW_TILE = 1024
FFN_SUB_ROWS = 512
MERGE_ROW_TILE = 1024
MERGE_SUB_ROWS = 256
FF_CHUNK = MXU_DIM
Q_ROWS = 4
K_ROWS = Q_ROWS + WIN_ROWS
KEY_BLOCK = GRID_W * math.gcd(Q_ROWS, WIN_ROWS // 2)
Q_BLOCKS_PER_STEP = 4
IN_CHUNK = 2 * MXU_DIM
MOD_ROWS = 8
MODS_K_ROWS = 256
FOURIER_COLS = 2 * FNET_GROUP_DIM
SLAB_BLOCK = 8


def _params(sem):
    return pltpu.CompilerParams(dimension_semantics=sem, vmem_limit_bytes=VMEM_LIMIT_BYTES)


def _const_spec(shape):
    nd = len(shape)
    return pl.BlockSpec(shape, lambda *_: (0,) * nd, pipeline_mode=pl.Buffered(1))


def _layer_spec(w, layer, cols=None, col_block=0):
    _, r, n = w.shape
    cols = n if cols is None else cols
    return pl.BlockSpec((None, r, cols), lambda *_: (layer, 0, col_block),
                        pipeline_mode=pl.Buffered(1))


def _layer_norm(r, g, b):
    mu = jnp.mean(r, axis=-1, keepdims=True)
    d = r - mu
    var = jnp.mean(d * d, axis=-1, keepdims=True)
    return d * lax.rsqrt(var + LN_EPS) * g + b


def _bdot(a, b):
    return jnp.dot(a, b, preferred_element_type=F32)


def _dot_nt(a, b):
    return lax.dot_general(a, b, (((1,), (1,)), ((), ())), preferred_element_type=F32)


def _dft_mats(n):
    k = np.arange(n)
    ang = 2.0 * np.pi * ((k[:, None] * k[None, :]) % n) / n
    return np.cos(ang), np.sin(ang)


def _mods_kernel(c_ref, w_ref, b_ref, o_ref):
    k = pl.program_id(1)
    c = c_ref[...]
    s = (c * jax.nn.sigmoid(c)).astype(BF16)
    part = _bdot(s, w_ref[0].astype(BF16))

    @pl.when(k == 0)
    def _():
        o_ref[0] = part + b_ref[0]

    @pl.when(k > 0)
    def _():
        o_ref[0] += part


def _mods(cvec, w_ada, b_ada):
    depth, d, n = w_ada.shape
    tk = MODS_K_ROWS
    return pl.pallas_call(
        _mods_kernel,
        grid=(depth, d // tk),
        in_specs=[
            pl.BlockSpec((MOD_ROWS, tk), lambda l, k: (0, k)),
            pl.BlockSpec((1, tk, n), lambda l, k: (l, k, 0)),
            pl.BlockSpec((1, 1, n), lambda l, k: (l, 0, 0)),
        ],
        out_specs=pl.BlockSpec((1, MOD_ROWS, n), lambda l, k: (l, 0, 0)),
        out_shape=jax.ShapeDtypeStruct((depth, MOD_ROWS, n), F32),
        compiler_params=_params(("arbitrary", "arbitrary")),
        name="mods",
    )(cvec, w_ada, b_ada.reshape(depth, 1, n))


def _ffn_kernel(*refs, sub, alpha, seg_starts):
    n_seg = len(seg_starts)
    x_refs = refs[:n_seg]
    m_ref, wup_ref, wdn_ref, g_ref, b_ref, o_ref, act_ref = refs[n_seg:]
    step = pl.program_id(0)

    def read_x(rs):
        x = x_refs[-1][rs, :]
        for s in range(n_seg - 2, -1, -1):
            x = jnp.where(step < seg_starts[s + 1], x_refs[s][rs, :], x)
        return x

    shift = m_ref[0, 3 * sub:3 * sub + 1, :]
    scale = m_ref[0, 3 * sub + 1:3 * sub + 2, :]
    gate = m_ref[0, 3 * sub + 2:3 * sub + 3, :]
    ff = wdn_ref.shape[0]
    tm = o_ref.shape[0]
    sr = min(tm, FFN_SUB_ROWS)
    slabs = [slice(r0, r0 + sr) for r0 in range(0, tm, sr)]
    us = [(read_x(rs) * (1.0 + scale) + shift).astype(BF16) for rs in slabs]
    for c0 in range(0, ff, FF_CHUNK):
        for rs, u in zip(slabs, us):
            a = _bdot(u, wup_ref[:, c0:c0 + FF_CHUNK])
            g = _bdot(u, wup_ref[:, ff + c0:ff + c0 + FF_CHUNK])
            act_ref[rs, c0:c0 + FF_CHUNK] = ((g * jax.nn.sigmoid(g)) * a).astype(BF16)
    for rs in slabs:
        y = _bdot(act_ref[rs, :], wdn_ref[...])
        r = alpha * read_x(rs) + (0.5 * gate) * y
        o_ref[rs, :] = _layer_norm(r, g_ref[sub:sub + 1, :], b_ref[sub:sub + 1, :])


def _ffn(segments, mods_l, wup, wdn, layer, ln_g, ln_b, sub, alpha):
    d = segments[0][0].shape[1]
    tm = FFN_ROW_TILE
    for _, first_row, n_rows, _, rows_per_group in segments:
        tm = math.gcd(math.gcd(tm, rows_per_group), math.gcd(n_rows, first_row))
    starts, n_steps = [], 0
    for _, _, n_rows, _, _ in segments:
        starts.append(n_steps)
        n_steps += n_rows // tm

    def x_spec(s):
        _, first_row, n_rows, _, _ = segments[s]
        first, count = first_row // tm, n_rows // tm
        return pl.BlockSpec((tm, d), lambda i: (first + jnp.clip(i - starts[s], 0, count - 1), 0))

    def group(i):
        g = None
        for s in range(len(segments) - 1, -1, -1):
            _, _, _, group0, rows_per_group = segments[s]
            g_s = group0 + jnp.maximum(i - starts[s], 0) // (rows_per_group // tm)
            g = g_s if g is None else jnp.where(i < starts[s + 1], g_s, g)
        return g

    return pl.pallas_call(
        functools.partial(_ffn_kernel, sub=sub, alpha=alpha, seg_starts=tuple(starts)),
        grid=(n_steps,),
        in_specs=[x_spec(s) for s in range(len(segments))] + [
            pl.BlockSpec((1, N_MOD, d), lambda i: (group(i), 0, 0)),
            _layer_spec(wup, layer),
            _layer_spec(wdn, layer),
            _const_spec(ln_g.shape),
            _const_spec(ln_b.shape),
        ],
        out_specs=pl.BlockSpec((tm, d), lambda i: (i, 0)),
        out_shape=jax.ShapeDtypeStruct((n_steps * tm, d), F32),
        scratch_shapes=[pltpu.VMEM((tm, wdn.shape[1]), BF16)],
        compiler_params=_params(("arbitrary",)),
        name=f"ffn{sub}",
    )(*[seg[0] for seg in segments], mods_l, wup, wdn, ln_g, ln_b)


def _row_reduce(arrays, combine, reduce):
    acc = None
    for a in arrays:
        for c0 in range(0, a.shape[1], LANES):
            chunk = a[:, c0:c0 + LANES]
            acc = chunk if acc is None else combine(acc, chunk)
    return reduce(acc, axis=-1, keepdims=True)


def _stacked_pair_attention(problems):
    lane = lax.broadcasted_iota(jnp.int32, (1, LANES), 1)
    scores = []
    for q, k_list, _, bias_list in problems:
        zero = jnp.zeros_like(q)
        q2 = jnp.concatenate([jnp.where(lane < HEAD_DIM, q, zero),
                              jnp.where(lane >= HEAD_DIM, q, zero)], axis=0)
        s = []
        for k, bias in zip(k_list, bias_list):
            sk = _dot_nt(q2, k)
            if bias is not None:
                sk = sk + jnp.concatenate(bias, axis=0)
            s.append(sk)
        scores.append(s)
    maxes = [_row_reduce(s, jnp.maximum, jnp.max) for s in scores]
    probs = [[jnp.exp(sk - m) for sk in s] for s, m in zip(scores, maxes)]
    dens = [_row_reduce(p, jnp.add, jnp.sum) for p in probs]
    outs = []
    for (q, _, v_list, _), p, den in zip(problems, probs, dens):
        o = None
        for pk, v in zip(p, v_list):
            pv = _bdot(pk.astype(BF16), v)
            o = pv if o is None else o + pv
        o = o / den
        m_rows = q.shape[0]
        outs.append(jnp.where(lane < HEAD_DIM, o[:m_rows], o[m_rows:]))
    return outs


def _merge_out(slabs, gate, wf_ref, wna_ref, wout_ref, ln_g, ln_b, alpha):
    branches = [(_bdot(fm.astype(BF16), wf_ref[...]), _bdot(o, wna_ref[...]))
                for fm, o, _, _, _ in slabs]
    merged = [(jax.nn.sigmoid(ga) * a + jax.nn.sigmoid(gb) * b).astype(BF16)
              for (_, _, ga, gb, _), (a, b) in zip(slabs, branches)]
    mixes = [_bdot(mg, wout_ref[...]) for mg in merged]
    return [_layer_norm(alpha * x + gate * mix, ln_g, ln_b)
            for (_, _, _, _, x), mix in zip(slabs, mixes)]


def _ctx_mix_kernel(x_ref, m_ref, win_ref, wf_ref, wna_ref, wout_ref, g_ref, b_ref,
                    cdft_ref, pdft_ref, *rest, alpha, n_alias, kv_slot):
    o_ref, k_ref, v_ref, z_ref, oatt_ref, fm_ref = rest[n_alias:]
    nb, seq, d = x_ref.shape
    rows = nb * seq
    x = x_ref[...].reshape(rows, d)
    shift = m_ref[0, 3:4, :]
    scale = m_ref[0, 4:5, :]
    gate = m_ref[0, 5:6, :]
    u = (x * (1.0 + scale) + shift).astype(BF16)
    for c0 in range(0, win_ref.shape[1], IN_CHUNK):
        z_ref[:, c0:c0 + IN_CHUNK] = _bdot(u, win_ref[:, c0:c0 + IN_CHUNK])

    cdft = cdft_ref[...].astype(BF16)
    pdft = pdft_ref[...].astype(BF16)
    q_off, k_off, v_off, f_off = 0, NA_WIDTH, 2 * NA_WIDTH, 3 * NA_WIDTH
    ga_off = f_off + FNET_WIDTH
    gb_off = ga_off + d
    for b in range(nb):
        r0 = b * seq
        for ref, off in ((k_ref, k_off), (v_ref, v_off)):
            t = z_ref[r0:r0 + seq, off:off + NA_WIDTH].T
            ref[b, kv_slot] = t.reshape(NA_HEADS, HEAD_DIM, seq)
        for slot in range(k_ref.shape[1]):
            if slot != kv_slot:
                k_ref[b, slot] = jnp.zeros(k_ref.shape[2:], F32)
                v_ref[b, slot] = jnp.zeros(v_ref.shape[2:], F32)
        zc = [_bdot(z_ref[r0:r0 + seq, f_off + g * FNET_GROUP_DIM:f_off + (g + 1) * FNET_GROUP_DIM]
                    .astype(BF16), cdft).astype(BF16) for g in range(FNET_GROUPS)]
        stacked = jnp.concatenate(
            [jnp.concatenate([z[:, :FNET_GROUP_DIM] for z in zc], axis=1),
             jnp.concatenate([z[:, FNET_GROUP_DIM:] for z in zc], axis=1)], axis=0)
        fm_ref[r0:r0 + seq, :] = _bdot(pdft, stacked)

    problems = []
    for b in range(nb):
        r0 = b * seq
        for hp in range(NA_WIDTH // LANES):
            c0 = hp * LANES
            q = (z_ref[r0:r0 + seq, q_off + c0:q_off + c0 + LANES] * ATTN_SCALE).astype(BF16)
            k = z_ref[r0:r0 + seq, k_off + c0:k_off + c0 + LANES].astype(BF16)
            v = z_ref[r0:r0 + seq, v_off + c0:v_off + c0 + LANES].astype(BF16)
            problems.append((b * seq, c0, (q, [k], [v], [None])))
    outs = _stacked_pair_attention([p for _, _, p in problems])
    for (r0, c0, _), o in zip(problems, outs):
        oatt_ref[r0:r0 + seq, c0:c0 + LANES] = o.astype(BF16)

    rows = [slice(b * seq, (b + 1) * seq) for b in range(nb)]
    slabs = [(fm_ref[rs, :], oatt_ref[rs, :], z_ref[rs, ga_off:ga_off + d],
              z_ref[rs, gb_off:gb_off + d], x_ref[b]) for b, rs in enumerate(rows)]
    outs = _merge_out(slabs, gate, wf_ref, wna_ref, wout_ref, g_ref[1:2, :], b_ref[1:2, :], alpha)
    for b, out in enumerate(outs):
        o_ref[b] = out


def _ctx_mix(x3, batch, mods_l, win, wf, wna, wout, ln_g, ln_b, alpha, layer, depth, kv_prev):
    _, seq, d = x3.shape
    nb = 2 if batch % 2 == 0 else 1
    in_width = win.shape[2]
    cc, cs = _dft_mats(FNET_GROUP_DIM)
    pc, ps = _dft_mats(seq)
    norm = 1.0 / math.sqrt(seq * FNET_GROUP_DIM)
    cdft = jnp.asarray(np.concatenate([cc, -cs], axis=1) * norm, F32)
    pdft = jnp.asarray(np.concatenate([pc, ps], axis=1), F32)
    kv_shape = jax.ShapeDtypeStruct((batch, depth, NA_HEADS, HEAD_DIM, seq), F32)
    if kv_prev is None:
        kv_spec = pl.BlockSpec((nb, depth, NA_HEADS, HEAD_DIM, seq), lambda i: (i, 0, 0, 0, 0))
        kv_slot = layer
    else:
        kv_spec = pl.BlockSpec((nb, 1, NA_HEADS, HEAD_DIM, seq), lambda i: (i, layer, 0, 0, 0))
        kv_slot = 0
    operands = [x3, mods_l, win, wf, wna, wout, ln_g, ln_b, cdft, pdft]
    in_specs = [
        pl.BlockSpec((nb, seq, d), lambda i: (i, 0, 0)),
        pl.BlockSpec((1, N_MOD, d), lambda i: (0, 0, 0)),
        _layer_spec(win, layer),
        _layer_spec(wf, layer),
        _layer_spec(wna, layer),
        _layer_spec(wout, layer),
        _const_spec(ln_g.shape),
        _const_spec(ln_b.shape),
        _const_spec(cdft.shape),
        _const_spec(pdft.shape),
    ]
    aliases = {}
    if kv_prev is not None:
        aliases = {len(operands): 1, len(operands) + 1: 2}
        operands += list(kv_prev)
        in_specs += [pl.BlockSpec(memory_space=pl.ANY)] * 2
    return pl.pallas_call(
        functools.partial(_ctx_mix_kernel, alpha=alpha, n_alias=len(aliases), kv_slot=kv_slot),
        grid=(batch // nb,),
        in_specs=in_specs,
        out_specs=[pl.BlockSpec((nb, seq, d), lambda i: (i, 0, 0)), kv_spec, kv_spec],
        out_shape=[jax.ShapeDtypeStruct((batch, seq, d), F32), kv_shape, kv_shape],
        input_output_aliases=aliases,
        scratch_shapes=[
            pltpu.VMEM((nb * seq, in_width), F32),
            pltpu.VMEM((nb * seq, NA_WIDTH), BF16),
            pltpu.VMEM((nb * seq, FNET_WIDTH), F32),
        ],
        compiler_params=_params(("arbitrary",)),
        name="ctx_mix",
    )(*operands)


def _in_proj_kernel(x_ref, m_ref, win_ref, q_ref, k_ref, v_ref, f_ref):
    x = x_ref[...]
    shift = m_ref[0, 3:4, :]
    scale = m_ref[0, 4:5, :]
    u = (x * (1.0 + scale) + shift).astype(BF16)
    cw = NA_WIDTH
    q_ref[...] = (_bdot(u, win_ref[:, 0:cw]) * ATTN_SCALE).astype(BF16)
    k_ref[...] = _bdot(u, win_ref[:, cw:2 * cw]).astype(BF16)
    v_ref[...] = _bdot(u, win_ref[:, 2 * cw:3 * cw]).astype(BF16)
    f_ref[...] = _bdot(u, win_ref[:, 3 * cw:3 * cw + FNET_WIDTH]).astype(BF16)


def _in_proj(x2, first_row, rows, mods_l, rows_per_group, group0, win, layer):
    d = x2.shape[1]
    tm = math.gcd(math.gcd(ROW_TILE, rows_per_group), first_row)
    tiles_per_group = rows_per_group // tm
    first_tile = first_row // tm
    cw = NA_WIDTH
    qkvf = 3 * NA_WIDTH + FNET_WIDTH
    row_spec = lambda w: pl.BlockSpec((tm, w), lambda i: (i, 0))
    return pl.pallas_call(
        _in_proj_kernel,
        grid=(rows // tm,),
        in_specs=[
            pl.BlockSpec((tm, d), lambda i: (first_tile + i, 0)),
            pl.BlockSpec((1, N_MOD, d), lambda i: (group0 + i // tiles_per_group, 0, 0)),
            _layer_spec(win, layer, cols=qkvf, col_block=0),
        ],
        out_specs=[row_spec(cw), row_spec(cw), row_spec(cw), row_spec(FNET_WIDTH)],
        out_shape=[
            jax.ShapeDtypeStruct((rows, cw), BF16),
            jax.ShapeDtypeStruct((rows, cw), BF16),
            jax.ShapeDtypeStruct((rows, cw), BF16),
            jax.ShapeDtypeStruct((rows, FNET_WIDTH), BF16),
        ],
        compiler_params=_params(("arbitrary",)),
        name="in_proj",
    )(x2, mods_l, win)


def _window_start_row(i, rows):
    return jnp.clip(Q_ROWS * i - WIN_ROWS // 2, 0, rows - K_ROWS)


def _nbr_attn_kernel(q_ref, k_ref, v_ref, ck_ref, cv_ref, *rest, rows):
    bias_refs, o_ref = rest[:-1], rest[-1]
    step = pl.program_id(1)
    n_keys = K_ROWS * GRID_W
    tq = Q_ROWS * GRID_W
    problems, places = [], []
    for j, bias_ref in enumerate(bias_refs):
        i = step * len(bias_refs) + j
        start = pl.multiple_of(_window_start_row(i, rows) * GRID_W, KEY_BLOCK)
        bias = (bias_ref[0, 0], bias_ref[1, 0])
        for b in range(q_ref.shape[0]):
            problems.append(
                (q_ref[b, j * tq:(j + 1) * tq, :],
                 [k_ref[b, pl.ds(start, n_keys), :], ck_ref[b]],
                 [v_ref[b, pl.ds(start, n_keys), :], cv_ref[b]],
                 [bias, None]))
            places.append((b, j))
    for (b, j), o in zip(places, _stacked_pair_attention(problems)):
        o_ref[b, j * tq:(j + 1) * tq, :] = o.astype(BF16)


def _bias_table_kernel(rpb_ref, ok_ref, o_ref, *, n_dr, n_dc, slot0, n_cols, starts):
    h = pl.program_id(0)
    base = h * (n_dr * n_dc)
    row = lax.broadcasted_iota(jnp.int32, (GRID_W, LANES), 0)
    lane = lax.broadcasted_iota(jnp.int32, (GRID_W, LANES), 1)
    rel = (lane & (GRID_W - 1)) - row + (WIN_COLS - 1)
    low = lane < GRID_W

    def column(dr_low, dr_high):
        tile = jnp.zeros((GRID_W, LANES), F32)
        valid = [0 <= dr < n_dr for dr in (dr_low, dr_high)]
        if not any(valid):
            return tile
        for d in range(n_dc):
            lo = rpb_ref[base + dr_low * n_dc + d] if valid[0] else 0.0
            hi = rpb_ref[base + dr_high * n_dc + d] if valid[1] else 0.0
            tile = jnp.where(rel == d, jnp.where(low, lo, hi), tile)
        return tile

    even = [column(2 * p - slot0, 2 * p + 1 - slot0) for p in range(n_cols)]
    swapped = [pltpu.roll(c, GRID_W, axis=1) for c in even]
    before = [jnp.zeros((GRID_W, LANES), F32)] + swapped[:-1]
    odd = [jnp.where(low, prev, cur) for prev, cur in zip(before, swapped)]
    builds = [even, odd]
    cols_per_slab = K_ROWS * GRID_W // LANES
    for t, start in enumerate(starts):
        shift = start % 2
        c0 = (start + shift) // 2
        window = jnp.concatenate(builds[shift][c0:c0 + cols_per_slab], axis=1)
        o_ref[0, t] = jnp.where(ok_ref[t] > 0.0, window, NEG_INF)


def _nbr_bias_table(rpb_all, rows):
    h, n_dr, n_dc = rpb_all.shape
    n_blocks = rows // Q_ROWS
    kr = min(WIN_ROWS, rows)
    variants = np.array([0, min(1, n_blocks - 1), n_blocks - 1])
    r = Q_ROWS * variants[:, None] + np.arange(Q_ROWS)[None, :]
    ks = np.clip(Q_ROWS * variants - WIN_ROWS // 2, 0, rows - K_ROWS)
    key_row = ks[:, None] + np.arange(K_ROWS)[None, :]
    r0 = np.clip(r - kr // 2, 0, rows - kr)
    row_ok = (key_row[:, None, :] >= r0[:, :, None]) & (key_row[:, None, :] < r0[:, :, None] + kr)
    c_idx = np.arange(GRID_W)
    c0 = np.clip(c_idx - WIN_COLS // 2, 0, GRID_W - WIN_COLS)
    col_ok = (c_idx[None, :] >= c0[:, None]) & (c_idx[None, :] < c0[:, None] + WIN_COLS)
    rel = c_idx[None, :] - c_idx[:, None] + (WIN_COLS - 1)
    assert rel[col_ok].min() >= 0 and rel[col_ok].max() < n_dc

    first_dr = (ks[:, None] - r + (WIN_ROWS - 1)).reshape(-1)
    slot0 = int(max(0, -first_dr.min()))
    starts = tuple(int(d) + slot0 for d in first_dr)
    n_cols = (max(starts) + 1 + K_ROWS + 1) // 2
    ok = (row_ok[:, :, None, :, None] & col_ok[None, None, :, None, :]).reshape(
        3 * Q_ROWS, GRID_W, K_ROWS * GRID_W).astype(np.float32)
    n_slabs = 3 * Q_ROWS
    table = pl.pallas_call(
        functools.partial(_bias_table_kernel, n_dr=n_dr, n_dc=n_dc, slot0=slot0, n_cols=n_cols,
                          starts=starts),
        grid=(h,),
        in_specs=[
            pl.BlockSpec(memory_space=pltpu.SMEM),
            _const_spec(ok.shape),
        ],
        out_specs=pl.BlockSpec((1, n_slabs, GRID_W, K_ROWS * GRID_W), lambda i: (i, 0, 0, 0)),
        out_shape=jax.ShapeDtypeStruct((h, n_slabs, GRID_W, K_ROWS * GRID_W), F32),
        compiler_params=_params(("arbitrary",)),
        name="bias_table",
    )(rpb_all.reshape(-1).astype(F32), jnp.asarray(ok))
    return table.reshape(h, 3, Q_ROWS * GRID_W, K_ROWS * GRID_W)


def _nbr_attn(q3, k3, v3, ck3, cv3, bias, layer):
    nb, seq, width = q3.shape
    rows = seq // GRID_W
    n_blocks = rows // Q_ROWS
    n_pairs = width // LANES
    past = ck3.shape[1]
    tq = Q_ROWS * GRID_W
    tk = K_ROWS * GRID_W

    per_step = Q_BLOCKS_PER_STEP if n_blocks % Q_BLOCKS_PER_STEP == 0 else 1

    def variant(i):
        return jnp.where(i == 0, 0, jnp.where(i == n_blocks - 1, 2, 1))

    def bias_spec(j):
        return pl.BlockSpec((2, 1, tq, tk),
                            lambda p, s: (layer * n_pairs + p, variant(s * per_step + j), 0, 0))

    return pl.pallas_call(
        functools.partial(_nbr_attn_kernel, rows=rows),
        grid=(n_pairs, n_blocks // per_step),
        in_specs=[
            pl.BlockSpec((nb, per_step * tq, LANES), lambda p, s: (0, s, p)),
            pl.BlockSpec((nb, seq, LANES), lambda p, s: (0, 0, p)),
            pl.BlockSpec((nb, seq, LANES), lambda p, s: (0, 0, p)),
            pl.BlockSpec((nb, past, LANES), lambda p, s: (0, 0, p)),
            pl.BlockSpec((nb, past, LANES), lambda p, s: (0, 0, p)),
        ] + [bias_spec(j) for j in range(per_step)],
        out_specs=pl.BlockSpec((nb, per_step * tq, LANES), lambda p, s: (0, s, p)),
        out_shape=jax.ShapeDtypeStruct((nb, seq, width), BF16),
        compiler_params=_params(("arbitrary", "arbitrary")),
        name="nbr_attn",
    )(q3, k3, v3, ck3, cv3, *([bias] * per_step))


def _fourier_kernel(x_ref, cdft_ref, d1_ref, d2_ref, twc_ref, tws_ref, o_ref,
                    zr_ref, zi_ref, tr_ref, ti_ref, *, n1, n2):
    gd = cdft_ref.shape[0]
    n_groups = x_ref.shape[3] // gd
    groups = range(n_groups)
    blk = range(SLAB_BLOCK)
    cdft = cdft_ref[...].astype(BF16)
    d1 = d1_ref[...].astype(BF16)
    d2 = d2_ref[...].astype(BF16)
    x = x_ref[0].reshape(n1 * n2, n_groups * gd)
    for g in groups:
        zc = _bdot(x[:, g * gd:(g + 1) * gd], cdft)
        zr_ref[g] = zc[:, :gd].reshape(n1, n2, gd)
        zi_ref[g] = zc[:, gd:].reshape(n1, n2, gd)

    def slabs(re_ref, im_ref, first):
        cols = pl.ds(pl.multiple_of(first, SLAB_BLOCK), SLAB_BLOCK)
        re = [jnp.swapaxes(re_ref[g, :, cols, :], 0, 1) for g in groups]
        im = [jnp.swapaxes(im_ref[g, :, cols, :], 0, 1) for g in groups]
        return jnp.concatenate(
            [jnp.concatenate([re[g][j], im[g][j]], axis=0) for j in blk for g in groups], axis=1)

    def stage1(cb, carry):
        first = cb * SLAB_BLOCK
        t = _bdot(d1, slabs(zr_ref, zi_ref, first).astype(BF16))
        for j in blk:
            cs = twc_ref[first + j]
            sn = tws_ref[first + j]
            for g in groups:
                col = (j * n_groups + g) * gd
                tr = t[:n1, col:col + gd]
                ti = t[n1:, col:col + gd]
                tr_ref[g, first + j] = tr * cs + ti * sn
                ti_ref[g, first + j] = ti * cs - tr * sn
        return carry

    lax.fori_loop(0, n2 // SLAB_BLOCK, stage1, 0)

    def stage2(kb, carry):
        first = kb * SLAB_BLOCK
        y = _bdot(d2, slabs(tr_ref, ti_ref, first).astype(BF16))
        rows = pl.ds(pl.multiple_of(first, SLAB_BLOCK), SLAB_BLOCK)
        for g in groups:
            yg = jnp.stack([y[:, (j * n_groups + g) * gd:(j * n_groups + g + 1) * gd] for j in blk])
            o_ref[0, :, rows, g * gd:(g + 1) * gd] = jnp.swapaxes(yg, 0, 1)
        return carry

    lax.fori_loop(0, n1 // SLAB_BLOCK, stage2, 0)


def _fourier(f3):
    nb, seq, width = f3.shape
    n2 = GRID_W
    n1 = seq // n2
    gd = FNET_GROUP_DIM
    cc, cs = _dft_mats(gd)
    c1, s1 = _dft_mats(n1)
    c2, s2 = _dft_mats(n2)
    norm = 1.0 / math.sqrt(seq * gd)
    cdft = jnp.asarray(np.concatenate([cc, -cs], axis=1) * norm, F32)
    d1 = jnp.asarray(np.block([[c1, s1], [-s1, c1]]), F32)
    d2 = jnp.asarray(np.concatenate([c2, s2], axis=1), F32)
    ang = 2.0 * np.pi * ((np.arange(n2)[:, None] * np.arange(n1)[None, :]) % seq) / seq
    twc = jnp.asarray(np.broadcast_to(np.cos(ang)[:, :, None], (n2, n1, gd)), F32)
    tws = jnp.asarray(np.broadcast_to(np.sin(ang)[:, :, None], (n2, n1, gd)), F32)
    n_groups = FOURIER_COLS // gd
    out = pl.pallas_call(
        functools.partial(_fourier_kernel, n1=n1, n2=n2),
        grid=(nb, width // FOURIER_COLS),
        in_specs=[
            pl.BlockSpec((1, n1, n2, FOURIER_COLS), lambda b, g: (b, 0, 0, g)),
            _const_spec(cdft.shape),
            _const_spec(d1.shape),
            _const_spec(d2.shape),
            _const_spec(twc.shape),
            _const_spec(tws.shape),
        ],
        out_specs=pl.BlockSpec((1, n2, n1, FOURIER_COLS), lambda b, g: (b, 0, 0, g)),
        out_shape=jax.ShapeDtypeStruct((nb, n2, n1, width), F32),
        scratch_shapes=[pltpu.VMEM((n_groups, n1, n2, gd), F32)] * 2
        + [pltpu.VMEM((n_groups, n2, n1, gd), F32)] * 2,
        compiler_params=_params(("arbitrary", "arbitrary")),
        name="fourier",
    )(f3.reshape(nb, n1, n2, width), cdft, d1, d2, twc, tws)
    return out.reshape(nb, seq, width)


def _merge_kernel(fm_ref, o_ref_in, x_ref, m_ref, wgate_ref, wf_ref, wna_ref, wout_ref,
                  g_ref, b_ref, out_ref, *, alpha):
    tm, d = x_ref.shape
    shift = m_ref[0, 3:4, :]
    scale = m_ref[0, 4:5, :]
    gate = m_ref[0, 5:6, :]
    sr = min(tm, MERGE_SUB_ROWS)
    rows = [slice(r0, r0 + sr) for r0 in range(0, tm, sr)]
    xs = [x_ref[rs, :] for rs in rows]
    us = [(x * (1.0 + scale) + shift).astype(BF16) for x in xs]
    slabs = [(fm_ref[rs, :], o_ref_in[rs, :], _bdot(u, wgate_ref[:, :d]), _bdot(u, wgate_ref[:, d:]), x)
             for rs, u, x in zip(rows, us, xs)]
    outs = _merge_out(slabs, gate, wf_ref, wna_ref, wout_ref, g_ref[1:2, :], b_ref[1:2, :], alpha)
    for rs, out in zip(rows, outs):
        out_ref[rs, :] = out


def _merge(fm2, o2, x2, x_first_row, mods_l, rows_per_group, group0, win, wf, wna, wout, layer,
           ln_g, ln_b, alpha):
    rows, d = fm2.shape[0], x2.shape[1]
    tm = math.gcd(math.gcd(MERGE_ROW_TILE, rows_per_group), x_first_row)
    tiles_per_group = rows_per_group // tm
    x_first_tile = x_first_row // tm
    row_spec = lambda w: pl.BlockSpec((tm, w), lambda i: (i, 0))
    assert win.shape[2] == 2 * (2 * d), "gate columns must be the second half of w_in"
    return pl.pallas_call(
        functools.partial(_merge_kernel, alpha=alpha),
        grid=(rows // tm,),
        in_specs=[
            row_spec(fm2.shape[1]), row_spec(o2.shape[1]),
            pl.BlockSpec((tm, d), lambda i: (x_first_tile + i, 0)),
            pl.BlockSpec((1, N_MOD, d), lambda i: (group0 + i // tiles_per_group, 0, 0)),
            _layer_spec(win, layer, cols=2 * d, col_block=1),
            _layer_spec(wf, layer), _layer_spec(wna, layer), _layer_spec(wout, layer),
            _const_spec(ln_g.shape), _const_spec(ln_b.shape),
        ],
        out_specs=row_spec(d),
        out_shape=jax.ShapeDtypeStruct((rows, d), F32),
        compiler_params=_params(("arbitrary",)),
        name="merge",
    )(fm2, o2, x2, mods_l, win, wf, wna, wout, ln_g, ln_b)


def kernel(x_prompt, x_sample, cache_k, cache_v, c, c_ctx, w_ada, b_ada, ln_g, ln_b, w_ff1_up,
           w_ff1_down, w_in, rpb, w_fourier, w_na_out, w_out, w_ff2_up, w_ff2_down):
    batch, seq, d = x_prompt.shape
    dec_batch, dec_seq, _ = x_sample.shape
    depth = w_ada.shape[0]
    alpha = (2 * depth) ** 0.25
    rows_lat = dec_seq // GRID_W

    cvec = jnp.zeros((MOD_ROWS, d), F32).at[0].set(c_ctx).at[1:1 + dec_batch].set(c)
    mods = _mods(cvec, w_ada, b_ada).reshape(depth, MOD_ROWS, N_MOD, d)

    y_p = x_prompt.reshape(batch * seq, d)
    y_s = x_sample.reshape(dec_batch * dec_seq, d)
    up1, dn1 = w_ff1_up.astype(BF16), w_ff1_down.astype(BF16)
    up2, dn2 = w_ff2_up.astype(BF16), w_ff2_down.astype(BF16)
    win = w_in.astype(BF16)
    wf = w_fourier.astype(BF16)
    wna = w_na_out.astype(BF16)
    wout = w_out.astype(BF16)
    ck = cache_k.transpose(1, 0, 3, 2, 4).reshape(depth, dec_batch, -1, NA_WIDTH).astype(BF16)
    cv = cache_v.transpose(1, 0, 3, 2, 4).reshape(depth, dec_batch, -1, NA_WIDTH).astype(BF16)

    bias = _nbr_bias_table(rpb.reshape((-1,) + rpb.shape[2:]), rows_lat)

    rows_p, rows_s = batch * seq, dec_batch * dec_seq

    def segments(src_p, first_p, src_s, first_s):
        return [(src_p, first_p, rows_p, 0, rows_p), (src_s, first_s, rows_s, 1, dec_seq)]

    kv = None
    pending = segments(y_p, 0, y_s, 0)
    for l in range(depth):
        m_l = mods[l]
        g_l, b_l = ln_g[l], ln_b[l]
        y = _ffn(pending, m_l, up1, dn1, l, g_l, b_l, 0, alpha)

        q2, k2, v2, f2 = _in_proj(y, rows_p, rows_s, m_l, dec_seq, 1, win, l)
        to3 = lambda t: t.reshape(dec_batch, dec_seq, t.shape[-1])
        o3 = _nbr_attn(to3(q2), to3(k2), to3(v2), ck[l], cv[l], bias, l)
        fm3 = _fourier(to3(f2))
        y_s = _merge(fm3.reshape(-1, FNET_WIDTH), o3.reshape(-1, NA_WIDTH), y, rows_p, m_l,
                     dec_seq, 1, win, wf, wna, wout, l, g_l, b_l, alpha)

        y, y_s = lax.optimization_barrier((y, y_s))
        y_p3, new_k, new_v = _ctx_mix(y.reshape(-1, seq, d), batch, m_l, win, wf, wna, wout,
                                      g_l, b_l, alpha, l, depth, kv)
        kv = (new_k, new_v)

        mixed = segments(y_p3.reshape(rows_p, d), 0, y_s, 0)
        if l + 1 < depth:
            y = _ffn(mixed, m_l, up2, dn2, l, g_l, b_l, 2, alpha)
            pending = segments(y, 0, y, rows_p)
        else:
            y_p = _ffn(mixed[:1], m_l, up2, dn2, l, g_l, b_l, 2, alpha)
            y_s = _ffn(mixed[1:], m_l, up2, dn2, l, g_l, b_l, 2, alpha)

    new_k, new_v = (t.transpose(0, 1, 2, 4, 3) for t in kv)
    return (y_p.reshape(batch, seq, d), y_s.reshape(dec_batch, dec_seq, d), new_k, new_v)
```

```python
import functools
import math

import numpy as np
import jax
import jax.numpy as jnp
from jax import lax
from jax.experimental import pallas as pl
from jax.experimental.pallas import tpu as pltpu

F32 = jnp.float32
BF16 = jnp.bfloat16

HEAD_DIM = 64
NA_HEADS = 8
NA_WIDTH = NA_HEADS * HEAD_DIM
FNET_GROUPS = 4
FNET_GROUP_DIM = 128
FNET_WIDTH = FNET_GROUPS * FNET_GROUP_DIM
GRID_W = 64
WIN_ROWS = 8
WIN_COLS = 16
N_SUB = 3
N_MOD = 3 * N_SUB
ATTN_SCALE = HEAD_DIM ** -0.5
LOG2_E = math.log2(math.e)
Q_SCALE = ATTN_SCALE * LOG2_E
LN_EPS = 1e-5
NEG_INF = -1e30

LANES = 128
MXU_DIM = 256
V7X_VMEM_BYTES = 64 * 1024 * 1024
VMEM_LIMIT_BYTES = V7X_VMEM_BYTES * 7 // 8

ROW_TILE = 512
FFN_ROW_TILE = 1024
FFN_SUB_ROWS = 512
MERGE_ROW_TILE = 1024
MERGE_SUB_ROWS = 256
FF_CHUNK = MXU_DIM
Q_ROWS = 4
K_ROWS = Q_ROWS + WIN_ROWS
KEY_BLOCK = GRID_W * math.gcd(Q_ROWS, WIN_ROWS // 2)
Q_BLOCKS_PER_STEP = 4
IN_CHUNK = 2 * MXU_DIM
MOD_ROWS = 8
MODS_K_ROWS = 256
FOURIER_COLS = 2 * FNET_GROUP_DIM
SLAB_BLOCK = 8


def _params(sem):
    return pltpu.CompilerParams(dimension_semantics=sem, vmem_limit_bytes=VMEM_LIMIT_BYTES)


def _const_spec(shape):
    nd = len(shape)
    return pl.BlockSpec(shape, lambda *_: (0,) * nd, pipeline_mode=pl.Buffered(1))


def _layer_spec(w, layer, cols=None, col_block=0):
    _, r, n = w.shape
    cols = n if cols is None else cols
    return pl.BlockSpec((None, r, cols), lambda *_: (layer, 0, col_block),
                        pipeline_mode=pl.Buffered(1))


def _layer_norm(r, g, b):
    mu = jnp.mean(r, axis=-1, keepdims=True)
    d = r - mu
    var = jnp.mean(d * d, axis=-1, keepdims=True)
    return d * lax.rsqrt(var + LN_EPS) * g + b


def _bdot(a, b):
    return jnp.dot(a, b, preferred_element_type=F32)


def _dot_nt(a, b):
    return lax.dot_general(a, b, (((1,), (1,)), ((), ())), preferred_element_type=F32)


def _dft_mats(n):
    k = np.arange(n)
    ang = 2.0 * np.pi * ((k[:, None] * k[None, :]) % n) / n
    return np.cos(ang), np.sin(ang)


def _mods_kernel(c_ref, w_ref, b_ref, o_ref):
    k = pl.program_id(1)
    c = c_ref[...]
    s = (c * jax.nn.sigmoid(c)).astype(BF16)
    part = _bdot(s, w_ref[0].astype(BF16))

    @pl.when(k == 0)
    def _():
        o_ref[0] = part + b_ref[0]

    @pl.when(k > 0)
    def _():
        o_ref[0] += part


def _mods(cvec, w_ada, b_ada):
    depth, d, n = w_ada.shape
    tk = MODS_K_ROWS
    return pl.pallas_call(
        _mods_kernel,
        grid=(depth, d // tk),
        in_specs=[
            pl.BlockSpec((MOD_ROWS, tk), lambda l, k: (0, k)),
            pl.BlockSpec((1, tk, n), lambda l, k: (l, k, 0)),
            pl.BlockSpec((1, 1, n), lambda l, k: (l, 0, 0)),
        ],
        out_specs=pl.BlockSpec((1, MOD_ROWS, n), lambda l, k: (l, 0, 0)),
        out_shape=jax.ShapeDtypeStruct((depth, MOD_ROWS, n), F32),
        compiler_params=_params(("arbitrary", "arbitrary")),
        name="mods",
    )(cvec, w_ada, b_ada.reshape(depth, 1, n))


def _ffn_kernel(*refs, sub, alpha, seg_starts):
    n_seg = len(seg_starts)
    x_refs = refs[:n_seg]
    m_ref, wup_ref, wdn_ref, g_ref, b_ref, o_ref, act_ref = refs[n_seg:]
    step = pl.program_id(0)

    def read_x(rs):
        x = x_refs[-1][rs, :]
        for s in range(n_seg - 2, -1, -1):
            x = jnp.where(step < seg_starts[s + 1], x_refs[s][rs, :], x)
        return x

    shift = m_ref[0, 3 * sub:3 * sub + 1, :]
    scale = m_ref[0, 3 * sub + 1:3 * sub + 2, :]
    gate = m_ref[0, 3 * sub + 2:3 * sub + 3, :]
    ff = wdn_ref.shape[0]
    tm = o_ref.shape[0]
    sr = min(tm, FFN_SUB_ROWS)
    slabs = [slice(r0, r0 + sr) for r0 in range(0, tm, sr)]
    us = [(read_x(rs) * (1.0 + scale) + shift).astype(BF16) for rs in slabs]
    for c0 in range(0, ff, FF_CHUNK):
        for rs, u in zip(slabs, us):
            a = _bdot(u, wup_ref[:, c0:c0 + FF_CHUNK])
            g = _bdot(u, wup_ref[:, ff + c0:ff + c0 + FF_CHUNK])
            act_ref[rs, c0:c0 + FF_CHUNK] = ((g * jax.nn.sigmoid(g)) * a).astype(BF16)
    for rs in slabs:
        y = _bdot(act_ref[rs, :], wdn_ref[...])
        r = alpha * read_x(rs) + (0.5 * gate) * y
        o_ref[rs, :] = _layer_norm(r, g_ref[sub:sub + 1, :], b_ref[sub:sub + 1, :])


def _ffn(segments, mods_l, wup, wdn, layer, ln_g, ln_b, sub, alpha):
    d = segments[0][0].shape[1]
    tm = FFN_ROW_TILE
    for _, first_row, n_rows, _, rows_per_group in segments:
        tm = math.gcd(math.gcd(tm, rows_per_group), math.gcd(n_rows, first_row))
    starts, n_steps = [], 0
    for _, _, n_rows, _, _ in segments:
        starts.append(n_steps)
        n_steps += n_rows // tm

    def x_spec(s):
        _, first_row, n_rows, _, _ = segments[s]
        first, count = first_row // tm, n_rows // tm
        return pl.BlockSpec((tm, d), lambda i: (first + jnp.clip(i - starts[s], 0, count - 1), 0))

    def group(i):
        g = None
        for s in range(len(segments) - 1, -1, -1):
            _, _, _, group0, rows_per_group = segments[s]
            g_s = group0 + jnp.maximum(i - starts[s], 0) // (rows_per_group // tm)
            g = g_s if g is None else jnp.where(i < starts[s + 1], g_s, g)
        return g

    return pl.pallas_call(
        functools.partial(_ffn_kernel, sub=sub, alpha=alpha, seg_starts=tuple(starts)),
        grid=(n_steps,),
        in_specs=[x_spec(s) for s in range(len(segments))] + [
            pl.BlockSpec((1, N_MOD, d), lambda i: (group(i), 0, 0)),
            _layer_spec(wup, layer),
            _layer_spec(wdn, layer),
            _const_spec(ln_g.shape),
            _const_spec(ln_b.shape),
        ],
        out_specs=pl.BlockSpec((tm, d), lambda i: (i, 0)),
        out_shape=jax.ShapeDtypeStruct((n_steps * tm, d), F32),
        scratch_shapes=[pltpu.VMEM((tm, wdn.shape[1]), BF16)],
        compiler_params=_params(("arbitrary",)),
        name=f"ffn{sub}",
    )(*[seg[0] for seg in segments], mods_l, wup, wdn, ln_g, ln_b)


def _row_reduce(arrays, combine, reduce):
    acc = None
    for a in arrays:
        for c0 in range(0, a.shape[1], LANES):
            chunk = a[:, c0:c0 + LANES]
            acc = chunk if acc is None else combine(acc, chunk)
    return reduce(acc, axis=-1, keepdims=True)


def _stacked_pair_attention(problems):
    lane = lax.broadcasted_iota(jnp.int32, (1, LANES), 1)
    scores = []
    for q, k_list, _, bias_list in problems:
        zero = jnp.zeros_like(q)
        q2 = jnp.concatenate([jnp.where(lane < HEAD_DIM, q, zero),
                              jnp.where(lane >= HEAD_DIM, q, zero)], axis=0)
        s = []
        for k, bias in zip(k_list, bias_list):
            sk = _dot_nt(q2, k)
            if bias is not None:
                sk = sk + jnp.concatenate(bias, axis=0)
            s.append(sk)
        scores.append(s)
    maxes = [_row_reduce(s, jnp.maximum, jnp.max) for s in scores]
    probs = [[jnp.exp2(sk - m) for sk in s] for s, m in zip(scores, maxes)]
    dens = [_row_reduce(p, jnp.add, jnp.sum) for p in probs]
    outs = []
    for (q, _, v_list, _), p, den in zip(problems, probs, dens):
        o = None
        for pk, v in zip(p, v_list):
            pv = _bdot(pk.astype(BF16), v)
            o = pv if o is None else o + pv
        o = o / den
        m_rows = q.shape[0]
        outs.append(jnp.where(lane < HEAD_DIM, o[:m_rows], o[m_rows:]))
    return outs


def _merge_out(slabs, gate, wf_ref, wna_ref, wout_ref, ln_g, ln_b, alpha):
    branches = [(_bdot(fm.astype(BF16), wf_ref[...]), _bdot(o, wna_ref[...]))
                for fm, o, _, _, _ in slabs]
    merged = [(jax.nn.sigmoid(ga) * a + jax.nn.sigmoid(gb) * b).astype(BF16)
              for (_, _, ga, gb, _), (a, b) in zip(slabs, branches)]
    mixes = [_bdot(mg, wout_ref[...]) for mg in merged]
    return [_layer_norm(alpha * x + gate * mix, ln_g, ln_b)
            for (_, _, _, _, x), mix in zip(slabs, mixes)]


def _ctx_mix_kernel(x_ref, m_ref, win_ref, wf_ref, wna_ref, wout_ref, g_ref, b_ref,
                    cdft_ref, pdft_ref, *rest, alpha, n_alias, kv_slot):
    o_ref, k_ref, v_ref, z_ref, oatt_ref, fm_ref = rest[n_alias:]
    nb, seq, d = x_ref.shape
    rows = nb * seq
    x = x_ref[...].reshape(rows, d)
    shift = m_ref[0, 3:4, :]
    scale = m_ref[0, 4:5, :]
    gate = m_ref[0, 5:6, :]
    u = (x * (1.0 + scale) + shift).astype(BF16)
    for c0 in range(0, win_ref.shape[1], IN_CHUNK):
        z_ref[:, c0:c0 + IN_CHUNK] = _bdot(u, win_ref[:, c0:c0 + IN_CHUNK])

    cdft = cdft_ref[...].astype(BF16)
    pdft = pdft_ref[...].astype(BF16)
    q_off, k_off, v_off, f_off = 0, NA_WIDTH, 2 * NA_WIDTH, 3 * NA_WIDTH
    ga_off = f_off + FNET_WIDTH
    gb_off = ga_off + d
    for b in range(nb):
        r0 = b * seq
        for ref, off in ((k_ref, k_off), (v_ref, v_off)):
            t = z_ref[r0:r0 + seq, off:off + NA_WIDTH].T
            ref[b, kv_slot] = t.reshape(NA_HEADS, HEAD_DIM, seq)
        for slot in range(k_ref.shape[1]):
            if slot != kv_slot:
                k_ref[b, slot] = jnp.zeros(k_ref.shape[2:], F32)
                v_ref[b, slot] = jnp.zeros(v_ref.shape[2:], F32)
        zc = [_bdot(z_ref[r0:r0 + seq, f_off + g * FNET_GROUP_DIM:f_off + (g + 1) * FNET_GROUP_DIM]
                    .astype(BF16), cdft).astype(BF16) for g in range(FNET_GROUPS)]
        stacked = jnp.concatenate(
            [jnp.concatenate([z[:, :FNET_GROUP_DIM] for z in zc], axis=1),
             jnp.concatenate([z[:, FNET_GROUP_DIM:] for z in zc], axis=1)], axis=0)
        fm_ref[r0:r0 + seq, :] = _bdot(pdft, stacked)

    problems = []
    for b in range(nb):
        r0 = b * seq
        for hp in range(NA_WIDTH // LANES):
            c0 = hp * LANES
            q = (z_ref[r0:r0 + seq, q_off + c0:q_off + c0 + LANES] * Q_SCALE).astype(BF16)
            k = z_ref[r0:r0 + seq, k_off + c0:k_off + c0 + LANES].astype(BF16)
            v = z_ref[r0:r0 + seq, v_off + c0:v_off + c0 + LANES].astype(BF16)
            problems.append((b * seq, c0, (q, [k], [v], [None])))
    outs = _stacked_pair_attention([p for _, _, p in problems])
    for (r0, c0, _), o in zip(problems, outs):
        oatt_ref[r0:r0 + seq, c0:c0 + LANES] = o.astype(BF16)

    rows = [slice(b * seq, (b + 1) * seq) for b in range(nb)]
    slabs = [(fm_ref[rs, :], oatt_ref[rs, :], z_ref[rs, ga_off:ga_off + d],
              z_ref[rs, gb_off:gb_off + d], x_ref[b]) for b, rs in enumerate(rows)]
    outs = _merge_out(slabs, gate, wf_ref, wna_ref, wout_ref, g_ref[1:2, :], b_ref[1:2, :], alpha)
    for b, out in enumerate(outs):
        o_ref[b] = out


def _ctx_mix(x3, batch, mods_l, win, wf, wna, wout, ln_g, ln_b, alpha, layer, depth, kv_prev):
    _, seq, d = x3.shape
    nb = 2 if batch % 2 == 0 else 1
    in_width = win.shape[2]
    cc, cs = _dft_mats(FNET_GROUP_DIM)
    pc, ps = _dft_mats(seq)
    norm = 1.0 / math.sqrt(seq * FNET_GROUP_DIM)
    cdft = jnp.asarray(np.concatenate([cc, -cs], axis=1) * norm, F32)
    pdft = jnp.asarray(np.concatenate([pc, ps], axis=1), F32)
    kv_shape = jax.ShapeDtypeStruct((batch, depth, NA_HEADS, HEAD_DIM, seq), F32)
    if kv_prev is None:
        kv_spec = pl.BlockSpec((nb, depth, NA_HEADS, HEAD_DIM, seq), lambda i: (i, 0, 0, 0, 0))
        kv_slot = layer
    else:
        kv_spec = pl.BlockSpec((nb, 1, NA_HEADS, HEAD_DIM, seq), lambda i: (i, layer, 0, 0, 0))
        kv_slot = 0
    operands = [x3, mods_l, win, wf, wna, wout, ln_g, ln_b, cdft, pdft]
    in_specs = [
        pl.BlockSpec((nb, seq, d), lambda i: (i, 0, 0)),
        pl.BlockSpec((1, N_MOD, d), lambda i: (0, 0, 0)),
        _layer_spec(win, layer),
        _layer_spec(wf, layer),
        _layer_spec(wna, layer),
        _layer_spec(wout, layer),
        _const_spec(ln_g.shape),
        _const_spec(ln_b.shape),
        _const_spec(cdft.shape),
        _const_spec(pdft.shape),
    ]
    aliases = {}
    if kv_prev is not None:
        aliases = {len(operands): 1, len(operands) + 1: 2}
        operands += list(kv_prev)
        in_specs += [pl.BlockSpec(memory_space=pl.ANY)] * 2
    return pl.pallas_call(
        functools.partial(_ctx_mix_kernel, alpha=alpha, n_alias=len(aliases), kv_slot=kv_slot),
        grid=(batch // nb,),
        in_specs=in_specs,
        out_specs=[pl.BlockSpec((nb, seq, d), lambda i: (i, 0, 0)), kv_spec, kv_spec],
        out_shape=[jax.ShapeDtypeStruct((batch, seq, d), F32), kv_shape, kv_shape],
        input_output_aliases=aliases,
        scratch_shapes=[
            pltpu.VMEM((nb * seq, in_width), F32),
            pltpu.VMEM((nb * seq, NA_WIDTH), BF16),
            pltpu.VMEM((nb * seq, FNET_WIDTH), F32),
        ],
        compiler_params=_params(("arbitrary",)),
        name="ctx_mix",
    )(*operands)


def _in_proj_kernel(x_ref, m_ref, win_ref, q_ref, k_ref, v_ref, f_ref):
    x = x_ref[...]
    shift = m_ref[0, 3:4, :]
    scale = m_ref[0, 4:5, :]
    u = (x * (1.0 + scale) + shift).astype(BF16)
    cw = NA_WIDTH
    q_ref[...] = (_bdot(u, win_ref[:, 0:cw]) * Q_SCALE).astype(BF16)
    k_ref[...] = _bdot(u, win_ref[:, cw:2 * cw]).astype(BF16)
    v_ref[...] = _bdot(u, win_ref[:, 2 * cw:3 * cw]).astype(BF16)
    f_ref[...] = _bdot(u, win_ref[:, 3 * cw:3 * cw + FNET_WIDTH]).astype(BF16)


def _in_proj(x2, first_row, rows, mods_l, rows_per_group, group0, win, layer):
    d = x2.shape[1]
    tm = math.gcd(math.gcd(ROW_TILE, rows_per_group), first_row)
    tiles_per_group = rows_per_group // tm
    first_tile = first_row // tm
    cw = NA_WIDTH
    qkvf = 3 * NA_WIDTH + FNET_WIDTH
    row_spec = lambda w: pl.BlockSpec((tm, w), lambda i: (i, 0))
    return pl.pallas_call(
        _in_proj_kernel,
        grid=(rows // tm,),
        in_specs=[
            pl.BlockSpec((tm, d), lambda i: (first_tile + i, 0)),
            pl.BlockSpec((1, N_MOD, d), lambda i: (group0 + i // tiles_per_group, 0, 0)),
            _layer_spec(win, layer, cols=qkvf, col_block=0),
        ],
        out_specs=[row_spec(cw), row_spec(cw), row_spec(cw), row_spec(FNET_WIDTH)],
        out_shape=[
            jax.ShapeDtypeStruct((rows, cw), BF16),
            jax.ShapeDtypeStruct((rows, cw), BF16),
            jax.ShapeDtypeStruct((rows, cw), BF16),
            jax.ShapeDtypeStruct((rows, FNET_WIDTH), BF16),
        ],
        compiler_params=_params(("arbitrary",)),
        name="in_proj",
    )(x2, mods_l, win)


def _window_start_row(i, rows):
    return jnp.clip(Q_ROWS * i - WIN_ROWS // 2, 0, rows - K_ROWS)


def _nbr_attn_kernel(q_ref, k_ref, v_ref, ck_ref, cv_ref, *rest, rows):
    bias_refs, o_ref = rest[:-1], rest[-1]
    step = pl.program_id(1)
    n_keys = K_ROWS * GRID_W
    tq = Q_ROWS * GRID_W
    problems, places = [], []
    for j, bias_ref in enumerate(bias_refs):
        i = step * len(bias_refs) + j
        start = pl.multiple_of(_window_start_row(i, rows) * GRID_W, KEY_BLOCK)
        bias = (bias_ref[0, 0], bias_ref[1, 0])
        for b in range(q_ref.shape[0]):
            problems.append(
                (q_ref[b, j * tq:(j + 1) * tq, :],
                 [k_ref[b, pl.ds(start, n_keys), :], ck_ref[b]],
                 [v_ref[b, pl.ds(start, n_keys), :], cv_ref[b]],
                 [bias, None]))
            places.append((b, j))
    for (b, j), o in zip(places, _stacked_pair_attention(problems)):
        o_ref[b, j * tq:(j + 1) * tq, :] = o.astype(BF16)


def _bias_table_kernel(rpb_ref, ok_ref, o_ref, *, n_dr, n_dc, slot0, n_cols, starts):
    h = pl.program_id(0)
    base = h * (n_dr * n_dc)
    row = lax.broadcasted_iota(jnp.int32, (GRID_W, LANES), 0)
    lane = lax.broadcasted_iota(jnp.int32, (GRID_W, LANES), 1)
    rel = (lane & (GRID_W - 1)) - row + (WIN_COLS - 1)
    low = lane < GRID_W

    def column(dr_low, dr_high):
        tile = jnp.zeros((GRID_W, LANES), F32)
        valid = [0 <= dr < n_dr for dr in (dr_low, dr_high)]
        if not any(valid):
            return tile
        for d in range(n_dc):
            lo = rpb_ref[base + dr_low * n_dc + d] if valid[0] else 0.0
            hi = rpb_ref[base + dr_high * n_dc + d] if valid[1] else 0.0
            tile = jnp.where(rel == d, jnp.where(low, lo, hi), tile)
        return tile

    even = [column(2 * p - slot0, 2 * p + 1 - slot0) for p in range(n_cols)]
    swapped = [pltpu.roll(c, GRID_W, axis=1) for c in even]
    before = [jnp.zeros((GRID_W, LANES), F32)] + swapped[:-1]
    odd = [jnp.where(low, prev, cur) for prev, cur in zip(before, swapped)]
    builds = [even, odd]
    cols_per_slab = K_ROWS * GRID_W // LANES
    for t, start in enumerate(starts):
        shift = start % 2
        c0 = (start + shift) // 2
        window = jnp.concatenate(builds[shift][c0:c0 + cols_per_slab], axis=1)
        o_ref[0, t] = jnp.where(ok_ref[t] > 0.0, window * LOG2_E, NEG_INF)


def _nbr_bias_table(rpb_all, rows):
    h, n_dr, n_dc = rpb_all.shape
    n_blocks = rows // Q_ROWS
    kr = min(WIN_ROWS, rows)
    variants = np.array([0, min(1, n_blocks - 1), n_blocks - 1])
    r = Q_ROWS * variants[:, None] + np.arange(Q_ROWS)[None, :]
    ks = np.clip(Q_ROWS * variants - WIN_ROWS // 2, 0, rows - K_ROWS)
    key_row = ks[:, None] + np.arange(K_ROWS)[None, :]
    r0 = np.clip(r - kr // 2, 0, rows - kr)
    row_ok = (key_row[:, None, :] >= r0[:, :, None]) & (key_row[:, None, :] < r0[:, :, None] + kr)
    c_idx = np.arange(GRID_W)
    c0 = np.clip(c_idx - WIN_COLS // 2, 0, GRID_W - WIN_COLS)
    col_ok = (c_idx[None, :] >= c0[:, None]) & (c_idx[None, :] < c0[:, None] + WIN_COLS)
    rel = c_idx[None, :] - c_idx[:, None] + (WIN_COLS - 1)
    assert rel[col_ok].min() >= 0 and rel[col_ok].max() < n_dc

    first_dr = (ks[:, None] - r + (WIN_ROWS - 1)).reshape(-1)
    slot0 = int(max(0, -first_dr.min()))
    starts = tuple(int(d) + slot0 for d in first_dr)
    n_cols = (max(starts) + 1 + K_ROWS + 1) // 2
    ok = (row_ok[:, :, None, :, None] & col_ok[None, None, :, None, :]).reshape(
        3 * Q_ROWS, GRID_W, K_ROWS * GRID_W).astype(np.float32)
    n_slabs = 3 * Q_ROWS
    table = pl.pallas_call(
        functools.partial(_bias_table_kernel, n_dr=n_dr, n_dc=n_dc, slot0=slot0, n_cols=n_cols,
                          starts=starts),
        grid=(h,),
        in_specs=[
            pl.BlockSpec(memory_space=pltpu.SMEM),
            _const_spec(ok.shape),
        ],
        out_specs=pl.BlockSpec((1, n_slabs, GRID_W, K_ROWS * GRID_W), lambda i: (i, 0, 0, 0)),
        out_shape=jax.ShapeDtypeStruct((h, n_slabs, GRID_W, K_ROWS * GRID_W), F32),
        compiler_params=_params(("arbitrary",)),
        name="bias_table",
    )(rpb_all.reshape(-1).astype(F32), jnp.asarray(ok))
    return table.reshape(h, 3, Q_ROWS * GRID_W, K_ROWS * GRID_W)


def _nbr_attn(q3, k3, v3, ck3, cv3, bias, layer):
    nb, seq, width = q3.shape
    rows = seq // GRID_W
    n_blocks = rows // Q_ROWS
    n_pairs = width // LANES
    past = ck3.shape[1]
    tq = Q_ROWS * GRID_W
    tk = K_ROWS * GRID_W

    per_step = Q_BLOCKS_PER_STEP if n_blocks % Q_BLOCKS_PER_STEP == 0 else 1

    def variant(i):
        return jnp.where(i == 0, 0, jnp.where(i == n_blocks - 1, 2, 1))

    def bias_spec(j):
        return pl.BlockSpec((2, 1, tq, tk),
                            lambda p, s: (layer * n_pairs + p, variant(s * per_step + j), 0, 0))

    return pl.pallas_call(
        functools.partial(_nbr_attn_kernel, rows=rows),
        grid=(n_pairs, n_blocks // per_step),
        in_specs=[
            pl.BlockSpec((nb, per_step * tq, LANES), lambda p, s: (0, s, p)),
            pl.BlockSpec((nb, seq, LANES), lambda p, s: (0, 0, p)),
            pl.BlockSpec((nb, seq, LANES), lambda p, s: (0, 0, p)),
            pl.BlockSpec((nb, past, LANES), lambda p, s: (0, 0, p)),
            pl.BlockSpec((nb, past, LANES), lambda p, s: (0, 0, p)),
        ] + [bias_spec(j) for j in range(per_step)],
        out_specs=pl.BlockSpec((nb, per_step * tq, LANES), lambda p, s: (0, s, p)),
        out_shape=jax.ShapeDtypeStruct((nb, seq, width), BF16),
        compiler_params=_params(("arbitrary", "arbitrary")),
        name="nbr_attn",
    )(q3, k3, v3, ck3, cv3, *([bias] * per_step))


def _fourier_kernel(x_ref, cdft_ref, d1_ref, d2_ref, twc_ref, tws_ref, o_ref,
                    zr_ref, zi_ref, tr_ref, ti_ref, *, n1, n2):
    gd = cdft_ref.shape[0]
    n_groups = x_ref.shape[3] // gd
    groups = range(n_groups)
    blk = range(SLAB_BLOCK)
    cdft = cdft_ref[...].astype(BF16)
    d1 = d1_ref[...].astype(BF16)
    d2 = d2_ref[...].astype(BF16)
    x = x_ref[0].reshape(n1 * n2, n_groups * gd)
    for g in groups:
        zc = _bdot(x[:, g * gd:(g + 1) * gd], cdft)
        zr_ref[g] = zc[:, :gd].reshape(n1, n2, gd)
        zi_ref[g] = zc[:, gd:].reshape(n1, n2, gd)

    def slabs(re_ref, im_ref, first):
        cols = pl.ds(pl.multiple_of(first, SLAB_BLOCK), SLAB_BLOCK)
        re = [jnp.swapaxes(re_ref[g, :, cols, :], 0, 1) for g in groups]
        im = [jnp.swapaxes(im_ref[g, :, cols, :], 0, 1) for g in groups]
        return jnp.concatenate(
            [jnp.concatenate([re[g][j], im[g][j]], axis=0) for j in blk for g in groups], axis=1)

    def stage1(cb, carry):
        first = cb * SLAB_BLOCK
        t = _bdot(d1, slabs(zr_ref, zi_ref, first).astype(BF16))
        for j in blk:
            cs = twc_ref[first + j]
            sn = tws_ref[first + j]
            for g in groups:
                col = (j * n_groups + g) * gd
                tr = t[:n1, col:col + gd]
                ti = t[n1:, col:col + gd]
                tr_ref[g, first + j] = tr * cs + ti * sn
                ti_ref[g, first + j] = ti * cs - tr * sn
        return carry

    lax.fori_loop(0, n2 // SLAB_BLOCK, stage1, 0)

    def stage2(kb, carry):
        first = kb * SLAB_BLOCK
        y = _bdot(d2, slabs(tr_ref, ti_ref, first).astype(BF16))
        rows = pl.ds(pl.multiple_of(first, SLAB_BLOCK), SLAB_BLOCK)
        for g in groups:
            yg = jnp.stack([y[:, (j * n_groups + g) * gd:(j * n_groups + g + 1) * gd] for j in blk])
            o_ref[0, :, rows, g * gd:(g + 1) * gd] = jnp.swapaxes(yg, 0, 1)
        return carry

    lax.fori_loop(0, n1 // SLAB_BLOCK, stage2, 0)


def _fourier(f3):
    nb, seq, width = f3.shape
    n2 = GRID_W
    n1 = seq // n2
    gd = FNET_GROUP_DIM
    cc, cs = _dft_mats(gd)
    c1, s1 = _dft_mats(n1)
    c2, s2 = _dft_mats(n2)
    norm = 1.0 / math.sqrt(seq * gd)
    cdft = jnp.asarray(np.concatenate([cc, -cs], axis=1) * norm, F32)
    d1 = jnp.asarray(np.block([[c1, s1], [-s1, c1]]), F32)
    d2 = jnp.asarray(np.concatenate([c2, s2], axis=1), F32)
    ang = 2.0 * np.pi * ((np.arange(n2)[:, None] * np.arange(n1)[None, :]) % seq) / seq
    twc = jnp.asarray(np.broadcast_to(np.cos(ang)[:, :, None], (n2, n1, gd)), F32)
    tws = jnp.asarray(np.broadcast_to(np.sin(ang)[:, :, None], (n2, n1, gd)), F32)
    n_groups = FOURIER_COLS // gd
    out = pl.pallas_call(
        functools.partial(_fourier_kernel, n1=n1, n2=n2),
        grid=(nb, width // FOURIER_COLS),
        in_specs=[
            pl.BlockSpec((1, n1, n2, FOURIER_COLS), lambda b, g: (b, 0, 0, g)),
            _const_spec(cdft.shape),
            _const_spec(d1.shape),
            _const_spec(d2.shape),
            _const_spec(twc.shape),
            _const_spec(tws.shape),
        ],
        out_specs=pl.BlockSpec((1, n2, n1, FOURIER_COLS), lambda b, g: (b, 0, 0, g)),
        out_shape=jax.ShapeDtypeStruct((nb, n2, n1, width), F32),
        scratch_shapes=[pltpu.VMEM((n_groups, n1, n2, gd), F32)] * 2
        + [pltpu.VMEM((n_groups, n2, n1, gd), F32)] * 2,
        compiler_params=_params(("arbitrary", "arbitrary")),
        name="fourier",
    )(f3.reshape(nb, n1, n2, width), cdft, d1, d2, twc, tws)
    return out.reshape(nb, seq, width)


def _merge_kernel(fm_ref, o_ref_in, x_ref, m_ref, wgate_ref, wf_ref, wna_ref, wout_ref,
                  g_ref, b_ref, out_ref, *, alpha):
    tm, d = x_ref.shape
    shift = m_ref[0, 3:4, :]
    scale = m_ref[0, 4:5, :]
    gate = m_ref[0, 5:6, :]
    sr = min(tm, MERGE_SUB_ROWS)
    rows = [slice(r0, r0 + sr) for r0 in range(0, tm, sr)]
    xs = [x_ref[rs, :] for rs in rows]
    us = [(x * (1.0 + scale) + shift).astype(BF16) for x in xs]
    slabs = [(fm_ref[rs, :], o_ref_in[rs, :], _bdot(u, wgate_ref[:, :d]), _bdot(u, wgate_ref[:, d:]), x)
             for rs, u, x in zip(rows, us, xs)]
    outs = _merge_out(slabs, gate, wf_ref, wna_ref, wout_ref, g_ref[1:2, :], b_ref[1:2, :], alpha)
    for rs, out in zip(rows, outs):
        out_ref[rs, :] = out


def _merge(fm2, o2, x2, x_first_row, mods_l, rows_per_group, group0, win, wf, wna, wout, layer,
           ln_g, ln_b, alpha):
    rows, d = fm2.shape[0], x2.shape[1]
    tm = math.gcd(math.gcd(MERGE_ROW_TILE, rows_per_group), x_first_row)
    tiles_per_group = rows_per_group // tm
    x_first_tile = x_first_row // tm
    row_spec = lambda w: pl.BlockSpec((tm, w), lambda i: (i, 0))
    assert win.shape[2] == 2 * (2 * d), "gate columns must be the second half of w_in"
    return pl.pallas_call(
        functools.partial(_merge_kernel, alpha=alpha),
        grid=(rows // tm,),
        in_specs=[
            row_spec(fm2.shape[1]), row_spec(o2.shape[1]),
            pl.BlockSpec((tm, d), lambda i: (x_first_tile + i, 0)),
            pl.BlockSpec((1, N_MOD, d), lambda i: (group0 + i // tiles_per_group, 0, 0)),
            _layer_spec(win, layer, cols=2 * d, col_block=1),
            _layer_spec(wf, layer), _layer_spec(wna, layer), _layer_spec(wout, layer),
            _const_spec(ln_g.shape), _const_spec(ln_b.shape),
        ],
        out_specs=row_spec(d),
        out_shape=jax.ShapeDtypeStruct((rows, d), F32),
        compiler_params=_params(("arbitrary",)),
        name="merge",
    )(fm2, o2, x2, mods_l, win, wf, wna, wout, ln_g, ln_b)


def kernel(x_prompt, x_sample, cache_k, cache_v, c, c_ctx, w_ada, b_ada, ln_g, ln_b, w_ff1_up,
           w_ff1_down, w_in, rpb, w_fourier, w_na_out, w_out, w_ff2_up, w_ff2_down):
    batch, seq, d = x_prompt.shape
    dec_batch, dec_seq, _ = x_sample.shape
    depth = w_ada.shape[0]
    alpha = (2 * depth) ** 0.25
    rows_lat = dec_seq // GRID_W

    cvec = jnp.zeros((MOD_ROWS, d), F32).at[0].set(c_ctx).at[1:1 + dec_batch].set(c)
    mods = _mods(cvec, w_ada, b_ada).reshape(depth, MOD_ROWS, N_MOD, d)

    y_p = x_prompt.reshape(batch * seq, d)
    y_s = x_sample.reshape(dec_batch * dec_seq, d)
    up1, dn1 = w_ff1_up.astype(BF16), w_ff1_down.astype(BF16)
    up2, dn2 = w_ff2_up.astype(BF16), w_ff2_down.astype(BF16)
    win = w_in.astype(BF16)
    wf = w_fourier.astype(BF16)
    wna = w_na_out.astype(BF16)
    wout = w_out.astype(BF16)
    ck = cache_k.transpose(1, 0, 3, 2, 4).reshape(depth, dec_batch, -1, NA_WIDTH).astype(BF16)
    cv = cache_v.transpose(1, 0, 3, 2, 4).reshape(depth, dec_batch, -1, NA_WIDTH).astype(BF16)

    bias = _nbr_bias_table(rpb.reshape((-1,) + rpb.shape[2:]), rows_lat)

    rows_p, rows_s = batch * seq, dec_batch * dec_seq

    def segments(src_p, first_p, src_s, first_s):
        return [(src_p, first_p, rows_p, 0, rows_p), (src_s, first_s, rows_s, 1, dec_seq)]

    kv = None
    pending = segments(y_p, 0, y_s, 0)
    for l in range(depth):
        m_l = mods[l]
        g_l, b_l = ln_g[l], ln_b[l]
        y = _ffn(pending, m_l, up1, dn1, l, g_l, b_l, 0, alpha)

        q2, k2, v2, f2 = _in_proj(y, rows_p, rows_s, m_l, dec_seq, 1, win, l)
        to3 = lambda t: t.reshape(dec_batch, dec_seq, t.shape[-1])
        o3 = _nbr_attn(to3(q2), to3(k2), to3(v2), ck[l], cv[l], bias, l)
        fm3 = _fourier(to3(f2))
        y_s = _merge(fm3.reshape(-1, FNET_WIDTH), o3.reshape(-1, NA_WIDTH), y, rows_p, m_l,
                     dec_seq, 1, win, wf, wna, wout, l, g_l, b_l, alpha)

        y, y_s = lax.optimization_barrier((y, y_s))
        y_p3, new_k, new_v = _ctx_mix(y.reshape(-1, seq, d), batch, m_l, win, wf, wna, wout,
                                      g_l, b_l, alpha, l, depth, kv)
        kv = (new_k, new_v)

        mixed = segments(y_p3.reshape(rows_p, d), 0, y_s, 0)
        if l + 1 < depth:
            y = _ffn(mixed, m_l, up2, dn2, l, g_l, b_l, 2, alpha)
            pending = segments(y, 0, y, rows_p)
        else:
            y_p = _ffn(mixed[:1], m_l, up2, dn2, l, g_l, b_l, 2, alpha)
            y_s = _ffn(mixed[1:], m_l, up2, dn2, l, g_l, b_l, 2, alpha)

    new_k, new_v = (t.transpose(0, 1, 2, 4, 3) for t in kv)
    return (y_p.reshape(batch, seq, d), y_s.reshape(dec_batch, dec_seq, d), new_k, new_v)
```

```python
import functools
import math

import numpy as np
import jax
import jax.numpy as jnp
from jax import lax
from jax.experimental import pallas as pl
from jax.experimental.pallas import tpu as pltpu

F32 = jnp.float32
BF16 = jnp.bfloat16

HEAD_DIM = 64
NA_HEADS = 8
NA_WIDTH = NA_HEADS * HEAD_DIM
FNET_GROUPS = 4
FNET_GROUP_DIM = 128
FNET_WIDTH = FNET_GROUPS * FNET_GROUP_DIM
GRID_W = 64
WIN_ROWS = 8
WIN_COLS = 16
N_SUB = 3
N_MOD = 3 * N_SUB
ATTN_SCALE = HEAD_DIM ** -0.5
LN_EPS = 1e-5
NEG_INF = -1e30

LANES = 128
MXU_DIM = 256
V7X_VMEM_BYTES = 64 * 1024 * 1024
VMEM_LIMIT_BYTES = V7X_VMEM_BYTES * 7 // 8

ROW_TILE = 512
FFN_ROW_TILE = 1024
FFN_SUB_ROWS = 512
MERGE_ROW_TILE = 1024
MERGE_SUB_ROWS = 256
FF_CHUNK = MXU_DIM
Q_ROWS = 4
K_ROWS = Q_ROWS + WIN_ROWS
KEY_BLOCK = GRID_W * math.gcd(Q_ROWS, WIN_ROWS // 2)
Q_BLOCKS_PER_STEP = 4
IN_CHUNK = 2 * MXU_DIM
MOD_ROWS = 8
MODS_K_ROWS = 256
FOURIER_COLS = 2 * FNET_GROUP_DIM
SLAB_BLOCK = 8
OUT_SLAB_BLOCK = 16


def _params(sem):
    return pltpu.CompilerParams(dimension_semantics=sem, vmem_limit_bytes=VMEM_LIMIT_BYTES)


def _const_spec(shape):
    nd = len(shape)
    return pl.BlockSpec(shape, lambda *_: (0,) * nd, pipeline_mode=pl.Buffered(1))


def _layer_spec(w, layer, cols=None, col_block=0):
    _, r, n = w.shape
    cols = n if cols is None else cols
    return pl.BlockSpec((None, r, cols), lambda *_: (layer, 0, col_block),
                        pipeline_mode=pl.Buffered(1))


def _layer_norm(r, g, b):
    mu = jnp.mean(r, axis=-1, keepdims=True)
    d = r - mu
    var = jnp.mean(d * d, axis=-1, keepdims=True)
    return d * lax.rsqrt(var + LN_EPS) * g + b


def _bdot(a, b):
    return jnp.dot(a, b, preferred_element_type=F32)


def _dot_nt(a, b):
    return lax.dot_general(a, b, (((1,), (1,)), ((), ())), preferred_element_type=F32)


def _dft_mats(n):
    k = np.arange(n)
    ang = 2.0 * np.pi * ((k[:, None] * k[None, :]) % n) / n
    return np.cos(ang), np.sin(ang)


def _mods_kernel(c_ref, w_ref, b_ref, o_ref):
    k = pl.program_id(1)
    c = c_ref[...]
    s = (c * jax.nn.sigmoid(c)).astype(BF16)
    part = _bdot(s, w_ref[0].astype(BF16))

    @pl.when(k == 0)
    def _():
        o_ref[0] = part + b_ref[0]

    @pl.when(k > 0)
    def _():
        o_ref[0] += part


def _mods(cvec, w_ada, b_ada):
    depth, d, n = w_ada.shape
    tk = MODS_K_ROWS
    return pl.pallas_call(
        _mods_kernel,
        grid=(depth, d // tk),
        in_specs=[
            pl.BlockSpec((MOD_ROWS, tk), lambda l, k: (0, k)),
            pl.BlockSpec((1, tk, n), lambda l, k: (l, k, 0)),
            pl.BlockSpec((1, 1, n), lambda l, k: (l, 0, 0)),
        ],
        out_specs=pl.BlockSpec((1, MOD_ROWS, n), lambda l, k: (l, 0, 0)),
        out_shape=jax.ShapeDtypeStruct((depth, MOD_ROWS, n), F32),
        compiler_params=_params(("arbitrary", "arbitrary")),
        name="mods",
    )(cvec, w_ada, b_ada.reshape(depth, 1, n))


def _ffn_kernel(*refs, sub, alpha, seg_starts):
    n_seg = len(seg_starts)
    x_refs = refs[:n_seg]
    m_ref, wup_ref, wdn_ref, g_ref, b_ref, o_ref, act_ref = refs[n_seg:]
    step = pl.program_id(0)

    def read_x(rs):
        x = x_refs[-1][rs, :]
        for s in range(n_seg - 2, -1, -1):
            x = jnp.where(step < seg_starts[s + 1], x_refs[s][rs, :], x)
        return x

    shift = m_ref[0, 3 * sub:3 * sub + 1, :]
    scale = m_ref[0, 3 * sub + 1:3 * sub + 2, :]
    gate = m_ref[0, 3 * sub + 2:3 * sub + 3, :]
    ff = wdn_ref.shape[0]
    tm = o_ref.shape[0]
    sr = min(tm, FFN_SUB_ROWS)
    slabs = [slice(r0, r0 + sr) for r0 in range(0, tm, sr)]
    us = [(read_x(rs) * (1.0 + scale) + shift).astype(BF16) for rs in slabs]
    for c0 in range(0, ff, FF_CHUNK):
        for rs, u in zip(slabs, us):
            a = _bdot(u, wup_ref[:, c0:c0 + FF_CHUNK])
            g = _bdot(u, wup_ref[:, ff + c0:ff + c0 + FF_CHUNK])
            act_ref[rs, c0:c0 + FF_CHUNK] = ((g * jax.nn.sigmoid(g)) * a).astype(BF16)
    for rs in slabs:
        y = _bdot(act_ref[rs, :], wdn_ref[...])
        r = alpha * read_x(rs) + (0.5 * gate) * y
        o_ref[rs, :] = _layer_norm(r, g_ref[sub:sub + 1, :], b_ref[sub:sub + 1, :])


def _ffn(segments, mods_l, wup, wdn, layer, ln_g, ln_b, sub, alpha):
    d = segments[0][0].shape[1]
    tm = FFN_ROW_TILE
    for _, first_row, n_rows, _, rows_per_group in segments:
        tm = math.gcd(math.gcd(tm, rows_per_group), math.gcd(n_rows, first_row))
    starts, n_steps = [], 0
    for _, _, n_rows, _, _ in segments:
        starts.append(n_steps)
        n_steps += n_rows // tm

    def x_spec(s):
        _, first_row, n_rows, _, _ = segments[s]
        first, count = first_row // tm, n_rows // tm
        return pl.BlockSpec((tm, d), lambda i: (first + jnp.clip(i - starts[s], 0, count - 1), 0))

    def group(i):
        g = None
        for s in range(len(segments) - 1, -1, -1):
            _, _, _, group0, rows_per_group = segments[s]
            g_s = group0 + jnp.maximum(i - starts[s], 0) // (rows_per_group // tm)
            g = g_s if g is None else jnp.where(i < starts[s + 1], g_s, g)
        return g

    return pl.pallas_call(
        functools.partial(_ffn_kernel, sub=sub, alpha=alpha, seg_starts=tuple(starts)),
        grid=(n_steps,),
        in_specs=[x_spec(s) for s in range(len(segments))] + [
            pl.BlockSpec((1, N_MOD, d), lambda i: (group(i), 0, 0)),
            _layer_spec(wup, layer),
            _layer_spec(wdn, layer),
            _const_spec(ln_g.shape),
            _const_spec(ln_b.shape),
        ],
        out_specs=pl.BlockSpec((tm, d), lambda i: (i, 0)),
        out_shape=jax.ShapeDtypeStruct((n_steps * tm, d), F32),
        scratch_shapes=[pltpu.VMEM((tm, wdn.shape[1]), BF16)],
        compiler_params=_params(("arbitrary",)),
        name=f"ffn{sub}",
    )(*[seg[0] for seg in segments], mods_l, wup, wdn, ln_g, ln_b)


def _row_reduce(arrays, combine, reduce):
    acc = None
    for a in arrays:
        for c0 in range(0, a.shape[1], LANES):
            chunk = a[:, c0:c0 + LANES]
            acc = chunk if acc is None else combine(acc, chunk)
    return reduce(acc, axis=-1, keepdims=True)


def _stacked_pair_attention(problems):
    lane = lax.broadcasted_iota(jnp.int32, (1, LANES), 1)
    scores = []
    for q, k_list, _, bias_list in problems:
        zero = jnp.zeros_like(q)
        q2 = jnp.concatenate([jnp.where(lane < HEAD_DIM, q, zero),
                              jnp.where(lane >= HEAD_DIM, q, zero)], axis=0)
        s = []
        for k, bias in zip(k_list, bias_list):
            sk = _dot_nt(q2, k)
            if bias is not None:
                sk = sk + jnp.concatenate(bias, axis=0)
            s.append(sk)
        scores.append(s)
    maxes = [_row_reduce(s, jnp.maximum, jnp.max) for s in scores]
    probs = [[jnp.exp(sk - m) for sk in s] for s, m in zip(scores, maxes)]
    dens = [_row_reduce(p, jnp.add, jnp.sum) for p in probs]
    outs = []
    for (q, _, v_list, _), p, den in zip(problems, probs, dens):
        o = None
        for pk, v in zip(p, v_list):
            pv = _bdot(pk.astype(BF16), v)
            o = pv if o is None else o + pv
        o = o / den
        m_rows = q.shape[0]
        outs.append(jnp.where(lane < HEAD_DIM, o[:m_rows], o[m_rows:]))
    return outs


def _merge_out(slabs, gate, wf_ref, wna_ref, wout_ref, ln_g, ln_b, alpha):
    branches = [(_bdot(fm.astype(BF16), wf_ref[...]), _bdot(o, wna_ref[...]))
                for fm, o, _, _, _ in slabs]
    merged = [(jax.nn.sigmoid(ga) * a + jax.nn.sigmoid(gb) * b).astype(BF16)
              for (_, _, ga, gb, _), (a, b) in zip(slabs, branches)]
    mixes = [_bdot(mg, wout_ref[...]) for mg in merged]
    return [_layer_norm(alpha * x + gate * mix, ln_g, ln_b)
            for (_, _, _, _, x), mix in zip(slabs, mixes)]


def _ctx_mix_kernel(x_ref, m_ref, win_ref, wf_ref, wna_ref, wout_ref, g_ref, b_ref,
                    cdft_ref, pdft_ref, *rest, alpha, n_alias, kv_slot):
    o_ref, k_ref, v_ref, z_ref, oatt_ref, fm_ref = rest[n_alias:]
    nb, seq, d = x_ref.shape
    rows = nb * seq
    x = x_ref[...].reshape(rows, d)
    shift = m_ref[0, 3:4, :]
    scale = m_ref[0, 4:5, :]
    gate = m_ref[0, 5:6, :]
    u = (x * (1.0 + scale) + shift).astype(BF16)
    for c0 in range(0, win_ref.shape[1], IN_CHUNK):
        z_ref[:, c0:c0 + IN_CHUNK] = _bdot(u, win_ref[:, c0:c0 + IN_CHUNK])

    cdft = cdft_ref[...].astype(BF16)
    pdft = pdft_ref[...].astype(BF16)
    q_off, k_off, v_off, f_off = 0, NA_WIDTH, 2 * NA_WIDTH, 3 * NA_WIDTH
    ga_off = f_off + FNET_WIDTH
    gb_off = ga_off + d
    for b in range(nb):
        r0 = b * seq
        for ref, off in ((k_ref, k_off), (v_ref, v_off)):
            t = z_ref[r0:r0 + seq, off:off + NA_WIDTH].T
            ref[b, kv_slot] = t.reshape(NA_HEADS, HEAD_DIM, seq)
        for slot in range(k_ref.shape[1]):
            if slot != kv_slot:
                k_ref[b, slot] = jnp.zeros(k_ref.shape[2:], F32)
                v_ref[b, slot] = jnp.zeros(v_ref.shape[2:], F32)
        zc = [_bdot(z_ref[r0:r0 + seq, f_off + g * FNET_GROUP_DIM:f_off + (g + 1) * FNET_GROUP_DIM]
                    .astype(BF16), cdft).astype(BF16) for g in range(FNET_GROUPS)]
        stacked = jnp.concatenate(
            [jnp.concatenate([z[:, :FNET_GROUP_DIM] for z in zc], axis=1),
             jnp.concatenate([z[:, FNET_GROUP_DIM:] for z in zc], axis=1)], axis=0)
        fm_ref[r0:r0 + seq, :] = _bdot(pdft, stacked)

    problems = []
    for b in range(nb):
        r0 = b * seq
        for hp in range(NA_WIDTH // LANES):
            c0 = hp * LANES
            q = (z_ref[r0:r0 + seq, q_off + c0:q_off + c0 + LANES] * ATTN_SCALE).astype(BF16)
            k = z_ref[r0:r0 + seq, k_off + c0:k_off + c0 + LANES].astype(BF16)
            v = z_ref[r0:r0 + seq, v_off + c0:v_off + c0 + LANES].astype(BF16)
            problems.append((b * seq, c0, (q, [k], [v], [None])))
    outs = _stacked_pair_attention([p for _, _, p in problems])
    for (r0, c0, _), o in zip(problems, outs):
        oatt_ref[r0:r0 + seq, c0:c0 + LANES] = o.astype(BF16)

    rows = [slice(b * seq, (b + 1) * seq) for b in range(nb)]
    slabs = [(fm_ref[rs, :], oatt_ref[rs, :], z_ref[rs, ga_off:ga_off + d],
              z_ref[rs, gb_off:gb_off + d], x_ref[b]) for b, rs in enumerate(rows)]
    outs = _merge_out(slabs, gate, wf_ref, wna_ref, wout_ref, g_ref[1:2, :], b_ref[1:2, :], alpha)
    for b, out in enumerate(outs):
        o_ref[b] = out


def _ctx_mix(x3, batch, mods_l, win, wf, wna, wout, ln_g, ln_b, alpha, layer, depth, kv_prev):
    _, seq, d = x3.shape
    nb = 2 if batch % 2 == 0 else 1
    in_width = win.shape[2]
    cc, cs = _dft_mats(FNET_GROUP_DIM)
    pc, ps = _dft_mats(seq)
    norm = 1.0 / math.sqrt(seq * FNET_GROUP_DIM)
    cdft = jnp.asarray(np.concatenate([cc, -cs], axis=1) * norm, F32)
    pdft = jnp.asarray(np.concatenate([pc, ps], axis=1), F32)
    kv_shape = jax.ShapeDtypeStruct((batch, depth, NA_HEADS, HEAD_DIM, seq), F32)
    if kv_prev is None:
        kv_spec = pl.BlockSpec((nb, depth, NA_HEADS, HEAD_DIM, seq), lambda i: (i, 0, 0, 0, 0))
        kv_slot = layer
    else:
        kv_spec = pl.BlockSpec((nb, 1, NA_HEADS, HEAD_DIM, seq), lambda i: (i, layer, 0, 0, 0))
        kv_slot = 0
    operands = [x3, mods_l, win, wf, wna, wout, ln_g, ln_b, cdft, pdft]
    in_specs = [
        pl.BlockSpec((nb, seq, d), lambda i: (i, 0, 0)),
        pl.BlockSpec((1, N_MOD, d), lambda i: (0, 0, 0)),
        _layer_spec(win, layer),
        _layer_spec(wf, layer),
        _layer_spec(wna, layer),
        _layer_spec(wout, layer),
        _const_spec(ln_g.shape),
        _const_spec(ln_b.shape),
        _const_spec(cdft.shape),
        _const_spec(pdft.shape),
    ]
    aliases = {}
    if kv_prev is not None:
        aliases = {len(operands): 1, len(operands) + 1: 2}
        operands += list(kv_prev)
        in_specs += [pl.BlockSpec(memory_space=pl.ANY)] * 2
    return pl.pallas_call(
        functools.partial(_ctx_mix_kernel, alpha=alpha, n_alias=len(aliases), kv_slot=kv_slot),
        grid=(batch // nb,),
        in_specs=in_specs,
        out_specs=[pl.BlockSpec((nb, seq, d), lambda i: (i, 0, 0)), kv_spec, kv_spec],
        out_shape=[jax.ShapeDtypeStruct((batch, seq, d), F32), kv_shape, kv_shape],
        input_output_aliases=aliases,
        scratch_shapes=[
            pltpu.VMEM((nb * seq, in_width), F32),
            pltpu.VMEM((nb * seq, NA_WIDTH), BF16),
            pltpu.VMEM((nb * seq, FNET_WIDTH), F32),
        ],
        compiler_params=_params(("arbitrary",)),
        name="ctx_mix",
    )(*operands)


def _in_proj_kernel(x_ref, m_ref, win_ref, q_ref, k_ref, v_ref, f_ref):
    x = x_ref[...]
    shift = m_ref[0, 3:4, :]
    scale = m_ref[0, 4:5, :]
    u = (x * (1.0 + scale) + shift).astype(BF16)
    cw = NA_WIDTH
    q_ref[...] = (_bdot(u, win_ref[:, 0:cw]) * ATTN_SCALE).astype(BF16)
    k_ref[...] = _bdot(u, win_ref[:, cw:2 * cw]).astype(BF16)
    v_ref[...] = _bdot(u, win_ref[:, 2 * cw:3 * cw]).astype(BF16)
    f_ref[...] = _bdot(u, win_ref[:, 3 * cw:3 * cw + FNET_WIDTH]).astype(BF16)


def _in_proj(x2, first_row, rows, mods_l, rows_per_group, group0, win, layer):
    d = x2.shape[1]
    tm = math.gcd(math.gcd(ROW_TILE, rows_per_group), first_row)
    tiles_per_group = rows_per_group // tm
    first_tile = first_row // tm
    cw = NA_WIDTH
    qkvf = 3 * NA_WIDTH + FNET_WIDTH
    row_spec = lambda w: pl.BlockSpec((tm, w), lambda i: (i, 0))
    return pl.pallas_call(
        _in_proj_kernel,
        grid=(rows // tm,),
        in_specs=[
            pl.BlockSpec((tm, d), lambda i: (first_tile + i, 0)),
            pl.BlockSpec((1, N_MOD, d), lambda i: (group0 + i // tiles_per_group, 0, 0)),
            _layer_spec(win, layer, cols=qkvf, col_block=0),
        ],
        out_specs=[row_spec(cw), row_spec(cw), row_spec(cw), row_spec(FNET_WIDTH)],
        out_shape=[
            jax.ShapeDtypeStruct((rows, cw), BF16),
            jax.ShapeDtypeStruct((rows, cw), BF16),
            jax.ShapeDtypeStruct((rows, cw), BF16),
            jax.ShapeDtypeStruct((rows, FNET_WIDTH), BF16),
        ],
        compiler_params=_params(("arbitrary",)),
        name="in_proj",
    )(x2, mods_l, win)


def _window_start_row(i, rows):
    return jnp.clip(Q_ROWS * i - WIN_ROWS // 2, 0, rows - K_ROWS)


def _nbr_attn_kernel(q_ref, k_ref, v_ref, ck_ref, cv_ref, *rest, rows):
    bias_refs, o_ref = rest[:-1], rest[-1]
    step = pl.program_id(1)
    n_keys = K_ROWS * GRID_W
    tq = Q_ROWS * GRID_W
    problems, places = [], []
    for j, bias_ref in enumerate(bias_refs):
        i = step * len(bias_refs) + j
        start = pl.multiple_of(_window_start_row(i, rows) * GRID_W, KEY_BLOCK)
        bias = (bias_ref[0, 0], bias_ref[1, 0])
        for b in range(q_ref.shape[0]):
            problems.append(
                (q_ref[b, j * tq:(j + 1) * tq, :],
                 [k_ref[b, pl.ds(start, n_keys), :], ck_ref[b]],
                 [v_ref[b, pl.ds(start, n_keys), :], cv_ref[b]],
                 [bias, None]))
            places.append((b, j))
    for (b, j), o in zip(places, _stacked_pair_attention(problems)):
        o_ref[b, j * tq:(j + 1) * tq, :] = o.astype(BF16)


def _bias_table_kernel(rpb_ref, ok_ref, o_ref, *, n_dr, n_dc, slot0, n_cols, starts):
    h = pl.program_id(0)
    base = h * (n_dr * n_dc)
    row = lax.broadcasted_iota(jnp.int32, (GRID_W, LANES), 0)
    lane = lax.broadcasted_iota(jnp.int32, (GRID_W, LANES), 1)
    rel = (lane & (GRID_W - 1)) - row + (WIN_COLS - 1)
    low = lane < GRID_W

    def column(dr_low, dr_high):
        tile = jnp.zeros((GRID_W, LANES), F32)
        valid = [0 <= dr < n_dr for dr in (dr_low, dr_high)]
        if not any(valid):
            return tile
        for d in range(n_dc):
            lo = rpb_ref[base + dr_low * n_dc + d] if valid[0] else 0.0
            hi = rpb_ref[base + dr_high * n_dc + d] if valid[1] else 0.0
            tile = jnp.where(rel == d, jnp.where(low, lo, hi), tile)
        return tile

    even = [column(2 * p - slot0, 2 * p + 1 - slot0) for p in range(n_cols)]
    swapped = [pltpu.roll(c, GRID_W, axis=1) for c in even]
    before = [jnp.zeros((GRID_W, LANES), F32)] + swapped[:-1]
    odd = [jnp.where(low, prev, cur) for prev, cur in zip(before, swapped)]
    builds = [even, odd]
    cols_per_slab = K_ROWS * GRID_W // LANES
    for t, start in enumerate(starts):
        shift = start % 2
        c0 = (start + shift) // 2
        window = jnp.concatenate(builds[shift][c0:c0 + cols_per_slab], axis=1)
        o_ref[0, t] = jnp.where(ok_ref[t] > 0.0, window, NEG_INF)


def _nbr_bias_table(rpb_all, rows):
    h, n_dr, n_dc = rpb_all.shape
    n_blocks = rows // Q_ROWS
    kr = min(WIN_ROWS, rows)
    variants = np.array([0, min(1, n_blocks - 1), n_blocks - 1])
    r = Q_ROWS * variants[:, None] + np.arange(Q_ROWS)[None, :]
    ks = np.clip(Q_ROWS * variants - WIN_ROWS // 2, 0, rows - K_ROWS)
    key_row = ks[:, None] + np.arange(K_ROWS)[None, :]
    r0 = np.clip(r - kr // 2, 0, rows - kr)
    row_ok = (key_row[:, None, :] >= r0[:, :, None]) & (key_row[:, None, :] < r0[:, :, None] + kr)
    c_idx = np.arange(GRID_W)
    c0 = np.clip(c_idx - WIN_COLS // 2, 0, GRID_W - WIN_COLS)
    col_ok = (c_idx[None, :] >= c0[:, None]) & (c_idx[None, :] < c0[:, None] + WIN_COLS)
    rel = c_idx[None, :] - c_idx[:, None] + (WIN_COLS - 1)
    assert rel[col_ok].min() >= 0 and rel[col_ok].max() < n_dc

    first_dr = (ks[:, None] - r + (WIN_ROWS - 1)).reshape(-1)
    slot0 = int(max(0, -first_dr.min()))
    starts = tuple(int(d) + slot0 for d in first_dr)
    n_cols = (max(starts) + 1 + K_ROWS + 1) // 2
    ok = (row_ok[:, :, None, :, None] & col_ok[None, None, :, None, :]).reshape(
        3 * Q_ROWS, GRID_W, K_ROWS * GRID_W).astype(np.float32)
    n_slabs = 3 * Q_ROWS
    table = pl.pallas_call(
        functools.partial(_bias_table_kernel, n_dr=n_dr, n_dc=n_dc, slot0=slot0, n_cols=n_cols,
                          starts=starts),
        grid=(h,),
        in_specs=[
            pl.BlockSpec(memory_space=pltpu.SMEM),
            _const_spec(ok.shape),
        ],
        out_specs=pl.BlockSpec((1, n_slabs, GRID_W, K_ROWS * GRID_W), lambda i: (i, 0, 0, 0)),
        out_shape=jax.ShapeDtypeStruct((h, n_slabs, GRID_W, K_ROWS * GRID_W), F32),
        compiler_params=_params(("arbitrary",)),
        name="bias_table",
    )(rpb_all.reshape(-1).astype(F32), jnp.asarray(ok))
    return table.reshape(h, 3, Q_ROWS * GRID_W, K_ROWS * GRID_W)


def _nbr_attn(q3, k3, v3, ck3, cv3, bias, layer):
    nb, seq, width = q3.shape
    rows = seq // GRID_W
    n_blocks = rows // Q_ROWS
    n_pairs = width // LANES
    past = ck3.shape[1]
    tq = Q_ROWS * GRID_W
    tk = K_ROWS * GRID_W

    per_step = Q_BLOCKS_PER_STEP if n_blocks % Q_BLOCKS_PER_STEP == 0 else 1

    def variant(i):
        return jnp.where(i == 0, 0, jnp.where(i == n_blocks - 1, 2, 1))

    def bias_spec(j):
        return pl.BlockSpec((2, 1, tq, tk),
                            lambda p, s: (layer * n_pairs + p, variant(s * per_step + j), 0, 0))

    return pl.pallas_call(
        functools.partial(_nbr_attn_kernel, rows=rows),
        grid=(n_pairs, n_blocks // per_step),
        in_specs=[
            pl.BlockSpec((nb, per_step * tq, LANES), lambda p, s: (0, s, p)),
            pl.BlockSpec((nb, seq, LANES), lambda p, s: (0, 0, p)),
            pl.BlockSpec((nb, seq, LANES), lambda p, s: (0, 0, p)),
            pl.BlockSpec((nb, past, LANES), lambda p, s: (0, 0, p)),
            pl.BlockSpec((nb, past, LANES), lambda p, s: (0, 0, p)),
        ] + [bias_spec(j) for j in range(per_step)],
        out_specs=pl.BlockSpec((nb, per_step * tq, LANES), lambda p, s: (0, s, p)),
        out_shape=jax.ShapeDtypeStruct((nb, seq, width), BF16),
        compiler_params=_params(("arbitrary", "arbitrary")),
        name="nbr_attn",
    )(q3, k3, v3, ck3, cv3, *([bias] * per_step))


def _fourier_kernel(x_ref, cdft_ref, d1_ref, d2_ref, twc_ref, tws_ref, o_ref,
                    zr_ref, zi_ref, tr_ref, ti_ref, *, n1, n2):
    gd = cdft_ref.shape[0]
    n_groups = x_ref.shape[3] // gd
    groups = range(n_groups)
    blk = range(SLAB_BLOCK)
    cdft = cdft_ref[...].astype(BF16)
    d1 = d1_ref[...].astype(BF16)
    d2 = d2_ref[...].astype(BF16)
    x = x_ref[0].reshape(n1 * n2, n_groups * gd)
    for g in groups:
        zc = _bdot(x[:, g * gd:(g + 1) * gd], cdft)
        zr_ref[g] = zc[:, :gd].reshape(n1, n2, gd)
        zi_ref[g] = zc[:, gd:].reshape(n1, n2, gd)

    def slabs(re_ref, im_ref, first, count):
        cols = pl.ds(pl.multiple_of(first, count), count)
        re = [jnp.swapaxes(re_ref[g, :, cols, :], 0, 1) for g in groups]
        im = [jnp.swapaxes(im_ref[g, :, cols, :], 0, 1) for g in groups]
        return jnp.concatenate(
            [jnp.concatenate([re[g][j], im[g][j]], axis=0) for j in range(count) for g in groups],
            axis=1)

    def stage1(cb, carry):
        first = cb * SLAB_BLOCK
        t = _bdot(d1, slabs(zr_ref, zi_ref, first, SLAB_BLOCK).astype(BF16))
        for j in blk:
            cs = twc_ref[first + j]
            sn = tws_ref[first + j]
            for g in groups:
                col = (j * n_groups + g) * gd
                tr = t[:n1, col:col + gd]
                ti = t[n1:, col:col + gd]
                tr_ref[g, first + j] = tr * cs + ti * sn
                ti_ref[g, first + j] = ti * cs - tr * sn
        return carry

    lax.fori_loop(0, n2 // SLAB_BLOCK, stage1, 0)

    def stage2(kb, carry):
        first = kb * OUT_SLAB_BLOCK
        y = _bdot(d2, slabs(tr_ref, ti_ref, first, OUT_SLAB_BLOCK).astype(BF16))
        rows = pl.ds(pl.multiple_of(first, OUT_SLAB_BLOCK), OUT_SLAB_BLOCK)
        for g in groups:
            yg = jnp.stack([y[:, (j * n_groups + g) * gd:(j * n_groups + g + 1) * gd]
                            for j in range(OUT_SLAB_BLOCK)])
            o_ref[0, :, rows, g * gd:(g + 1) * gd] = jnp.swapaxes(yg, 0, 1).astype(BF16)
        return carry

    lax.fori_loop(0, n1 // OUT_SLAB_BLOCK, stage2, 0)


def _fourier(f3):
    nb, seq, width = f3.shape
    n2 = GRID_W
    n1 = seq // n2
    gd = FNET_GROUP_DIM
    cc, cs = _dft_mats(gd)
    c1, s1 = _dft_mats(n1)
    c2, s2 = _dft_mats(n2)
    norm = 1.0 / math.sqrt(seq * gd)
    cdft = jnp.asarray(np.concatenate([cc, -cs], axis=1) * norm, F32)
    d1 = jnp.asarray(np.block([[c1, s1], [-s1, c1]]), F32)
    d2 = jnp.asarray(np.concatenate([c2, s2], axis=1), F32)
    ang = 2.0 * np.pi * ((np.arange(n2)[:, None] * np.arange(n1)[None, :]) % seq) / seq
    twc = jnp.asarray(np.broadcast_to(np.cos(ang)[:, :, None], (n2, n1, gd)), F32)
    tws = jnp.asarray(np.broadcast_to(np.sin(ang)[:, :, None], (n2, n1, gd)), F32)
    n_groups = FOURIER_COLS // gd
    out = pl.pallas_call(
        functools.partial(_fourier_kernel, n1=n1, n2=n2),
        grid=(nb, width // FOURIER_COLS),
        in_specs=[
            pl.BlockSpec((1, n1, n2, FOURIER_COLS), lambda b, g: (b, 0, 0, g)),
            _const_spec(cdft.shape),
            _const_spec(d1.shape),
            _const_spec(d2.shape),
            _const_spec(twc.shape),
            _const_spec(tws.shape),
        ],
        out_specs=pl.BlockSpec((1, n2, n1, FOURIER_COLS), lambda b, g: (b, 0, 0, g)),
        out_shape=jax.ShapeDtypeStruct((nb, n2, n1, width), BF16),
        scratch_shapes=[pltpu.VMEM((n_groups, n1, n2, gd), F32)] * 2
        + [pltpu.VMEM((n_groups, n2, n1, gd), F32)] * 2,
        compiler_params=_params(("arbitrary", "arbitrary")),
        name="fourier",
    )(f3.reshape(nb, n1, n2, width), cdft, d1, d2, twc, tws)
    return out.reshape(nb, seq, width)


def _merge_kernel(fm_ref, o_ref_in, x_ref, m_ref, wgate_ref, wf_ref, wna_ref, wout_ref,
                  g_ref, b_ref, out_ref, *, alpha):
    tm, d = x_ref.shape
    shift = m_ref[0, 3:4, :]
    scale = m_ref[0, 4:5, :]
    gate = m_ref[0, 5:6, :]
    sr = min(tm, MERGE_SUB_ROWS)
    rows = [slice(r0, r0 + sr) for r0 in range(0, tm, sr)]
    xs = [x_ref[rs, :] for rs in rows]
    us = [(x * (1.0 + scale) + shift).astype(BF16) for x in xs]
    slabs = [(fm_ref[rs, :], o_ref_in[rs, :], _bdot(u, wgate_ref[:, :d]), _bdot(u, wgate_ref[:, d:]), x)
             for rs, u, x in zip(rows, us, xs)]
    outs = _merge_out(slabs, gate, wf_ref, wna_ref, wout_ref, g_ref[1:2, :], b_ref[1:2, :], alpha)
    for rs, out in zip(rows, outs):
        out_ref[rs, :] = out


def _merge(fm2, o2, x2, x_first_row, mods_l, rows_per_group, group0, win, wf, wna, wout, layer,
           ln_g, ln_b, alpha):
    rows, d = fm2.shape[0], x2.shape[1]
    tm = math.gcd(math.gcd(MERGE_ROW_TILE, rows_per_group), x_first_row)
    tiles_per_group = rows_per_group // tm
    x_first_tile = x_first_row // tm
    row_spec = lambda w: pl.BlockSpec((tm, w), lambda i: (i, 0))
    assert win.shape[2] == 2 * (2 * d), "gate columns must be the second half of w_in"
    return pl.pallas_call(
        functools.partial(_merge_kernel, alpha=alpha),
        grid=(rows // tm,),
        in_specs=[
            row_spec(fm2.shape[1]), row_spec(o2.shape[1]),
            pl.BlockSpec((tm, d), lambda i: (x_first_tile + i, 0)),
            pl.BlockSpec((1, N_MOD, d), lambda i: (group0 + i // tiles_per_group, 0, 0)),
            _layer_spec(win, layer, cols=2 * d, col_block=1),
            _layer_spec(wf, layer), _layer_spec(wna, layer), _layer_spec(wout, layer),
            _const_spec(ln_g.shape), _const_spec(ln_b.shape),
        ],
        out_specs=row_spec(d),
        out_shape=jax.ShapeDtypeStruct((rows, d), F32),
        compiler_params=_params(("arbitrary",)),
        name="merge",
    )(fm2, o2, x2, mods_l, win, wf, wna, wout, ln_g, ln_b)


def kernel(x_prompt, x_sample, cache_k, cache_v, c, c_ctx, w_ada, b_ada, ln_g, ln_b, w_ff1_up,
           w_ff1_down, w_in, rpb, w_fourier, w_na_out, w_out, w_ff2_up, w_ff2_down):
    batch, seq, d = x_prompt.shape
    dec_batch, dec_seq, _ = x_sample.shape
    depth = w_ada.shape[0]
    alpha = (2 * depth) ** 0.25
    rows_lat = dec_seq // GRID_W

    cvec = jnp.zeros((MOD_ROWS, d), F32).at[0].set(c_ctx).at[1:1 + dec_batch].set(c)
    mods = _mods(cvec, w_ada, b_ada).reshape(depth, MOD_ROWS, N_MOD, d)

    y_p = x_prompt.reshape(batch * seq, d)
    y_s = x_sample.reshape(dec_batch * dec_seq, d)
    up1, dn1 = w_ff1_up.astype(BF16), w_ff1_down.astype(BF16)
    up2, dn2 = w_ff2_up.astype(BF16), w_ff2_down.astype(BF16)
    win = w_in.astype(BF16)
    wf = w_fourier.astype(BF16)
    wna = w_na_out.astype(BF16)
    wout = w_out.astype(BF16)
    ck = cache_k.transpose(1, 0, 3, 2, 4).reshape(depth, dec_batch, -1, NA_WIDTH).astype(BF16)
    cv = cache_v.transpose(1, 0, 3, 2, 4).reshape(depth, dec_batch, -1, NA_WIDTH).astype(BF16)

    bias = _nbr_bias_table(rpb.reshape((-1,) + rpb.shape[2:]), rows_lat)

    rows_p, rows_s = batch * seq, dec_batch * dec_seq

    def segments(src_p, first_p, src_s, first_s):
        return [(src_p, first_p, rows_p, 0, rows_p), (src_s, first_s, rows_s, 1, dec_seq)]

    kv = None
    pending = segments(y_p, 0, y_s, 0)
    for l in range(depth):
        m_l = mods[l]
        g_l, b_l = ln_g[l], ln_b[l]
        y = _ffn(pending, m_l, up1, dn1, l, g_l, b_l, 0, alpha)

        q2, k2, v2, f2 = _in_proj(y, rows_p, rows_s, m_l, dec_seq, 1, win, l)
        to3 = lambda t: t.reshape(dec_batch, dec_seq, t.shape[-1])
        o3 = _nbr_attn(to3(q2), to3(k2), to3(v2), ck[l], cv[l], bias, l)
        fm3 = _fourier(to3(f2))
        y_s = _merge(fm3.reshape(-1, FNET_WIDTH), o3.reshape(-1, NA_WIDTH), y, rows_p, m_l,
                     dec_seq, 1, win, wf, wna, wout, l, g_l, b_l, alpha)

        y, y_s = lax.optimization_barrier((y, y_s))
        y_p3, new_k, new_v = _ctx_mix(y.reshape(-1, seq, d), batch, m_l, win, wf, wna, wout,
                                      g_l, b_l, alpha, l, depth, kv)
        kv = (new_k, new_v)

        mixed = segments(y_p3.reshape(rows_p, d), 0, y_s, 0)
        if l + 1 < depth:
            y = _ffn(mixed, m_l, up2, dn2, l, g_l, b_l, 2, alpha)
            pending = segments(y, 0, y, rows_p)
        else:
            y_p = _ffn(mixed[:1], m_l, up2, dn2, l, g_l, b_l, 2, alpha)
            y_s = _ffn(mixed[1:], m_l, up2, dn2, l, g_l, b_l, 2, alpha)

    new_k, new_v = (t.transpose(0, 1, 2, 4, 3) for t in kv)
    return (y_p.reshape(batch, seq, d), y_s.reshape(dec_batch, dec_seq, d), new_k, new_v)
```

```python
import functools
import math

import numpy as np
import jax
import jax.numpy as jnp
from jax import lax
from jax.experimental import pallas as pl
from jax.experimental.pallas import tpu as pltpu

F32 = jnp.float32
BF16 = jnp.bfloat16

HEAD_DIM = 64
NA_HEADS = 8
NA_WIDTH = NA_HEADS * HEAD_DIM
FNET_GROUPS = 4
FNET_GROUP_DIM = 128
FNET_WIDTH = FNET_GROUPS * FNET_GROUP_DIM
GRID_W = 64
WIN_ROWS = 8
WIN_COLS = 16
N_SUB = 3
N_MOD = 3 * N_SUB
ATTN_SCALE = HEAD_DIM ** -0.5
LN_EPS = 1e-5
NEG_INF = -1e30

LANES = 128
MXU_DIM = 256
V7X_VMEM_BYTES = 64 * 1024 * 1024
VMEM_LIMIT_BYTES = V7X_VMEM_BYTES * 7 // 8

ROW_TILE = 512
FFN_ROW_TILE = 1024
FFN_SUB_ROWS = 512
MERGE_ROW_TILE = 1024
MERGE_SUB_ROWS = 256
FF_CHUNK = MXU_DIM
Q_ROWS = 4
K_ROWS = Q_ROWS + WIN_ROWS
KEY_BLOCK = GRID_W * math.gcd(Q_ROWS, WIN_ROWS // 2)
Q_BLOCKS_PER_STEP = 4
IN_CHUNK = 2 * MXU_DIM
MOD_ROWS = 8
MODS_K_ROWS = 256
FOURIER_COLS = 2 * FNET_GROUP_DIM
SLAB_BLOCK = 8


def _params(sem):
    return pltpu.CompilerParams(dimension_semantics=sem, vmem_limit_bytes=VMEM_LIMIT_BYTES)


def _const_spec(shape):
    nd = len(shape)
    return pl.BlockSpec(shape, lambda *_: (0,) * nd, pipeline_mode=pl.Buffered(1))


def _layer_spec(w, layer, cols=None, col_block=0):
    _, r, n = w.shape
    cols = n if cols is None else cols
    return pl.BlockSpec((None, r, cols), lambda *_: (layer, 0, col_block),
                        pipeline_mode=pl.Buffered(1))


def _layer_norm(r, g, b):
    mu = jnp.mean(r, axis=-1, keepdims=True)
    d = r - mu
    var = jnp.mean(d * d, axis=-1, keepdims=True)
    return d * lax.rsqrt(var + LN_EPS) * g + b


def _bdot(a, b):
    return jnp.dot(a, b, preferred_element_type=F32)


def _dot_nt(a, b):
    return lax.dot_general(a, b, (((1,), (1,)), ((), ())), preferred_element_type=F32)


def _dft_mats(n):
    k = np.arange(n)
    ang = 2.0 * np.pi * ((k[:, None] * k[None, :]) % n) / n
    return np.cos(ang), np.sin(ang)


def _mods_kernel(c_ref, w_ref, b_ref, o_ref):
    k = pl.program_id(1)
    c = c_ref[...]
    s = (c * jax.nn.sigmoid(c)).astype(BF16)
    part = _bdot(s, w_ref[0].astype(BF16))

    @pl.when(k == 0)
    def _():
        o_ref[0] = part + b_ref[0]

    @pl.when(k > 0)
    def _():
        o_ref[0] += part


def _mods(cvec, w_ada, b_ada):
    depth, d, n = w_ada.shape
    tk = MODS_K_ROWS
    return pl.pallas_call(
        _mods_kernel,
        grid=(depth, d // tk),
        in_specs=[
            pl.BlockSpec((MOD_ROWS, tk), lambda l, k: (0, k)),
            pl.BlockSpec((1, tk, n), lambda l, k: (l, k, 0)),
            pl.BlockSpec((1, 1, n), lambda l, k: (l, 0, 0)),
        ],
        out_specs=pl.BlockSpec((1, MOD_ROWS, n), lambda l, k: (l, 0, 0)),
        out_shape=jax.ShapeDtypeStruct((depth, MOD_ROWS, n), F32),
        compiler_params=_params(("arbitrary", "arbitrary")),
        name="mods",
    )(cvec, w_ada, b_ada.reshape(depth, 1, n))


def _ffn_kernel(*refs, sub, alpha, seg_starts):
    n_seg = len(seg_starts)
    x_refs = refs[:n_seg]
    m_ref, wup_ref, wdn_ref, g_ref, b_ref, o_ref, act_ref = refs[n_seg:]
    step = pl.program_id(0)

    def read_x(rs):
        x = x_refs[-1][rs, :]
        for s in range(n_seg - 2, -1, -1):
            x = jnp.where(step < seg_starts[s + 1], x_refs[s][rs, :], x)
        return x

    shift = m_ref[0, 3 * sub:3 * sub + 1, :]
    scale = m_ref[0, 3 * sub + 1:3 * sub + 2, :]
    gate = m_ref[0, 3 * sub + 2:3 * sub + 3, :]
    ff = wdn_ref.shape[0]
    tm = o_ref.shape[0]
    sr = min(tm, FFN_SUB_ROWS)
    slabs = [slice(r0, r0 + sr) for r0 in range(0, tm, sr)]
    us = [(read_x(rs) * (1.0 + scale) + shift).astype(BF16) for rs in slabs]
    for c0 in range(0, ff, FF_CHUNK):
        for rs, u in zip(slabs, us):
            a = _bdot(u, wup_ref[:, c0:c0 + FF_CHUNK])
            g = _bdot(u, wup_ref[:, ff + c0:ff + c0 + FF_CHUNK])
            act_ref[rs, c0:c0 + FF_CHUNK] = ((g * jax.nn.sigmoid(g)) * a).astype(BF16)
    for rs in slabs:
        y = _bdot(act_ref[rs, :], wdn_ref[...])
        r = alpha * read_x(rs) + (0.5 * gate) * y
        o_ref[rs, :] = _layer_norm(r, g_ref[sub:sub + 1, :], b_ref[sub:sub + 1, :])


def _ffn(segments, mods_l, wup, wdn, layer, ln_g, ln_b, sub, alpha):
    d = segments[0][0].shape[1]
    tm = FFN_ROW_TILE
    for _, first_row, n_rows, _, rows_per_group in segments:
        tm = math.gcd(math.gcd(tm, rows_per_group), math.gcd(n_rows, first_row))
    starts, n_steps = [], 0
    for _, _, n_rows, _, _ in segments:
        starts.append(n_steps)
        n_steps += n_rows // tm

    def x_spec(s):
        _, first_row, n_rows, _, _ = segments[s]
        first, count = first_row // tm, n_rows // tm
        return pl.BlockSpec((tm, d), lambda i: (first + jnp.clip(i - starts[s], 0, count - 1), 0))

    def group(i):
        g = None
        for s in range(len(segments) - 1, -1, -1):
            _, _, _, group0, rows_per_group = segments[s]
            g_s = group0 + jnp.maximum(i - starts[s], 0) // (rows_per_group // tm)
            g = g_s if g is None else jnp.where(i < starts[s + 1], g_s, g)
        return g

    return pl.pallas_call(
        functools.partial(_ffn_kernel, sub=sub, alpha=alpha, seg_starts=tuple(starts)),
        grid=(n_steps,),
        in_specs=[x_spec(s) for s in range(len(segments))] + [
            pl.BlockSpec((1, N_MOD, d), lambda i: (group(i), 0, 0)),
            _layer_spec(wup, layer),
            _layer_spec(wdn, layer),
            _const_spec(ln_g.shape),
            _const_spec(ln_b.shape),
        ],
        out_specs=pl.BlockSpec((tm, d), lambda i: (i, 0)),
        out_shape=jax.ShapeDtypeStruct((n_steps * tm, d), F32),
        scratch_shapes=[pltpu.VMEM((tm, wdn.shape[1]), BF16)],
        compiler_params=_params(("arbitrary",)),
        name=f"ffn{sub}",
    )(*[seg[0] for seg in segments], mods_l, wup, wdn, ln_g, ln_b)


def _row_reduce(arrays, combine, reduce):
    acc = None
    for a in arrays:
        for c0 in range(0, a.shape[1], LANES):
            chunk = a[:, c0:c0 + LANES]
            acc = chunk if acc is None else combine(acc, chunk)
    return reduce(acc, axis=-1, keepdims=True)


def _stacked_pair_attention(problems):
    lane = lax.broadcasted_iota(jnp.int32, (1, LANES), 1)
    scores = []
    for q, k_list, _, bias_list in problems:
        zero = jnp.zeros_like(q)
        q2 = jnp.concatenate([jnp.where(lane < HEAD_DIM, q, zero),
                              jnp.where(lane >= HEAD_DIM, q, zero)], axis=0)
        s = []
        for k, bias in zip(k_list, bias_list):
            sk = _dot_nt(q2, k)
            if bias is not None:
                sk = sk + jnp.concatenate(bias, axis=0)
            s.append(sk)
        scores.append(s)
    maxes = [_row_reduce(s, jnp.maximum, jnp.max) for s in scores]
    probs = [[jnp.exp(sk - m) for sk in s] for s, m in zip(scores, maxes)]
    dens = [_row_reduce(p, jnp.add, jnp.sum) for p in probs]
    outs = []
    for (q, _, v_list, _), p, den in zip(problems, probs, dens):
        o = None
        for pk, v in zip(p, v_list):
            pv = _bdot(pk.astype(BF16), v)
            o = pv if o is None else o + pv
        o = o / den
        m_rows = q.shape[0]
        outs.append(jnp.where(lane < HEAD_DIM, o[:m_rows], o[m_rows:]))
    return outs


def _merge_out(slabs, gate, wf_ref, wna_ref, wout_ref, ln_g, ln_b, alpha):
    branches = [(_bdot(fm.astype(BF16), wf_ref[...]), _bdot(o, wna_ref[...]))
                for fm, o, _, _, _ in slabs]
    merged = [(jax.nn.sigmoid(ga) * a + jax.nn.sigmoid(gb) * b).astype(BF16)
              for (_, _, ga, gb, _), (a, b) in zip(slabs, branches)]
    mixes = [_bdot(mg, wout_ref[...]) for mg in merged]
    return [_layer_norm(alpha * x + gate * mix, ln_g, ln_b)
            for (_, _, _, _, x), mix in zip(slabs, mixes)]


def _ctx_mix_kernel(x_ref, m_ref, win_ref, wf_ref, wna_ref, wout_ref, g_ref, b_ref,
                    cdft_ref, pdft_ref, *rest, alpha, n_alias, kv_slot):
    o_ref, k_ref, v_ref, z_ref, oatt_ref, fm_ref = rest[n_alias:]
    nb, seq, d = x_ref.shape
    rows = nb * seq
    x = x_ref[...].reshape(rows, d)
    shift = m_ref[0, 3:4, :]
    scale = m_ref[0, 4:5, :]
    gate = m_ref[0, 5:6, :]
    u = (x * (1.0 + scale) + shift).astype(BF16)
    for c0 in range(0, win_ref.shape[1], IN_CHUNK):
        z_ref[:, c0:c0 + IN_CHUNK] = _bdot(u, win_ref[:, c0:c0 + IN_CHUNK])

    cdft = cdft_ref[...].astype(BF16)
    pdft = pdft_ref[...].astype(BF16)
    q_off, k_off, v_off, f_off = 0, NA_WIDTH, 2 * NA_WIDTH, 3 * NA_WIDTH
    ga_off = f_off + FNET_WIDTH
    gb_off = ga_off + d
    for b in range(nb):
        r0 = b * seq
        for ref, off in ((k_ref, k_off), (v_ref, v_off)):
            t = z_ref[r0:r0 + seq, off:off + NA_WIDTH].T
            ref[b, kv_slot] = t.reshape(NA_HEADS, HEAD_DIM, seq)
        for slot in range(k_ref.shape[1]):
            if slot != kv_slot:
                k_ref[b, slot] = jnp.zeros(k_ref.shape[2:], F32)
                v_ref[b, slot] = jnp.zeros(v_ref.shape[2:], F32)
        zc = [_bdot(z_ref[r0:r0 + seq, f_off + g * FNET_GROUP_DIM:f_off + (g + 1) * FNET_GROUP_DIM]
                    .astype(BF16), cdft).astype(BF16) for g in range(FNET_GROUPS)]
        stacked = jnp.concatenate(
            [jnp.concatenate([z[:, :FNET_GROUP_DIM] for z in zc], axis=1),
             jnp.concatenate([z[:, FNET_GROUP_DIM:] for z in zc], axis=1)], axis=0)
        fm_ref[r0:r0 + seq, :] = _bdot(pdft, stacked)

    problems = []
    for b in range(nb):
        r0 = b * seq
        for hp in range(NA_WIDTH // LANES):
            c0 = hp * LANES
            q = (z_ref[r0:r0 + seq, q_off + c0:q_off + c0 + LANES] * ATTN_SCALE).astype(BF16)
            k = z_ref[r0:r0 + seq, k_off + c0:k_off + c0 + LANES].astype(BF16)
            v = z_ref[r0:r0 + seq, v_off + c0:v_off + c0 + LANES].astype(BF16)
            problems.append((b * seq, c0, (q, [k], [v], [None])))
    outs = _stacked_pair_attention([p for _, _, p in problems])
    for (r0, c0, _), o in zip(problems, outs):
        oatt_ref[r0:r0 + seq, c0:c0 + LANES] = o.astype(BF16)

    rows = [slice(b * seq, (b + 1) * seq) for b in range(nb)]
    slabs = [(fm_ref[rs, :], oatt_ref[rs, :], z_ref[rs, ga_off:ga_off + d],
              z_ref[rs, gb_off:gb_off + d], x_ref[b]) for b, rs in enumerate(rows)]
    outs = _merge_out(slabs, gate, wf_ref, wna_ref, wout_ref, g_ref[1:2, :], b_ref[1:2, :], alpha)
    for b, out in enumerate(outs):
        o_ref[b] = out


def _ctx_mix(x3, batch, mods_l, win, wf, wna, wout, ln_g, ln_b, alpha, layer, depth, kv_prev):
    _, seq, d = x3.shape
    nb = 2 if batch % 2 == 0 else 1
    in_width = win.shape[2]
    cc, cs = _dft_mats(FNET_GROUP_DIM)
    pc, ps = _dft_mats(seq)
    norm = 1.0 / math.sqrt(seq * FNET_GROUP_DIM)
    cdft = jnp.asarray(np.concatenate([cc, -cs], axis=1) * norm, F32)
    pdft = jnp.asarray(np.concatenate([pc, ps], axis=1), F32)
    kv_shape = jax.ShapeDtypeStruct((batch, depth, NA_HEADS, HEAD_DIM, seq), F32)
    if kv_prev is None:
        kv_spec = pl.BlockSpec((nb, depth, NA_HEADS, HEAD_DIM, seq), lambda i: (i, 0, 0, 0, 0))
        kv_slot = layer
    else:
        kv_spec = pl.BlockSpec((nb, 1, NA_HEADS, HEAD_DIM, seq), lambda i: (i, layer, 0, 0, 0))
        kv_slot = 0
    operands = [x3, mods_l, win, wf, wna, wout, ln_g, ln_b, cdft, pdft]
    in_specs = [
        pl.BlockSpec((nb, seq, d), lambda i: (i, 0, 0)),
        pl.BlockSpec((1, N_MOD, d), lambda i: (0, 0, 0)),
        _layer_spec(win, layer),
        _layer_spec(wf, layer),
        _layer_spec(wna, layer),
        _layer_spec(wout, layer),
        _const_spec(ln_g.shape),
        _const_spec(ln_b.shape),
        _const_spec(cdft.shape),
        _const_spec(pdft.shape),
    ]
    aliases = {}
    if kv_prev is not None:
        aliases = {len(operands): 1, len(operands) + 1: 2}
        operands += list(kv_prev)
        in_specs += [pl.BlockSpec(memory_space=pl.ANY)] * 2
    return pl.pallas_call(
        functools.partial(_ctx_mix_kernel, alpha=alpha, n_alias=len(aliases), kv_slot=kv_slot),
        grid=(batch // nb,),
        in_specs=in_specs,
        out_specs=[pl.BlockSpec((nb, seq, d), lambda i: (i, 0, 0)), kv_spec, kv_spec],
        out_shape=[jax.ShapeDtypeStruct((batch, seq, d), F32), kv_shape, kv_shape],
        input_output_aliases=aliases,
        scratch_shapes=[
            pltpu.VMEM((nb * seq, in_width), F32),
            pltpu.VMEM((nb * seq, NA_WIDTH), BF16),
            pltpu.VMEM((nb * seq, FNET_WIDTH), F32),
        ],
        compiler_params=_params(("arbitrary",)),
        name="ctx_mix",
    )(*operands)


def _in_proj_kernel(x_ref, m_ref, win_ref, q_ref, k_ref, v_ref, f_ref):
    x = x_ref[...]
    shift = m_ref[0, 3:4, :]
    scale = m_ref[0, 4:5, :]
    u = (x * (1.0 + scale) + shift).astype(BF16)
    cw = NA_WIDTH
    q_ref[...] = (_bdot(u, win_ref[:, 0:cw]) * ATTN_SCALE).astype(BF16)
    k_ref[...] = _bdot(u, win_ref[:, cw:2 * cw]).astype(BF16)
    v_ref[...] = _bdot(u, win_ref[:, 2 * cw:3 * cw]).astype(BF16)
    f_ref[...] = _bdot(u, win_ref[:, 3 * cw:3 * cw + FNET_WIDTH]).astype(BF16)


def _in_proj(x2, first_row, rows, mods_l, rows_per_group, group0, win, layer):
    d = x2.shape[1]
    tm = math.gcd(math.gcd(ROW_TILE, rows_per_group), first_row)
    tiles_per_group = rows_per_group // tm
    first_tile = first_row // tm
    cw = NA_WIDTH
    qkvf = 3 * NA_WIDTH + FNET_WIDTH
    row_spec = lambda w: pl.BlockSpec((tm, w), lambda i: (i, 0))
    return pl.pallas_call(
        _in_proj_kernel,
        grid=(rows // tm,),
        in_specs=[
            pl.BlockSpec((tm, d), lambda i: (first_tile + i, 0)),
            pl.BlockSpec((1, N_MOD, d), lambda i: (group0 + i // tiles_per_group, 0, 0)),
            _layer_spec(win, layer, cols=qkvf, col_block=0),
        ],
        out_specs=[row_spec(cw), row_spec(cw), row_spec(cw), row_spec(FNET_WIDTH)],
        out_shape=[
            jax.ShapeDtypeStruct((rows, cw), BF16),
            jax.ShapeDtypeStruct((rows, cw), BF16),
            jax.ShapeDtypeStruct((rows, cw), BF16),
            jax.ShapeDtypeStruct((rows, FNET_WIDTH), BF16),
        ],
        compiler_params=_params(("arbitrary",)),
        name="in_proj",
    )(x2, mods_l, win)


def _window_start_row(i, rows):
    return jnp.clip(Q_ROWS * i - WIN_ROWS // 2, 0, rows - K_ROWS)


def _nbr_attn_kernel(q_ref, k_ref, v_ref, ck_ref, cv_ref, *rest, rows):
    bias_refs, o_ref = rest[:-1], rest[-1]
    step = pl.program_id(1)
    n_keys = K_ROWS * GRID_W
    tq = Q_ROWS * GRID_W
    problems, places = [], []
    for j, bias_ref in enumerate(bias_refs):
        i = step * len(bias_refs) + j
        start = pl.multiple_of(_window_start_row(i, rows) * GRID_W, KEY_BLOCK)
        bias = (bias_ref[0, 0], bias_ref[1, 0])
        for b in range(q_ref.shape[0]):
            problems.append(
                (q_ref[b, j * tq:(j + 1) * tq, :],
                 [k_ref[b, pl.ds(start, n_keys), :], ck_ref[b]],
                 [v_ref[b, pl.ds(start, n_keys), :], cv_ref[b]],
                 [bias, None]))
            places.append((b, j))
    for (b, j), o in zip(places, _stacked_pair_attention(problems)):
        o_ref[b, j * tq:(j + 1) * tq, :] = o.astype(BF16)


def _bias_table_kernel(rpb_ref, ok_ref, o_ref, *, n_dr, n_dc, slot0, n_cols, starts):
    h = pl.program_id(0)
    base = h * (n_dr * n_dc)
    row = lax.broadcasted_iota(jnp.int32, (GRID_W, LANES), 0)
    lane = lax.broadcasted_iota(jnp.int32, (GRID_W, LANES), 1)
    rel = (lane & (GRID_W - 1)) - row + (WIN_COLS - 1)
    low = lane < GRID_W

    def column(dr_low, dr_high):
        tile = jnp.zeros((GRID_W, LANES), F32)
        valid = [0 <= dr < n_dr for dr in (dr_low, dr_high)]
        if not any(valid):
            return tile
        for d in range(n_dc):
            lo = rpb_ref[base + dr_low * n_dc + d] if valid[0] else 0.0
            hi = rpb_ref[base + dr_high * n_dc + d] if valid[1] else 0.0
            tile = jnp.where(rel == d, jnp.where(low, lo, hi), tile)
        return tile

    even = [column(2 * p - slot0, 2 * p + 1 - slot0) for p in range(n_cols)]
    swapped = [pltpu.roll(c, GRID_W, axis=1) for c in even]
    before = [jnp.zeros((GRID_W, LANES), F32)] + swapped[:-1]
    odd = [jnp.where(low, prev, cur) for prev, cur in zip(before, swapped)]
    builds = [even, odd]
    cols_per_slab = K_ROWS * GRID_W // LANES
    for t, start in enumerate(starts):
        shift = start % 2
        c0 = (start + shift) // 2
        window = jnp.concatenate(builds[shift][c0:c0 + cols_per_slab], axis=1)
        o_ref[0, t] = jnp.where(ok_ref[t] > 0.0, window, NEG_INF)


def _nbr_bias_table(rpb_all, rows):
    h, n_dr, n_dc = rpb_all.shape
    n_blocks = rows // Q_ROWS
    kr = min(WIN_ROWS, rows)

    def geometry(blocks):
        r = Q_ROWS * blocks[:, None] + np.arange(Q_ROWS)[None, :]
        ks = np.clip(Q_ROWS * blocks - WIN_ROWS // 2, 0, rows - K_ROWS)
        key_row = ks[:, None] + np.arange(K_ROWS)[None, :]
        r0 = np.clip(r - kr // 2, 0, rows - kr)
        ok = (key_row[:, None, :] >= r0[:, :, None]) & (key_row[:, None, :] < r0[:, :, None] + kr)
        return ks[:, None] - r + (WIN_ROWS - 1), ok

    variants = np.array([0, min(1, n_blocks - 1), n_blocks - 1])
    first_dr, row_ok = geometry(variants)
    blocks = np.arange(n_blocks)
    picked = np.where(blocks == 0, 0, np.where(blocks == n_blocks - 1, 2, 1))
    all_dr, all_ok = geometry(blocks)
    assert (all_dr == first_dr[picked]).all() and (all_ok == row_ok[picked]).all(), \
        "query blocks are not translation invariant: Q_ROWS must be at least WIN_ROWS // 2"
    c_idx = np.arange(GRID_W)
    c0 = np.clip(c_idx - WIN_COLS // 2, 0, GRID_W - WIN_COLS)
    col_ok = (c_idx[None, :] >= c0[:, None]) & (c_idx[None, :] < c0[:, None] + WIN_COLS)
    rel = c_idx[None, :] - c_idx[:, None] + (WIN_COLS - 1)
    assert rel[col_ok].min() >= 0 and rel[col_ok].max() < n_dc

    first_dr = first_dr.reshape(-1)
    slot0 = int(max(0, -first_dr.min()))
    starts = tuple(int(d) + slot0 for d in first_dr)
    n_cols = (max(starts) + 1 + K_ROWS + 1) // 2
    ok = (row_ok[:, :, None, :, None] & col_ok[None, None, :, None, :]).reshape(
        3 * Q_ROWS, GRID_W, K_ROWS * GRID_W).astype(np.float32)
    n_slabs = 3 * Q_ROWS
    table = pl.pallas_call(
        functools.partial(_bias_table_kernel, n_dr=n_dr, n_dc=n_dc, slot0=slot0, n_cols=n_cols,
                          starts=starts),
        grid=(h,),
        in_specs=[
            pl.BlockSpec(memory_space=pltpu.SMEM),
            _const_spec(ok.shape),
        ],
        out_specs=pl.BlockSpec((1, n_slabs, GRID_W, K_ROWS * GRID_W), lambda i: (i, 0, 0, 0)),
        out_shape=jax.ShapeDtypeStruct((h, n_slabs, GRID_W, K_ROWS * GRID_W), F32),
        compiler_params=_params(("arbitrary",)),
        name="bias_table",
    )(rpb_all.reshape(-1).astype(F32), jnp.asarray(ok))
    return table.reshape(h, 3, Q_ROWS * GRID_W, K_ROWS * GRID_W)


def _nbr_attn(q3, k3, v3, ck3, cv3, bias, layer):
    nb, seq, width = q3.shape
    rows = seq // GRID_W
    n_blocks = rows // Q_ROWS
    n_pairs = width // LANES
    past = ck3.shape[1]
    tq = Q_ROWS * GRID_W
    tk = K_ROWS * GRID_W

    per_step = Q_BLOCKS_PER_STEP if n_blocks % Q_BLOCKS_PER_STEP == 0 else 1

    def variant(i):
        return jnp.where(i == 0, 0, jnp.where(i == n_blocks - 1, 2, 1))

    def bias_spec(j):
        return pl.BlockSpec((2, 1, tq, tk),
                            lambda p, s: (layer * n_pairs + p, variant(s * per_step + j), 0, 0))

    return pl.pallas_call(
        functools.partial(_nbr_attn_kernel, rows=rows),
        grid=(n_pairs, n_blocks // per_step),
        in_specs=[
            pl.BlockSpec((nb, per_step * tq, LANES), lambda p, s: (0, s, p)),
            pl.BlockSpec((nb, seq, LANES), lambda p, s: (0, 0, p)),
            pl.BlockSpec((nb, seq, LANES), lambda p, s: (0, 0, p)),
            pl.BlockSpec((nb, past, LANES), lambda p, s: (0, 0, p)),
            pl.BlockSpec((nb, past, LANES), lambda p, s: (0, 0, p)),
        ] + [bias_spec(j) for j in range(per_step)],
        out_specs=pl.BlockSpec((nb, per_step * tq, LANES), lambda p, s: (0, s, p)),
        out_shape=jax.ShapeDtypeStruct((nb, seq, width), BF16),
        compiler_params=_params(("arbitrary", "arbitrary")),
        name="nbr_attn",
    )(q3, k3, v3, ck3, cv3, *([bias] * per_step))


def _fourier_kernel(x_ref, cdft_ref, d1_ref, d2_ref, twc_ref, tws_ref, o_ref,
                    zr_ref, zi_ref, tr_ref, ti_ref, *, n1, n2):
    gd = cdft_ref.shape[0]
    n_groups = x_ref.shape[3] // gd
    groups = range(n_groups)
    blk = range(SLAB_BLOCK)
    cdft = cdft_ref[...].astype(BF16)
    d1 = d1_ref[...].astype(BF16)
    d2 = d2_ref[...].astype(BF16)
    x = x_ref[0].reshape(n1 * n2, n_groups * gd)
    for g in groups:
        zc = _bdot(x[:, g * gd:(g + 1) * gd], cdft)
        zr_ref[g] = zc[:, :gd].reshape(n1, n2, gd)
        zi_ref[g] = zc[:, gd:].reshape(n1, n2, gd)

    def slabs(re_ref, im_ref, first):
        cols = pl.ds(pl.multiple_of(first, SLAB_BLOCK), SLAB_BLOCK)
        re = [jnp.swapaxes(re_ref[g, :, cols, :], 0, 1) for g in groups]
        im = [jnp.swapaxes(im_ref[g, :, cols, :], 0, 1) for g in groups]
        return jnp.concatenate(
            [jnp.concatenate([re[g][j], im[g][j]], axis=0) for j in blk for g in groups], axis=1)

    def stage1(cb, carry):
        first = cb * SLAB_BLOCK
        t = _bdot(d1, slabs(zr_ref, zi_ref, first).astype(BF16))
        for j in blk:
            cs = twc_ref[first + j]
            sn = tws_ref[first + j]
            for g in groups:
                col = (j * n_groups + g) * gd
                tr = t[:n1, col:col + gd]
                ti = t[n1:, col:col + gd]
                tr_ref[g, first + j] = tr * cs + ti * sn
                ti_ref[g, first + j] = ti * cs - tr * sn
        return carry

    lax.fori_loop(0, n2 // SLAB_BLOCK, stage1, 0)

    def stage2(kb, carry):
        first = kb * SLAB_BLOCK
        y = _bdot(d2, slabs(tr_ref, ti_ref, first).astype(BF16))
        rows = pl.ds(pl.multiple_of(first, SLAB_BLOCK), SLAB_BLOCK)
        for g in groups:
            yg = jnp.stack([y[:, (j * n_groups + g) * gd:(j * n_groups + g + 1) * gd] for j in blk])
            o_ref[0, :, rows, g * gd:(g + 1) * gd] = jnp.swapaxes(yg, 0, 1)
        return carry

    lax.fori_loop(0, n1 // SLAB_BLOCK, stage2, 0)


def _fourier(f3):
    nb, seq, width = f3.shape
    n2 = GRID_W
    n1 = seq // n2
    gd = FNET_GROUP_DIM
    cc, cs = _dft_mats(gd)
    c1, s1 = _dft_mats(n1)
    c2, s2 = _dft_mats(n2)
    norm = 1.0 / math.sqrt(seq * gd)
    cdft = jnp.asarray(np.concatenate([cc, -cs], axis=1) * norm, F32)
    d1 = jnp.asarray(np.block([[c1, s1], [-s1, c1]]), F32)
    d2 = jnp.asarray(np.concatenate([c2, s2], axis=1), F32)
    ang = 2.0 * np.pi * ((np.arange(n2)[:, None] * np.arange(n1)[None, :]) % seq) / seq
    twc = jnp.asarray(np.broadcast_to(np.cos(ang)[:, :, None], (n2, n1, gd)), F32)
    tws = jnp.asarray(np.broadcast_to(np.sin(ang)[:, :, None], (n2, n1, gd)), F32)
    n_groups = FOURIER_COLS // gd
    out = pl.pallas_call(
        functools.partial(_fourier_kernel, n1=n1, n2=n2),
        grid=(nb, width // FOURIER_COLS),
        in_specs=[
            pl.BlockSpec((1, n1, n2, FOURIER_COLS), lambda b, g: (b, 0, 0, g)),
            _const_spec(cdft.shape),
            _const_spec(d1.shape),
            _const_spec(d2.shape),
            _const_spec(twc.shape),
            _const_spec(tws.shape),
        ],
        out_specs=pl.BlockSpec((1, n2, n1, FOURIER_COLS), lambda b, g: (b, 0, 0, g)),
        out_shape=jax.ShapeDtypeStruct((nb, n2, n1, width), F32),
        scratch_shapes=[pltpu.VMEM((n_groups, n1, n2, gd), F32)] * 2
        + [pltpu.VMEM((n_groups, n2, n1, gd), F32)] * 2,
        compiler_params=_params(("arbitrary", "arbitrary")),
        name="fourier",
    )(f3.reshape(nb, n1, n2, width), cdft, d1, d2, twc, tws)
    return out.reshape(nb, seq, width)


def _merge_kernel(fm_ref, o_ref_in, x_ref, m_ref, wgate_ref, wf_ref, wna_ref, wout_ref,
                  g_ref, b_ref, out_ref, *, alpha):
    tm, d = x_ref.shape
    shift = m_ref[0, 3:4, :]
    scale = m_ref[0, 4:5, :]
    gate = m_ref[0, 5:6, :]
    sr = min(tm, MERGE_SUB_ROWS)
    rows = [slice(r0, r0 + sr) for r0 in range(0, tm, sr)]
    xs = [x_ref[rs, :] for rs in rows]
    us = [(x * (1.0 + scale) + shift).astype(BF16) for x in xs]
    slabs = [(fm_ref[rs, :], o_ref_in[rs, :], _bdot(u, wgate_ref[:, :d]), _bdot(u, wgate_ref[:, d:]), x)
             for rs, u, x in zip(rows, us, xs)]
    outs = _merge_out(slabs, gate, wf_ref, wna_ref, wout_ref, g_ref[1:2, :], b_ref[1:2, :], alpha)
    for rs, out in zip(rows, outs):
        out_ref[rs, :] = out


def _merge(fm2, o2, x2, x_first_row, mods_l, rows_per_group, group0, win, wf, wna, wout, layer,
           ln_g, ln_b, alpha):
    rows, d = fm2.shape[0], x2.shape[1]
    tm = math.gcd(math.gcd(MERGE_ROW_TILE, rows_per_group), x_first_row)
    tiles_per_group = rows_per_group // tm
    x_first_tile = x_first_row // tm
    row_spec = lambda w: pl.BlockSpec((tm, w), lambda i: (i, 0))
    assert win.shape[2] == 2 * (2 * d), "gate columns must be the second half of w_in"
    return pl.pallas_call(
        functools.partial(_merge_kernel, alpha=alpha),
        grid=(rows // tm,),
        in_specs=[
            row_spec(fm2.shape[1]), row_spec(o2.shape[1]),
            pl.BlockSpec((tm, d), lambda i: (x_first_tile + i, 0)),
            pl.BlockSpec((1, N_MOD, d), lambda i: (group0 + i // tiles_per_group, 0, 0)),
            _layer_spec(win, layer, cols=2 * d, col_block=1),
            _layer_spec(wf, layer), _layer_spec(wna, layer), _layer_spec(wout, layer),
            _const_spec(ln_g.shape), _const_spec(ln_b.shape),
        ],
        out_specs=row_spec(d),
        out_shape=jax.ShapeDtypeStruct((rows, d), F32),
        compiler_params=_params(("arbitrary",)),
        name="merge",
    )(fm2, o2, x2, mods_l, win, wf, wna, wout, ln_g, ln_b)


def kernel(x_prompt, x_sample, cache_k, cache_v, c, c_ctx, w_ada, b_ada, ln_g, ln_b, w_ff1_up,
           w_ff1_down, w_in, rpb, w_fourier, w_na_out, w_out, w_ff2_up, w_ff2_down):
    batch, seq, d = x_prompt.shape
    dec_batch, dec_seq, _ = x_sample.shape
    depth = w_ada.shape[0]
    alpha = (2 * depth) ** 0.25
    rows_lat = dec_seq // GRID_W

    cvec = jnp.zeros((MOD_ROWS, d), F32).at[0].set(c_ctx).at[1:1 + dec_batch].set(c)
    mods = _mods(cvec, w_ada, b_ada).reshape(depth, MOD_ROWS, N_MOD, d)

    y_p = x_prompt.reshape(batch * seq, d)
    y_s = x_sample.reshape(dec_batch * dec_seq, d)
    up1, dn1 = w_ff1_up.astype(BF16), w_ff1_down.astype(BF16)
    up2, dn2 = w_ff2_up.astype(BF16), w_ff2_down.astype(BF16)
    win = w_in.astype(BF16)
    wf = w_fourier.astype(BF16)
    wna = w_na_out.astype(BF16)
    wout = w_out.astype(BF16)
    ck = cache_k.transpose(1, 0, 3, 2, 4).reshape(depth, dec_batch, -1, NA_WIDTH).astype(BF16)
    cv = cache_v.transpose(1, 0, 3, 2, 4).reshape(depth, dec_batch, -1, NA_WIDTH).astype(BF16)

    bias = _nbr_bias_table(rpb.reshape((-1,) + rpb.shape[2:]), rows_lat)

    rows_p, rows_s = batch * seq, dec_batch * dec_seq

    def segments(src_p, first_p, src_s, first_s):
        return [(src_p, first_p, rows_p, 0, rows_p), (src_s, first_s, rows_s, 1, dec_seq)]

    kv = None
    pending = segments(y_p, 0, y_s, 0)
    for l in range(depth):
        m_l = mods[l]
        g_l, b_l = ln_g[l], ln_b[l]
        y = _ffn(pending, m_l, up1, dn1, l, g_l, b_l, 0, alpha)

        q2, k2, v2, f2 = _in_proj(y, rows_p, rows_s, m_l, dec_seq, 1, win, l)
        to3 = lambda t: t.reshape(dec_batch, dec_seq, t.shape[-1])
        o3 = _nbr_attn(to3(q2), to3(k2), to3(v2), ck[l], cv[l], bias, l)
        fm3 = _fourier(to3(f2))
        y_s = _merge(fm3.reshape(-1, FNET_WIDTH), o3.reshape(-1, NA_WIDTH), y, rows_p, m_l,
                     dec_seq, 1, win, wf, wna, wout, l, g_l, b_l, alpha)

        y, y_s = lax.optimization_barrier((y, y_s))
        y_p3, new_k, new_v = _ctx_mix(y.reshape(-1, seq, d), batch, m_l, win, wf, wna, wout,
                                      g_l, b_l, alpha, l, depth, kv)
        kv = (new_k, new_v)

        mixed = segments(y_p3.reshape(rows_p, d), 0, y_s, 0)
        if l + 1 < depth:
            y = _ffn(mixed, m_l, up2, dn2, l, g_l, b_l, 2, alpha)
            pending = segments(y, 0, y, rows_p)
        else:
            y_p = _ffn(mixed[:1], m_l, up2, dn2, l, g_l, b_l, 2, alpha)
            y_s = _ffn(mixed[1:], m_l, up2, dn2, l, g_l, b_l, 2, alpha)

    new_k, new_v = (t.transpose(0, 1, 2, 4, 3) for t in kv)
    return (y_p.reshape(batch, seq, d), y_s.reshape(dec_batch, dec_seq, d), new_k, new_v)
```
